```python
import math
import jax
import jax.numpy as jnp
from jax import lax
import numpy as np

D_MODEL = 1024
BATCH = 8
SEQ = 2048
DEPTH = 1
DEC_BATCH = 128
DEC_SEQ = 8
PAST_LEN = 16384
PAGE_SIZE = 128

D_MIX = D_MODEL
D_HGRN = D_MIX // 2
HGRN_HEADS = 4
HGRN_DK = D_HGRN // HGRN_HEADS
HGRN_DV = D_HGRN // HGRN_HEADS
HGRN_CHUNK = 32
D_SSM = D_MIX - D_HGRN
SSM_GROUP = 16
SSM_GROUPS = D_SSM // SSM_GROUP
SSM_STATE = 64
D_IN = 4 * D_HGRN + D_SSM
MOE_GROUPS = 4
MOE_PER_GROUP = 8
MOE_TOP_K = 2
D_EXPERT = D_MODEL // 4
EPS = 1e-6
DT_MIN = 1e-3
DT_MAX = 1e-1
MAX_REAL = -1e-4

kernel_name = 'hybrid_hgrn2_s5_hmoe_step'


def _rms(x):
    x = x.astype(jnp.float32)
    return x * lax.rsqrt(jnp.mean(x * x, axis=-1, keepdims=True) + EPS)


def _adanorm(x, gain, shift, scale):
    h = _rms(x) * gain.astype(jnp.float32)
    return (h * (1.0 + scale[:, None, :]) + shift[:, None, :]).astype(x.dtype)


def _hgrn2_recurrence(q, k, v, logf, s0):
    bsz, t, nh, _ = q.shape
    dv = v.shape[-1]
    c = HGRN_CHUNK
    n = -(-t // c)
    pad = n * c - t

    def blocks(a):
        a = jnp.pad(a, ((0, 0), (0, pad), (0, 0), (0, 0)))
        return a.reshape(bsz, n, c, nh, a.shape[-1]).transpose(1, 0, 3, 2, 4)

    causal = jnp.tril(jnp.ones((c, c), dtype=bool))[:, :, None]

    def step(state, blk):
        qb, kb, vb, gb = blk
        cum = jnp.cumsum(gb, axis=2)
        rel = cum[:, :, :, None, :] - cum[:, :, None, :, :]
        decay = jnp.exp(jnp.where(causal, rel, -jnp.inf))
        scores = jnp.einsum('bhtk,bhsk,bhtsk->bhts', qb, kb, decay)
        o = (jnp.einsum('bhts,bhsv->bhtv', scores, vb)
             + jnp.einsum('bhtk,bhkv->bhtv', qb * jnp.exp(cum), state))
        last = cum[:, :, -1:, :]
        state = (jnp.exp(last[:, :, 0, :, None]) * state
                 + jnp.einsum('bhsk,bhsv->bhkv', kb * jnp.exp(last - cum), vb))
        return state, o

    s_final, o = lax.scan(step, s0, (blocks(q), blocks(k), blocks(v), blocks(logf)))
    o = o.transpose(1, 0, 3, 2, 4).reshape(bsz, n * c, nh, dv)[:, :t]
    return o, s_final


def _complex_affine_combine(e1, e2):
    a1r, a1i, b1r, b1i = e1
    a2r, a2i, b2r, b2i = e2
    return (a2r * a1r - a2i * a1i,
            a2r * a1i + a2i * a1r,
            a2r * b1r - a2i * b1i + b2r,
            a2r * b1i + a2i * b1r + b2i)


def _s5_mixer(u, s_re, s_im, p):
    f32 = jnp.float32
    bsz, t, _ = u.shape
    ug = u.astype(f32).reshape(bsz, t, SSM_GROUPS, SSM_GROUP)
    lam_re = jnp.minimum(p['ssm_a_re'].astype(f32), MAX_REAL)
    lam_im = p['ssm_a_im'].astype(f32)
    dt = jnp.exp(p['ssm_log_dt'].astype(f32))
    mag = jnp.exp(lam_re * dt)
    abar_re = mag * jnp.cos(lam_im * dt)
    abar_im = mag * jnp.sin(lam_im * dt)
    den = lam_re * lam_re + lam_im * lam_im
    coef_re = ((abar_re - 1.0) * lam_re + abar_im * lam_im) / den
    coef_im = (abar_im * lam_re - (abar_re - 1.0) * lam_im) / den
    b_re = p['ssm_b_re'].astype(f32)
    b_im = p['ssm_b_im'].astype(f32)
    bbar_re = coef_re[..., None] * b_re - coef_im[..., None] * b_im
    bbar_im = coef_re[..., None] * b_im + coef_im[..., None] * b_re
    bu_re = jnp.einsum('btgh,gph->btgp', ug, bbar_re)
    bu_im = jnp.einsum('btgh,gph->btgp', ug, bbar_im)
    x0_re = s_re.astype(f32)
    x0_im = s_im.astype(f32)
    bu_re = bu_re.at[:, 0].add(abar_re * x0_re - abar_im * x0_im)
    bu_im = bu_im.at[:, 0].add(abar_re * x0_im + abar_im * x0_re)
    a_re = jnp.broadcast_to(abar_re, bu_re.shape)
    a_im = jnp.broadcast_to(abar_im, bu_im.shape)
    _, _, xs_re, xs_im = lax.associative_scan(_complex_affine_combine, (a_re, a_im, bu_re, bu_im), axis=1)
    y = (jnp.einsum('btgp,ghp->btgh', xs_re, p['ssm_c_re'].astype(f32))
         - jnp.einsum('btgp,ghp->btgh', xs_im, p['ssm_c_im'].astype(f32))
         + p['ssm_d'].astype(f32) * ug)
    y = jax.nn.gelu(y).reshape(bsz, t, D_SSM)
    y = y * jax.nn.sigmoid(jnp.dot(y, p['ssm_w_glu'].astype(f32)) + p['ssm_b_glu'].astype(f32))
    y = _rms(y) * p['ssm_norm'].astype(f32)
    return y, xs_re[:, -1], xs_im[:, -1]


def _hier_moe(h, p):
    f32 = jnp.float32
    bsz, t, d = h.shape
    xt = h.reshape(bsz * t, d)
    grp_logits = (jnp.dot(xt, p['moe_w_group']) + p['moe_b_group']).astype(f32)
    grp_prob = jax.nn.softmax(grp_logits, axis=-1)
    grp_onehot = jax.nn.one_hot(jnp.argmax(grp_logits, axis=-1), MOE_GROUPS, dtype=f32)
    grp_w = jnp.sum(grp_prob * grp_onehot, axis=-1)
    exp_logits = (jnp.einsum('nd,gde->nge', xt, p['moe_w_router']) + p['moe_b_router']).astype(f32)
    sel_logits = jnp.einsum('nge,ng->ne', exp_logits, grp_onehot)
    top_val, top_idx = lax.top_k(sel_logits, MOE_TOP_K)
    top_w = jax.nn.softmax(top_val, axis=-1) * grp_w[:, None]
    within = jnp.einsum('nk,nke->ne', top_w, jax.nn.one_hot(top_idx, MOE_PER_GROUP, dtype=f32))
    combine = grp_onehot[:, :, None] * within[:, None, :]
    out = jnp.zeros((bsz * t, d), f32)
    for g in range(MOE_GROUPS):
        a = jnp.einsum('nd,edf->nef', xt, p['moe_w_gate'][g])
        b = jnp.einsum('nd,edf->nef', xt, p['moe_w_up'][g])
        act = jax.nn.silu(a) * b * combine[:, g, :, None].astype(a.dtype)
        out = out + jnp.einsum('nef,efd->nd', act, p['moe_w_down'][g])
    return out.reshape(bsz, t, d)


def _layer(x, c, s_h, s_re, s_im, lb, p):
    f32 = jnp.float32
    bsz, t, _ = x.shape
    ada = (jnp.dot(jax.nn.silu(c), p['w_ada']) + p['b_ada']).astype(f32)
    shift1, scale1, gate1, shift2, scale2, gate2 = jnp.split(ada, 6, axis=-1)
    h = _adanorm(x, p['norm_mix'], shift1, scale1)
    proj = jnp.einsum('btd,de->bte', h, p['w_in'])
    q_raw, f_raw, i_raw, g_raw, u = jnp.split(proj, [D_HGRN, 2 * D_HGRN, 3 * D_HGRN, 4 * D_HGRN], axis=-1)

    def heads(a):
        return a.astype(f32).reshape(bsz, t, HGRN_HEADS, -1)
    f = lb + (1.0 - lb) * jax.nn.sigmoid(f_raw.astype(f32))
    q = heads(jax.nn.silu(q_raw.astype(f32))) * (HGRN_DK ** -0.5)
    o_h, s_h_new = _hgrn2_recurrence(q, heads(1.0 - f), heads(i_raw), heads(jnp.log(f)), s_h.astype(f32))
    o_h = (_rms(o_h) * p['hgrn_norm'].astype(f32).reshape(HGRN_HEADS, HGRN_DV)).reshape(bsz, t, D_HGRN)
    o_h = o_h * jax.nn.silu(g_raw.astype(f32))

    o_s, s_re_new, s_im_new = _s5_mixer(u, s_re, s_im, p)

    mix = jnp.concatenate([o_h, o_s], axis=-1).astype(x.dtype)
    x = x + (gate1[:, None, :] * jnp.einsum('bte,ed->btd', mix, p['w_out'])).astype(x.dtype)
    h = _adanorm(x, p['norm_ffn'], shift2, scale2)
    x = x + (gate2[:, None, :] * _hier_moe(h, p)).astype(x.dtype)
    return x, s_h_new, s_re_new, s_im_new


def _trunk(x, c, s_h, s_re, s_im, lower_bounds, layer_params, w_ada_final, b_ada_final, norm_final):
    hs, res, ims = [], [], []
    for l in range(DEPTH):
        x, a, b, d = _layer(x, c, s_h[l], s_re[l], s_im[l], lower_bounds[l], layer_params[l])
        hs.append(a.astype(x.dtype))
        res.append(b.astype(x.dtype))
        ims.append(d.astype(x.dtype))
    ada = (jnp.dot(jax.nn.silu(c), w_ada_final) + b_ada_final).astype(jnp.float32)
    shift, scale = jnp.split(ada, 2, axis=-1)
    y = _adanorm(x, norm_final, shift, scale)
    return y, jnp.stack(hs), jnp.stack(res), jnp.stack(ims)


def setup_inputs(seed: int = 0) -> dict:
    key = jax.random.key(seed)
    ks = iter(jax.random.split(key, 64))
    f32 = jnp.float32
    d, L = D_MODEL, DEPTH
    G, P, H = SSM_GROUPS, SSM_STATE, SSM_GROUP

    def normal(shape, scale):
        return jax.random.normal(next(ks), shape, f32) * scale

    def gain(shape):
        return 1.0 + normal(shape, 0.01)

    out = {}
    out['x_prompt'] = normal((BATCH, SEQ, d), 1.0)
    out['x_sample'] = normal((DEC_BATCH, DEC_SEQ, d), 1.0)
    out['c_prompt'] = normal((BATCH, d), 1.0)
    out['c_sample'] = normal((DEC_BATCH, d), 1.0)
    out['state_hgrn'] = normal((L, DEC_BATCH, HGRN_HEADS, HGRN_DK, HGRN_DV), 0.5)
    out['state_ssm_re'] = normal((L, DEC_BATCH, G, P), 0.1)
    out['state_ssm_im'] = normal((L, DEC_BATCH, G, P), 0.1)
    out['hgrn_lb_logits'] = normal((L + 1, D_HGRN), 0.1)
    out['w_ada'] = normal((L, d, 6 * d), 0.2 * d ** -0.5)
    out['b_ada'] = normal((L, 6 * d), 0.02)
    out['norm_mix'] = gain((L, d))
    out['w_in'] = normal((L, d, D_IN), d ** -0.5)
    out['hgrn_norm'] = gain((L, D_HGRN))
    out['ssm_a_re'] = -0.5 + normal((L, G, P), 0.01)
    out['ssm_a_im'] = math.pi * jnp.arange(P, dtype=f32) + normal((L, G, P), 0.01)
    out['ssm_log_dt'] = jax.random.uniform(next(ks), (L, G, P), f32, math.log(DT_MIN), math.log(DT_MAX))
    out['ssm_b_re'] = normal((L, G, P, H), (2.0 * H) ** -0.5)
    out['ssm_b_im'] = normal((L, G, P, H), (2.0 * H) ** -0.5)
    out['ssm_c_re'] = normal((L, G, H, P), (2.0 * P) ** -0.5)
    out['ssm_c_im'] = normal((L, G, H, P), (2.0 * P) ** -0.5)
    out['ssm_d'] = normal((L, G, H), 1.0)
    out['ssm_w_glu'] = normal((L, D_SSM, D_SSM), D_SSM ** -0.5)
    out['ssm_b_glu'] = normal((L, D_SSM), 0.01)
    out['ssm_norm'] = gain((L, D_SSM))
    out['w_out'] = normal((L, D_MIX, d), D_MIX ** -0.5)
    out['norm_ffn'] = gain((L, d))
    out['moe_w_group'] = normal((L, d, MOE_GROUPS), d ** -0.5)
    out['moe_b_group'] = normal((L, MOE_GROUPS), 0.01)
    out['moe_w_router'] = normal((L, MOE_GROUPS, d, MOE_PER_GROUP), d ** -0.5)
    out['moe_b_router'] = normal((L, MOE_GROUPS, MOE_PER_GROUP), 0.01)
    out['moe_w_gate'] = normal((L, MOE_GROUPS, MOE_PER_GROUP, d, D_EXPERT), d ** -0.5)
    out['moe_w_up'] = normal((L, MOE_GROUPS, MOE_PER_GROUP, d, D_EXPERT), d ** -0.5)
    out['moe_w_down'] = normal((L, MOE_GROUPS, MOE_PER_GROUP, D_EXPERT, d), D_EXPERT ** -0.5)
    out['w_ada_final'] = normal((d, 2 * d), 0.2 * d ** -0.5)
    out['b_ada_final'] = normal((2 * d,), 0.02)
    out['norm_final'] = gain((d,))
    return out


def reference(x_prompt, x_sample, c_prompt, c_sample, state_hgrn, state_ssm_re, state_ssm_im,
              hgrn_lb_logits, w_ada, b_ada, norm_mix, w_in, hgrn_norm,
              ssm_a_re, ssm_a_im, ssm_log_dt, ssm_b_re, ssm_b_im, ssm_c_re, ssm_c_im, ssm_d,
              ssm_w_glu, ssm_b_glu, ssm_norm, w_out, norm_ffn,
              moe_w_group, moe_b_group, moe_w_router, moe_b_router, moe_w_gate, moe_w_up, moe_w_down,
              w_ada_final, b_ada_final, norm_final):
    lower_bounds = jnp.cumsum(jax.nn.softmax(hgrn_lb_logits.astype(jnp.float32), axis=0), axis=0)
    layer_params = [dict(w_ada=w_ada[l], b_ada=b_ada[l], norm_mix=norm_mix[l], w_in=w_in[l],
                         hgrn_norm=hgrn_norm[l], ssm_a_re=ssm_a_re[l], ssm_a_im=ssm_a_im[l],
                         ssm_log_dt=ssm_log_dt[l], ssm_b_re=ssm_b_re[l], ssm_b_im=ssm_b_im[l],
                         ssm_c_re=ssm_c_re[l], ssm_c_im=ssm_c_im[l], ssm_d=ssm_d[l],
                         ssm_w_glu=ssm_w_glu[l], ssm_b_glu=ssm_b_glu[l], ssm_norm=ssm_norm[l],
                         w_out=w_out[l], norm_ffn=norm_ffn[l],
                         moe_w_group=moe_w_group[l], moe_b_group=moe_b_group[l],
                         moe_w_router=moe_w_router[l], moe_b_router=moe_b_router[l],
                         moe_w_gate=moe_w_gate[l], moe_w_up=moe_w_up[l], moe_w_down=moe_w_down[l])
                    for l in range(DEPTH)]
    bp = x_prompt.shape[0]
    zeros_h = jnp.zeros((DEPTH, bp, HGRN_HEADS, HGRN_DK, HGRN_DV), jnp.float32)
    zeros_s = jnp.zeros((DEPTH, bp, SSM_GROUPS, SSM_STATE), jnp.float32)
    y_prompt, hgrn_prompt, ssm_re_prompt, ssm_im_prompt = _trunk(
        x_prompt, c_prompt, zeros_h, zeros_s, zeros_s, lower_bounds, layer_params,
        w_ada_final, b_ada_final, norm_final)
    y_sample, hgrn_sample, ssm_re_sample, ssm_im_sample = _trunk(
        x_sample, c_sample, state_hgrn, state_ssm_re, state_ssm_im, lower_bounds, layer_params,
        w_ada_final, b_ada_final, norm_final)
    return (y_prompt, y_sample, hgrn_prompt, ssm_re_prompt, ssm_im_prompt, hgrn_sample, ssm_re_sample, ssm_im_sample)
```

```python
import functools
import math

import jax
import jax.numpy as jnp
from jax import lax
from jax.experimental import pallas as pl
from jax.experimental.pallas import tpu as pltpu

F32 = jnp.float32
BF16 = jnp.bfloat16
HIGHEST = lax.Precision.HIGHEST

EPS = 1e-6
MAX_REAL = -1e-4
HGRN_HEADS = 4
SSM_GROUP = 16
SSM_STATE = 64
SSM_CHUNK = 8
SSM_SETS = 4
MOE_GROUPS = 4
MOE_PER_GROUP = 8
ROUTER_LANES = 128
EXPERT_LANE0 = 32
VMEM_LIMIT = 56 * 1024 * 1024


def _cparams(sem):
    return pltpu.CompilerParams(dimension_semantics=sem, vmem_limit_bytes=VMEM_LIMIT)


def _silu(x):
    return x * jax.nn.sigmoid(x)


def _rms(x):
    return x * lax.rsqrt(jnp.mean(x * x, axis=-1, keepdims=True) + EPS)


def _dot(a, b):
    return jnp.dot(a, b, preferred_element_type=F32)


def _dot_nt(a, b):
    return lax.dot_general(a, b, (((1,), (1,)), ((), ())), preferred_element_type=F32)


def _dot_tn(a, b, precision=None):
    return lax.dot_general(a, b, (((0,), (0,)), ((), ())), preferred_element_type=F32,
                           precision=precision)


def _silu_linear_kernel(c_ref, w_ref, b_ref, o_ref):
    a = _silu(c_ref[...]).astype(BF16)
    o_ref[...] = _dot(a, w_ref[...].astype(BF16)) + b_ref[...]


def _silu_linear(c, w, b):
    m, d = c.shape
    n = w.shape[1]
    tn = 1024
    return pl.pallas_call(
        _silu_linear_kernel,
        grid=(n // tn,),
        in_specs=[pl.BlockSpec((m, d), lambda j: (0, 0)),
                  pl.BlockSpec((d, tn), lambda j: (0, j)),
                  pl.BlockSpec((1, tn), lambda j: (0, j))],
        out_specs=pl.BlockSpec((m, tn), lambda j: (0, j)),
        out_shape=jax.ShapeDtypeStruct((m, n), F32),
        compiler_params=_cparams(("parallel",)),
        name="silu_linear",
    )(c, w, b.reshape(1, n))


def _inproj_kernel(x_ref, shift_ref, scale_ref, gain_ref, w_ref, lb_ref,
                   q_ref, k_ref, g_ref, v_ref, gs_ref, u_ref, *, dh):
    bb, tt, d = x_ref.shape
    h = _rms(x_ref[...]) * gain_ref[...]
    h = h * (1.0 + scale_ref[...]) + shift_ref[...]
    proj = _dot(h.reshape(bb * tt, d).astype(BF16), w_ref[...])
    lb = lb_ref[...]
    f = lb + (1.0 - lb) * jax.nn.sigmoid(proj[:, dh:2 * dh])
    q_ref[...] = _silu(proj[:, :dh]) * (float(dh // HGRN_HEADS) ** -0.5)
    k_ref[...] = 1.0 - f
    g_ref[...] = jnp.log(f)
    v_ref[...] = proj[:, 2 * dh:3 * dh]
    gs_ref[...] = _silu(proj[:, 3 * dh:4 * dh])
    u_ref[...] = proj[:, 4 * dh:]


def _inproj(x, ada3, gain, w_in_bf, lb, bb, tt):
    b, t, d = x.shape
    dh = lb.shape[-1]
    nt = t // tt
    rows = bb * tt
    n = b * t
    row_spec = pl.BlockSpec((rows, dh), lambda i, j: (i * nt + j, 0))
    out = jax.ShapeDtypeStruct((n, dh), F32)
    return pl.pallas_call(
        functools.partial(_inproj_kernel, dh=dh),
        grid=(b // bb, nt),
        in_specs=[pl.BlockSpec((bb, tt, d), lambda i, j: (i, j, 0)),
                  pl.BlockSpec((bb, 1, d), lambda i, j: (i, 0, 0)),
                  pl.BlockSpec((bb, 1, d), lambda i, j: (i, 0, 1)),
                  pl.BlockSpec((1, d), lambda i, j: (0, 0)),
                  pl.BlockSpec(w_in_bf.shape, lambda i, j: (0, 0)),
                  pl.BlockSpec((1, dh), lambda i, j: (0, 0))],
        out_specs=[row_spec] * 6,
        out_shape=[out] * 6,
        compiler_params=_cparams(("parallel", "parallel")),
        name="inproj",
    )(x, ada3, ada3, gain, w_in_bf, lb)


def _hgrn_kernel(q_ref, k_ref, g_ref, v_ref, gs_ref, s0_ref, hn_ref, o_ref, sf_ref, st_ref,
                 *, chunk, nt):
    j = pl.program_id(1)
    ct, dh = q_ref.shape
    dk = dh // HGRN_HEADS
    c = chunk

    @pl.when(j == 0)
    def _():
        st_ref[...] = s0_ref[0]

    r = lax.broadcasted_iota(jnp.int32, (c, c), 0)
    s = lax.broadcasted_iota(jnp.int32, (c, c), 1)
    causal = r >= s
    tri = causal.astype(F32)
    ones = jnp.ones((c, dk), F32)
    mid = c // 2 - 1

    for ci in range(ct // c):
        rows = slice(ci * c, (ci + 1) * c)
        for h in range(HGRN_HEADS):
            lanes = slice(h * dk, (h + 1) * dk)
            g = g_ref[rows, lanes]
            q = q_ref[rows, lanes]
            k = k_ref[rows, lanes]
            v = v_ref[rows, lanes].astype(BF16)
            a = jnp.dot(tri, g, precision=HIGHEST, preferred_element_type=F32)
            a_mid = a[mid:mid + 1]
            a_end = a[c - 1:c]
            qt = (q * jnp.exp(a - a_mid)).astype(BF16)
            kt = (k * jnp.exp(a_mid - a)).astype(BF16)
            sc = jnp.where(causal, _dot_nt(qt, kt), 0.0).astype(BF16)
            state = st_ref[h]
            o = _dot(sc, v) + _dot((q * jnp.exp(a)).astype(BF16), state.astype(BF16))
            decay = jnp.exp(_dot_tn(g, ones, precision=HIGHEST))
            st_ref[h] = decay * state + _dot_tn((k * jnp.exp(a_end - a)).astype(BF16), v)
            o_ref[rows, lanes] = _rms(o) * hn_ref[:, lanes] * gs_ref[rows, lanes]

    @pl.when(j == nt - 1)
    def _():
        sf_ref[0] = st_ref[...]


def _hgrn(q, k, g, v, gs, s0, hnorm, b, t, ct, chunk):
    n, dh = q.shape
    nt = t // ct
    dk = dh // HGRN_HEADS
    row_spec = pl.BlockSpec((ct, dh), lambda i, j: (i * nt + j, 0))
    st_spec = pl.BlockSpec((1, HGRN_HEADS, dk, dk), lambda i, j: (i, 0, 0, 0))
    return pl.pallas_call(
        functools.partial(_hgrn_kernel, chunk=chunk, nt=nt),
        grid=(b, nt),
        in_specs=[row_spec] * 5 + [st_spec, pl.BlockSpec((1, dh), lambda i, j: (0, 0))],
        out_specs=[row_spec, st_spec],
        out_shape=[jax.ShapeDtypeStruct((n, dh), F32),
                   jax.ShapeDtypeStruct((b, HGRN_HEADS, dk, dk), F32)],
        scratch_shapes=[pltpu.VMEM((HGRN_HEADS, dk, dk), F32)],
        compiler_params=_cparams(("parallel", "arbitrary")),
        name="hgrn",
    )(q, k, g, v, gs, s0, hnorm)


def _s5_prepare(a_re, a_im, log_dt, b_re, b_im, c_re, c_im, d_skip):
    ng, npp = a_re.shape
    nh = b_re.shape[-1]
    L = SSM_CHUNK
    gs = ng // SSM_SETS
    lam_re = jnp.minimum(a_re, MAX_REAL)
    lam_im = a_im
    dt = jnp.exp(log_dt)
    mag = jnp.exp(lam_re * dt)
    ab_re = mag * jnp.cos(lam_im * dt)
    ab_im = mag * jnp.sin(lam_im * dt)
    den = lam_re * lam_re + lam_im * lam_im
    co_re = ((ab_re - 1.0) * lam_re + ab_im * lam_im) / den
    co_im = (ab_im * lam_re - (ab_re - 1.0) * lam_im) / den
    bb_re = co_re[..., None] * b_re - co_im[..., None] * b_im
    bb_im = co_re[..., None] * b_im + co_im[..., None] * b_re
    pw_re = [jnp.ones_like(ab_re)]
    pw_im = [jnp.zeros_like(ab_im)]
    for _ in range(L):
        pr, pi = pw_re[-1], pw_im[-1]
        pw_re.append(pr * ab_re - pi * ab_im)
        pw_im.append(pr * ab_im + pi * ab_re)
    pw_re = jnp.stack(pw_re)
    pw_im = jnp.stack(pw_im)
    ab_b_re = pw_re[:L, :, :, None] * bb_re - pw_im[:L, :, :, None] * bb_im
    ab_b_im = pw_re[:L, :, :, None] * bb_im + pw_im[:L, :, :, None] * bb_re
    kern = (jnp.einsum('gkp,lgph->lghk', c_re, ab_b_re, precision=HIGHEST)
            - jnp.einsum('gkp,lgph->lghk', c_im, ab_b_im, precision=HIGHEST))
    kern = kern.at[0].add(d_skip[:, :, None] * jnp.eye(nh, dtype=F32))
    lag = jnp.arange(L)[None, :] - jnp.arange(L)[:, None]
    toep = jnp.where((lag >= 0)[:, :, None, None, None], kern[jnp.clip(lag, 0, L - 1)], 0.0)
    eye = jnp.eye(gs, dtype=F32)
    toep = toep.reshape(L, L, SSM_SETS, gs, nh, nh)
    w_t = jnp.einsum('stGghk,gq->Gsghtqk', toep, eye).reshape(SSM_SETS, L * gs * nh, L * gs * nh)
    inc_re = ab_b_re[::-1].reshape(L, SSM_SETS, gs, npp, nh)
    inc_im = ab_b_im[::-1].reshape(L, SSM_SETS, gs, npp, nh)
    n_re = jnp.einsum('sGgph,gq->Gsghqp', inc_re, eye).reshape(SSM_SETS, L * gs * nh, gs * npp)
    n_im = jnp.einsum('sGgph,gq->Gsghqp', inc_im, eye).reshape(SSM_SETS, L * gs * nh, gs * npp)
    w1 = jnp.concatenate([w_t, n_re, n_im], axis=-1).astype(BF16)
    ca_re = c_re[None] * pw_re[1:, :, None, :] - c_im[None] * pw_im[1:, :, None, :]
    ca_im = c_re[None] * pw_im[1:, :, None, :] + c_im[None] * pw_re[1:, :, None, :]
    ca_re = ca_re.reshape(L, SSM_SETS, gs, nh, npp)
    ca_im = ca_im.reshape(L, SSM_SETS, gs, nh, npp)
    m_re = jnp.einsum('tGgkp,gq->Ggptqk', ca_re, eye).reshape(SSM_SETS, gs * npp, L * gs * nh)
    m_im = jnp.einsum('tGgkp,gq->Ggptqk', -ca_im, eye).reshape(SSM_SETS, gs * npp, L * gs * nh)
    m = jnp.concatenate([m_re, m_im], axis=1).astype(BF16)
    a8 = jnp.concatenate([pw_re[L].reshape(SSM_SETS, 1, gs * npp),
                          pw_im[L].reshape(SSM_SETS, 1, gs * npp)], axis=-1)
    return w1, m, a8


def _s5_kernel(u_ref, xr_ref, xi_ref, w1_ref, m_ref, a8_ref, y_ref, fr_ref, fi_ref, *, sequential):
    n = u_ref.shape[1]
    ns = xr_ref.shape[-1]
    ny = y_ref.shape[-1]
    res = _dot(u_ref[0].astype(BF16), w1_ref[0])
    d_re = res[:, ny:ny + ns]
    d_im = res[:, ny + ns:]
    a_re = a8_ref[0][:, :ns]
    a_im = a8_ref[0][:, ns:]
    x0_re = xr_ref[0]
    x0_im = xi_ref[0]
    if sequential:
        row = lax.broadcasted_iota(jnp.int32, (n, ns), 0)
        first = row == 0
        x_re = d_re + jnp.where(first, a_re * x0_re - a_im * x0_im, 0.0)
        x_im = d_im + jnp.where(first, a_re * x0_im + a_im * x0_re, 0.0)
        p_re, p_im = a_re, a_im
        step = 1
        while step < n:
            s_re = jnp.where(row >= step, pltpu.roll(x_re, step, 0), 0.0)
            s_im = jnp.where(row >= step, pltpu.roll(x_im, step, 0), 0.0)
            x_re, x_im = x_re + p_re * s_re - p_im * s_im, x_im + p_re * s_im + p_im * s_re
            p_re, p_im = p_re * p_re - p_im * p_im, 2.0 * p_re * p_im
            step *= 2
        fr_ref[0] = x_re[n - 1:n]
        fi_ref[0] = x_im[n - 1:n]
        xc_re = jnp.where(first, x0_re, pltpu.roll(x_re, 1, 0))
        xc_im = jnp.where(first, x0_im, pltpu.roll(x_im, 1, 0))
    else:
        xc_re, xc_im = x0_re, x0_im
        fr_ref[0] = a_re * x0_re - a_im * x0_im + d_re
        fi_ref[0] = a_re * x0_im + a_im * x0_re + d_im
    xc = jnp.concatenate([xc_re, xc_im], axis=-1).astype(BF16)
    y_ref[0] = res[:, :ny] + _dot(xc, m_ref[0])


def _s5(u8, x_re, x_im, w1, m, a8, n_rows, sequential):
    sets, r, nu = u8.shape
    nb, rb, _ = x_re.shape
    ns = m.shape[1] // 2
    st_spec = pl.BlockSpec((1, rb, ns), lambda gi, i: (i, 0, gi))
    st_shape = jax.ShapeDtypeStruct(x_re.shape, F32)
    return pl.pallas_call(
        functools.partial(_s5_kernel, sequential=sequential),
        grid=(sets, nb),
        in_specs=[pl.BlockSpec((1, n_rows, nu), lambda gi, i: (gi, i, 0)),
                  st_spec, st_spec,
                  pl.BlockSpec((1,) + w1.shape[1:], lambda gi, i: (gi, 0, 0)),
                  pl.BlockSpec((1,) + m.shape[1:], lambda gi, i: (gi, 0, 0)),
                  pl.BlockSpec((1, 1, 2 * ns), lambda gi, i: (gi, 0, 0))],
        out_specs=[pl.BlockSpec((1, n_rows, nu), lambda gi, i: (gi, i, 0)), st_spec, st_spec],
        out_shape=[jax.ShapeDtypeStruct(u8.shape, F32), st_shape, st_shape],
        compiler_params=_cparams(("parallel", "parallel")),
        name="s5",
    )(u8, x_re, x_im, w1, m, a8)


def _to_chunk_rows(u, n):
    return u.reshape(n // SSM_CHUNK, SSM_CHUNK, SSM_SETS, -1).transpose(2, 0, 1, 3).reshape(
        SSM_SETS, n // SSM_CHUNK, -1)


def _from_chunk_rows(y8, n):
    return y8.reshape(SSM_SETS, n // SSM_CHUNK, SSM_CHUNK, -1).transpose(1, 2, 0, 3).reshape(n, -1)


def _gelu_tanh(x):
    return 0.5 * x * (1.0 + jnp.tanh(math.sqrt(2.0 / math.pi) * (x + 0.044715 * (x * x * x))))


def _outproj_kernel(x_ref, oh_ref, ys_ref, gate_ref, shift_ref, scale_ref, wglu_ref, bglu_ref, sn_ref,
                    wo_h_ref, wo_s_ref, nf_ref, wr_ref, br_ref, x1_ref, h2_ref, comb_ref):
    bb, tt, d = x_ref.shape
    rows = bb * tt
    y = _gelu_tanh(ys_ref[...])
    y = y * jax.nn.sigmoid(_dot(y.astype(BF16), wglu_ref[...]) + bglu_ref[...])
    o_s = _rms(y) * sn_ref[...]
    mix = _dot(oh_ref[...].astype(BF16), wo_h_ref[...]) + _dot(o_s.astype(BF16), wo_s_ref[...])
    x1 = x_ref[...] + gate_ref[...] * mix.reshape(bb, tt, d)
    x1_ref[...] = x1
    h2 = _rms(x1) * nf_ref[...]
    h2 = (h2 * (1.0 + scale_ref[...]) + shift_ref[...]).reshape(rows, d).astype(BF16)
    h2_ref[...] = h2

    logits = _dot(h2, wr_ref[...]) + br_ref[...]
    lane = lax.broadcasted_iota(jnp.int32, logits.shape, 1)
    neg = -jnp.inf
    gl = jnp.where(lane < MOE_GROUPS, logits, neg)
    gmax = jnp.max(gl, axis=-1, keepdims=True)
    gidx = jnp.min(jnp.where(gl == gmax, lane, ROUTER_LANES), axis=-1, keepdims=True)
    grp_w = 1.0 / jnp.sum(jnp.exp(gl - gmax), axis=-1, keepdims=True)
    e0 = EXPERT_LANE0 + gidx * MOE_PER_GROUP
    sel = jnp.where((lane >= e0) & (lane < e0 + MOE_PER_GROUP), logits, neg)
    m1 = jnp.max(sel, axis=-1, keepdims=True)
    i1 = jnp.min(jnp.where(sel == m1, lane, ROUTER_LANES), axis=-1, keepdims=True)
    sel2 = jnp.where(lane == i1, neg, sel)
    m2 = jnp.max(sel2, axis=-1, keepdims=True)
    i2 = jnp.min(jnp.where(sel2 == m2, lane, ROUTER_LANES), axis=-1, keepdims=True)
    e2 = jnp.exp(m2 - m1)
    w1 = 1.0 / (1.0 + e2)
    w2 = e2 / (1.0 + e2)
    comb_ref[...] = grp_w * (jnp.where(lane == i1, w1, 0.0) + jnp.where(lane == i2, w2, 0.0))


def _outproj(x, oh, ys, ada3, wglu_bf, bglu, snorm, wo_h, wo_s, nffn, wr, br, bb, tt):
    b, t, d = x.shape
    n = b * t
    dh = oh.shape[-1]
    nt = t // tt
    rows = bb * tt
    x_spec = pl.BlockSpec((bb, tt, d), lambda i, j: (i, j, 0))
    half_spec = pl.BlockSpec((rows, dh), lambda i, j: (i * nt + j, 0))

    def ada_spec(col):
        return pl.BlockSpec((bb, 1, d), lambda i, j: (i, 0, col))

    def full(a):
        return pl.BlockSpec(a.shape, lambda i, j: (0,) * a.ndim)

    return pl.pallas_call(
        _outproj_kernel,
        grid=(b // bb, nt),
        in_specs=[x_spec, half_spec, half_spec, ada_spec(2), ada_spec(3), ada_spec(4),
                  full(wglu_bf), full(bglu), full(snorm), full(wo_h), full(wo_s), full(nffn),
                  full(wr), full(br)],
        out_specs=[x_spec,
                   pl.BlockSpec((rows, d), lambda i, j: (i * nt + j, 0)),
                   pl.BlockSpec((rows, ROUTER_LANES), lambda i, j: (i * nt + j, 0))],
        out_shape=[jax.ShapeDtypeStruct((b, t, d), F32),
                   jax.ShapeDtypeStruct((n, d), BF16),
                   jax.ShapeDtypeStruct((n, ROUTER_LANES), F32)],
        compiler_params=_cparams(("parallel", "parallel")),
        name="outproj",
    )(x, oh, ys, ada3, ada3, ada3, wglu_bf, bglu, snorm, wo_h, wo_s, nffn, wr, br)


def _moe_kernel(h_ref, comb_ref, x1_ref, gate_ref, wg_ref, wu_ref, wd_ref, shift_ref, scale_ref, nfin_ref,
                y_ref, acc_ref, *, n_exp):
    e = pl.program_id(2)
    bb, tt, d = x1_ref.shape

    @pl.when(e == 0)
    def _():
        acc_ref[...] = jnp.zeros_like(acc_ref)

    h = h_ref[...]
    a = _dot(h, wg_ref[0])
    b = _dot(h, wu_ref[0])
    lane = lax.broadcasted_iota(jnp.int32, comb_ref.shape, 1)
    cw = jnp.sum(jnp.where(lane == EXPERT_LANE0 + e, comb_ref[...], 0.0), axis=-1, keepdims=True)
    act = (_silu(a) * b * cw).astype(BF16)
    acc_ref[...] += _dot(act, wd_ref[0])

    @pl.when(e == n_exp - 1)
    def _():
        x2 = x1_ref[...] + gate_ref[...] * acc_ref[...].reshape(bb, tt, d)
        hf = _rms(x2) * nfin_ref[...]
        y_ref[...] = hf * (1.0 + scale_ref[...]) + shift_ref[...]


def _moe(h2, comb, x1, ada3, adaf3, wg, wu, wd, nfin, bb, tt):
    b, t, d = x1.shape
    nt = t // tt
    rows = bb * tt
    n_exp, _, de = wg.shape
    x_spec = pl.BlockSpec((bb, tt, d), lambda i, j, e: (i, j, 0))

    def ada_spec(col):
        return pl.BlockSpec((bb, 1, d), lambda i, j, e: (i, 0, col))

    return pl.pallas_call(
        functools.partial(_moe_kernel, n_exp=n_exp),
        grid=(b // bb, nt, n_exp),
        in_specs=[pl.BlockSpec((rows, d), lambda i, j, e: (i * nt + j, 0)),
                  pl.BlockSpec((rows, ROUTER_LANES), lambda i, j, e: (i * nt + j, 0)),
                  x_spec, ada_spec(5),
                  pl.BlockSpec((1, d, de), lambda i, j, e: (e, 0, 0)),
                  pl.BlockSpec((1, d, de), lambda i, j, e: (e, 0, 0)),
                  pl.BlockSpec((1, de, d), lambda i, j, e: (e, 0, 0)),
                  ada_spec(0), ada_spec(1),
                  pl.BlockSpec((1, d), lambda i, j, e: (0, 0))],
        out_specs=x_spec,
        out_shape=jax.ShapeDtypeStruct((b, t, d), F32),
        scratch_shapes=[pltpu.VMEM((rows, d), F32)],
        compiler_params=_cparams(("parallel", "parallel", "arbitrary")),
        name="moe",
    )(h2, comb, x1, ada3, wg, wu, wd, adaf3, adaf3, nfin)


def _trunk(x, ada, adaf, s_h, s_re, s_im, p, *, bb, tt, hgrn_tile, hgrn_chunk, moe_bb, moe_tt, sequential):
    b, t, d = x.shape
    n = b * t
    ada3 = ada.reshape(b, 1, -1)
    adaf3 = adaf.reshape(b, 1, -1)
    q, k, g, v, gs, u = _inproj(x, ada3, p['norm_mix'], p['w_in'], p['lb'], bb, tt)
    oh, s_h_new = _hgrn(q, k, g, v, gs, s_h, p['hgrn_norm'], b, t, hgrn_tile, hgrn_chunk)
    u8 = _to_chunk_rows(u, n)
    if sequential:
        xr, xi = s_re.reshape(b, 1, -1), s_im.reshape(b, 1, -1)
        n_rows = t // SSM_CHUNK
    else:
        xr, xi = s_re.reshape(1, b, -1), s_im.reshape(1, b, -1)
        n_rows = b
    y8, fr, fi = _s5(u8, xr, xi, p['ssm_w1'], p['ssm_m'], p['ssm_a8'], n_rows, sequential)
    ys = _from_chunk_rows(y8, n)
    x1, h2, comb = _outproj(x, oh, ys, ada3, p['w_glu'], p['b_glu'], p['ssm_norm'], p['wo_h'], p['wo_s'],
                            p['norm_ffn'], p['w_rt'], p['b_rt'], bb, tt)
    y = _moe(h2, comb, x1, ada3, adaf3, p['wg'], p['wu'], p['wd'], p['norm_final'], moe_bb, moe_tt)
    return y, s_h_new[None], fr.reshape(1, b, s_re.shape[-2], s_re.shape[-1]), \
        fi.reshape(1, b, s_re.shape[-2], s_re.shape[-1])


def kernel(x_prompt, x_sample, c_prompt, c_sample, state_hgrn, state_ssm_re, state_ssm_im, hgrn_lb_logits, w_ada, b_ada, norm_mix, w_in, hgrn_norm, ssm_a_re, ssm_a_im, ssm_log_dt, ssm_b_re, ssm_b_im, ssm_c_re, ssm_c_im, ssm_d, ssm_w_glu, ssm_b_glu, ssm_norm, w_out, norm_ffn, moe_w_group, moe_b_group, moe_w_router, moe_b_router, moe_w_gate, moe_w_up, moe_w_down, w_ada_final, b_ada_final, norm_final):
    depth = w_ada.shape[0]
    assert depth == 1
    d = x_prompt.shape[-1]
    bp = x_prompt.shape[0]
    dh = hgrn_norm.shape[-1]
    dk = dh // HGRN_HEADS
    de = moe_w_gate.shape[-1]
    n_exp = MOE_GROUPS * MOE_PER_GROUP

    lb = jax.nn.softmax(hgrn_lb_logits.astype(F32), axis=0)[0].reshape(1, dh)
    w1, m, a8 = _s5_prepare(ssm_a_re[0], ssm_a_im[0], ssm_log_dt[0], ssm_b_re[0], ssm_b_im[0],
                            ssm_c_re[0], ssm_c_im[0], ssm_d[0])
    w_rt = jnp.zeros((d, ROUTER_LANES), F32)
    w_rt = w_rt.at[:, :MOE_GROUPS].set(moe_w_group[0])
    w_rt = w_rt.at[:, EXPERT_LANE0:EXPERT_LANE0 + n_exp].set(
        moe_w_router[0].transpose(1, 0, 2).reshape(d, n_exp))
    b_rt = jnp.zeros((1, ROUTER_LANES), F32)
    b_rt = b_rt.at[0, :MOE_GROUPS].set(moe_b_group[0])
    b_rt = b_rt.at[0, EXPERT_LANE0:EXPERT_LANE0 + n_exp].set(moe_b_router[0].reshape(n_exp))
    p = dict(
        lb=lb, norm_mix=norm_mix[0].reshape(1, d), w_in=w_in[0].astype(BF16),
        hgrn_norm=hgrn_norm[0].reshape(1, dh),
        ssm_w1=w1, ssm_m=m, ssm_a8=a8,
        w_glu=ssm_w_glu[0].astype(BF16), b_glu=ssm_b_glu[0].reshape(1, -1), ssm_norm=ssm_norm[0].reshape(1, -1),
        wo_h=w_out[0, :dh].astype(BF16), wo_s=w_out[0, dh:].astype(BF16),
        norm_ffn=norm_ffn[0].reshape(1, d), w_rt=w_rt.astype(BF16), b_rt=b_rt,
        wg=moe_w_gate[0].reshape(n_exp, d, de).astype(BF16),
        wu=moe_w_up[0].reshape(n_exp, d, de).astype(BF16),
        wd=moe_w_down[0].reshape(n_exp, de, d).astype(BF16),
        norm_final=norm_final.reshape(1, d),
    )

    c_all = jnp.concatenate([c_prompt, c_sample], axis=0)
    ada = _silu_linear(c_all, w_ada[0], b_ada[0])
    adaf = _silu_linear(c_all, w_ada_final, b_ada_final)

    zeros_h = jnp.zeros((bp, HGRN_HEADS, dk, dk), F32)
    zeros_s = jnp.zeros((bp,) + state_ssm_re.shape[2:], F32)
    y_p, h_p, re_p, im_p = _trunk(x_prompt, ada[:bp], adaf[:bp], zeros_h, zeros_s, zeros_s, p,
                                  bb=1, tt=256, hgrn_tile=256, hgrn_chunk=64, moe_bb=1, moe_tt=1024, sequential=True)
    y_s, h_s, re_s, im_s = _trunk(x_sample, ada[bp:], adaf[bp:], state_hgrn[0], state_ssm_re[0],
                                  state_ssm_im[0], p,
                                  bb=32, tt=8, hgrn_tile=8, hgrn_chunk=8, moe_bb=128, moe_tt=8, sequential=False)
    return (y_p, y_s, h_p, re_p, im_p, h_s, re_s, im_s)
```

```python
import functools
import math

import jax
import jax.numpy as jnp
from jax import lax
from jax.experimental import pallas as pl
from jax.experimental.pallas import tpu as pltpu

F32 = jnp.float32
BF16 = jnp.bfloat16
HIGHEST = lax.Precision.HIGHEST

EPS = 1e-6
MAX_REAL = -1e-4
HGRN_HEADS = 4
SSM_GROUP = 16
SSM_STATE = 64
SSM_CHUNK = 8
SSM_SETS = 4
MOE_GROUPS = 4
MOE_PER_GROUP = 8
N_EXPERTS = MOE_GROUPS * MOE_PER_GROUP
ROUTER_LANES = 128
EXPERT_LANE0 = 32
ROUTE_ROWS = 8
MOE_TILE = 256
VMEM_LIMIT = 56 * 1024 * 1024


def _cparams(sem):
    return pltpu.CompilerParams(dimension_semantics=sem, vmem_limit_bytes=VMEM_LIMIT)


def _silu(x):
    return x * jax.nn.sigmoid(x)


def _rms(x):
    return x * lax.rsqrt(jnp.mean(x * x, axis=-1, keepdims=True) + EPS)


def _dot(a, b):
    return jnp.dot(a, b, preferred_element_type=F32)


def _dot_nt(a, b):
    return lax.dot_general(a, b, (((1,), (1,)), ((), ())), preferred_element_type=F32)


def _dot_tn(a, b, precision=None):
    return lax.dot_general(a, b, (((0,), (0,)), ((), ())), preferred_element_type=F32,
                           precision=precision)


def _silu_linear_kernel(c_ref, w_ref, b_ref, o_ref):
    a = _silu(c_ref[...]).astype(BF16)
    o_ref[...] = _dot(a, w_ref[...].astype(BF16)) + b_ref[...]


def _silu_linear(c, w, b):
    m, d = c.shape
    n = w.shape[1]
    tn = 1024
    return pl.pallas_call(
        _silu_linear_kernel,
        grid=(n // tn,),
        in_specs=[pl.BlockSpec((m, d), lambda j: (0, 0)),
                  pl.BlockSpec((d, tn), lambda j: (0, j)),
                  pl.BlockSpec((1, tn), lambda j: (0, j))],
        out_specs=pl.BlockSpec((m, tn), lambda j: (0, j)),
        out_shape=jax.ShapeDtypeStruct((m, n), F32),
        compiler_params=_cparams(("parallel",)),
        name="silu_linear",
    )(c, w, b.reshape(1, n))


def _inproj_kernel(x_ref, shift_ref, scale_ref, gain_ref, w_ref, lb_ref,
                   q_ref, k_ref, g_ref, v_ref, gs_ref, u_ref, *, dh):
    bb, tt, d = x_ref.shape
    h = _rms(x_ref[...]) * gain_ref[...]
    h = h * (1.0 + scale_ref[...]) + shift_ref[...]
    proj = _dot(h.reshape(bb * tt, d).astype(BF16), w_ref[...])
    lb = lb_ref[...]
    f = lb + (1.0 - lb) * jax.nn.sigmoid(proj[:, dh:2 * dh])
    q_ref[...] = _silu(proj[:, :dh]) * (float(dh // HGRN_HEADS) ** -0.5)
    k_ref[...] = 1.0 - f
    g_ref[...] = jnp.log(f)
    v_ref[...] = proj[:, 2 * dh:3 * dh]
    gs_ref[...] = _silu(proj[:, 3 * dh:4 * dh])
    u_ref[...] = proj[:, 4 * dh:]


def _inproj(x, ada3, gain, w_in_bf, lb, bb, tt):
    b, t, d = x.shape
    dh = lb.shape[-1]
    nt = t // tt
    rows = bb * tt
    n = b * t
    row_spec = pl.BlockSpec((rows, dh), lambda i, j: (i * nt + j, 0))
    out = jax.ShapeDtypeStruct((n, dh), F32)
    return pl.pallas_call(
        functools.partial(_inproj_kernel, dh=dh),
        grid=(b // bb, nt),
        in_specs=[pl.BlockSpec((bb, tt, d), lambda i, j: (i, j, 0)),
                  pl.BlockSpec((bb, 1, d), lambda i, j: (i, 0, 0)),
                  pl.BlockSpec((bb, 1, d), lambda i, j: (i, 0, 1)),
                  pl.BlockSpec((1, d), lambda i, j: (0, 0)),
                  pl.BlockSpec(w_in_bf.shape, lambda i, j: (0, 0)),
                  pl.BlockSpec((1, dh), lambda i, j: (0, 0))],
        out_specs=[row_spec] * 6,
        out_shape=[out] * 6,
        compiler_params=_cparams(("parallel", "parallel")),
        name="inproj",
    )(x, ada3, ada3, gain, w_in_bf, lb)


def _hgrn_kernel(q_ref, k_ref, g_ref, v_ref, gs_ref, s0_ref, hn_ref, o_ref, sf_ref, st_ref,
                 *, chunk, nt):
    j = pl.program_id(1)
    ct, dh = q_ref.shape
    dk = dh // HGRN_HEADS
    c = chunk

    @pl.when(j == 0)
    def _():
        st_ref[...] = s0_ref[0]

    r = lax.broadcasted_iota(jnp.int32, (c, c), 0)
    s = lax.broadcasted_iota(jnp.int32, (c, c), 1)
    causal = r >= s
    tri = causal.astype(F32)
    ones = jnp.ones((c, dk), F32)
    mid = c // 2 - 1

    for ci in range(ct // c):
        rows = slice(ci * c, (ci + 1) * c)
        for h in range(HGRN_HEADS):
            lanes = slice(h * dk, (h + 1) * dk)
            g = g_ref[rows, lanes]
            q = q_ref[rows, lanes]
            k = k_ref[rows, lanes]
            v = v_ref[rows, lanes].astype(BF16)
            a = jnp.dot(tri, g, precision=HIGHEST, preferred_element_type=F32)
            a_mid = a[mid:mid + 1]
            a_end = a[c - 1:c]
            qt = (q * jnp.exp(a - a_mid)).astype(BF16)
            kt = (k * jnp.exp(a_mid - a)).astype(BF16)
            sc = jnp.where(causal, _dot_nt(qt, kt), 0.0).astype(BF16)
            state = st_ref[h]
            o = _dot(sc, v) + _dot((q * jnp.exp(a)).astype(BF16), state.astype(BF16))
            decay = jnp.exp(_dot_tn(g, ones, precision=HIGHEST))
            st_ref[h] = decay * state + _dot_tn((k * jnp.exp(a_end - a)).astype(BF16), v)
            o_ref[rows, lanes] = _rms(o) * hn_ref[:, lanes] * gs_ref[rows, lanes]

    @pl.when(j == nt - 1)
    def _():
        sf_ref[0] = st_ref[...]


def _hgrn(q, k, g, v, gs, s0, hnorm, b, t, ct, chunk):
    n, dh = q.shape
    nt = t // ct
    dk = dh // HGRN_HEADS
    row_spec = pl.BlockSpec((ct, dh), lambda i, j: (i * nt + j, 0))
    st_spec = pl.BlockSpec((1, HGRN_HEADS, dk, dk), lambda i, j: (i, 0, 0, 0))
    return pl.pallas_call(
        functools.partial(_hgrn_kernel, chunk=chunk, nt=nt),
        grid=(b, nt),
        in_specs=[row_spec] * 5 + [st_spec, pl.BlockSpec((1, dh), lambda i, j: (0, 0))],
        out_specs=[row_spec, st_spec],
        out_shape=[jax.ShapeDtypeStruct((n, dh), F32),
                   jax.ShapeDtypeStruct((b, HGRN_HEADS, dk, dk), F32)],
        scratch_shapes=[pltpu.VMEM((HGRN_HEADS, dk, dk), F32)],
        compiler_params=_cparams(("parallel", "arbitrary")),
        name="hgrn",
    )(q, k, g, v, gs, s0, hnorm)


def _s5_prepare(a_re, a_im, log_dt, b_re, b_im, c_re, c_im, d_skip):
    ng, npp = a_re.shape
    nh = b_re.shape[-1]
    L = SSM_CHUNK
    gs = ng // SSM_SETS
    lam_re = jnp.minimum(a_re, MAX_REAL)
    lam_im = a_im
    dt = jnp.exp(log_dt)
    mag = jnp.exp(lam_re * dt)
    ab_re = mag * jnp.cos(lam_im * dt)
    ab_im = mag * jnp.sin(lam_im * dt)
    den = lam_re * lam_re + lam_im * lam_im
    co_re = ((ab_re - 1.0) * lam_re + ab_im * lam_im) / den
    co_im = (ab_im * lam_re - (ab_re - 1.0) * lam_im) / den
    bb_re = co_re[..., None] * b_re - co_im[..., None] * b_im
    bb_im = co_re[..., None] * b_im + co_im[..., None] * b_re
    pw_re = [jnp.ones_like(ab_re)]
    pw_im = [jnp.zeros_like(ab_im)]
    for _ in range(L):
        pr, pi = pw_re[-1], pw_im[-1]
        pw_re.append(pr * ab_re - pi * ab_im)
        pw_im.append(pr * ab_im + pi * ab_re)
    pw_re = jnp.stack(pw_re)
    pw_im = jnp.stack(pw_im)
    ab_b_re = pw_re[:L, :, :, None] * bb_re - pw_im[:L, :, :, None] * bb_im
    ab_b_im = pw_re[:L, :, :, None] * bb_im + pw_im[:L, :, :, None] * bb_re
    kern = (jnp.einsum('gkp,lgph->lghk', c_re, ab_b_re, precision=HIGHEST)
            - jnp.einsum('gkp,lgph->lghk', c_im, ab_b_im, precision=HIGHEST))
    kern = kern.at[0].add(d_skip[:, :, None] * jnp.eye(nh, dtype=F32))
    lag = jnp.arange(L)[None, :] - jnp.arange(L)[:, None]
    toep = jnp.where((lag >= 0)[:, :, None, None, None], kern[jnp.clip(lag, 0, L - 1)], 0.0)
    eye = jnp.eye(gs, dtype=F32)
    toep = toep.reshape(L, L, SSM_SETS, gs, nh, nh)
    w_t = jnp.einsum('stGghk,gq->Gsghtqk', toep, eye).reshape(SSM_SETS, L * gs * nh, L * gs * nh)
    inc_re = ab_b_re[::-1].reshape(L, SSM_SETS, gs, npp, nh)
    inc_im = ab_b_im[::-1].reshape(L, SSM_SETS, gs, npp, nh)
    n_re = jnp.einsum('sGgph,gq->Gsghqp', inc_re, eye).reshape(SSM_SETS, L * gs * nh, gs * npp)
    n_im = jnp.einsum('sGgph,gq->Gsghqp', inc_im, eye).reshape(SSM_SETS, L * gs * nh, gs * npp)
    w1 = jnp.concatenate([w_t, n_re, n_im], axis=-1).astype(BF16)
    ca_re = c_re[None] * pw_re[1:, :, None, :] - c_im[None] * pw_im[1:, :, None, :]
    ca_im = c_re[None] * pw_im[1:, :, None, :] + c_im[None] * pw_re[1:, :, None, :]
    ca_re = ca_re.reshape(L, SSM_SETS, gs, nh, npp)
    ca_im = ca_im.reshape(L, SSM_SETS, gs, nh, npp)
    m_re = jnp.einsum('tGgkp,gq->Ggptqk', ca_re, eye).reshape(SSM_SETS, gs * npp, L * gs * nh)
    m_im = jnp.einsum('tGgkp,gq->Ggptqk', -ca_im, eye).reshape(SSM_SETS, gs * npp, L * gs * nh)
    m = jnp.concatenate([m_re, m_im], axis=1).astype(BF16)
    a8 = jnp.concatenate([pw_re[L].reshape(SSM_SETS, 1, gs * npp),
                          pw_im[L].reshape(SSM_SETS, 1, gs * npp)], axis=-1)
    return w1, m, a8


def _s5_kernel(u_ref, xr_ref, xi_ref, w1_ref, m_ref, a8_ref, y_ref, fr_ref, fi_ref, *, sequential):
    n = u_ref.shape[1]
    ns = xr_ref.shape[-1]
    ny = y_ref.shape[-1]
    res = _dot(u_ref[0].astype(BF16), w1_ref[0])
    d_re = res[:, ny:ny + ns]
    d_im = res[:, ny + ns:]
    a_re = a8_ref[0][:, :ns]
    a_im = a8_ref[0][:, ns:]
    x0_re = xr_ref[0]
    x0_im = xi_ref[0]
    if sequential:
        row = lax.broadcasted_iota(jnp.int32, (n, ns), 0)
        first = row == 0
        x_re = d_re + jnp.where(first, a_re * x0_re - a_im * x0_im, 0.0)
        x_im = d_im + jnp.where(first, a_re * x0_im + a_im * x0_re, 0.0)
        p_re, p_im = a_re, a_im
        step = 1
        while step < n:
            s_re = jnp.where(row >= step, pltpu.roll(x_re, step, 0), 0.0)
            s_im = jnp.where(row >= step, pltpu.roll(x_im, step, 0), 0.0)
            x_re, x_im = x_re + p_re * s_re - p_im * s_im, x_im + p_re * s_im + p_im * s_re
            p_re, p_im = p_re * p_re - p_im * p_im, 2.0 * p_re * p_im
            step *= 2
        fr_ref[0] = x_re[n - 1:n]
        fi_ref[0] = x_im[n - 1:n]
        xc_re = jnp.where(first, x0_re, pltpu.roll(x_re, 1, 0))
        xc_im = jnp.where(first, x0_im, pltpu.roll(x_im, 1, 0))
    else:
        xc_re, xc_im = x0_re, x0_im
        fr_ref[0] = a_re * x0_re - a_im * x0_im + d_re
        fi_ref[0] = a_re * x0_im + a_im * x0_re + d_im
    xc = jnp.concatenate([xc_re, xc_im], axis=-1).astype(BF16)
    y_ref[0] = res[:, :ny] + _dot(xc, m_ref[0])


def _s5(u8, x_re, x_im, w1, m, a8, n_rows, sequential):
    sets, r, nu = u8.shape
    nb, rb, _ = x_re.shape
    ns = m.shape[1] // 2
    st_spec = pl.BlockSpec((1, rb, ns), lambda gi, i: (i, 0, gi))
    st_shape = jax.ShapeDtypeStruct(x_re.shape, F32)
    return pl.pallas_call(
        functools.partial(_s5_kernel, sequential=sequential),
        grid=(sets, nb),
        in_specs=[pl.BlockSpec((1, n_rows, nu), lambda gi, i: (gi, i, 0)),
                  st_spec, st_spec,
                  pl.BlockSpec((1,) + w1.shape[1:], lambda gi, i: (gi, 0, 0)),
                  pl.BlockSpec((1,) + m.shape[1:], lambda gi, i: (gi, 0, 0)),
                  pl.BlockSpec((1, 1, 2 * ns), lambda gi, i: (gi, 0, 0))],
        out_specs=[pl.BlockSpec((1, n_rows, nu), lambda gi, i: (gi, i, 0)), st_spec, st_spec],
        out_shape=[jax.ShapeDtypeStruct(u8.shape, F32), st_shape, st_shape],
        compiler_params=_cparams(("parallel", "parallel")),
        name="s5",
    )(u8, x_re, x_im, w1, m, a8)


def _to_chunk_rows(u, n):
    return u.reshape(n // SSM_CHUNK, SSM_CHUNK, SSM_SETS, -1).transpose(2, 0, 1, 3).reshape(
        SSM_SETS, n // SSM_CHUNK, -1)


def _from_chunk_rows(y8, n):
    return y8.reshape(SSM_SETS, n // SSM_CHUNK, SSM_CHUNK, -1).transpose(1, 2, 0, 3).reshape(n, -1)


def _gelu_tanh(x):
    return 0.5 * x * (1.0 + jnp.tanh(math.sqrt(2.0 / math.pi) * (x + 0.044715 * (x * x * x))))


def _outproj_kernel(x_ref, oh_ref, ys_ref, gate_ref, shift_ref, scale_ref, wglu_ref, bglu_ref, sn_ref,
                    wo_h_ref, wo_s_ref, nf_ref, wr_ref, br_ref, cnt_in_ref,
                    x1_ref, h2_ref, route_ref, rw_ref, cnt_ref):
    bb, tt, d = x_ref.shape
    rows = bb * tt
    y = _gelu_tanh(ys_ref[...])
    y = y * jax.nn.sigmoid(_dot(y.astype(BF16), wglu_ref[...]) + bglu_ref[...])
    o_s = _rms(y) * sn_ref[...]
    mix = _dot(oh_ref[...].astype(BF16), wo_h_ref[...]) + _dot(o_s.astype(BF16), wo_s_ref[...])
    x1 = x_ref[...] + gate_ref[...] * mix.reshape(bb, tt, d)
    x1_ref[...] = x1
    h2 = _rms(x1) * nf_ref[...]
    h2 = (h2 * (1.0 + scale_ref[...]) + shift_ref[...]).reshape(rows, d)
    h2_ref[...] = h2

    logits = _dot(h2.astype(BF16), wr_ref[...]) + br_ref[...]
    lane = lax.broadcasted_iota(jnp.int32, logits.shape, 1)
    neg = -jnp.inf
    gl = jnp.where(lane < MOE_GROUPS, logits, neg)
    gmax = jnp.max(gl, axis=-1, keepdims=True)
    gidx = jnp.min(jnp.where(gl == gmax, lane, ROUTER_LANES), axis=-1, keepdims=True)
    grp_w = 1.0 / jnp.sum(jnp.exp(gl - gmax), axis=-1, keepdims=True)
    e0 = EXPERT_LANE0 + gidx * MOE_PER_GROUP
    sel = jnp.where((lane >= e0) & (lane < e0 + MOE_PER_GROUP), logits, neg)
    m1 = jnp.max(sel, axis=-1, keepdims=True)
    i1 = jnp.min(jnp.where(sel == m1, lane, ROUTER_LANES), axis=-1, keepdims=True)
    sel2 = jnp.where(lane == i1, neg, sel)
    m2 = jnp.max(sel2, axis=-1, keepdims=True)
    i2 = jnp.min(jnp.where(sel2 == m2, lane, ROUTER_LANES), axis=-1, keepdims=True)
    e2 = jnp.exp(m2 - m1)
    w1 = 1.0 / (1.0 + e2)
    w2 = e2 / (1.0 + e2)
    rw_ref[...] = grp_w * (jnp.where(lane == 0, w1, 0.0) + jnp.where(lane == 1, w2, 0.0))

    @pl.when((pl.program_id(0) == 0) & (pl.program_id(1) == 0))
    def _():
        cnt_ref[...] = cnt_in_ref[...]

    picked = (lane == i1) | (lane == i2)
    earlier = (lax.broadcasted_iota(jnp.int32, (rows, rows), 0)
               > lax.broadcasted_iota(jnp.int32, (rows, rows), 1))
    base = cnt_ref[...]
    before = _dot(earlier.astype(BF16), picked.astype(BF16)) + base
    rank1 = jnp.sum(jnp.where(lane == i1, before, 0.0), axis=-1, keepdims=True).astype(jnp.int32)
    rank2 = jnp.sum(jnp.where(lane == i2, before, 0.0), axis=-1, keepdims=True).astype(jnp.int32)
    cnt_ref[...] = base + jnp.sum(picked.astype(F32), axis=0, keepdims=True)
    info = jnp.where(lane == 0, i1 - EXPERT_LANE0,
                     jnp.where(lane == 1, i2 - EXPERT_LANE0,
                               jnp.where(lane == 2, rank1, jnp.where(lane == 3, rank2, 0))))
    route_ref[...] = info.T[:ROUTE_ROWS]


def _outproj(x, oh, ys, ada3, wglu_bf, bglu, snorm, wo_h, wo_s, nffn, wr, br, cnt_in, bb, tt):
    b, t, d = x.shape
    n = b * t
    dh = oh.shape[-1]
    nt = t // tt
    rows = bb * tt
    x_spec = pl.BlockSpec((bb, tt, d), lambda i, j: (i, j, 0))
    half_spec = pl.BlockSpec((rows, dh), lambda i, j: (i * nt + j, 0))

    def ada_spec(col):
        return pl.BlockSpec((bb, 1, d), lambda i, j: (i, 0, col))

    def full(a):
        return pl.BlockSpec(a.shape, lambda i, j: (0,) * a.ndim)

    return pl.pallas_call(
        _outproj_kernel,
        grid=(b // bb, nt),
        in_specs=[x_spec, half_spec, half_spec, ada_spec(2), ada_spec(3), ada_spec(4),
                  full(wglu_bf), full(bglu), full(snorm), full(wo_h), full(wo_s), full(nffn),
                  full(wr), full(br), full(cnt_in)],
        out_specs=[x_spec,
                   pl.BlockSpec((rows, d), lambda i, j: (i * nt + j, 0)),
                   pl.BlockSpec((ROUTE_ROWS, rows), lambda i, j: (0, i * nt + j)),
                   pl.BlockSpec((rows, ROUTER_LANES), lambda i, j: (i * nt + j, 0)),
                   pl.BlockSpec((1, ROUTER_LANES), lambda i, j: (0, 0))],
        out_shape=[jax.ShapeDtypeStruct((b, t, d), F32),
                   jax.ShapeDtypeStruct((n, d), F32),
                   jax.ShapeDtypeStruct((ROUTE_ROWS, n), jnp.int32),
                   jax.ShapeDtypeStruct((n, ROUTER_LANES), F32),
                   jax.ShapeDtypeStruct((1, ROUTER_LANES), F32)],
        compiler_params=_cparams(("arbitrary", "arbitrary")),
        name="outproj",
    )(x, oh, ys, ada3, ada3, ada3, wglu_bf, bglu, snorm, wo_h, wo_s, nffn, wr, br, cnt_in)


def _moe_schedule(cnt, n_slots):
    c = cnt[0, EXPERT_LANE0:EXPERT_LANE0 + N_EXPERTS].astype(jnp.int32)
    seg_end = jnp.cumsum(c)
    seg_start = seg_end - c
    first_tile = seg_start // MOE_TILE
    tiles = jnp.where(c > 0, (seg_end - 1) // MOE_TILE - first_tile + 1, 0)
    cum = jnp.cumsum(tiles)
    n_items = cum[-1]
    item = jnp.minimum(jnp.arange(n_slots, dtype=jnp.int32), n_items - 1)
    item_exp = jnp.sum(item[:, None] >= cum[None, :], axis=1).astype(jnp.int32)
    item_tile = first_tile[item_exp] + item - (cum - tiles)[item_exp]
    return (seg_start.astype(jnp.int32), seg_end.astype(jnp.int32), item_tile.astype(jnp.int32), item_exp,
            n_items.reshape(1).astype(jnp.int32))


def _pair_rows(seg_ref, route_ref, t):
    return (seg_ref[route_ref[0, t]] + route_ref[2, t], seg_ref[route_ref[1, t]] + route_ref[3, t])


def _scatter_kernel(seg_ref, route_ref, hp_ref, hs_ref, xs_ref, sem, *, prompt_tiles):
    ts = hp_ref.shape[0]

    def scatter_from(h_ref):
        def row_copies(t):
            src = h_ref.at[pl.ds(t, 1)]
            return [pltpu.make_async_copy(src, xs_ref.at[pl.ds(p, 1)], sem)
                    for p in _pair_rows(seg_ref, route_ref, t)]

        def start(t, carry):
            for cp in row_copies(t):
                cp.start()
            return carry

        def wait(t, carry):
            for cp in row_copies(t):
                cp.wait()
            return carry

        lax.fori_loop(0, ts, start, 0)
        lax.fori_loop(0, ts, wait, 0)

    @pl.when(pl.program_id(0) < prompt_tiles)
    def _():
        scatter_from(hp_ref)

    @pl.when(pl.program_id(0) >= prompt_tiles)
    def _():
        scatter_from(hs_ref)


def _scatter(seg, route, h_p, h_s, ts):
    n_p, d = h_p.shape
    n_s = h_s.shape[0]
    tiles_p = n_p // ts
    return pl.pallas_call(
        functools.partial(_scatter_kernel, prompt_tiles=tiles_p),
        grid_spec=pltpu.PrefetchScalarGridSpec(
            num_scalar_prefetch=1, grid=((n_p + n_s) // ts,),
            in_specs=[pl.BlockSpec((ROUTE_ROWS, ts), lambda i, *_: (0, i), memory_space=pltpu.SMEM),
                      pl.BlockSpec((ts, d), lambda i, *_: (jnp.minimum(i, tiles_p - 1), 0)),
                      pl.BlockSpec((ts, d), lambda i, *_: (jnp.maximum(i - tiles_p, 0), 0))],
            out_specs=pl.BlockSpec(memory_space=pltpu.HBM),
            scratch_shapes=[pltpu.SemaphoreType.DMA]),
        out_shape=jax.ShapeDtypeStruct((2 * (n_p + n_s), d), F32),
        compiler_params=_cparams(("arbitrary",)),
        name="moe_scatter",
    )(seg, route, h_p, h_s)


def _experts_kernel(lo_ref, hi_ref, tile_ref, exp_ref, items_ref, xs_ref, wg_ref, wu_ref, wd_ref, ys_ref,
                    wg_s, wu_s, wd_s):
    i = pl.program_id(0)
    prev = jnp.maximum(i - 1, 0)
    e = exp_ref[i]

    @pl.when((i == 0) | (e != exp_ref[prev]))
    def _():
        wg_s[...] = wg_ref[0].astype(BF16)
        wu_s[...] = wu_ref[0].astype(BF16)
        wd_s[...] = wd_ref[0].astype(BF16)

    @pl.when(i < items_ref[0])
    def _():
        x = xs_ref[...].astype(BF16)
        act = _silu(_dot(x, wg_s[...])) * _dot(x, wu_s[...])
        out = _dot(act.astype(BF16), wd_s[...])
        row = tile_ref[i] * MOE_TILE + lax.broadcasted_iota(jnp.int32, (MOE_TILE, 1), 0)
        mine = (row >= lo_ref[e]) & (row < hi_ref[e])
        first_visit = (i == 0) | (tile_ref[i] != tile_ref[prev])

        @pl.when(first_visit)
        def _():
            ys_ref[...] = jnp.where(mine, out, 0.0)

        @pl.when(jnp.logical_not(first_visit))
        def _():
            ys_ref[...] = jnp.where(mine, out, ys_ref[...])


def _experts(seg_lo, seg_hi, item_tile, item_exp, n_items, xs, wg, wu, wd):
    n_rows, d = xs.shape
    _, _, de = wg.shape
    row_spec = pl.BlockSpec((MOE_TILE, d), lambda i, lo, hi, tile, ex, items: (tile[i], 0))

    def w_spec(shape):
        return pl.BlockSpec((1,) + shape, lambda i, lo, hi, tile, ex, items: (ex[i], 0, 0))

    return pl.pallas_call(
        _experts_kernel,
        grid_spec=pltpu.PrefetchScalarGridSpec(
            num_scalar_prefetch=5, grid=(item_tile.shape[0],),
            in_specs=[row_spec, w_spec((d, de)), w_spec((d, de)), w_spec((de, d))],
            out_specs=row_spec,
            scratch_shapes=[pltpu.VMEM((d, de), BF16), pltpu.VMEM((d, de), BF16), pltpu.VMEM((de, d), BF16)]),
        out_shape=jax.ShapeDtypeStruct((n_rows, d), F32),
        compiler_params=_cparams(("arbitrary",)),
        name="moe_experts",
    )(seg_lo, seg_hi, item_tile, item_exp, n_items, xs, wg, wu, wd)


def _combine_kernel(seg_ref, route_ref, rw_ref, x1_ref, gate_ref, shift_ref, scale_ref, nfin_ref, ys_ref,
                    y_ref, rows_ref, sem):
    bb, tt, d = x1_ref.shape
    rows = bb * tt

    def row_copies(t):
        return [pltpu.make_async_copy(ys_ref.at[pl.ds(p, 1)], rows_ref.at[k, pl.ds(t, 1)], sem)
                for k, p in enumerate(_pair_rows(seg_ref, route_ref, t))]

    def start(t, carry):
        for cp in row_copies(t):
            cp.start()
        return carry

    def wait(t, carry):
        for cp in row_copies(t):
            cp.wait()
        return carry

    lax.fori_loop(0, rows, start, 0)
    lax.fori_loop(0, rows, wait, 0)
    rw = rw_ref[...]
    moe = rw[:, 0:1] * rows_ref[0] + rw[:, 1:2] * rows_ref[1]
    x2 = x1_ref[...] + gate_ref[...] * moe.reshape(bb, tt, d)
    hf = _rms(x2) * nfin_ref[...]
    y_ref[...] = hf * (1.0 + scale_ref[...]) + shift_ref[...]


def _combine(seg, route, rw, x1, ada3, adaf3, nfin, ys, bb, tt):
    b, t, d = x1.shape
    nt = t // tt
    rows = bb * tt
    x_spec = pl.BlockSpec((bb, tt, d), lambda i, j, *_: (i, j, 0))

    def ada_spec(col):
        return pl.BlockSpec((bb, 1, d), lambda i, j, *_: (i, 0, col))

    return pl.pallas_call(
        _combine_kernel,
        grid_spec=pltpu.PrefetchScalarGridSpec(
            num_scalar_prefetch=1, grid=(b // bb, nt),
            in_specs=[pl.BlockSpec((ROUTE_ROWS, rows), lambda i, j, *_: (0, i * nt + j),
                                   memory_space=pltpu.SMEM),
                      pl.BlockSpec((rows, ROUTER_LANES), lambda i, j, *_: (i * nt + j, 0)),
                      x_spec, ada_spec(5), ada_spec(0), ada_spec(1),
                      pl.BlockSpec((1, d), lambda i, j, *_: (0, 0)),
                      pl.BlockSpec(memory_space=pltpu.HBM)],
            out_specs=x_spec,
            scratch_shapes=[pltpu.VMEM((2, rows, d), F32), pltpu.SemaphoreType.DMA]),
        out_shape=jax.ShapeDtypeStruct((b, t, d), F32),
        compiler_params=_cparams(("arbitrary", "arbitrary")),
        name="moe_combine",
    )(seg, route, rw, x1, ada3, adaf3, adaf3, nfin, ys)


def _mixer(x, ada3, cnt_in, s_h, s_re, s_im, p, *, bb, tt, hgrn_tile, hgrn_chunk, sequential):
    b, t, d = x.shape
    n = b * t
    q, k, g, v, gs, u = _inproj(x, ada3, p['norm_mix'], p['w_in'], p['lb'], bb, tt)
    oh, s_h_new = _hgrn(q, k, g, v, gs, s_h, p['hgrn_norm'], b, t, hgrn_tile, hgrn_chunk)
    u8 = _to_chunk_rows(u, n)
    if sequential:
        xr, xi = s_re.reshape(b, 1, -1), s_im.reshape(b, 1, -1)
        n_rows = t // SSM_CHUNK
    else:
        xr, xi = s_re.reshape(1, b, -1), s_im.reshape(1, b, -1)
        n_rows = b
    y8, fr, fi = _s5(u8, xr, xi, p['ssm_w1'], p['ssm_m'], p['ssm_a8'], n_rows, sequential)
    ys = _from_chunk_rows(y8, n)
    x1, h2, route, rw, cnt = _outproj(x, oh, ys, ada3, p['w_glu'], p['b_glu'], p['ssm_norm'], p['wo_h'],
                                      p['wo_s'], p['norm_ffn'], p['w_rt'], p['b_rt'], cnt_in, bb, tt)
    states = (s_h_new[None], fr.reshape(1, b, s_re.shape[-2], s_re.shape[-1]),
              fi.reshape(1, b, s_re.shape[-2], s_re.shape[-1]))
    return (x1, h2, route, rw), cnt, states


def kernel(x_prompt, x_sample, c_prompt, c_sample, state_hgrn, state_ssm_re, state_ssm_im, hgrn_lb_logits, w_ada, b_ada, norm_mix, w_in, hgrn_norm, ssm_a_re, ssm_a_im, ssm_log_dt, ssm_b_re, ssm_b_im, ssm_c_re, ssm_c_im, ssm_d, ssm_w_glu, ssm_b_glu, ssm_norm, w_out, norm_ffn, moe_w_group, moe_b_group, moe_w_router, moe_b_router, moe_w_gate, moe_w_up, moe_w_down, w_ada_final, b_ada_final, norm_final):
    depth = w_ada.shape[0]
    assert depth == 1
    d = x_prompt.shape[-1]
    bp = x_prompt.shape[0]
    dh = hgrn_norm.shape[-1]
    dk = dh // HGRN_HEADS
    de = moe_w_gate.shape[-1]
    n_exp = MOE_GROUPS * MOE_PER_GROUP

    lb = jax.nn.softmax(hgrn_lb_logits.astype(F32), axis=0)[0].reshape(1, dh)
    w1, m, a8 = _s5_prepare(ssm_a_re[0], ssm_a_im[0], ssm_log_dt[0], ssm_b_re[0], ssm_b_im[0],
                            ssm_c_re[0], ssm_c_im[0], ssm_d[0])
    w_rt = jnp.zeros((d, ROUTER_LANES), F32)
    w_rt = w_rt.at[:, :MOE_GROUPS].set(moe_w_group[0])
    w_rt = w_rt.at[:, EXPERT_LANE0:EXPERT_LANE0 + n_exp].set(
        moe_w_router[0].transpose(1, 0, 2).reshape(d, n_exp))
    b_rt = jnp.zeros((1, ROUTER_LANES), F32)
    b_rt = b_rt.at[0, :MOE_GROUPS].set(moe_b_group[0])
    b_rt = b_rt.at[0, EXPERT_LANE0:EXPERT_LANE0 + n_exp].set(moe_b_router[0].reshape(n_exp))
    p = dict(
        lb=lb, norm_mix=norm_mix[0].reshape(1, d), w_in=w_in[0].astype(BF16),
        hgrn_norm=hgrn_norm[0].reshape(1, dh),
        ssm_w1=w1, ssm_m=m, ssm_a8=a8,
        w_glu=ssm_w_glu[0].astype(BF16), b_glu=ssm_b_glu[0].reshape(1, -1), ssm_norm=ssm_norm[0].reshape(1, -1),
        wo_h=w_out[0, :dh].astype(BF16), wo_s=w_out[0, dh:].astype(BF16),
        norm_ffn=norm_ffn[0].reshape(1, d), w_rt=w_rt.astype(BF16), b_rt=b_rt,
    )
    wg = moe_w_gate[0].reshape(n_exp, d, de)
    wu = moe_w_up[0].reshape(n_exp, d, de)
    wd = moe_w_down[0].reshape(n_exp, de, d)
    nfin = norm_final.reshape(1, d)

    c_all = jnp.concatenate([c_prompt, c_sample], axis=0)
    ada = _silu_linear(c_all, w_ada[0], b_ada[0])
    adaf = _silu_linear(c_all, w_ada_final, b_ada_final)
    ada_p, ada_s = ada[:bp].reshape(bp, 1, -1), ada[bp:].reshape(x_sample.shape[0], 1, -1)
    adaf_p, adaf_s = adaf[:bp].reshape(bp, 1, -1), adaf[bp:].reshape(x_sample.shape[0], 1, -1)

    zeros_h = jnp.zeros((bp, HGRN_HEADS, dk, dk), F32)
    zeros_s = jnp.zeros((bp,) + state_ssm_re.shape[2:], F32)
    cnt0 = jnp.zeros((1, ROUTER_LANES), F32)
    tok_p, cnt_p, st_p = _mixer(x_prompt, ada_p, cnt0, zeros_h, zeros_s, zeros_s, p,
                                bb=1, tt=256, hgrn_tile=256, hgrn_chunk=64, sequential=True)
    tok_s, cnt, st_s = _mixer(x_sample, ada_s, cnt_p, state_hgrn[0], state_ssm_re[0], state_ssm_im[0], p,
                              bb=32, tt=8, hgrn_tile=8, hgrn_chunk=8, sequential=False)

    n_pairs = 2 * (tok_p[1].shape[0] + tok_s[1].shape[0])
    n_slots = n_pairs // MOE_TILE + N_EXPERTS
    seg, seg_end, item_tile, item_exp, n_items = _moe_schedule(cnt, n_slots)
    route = jnp.concatenate([tok_p[2], tok_s[2]], axis=1)
    xs = _scatter(seg, route, tok_p[1], tok_s[1], 256)
    ys = _experts(seg, seg_end, item_tile, item_exp, n_items, xs, wg, wu, wd)
    y_p = _combine(seg, tok_p[2], tok_p[3], tok_p[0], ada_p, adaf_p, nfin, ys, 1, 256)
    y_s = _combine(seg, tok_s[2], tok_s[3], tok_s[0], ada_s, adaf_s, nfin, ys, 32, 8)
    return (y_p, y_s) + st_p + st_s
```

```python
import functools
import math

import jax
import jax.numpy as jnp
from jax import lax
from jax.experimental import pallas as pl
from jax.experimental.pallas import tpu as pltpu

F32 = jnp.float32
BF16 = jnp.bfloat16
HIGHEST = lax.Precision.HIGHEST

EPS = 1e-6
MAX_REAL = -1e-4
HGRN_HEADS = 4
SSM_GROUP = 16
SSM_STATE = 64
SSM_CHUNK = 8
SSM_SETS = 4
MOE_GROUPS = 4
MOE_PER_GROUP = 8
N_EXPERTS = MOE_GROUPS * MOE_PER_GROUP
ROUTER_LANES = 128
EXPERT_LANE0 = 32
ROUTE_ROWS = 8
MOE_TILE = 256
LANES = 128
TOKEN_TILE_ROWS = 8
DMA_UNROLL = 8
VMEM_LIMIT = 56 * 1024 * 1024


def _cparams(sem):
    return pltpu.CompilerParams(dimension_semantics=sem, vmem_limit_bytes=VMEM_LIMIT)


def _silu(x):
    return x * jax.nn.sigmoid(x)


def _rms(x):
    return x * lax.rsqrt(jnp.mean(x * x, axis=-1, keepdims=True) + EPS)


def _dot(a, b):
    return jnp.dot(a, b, preferred_element_type=F32)


def _dot_nt(a, b):
    return lax.dot_general(a, b, (((1,), (1,)), ((), ())), preferred_element_type=F32)


def _dot_tn(a, b, precision=None):
    return lax.dot_general(a, b, (((0,), (0,)), ((), ())), preferred_element_type=F32,
                           precision=precision)


def _silu_linear_kernel(c_ref, w_ref, b_ref, o_ref):
    a = _silu(c_ref[...]).astype(BF16)
    o_ref[...] = _dot(a, w_ref[...].astype(BF16)) + b_ref[...]


def _silu_linear(c, w, b):
    m, d = c.shape
    n = w.shape[1]
    tn = 1024
    return pl.pallas_call(
        _silu_linear_kernel,
        grid=(n // tn,),
        in_specs=[pl.BlockSpec((m, d), lambda j: (0, 0)),
                  pl.BlockSpec((d, tn), lambda j: (0, j)),
                  pl.BlockSpec((1, tn), lambda j: (0, j))],
        out_specs=pl.BlockSpec((m, tn), lambda j: (0, j)),
        out_shape=jax.ShapeDtypeStruct((m, n), F32),
        compiler_params=_cparams(("parallel",)),
        name="silu_linear",
    )(c, w, b.reshape(1, n))


def _inproj_kernel(x_ref, shift_ref, scale_ref, gain_ref, w_ref, lb_ref,
                   q_ref, k_ref, g_ref, v_ref, gs_ref, u_ref, *, dh):
    bb, tt, d = x_ref.shape
    h = _rms(x_ref[...]) * gain_ref[...]
    h = h * (1.0 + scale_ref[...]) + shift_ref[...]
    proj = _dot(h.reshape(bb * tt, d).astype(BF16), w_ref[...])
    lb = lb_ref[...]
    f = lb + (1.0 - lb) * jax.nn.sigmoid(proj[:, dh:2 * dh])
    q_ref[...] = _silu(proj[:, :dh]) * (float(dh // HGRN_HEADS) ** -0.5)
    k_ref[...] = 1.0 - f
    g_ref[...] = jnp.log(f)
    v_ref[...] = proj[:, 2 * dh:3 * dh]
    gs_ref[...] = _silu(proj[:, 3 * dh:4 * dh])
    u_ref[...] = proj[:, 4 * dh:]


def _inproj(x, ada3, gain, w_in_bf, lb, bb, tt):
    b, t, d = x.shape
    dh = lb.shape[-1]
    nt = t // tt
    rows = bb * tt
    n = b * t
    row_spec = pl.BlockSpec((rows, dh), lambda i, j: (i * nt + j, 0))
    out = jax.ShapeDtypeStruct((n, dh), F32)
    return pl.pallas_call(
        functools.partial(_inproj_kernel, dh=dh),
        grid=(b // bb, nt),
        in_specs=[pl.BlockSpec((bb, tt, d), lambda i, j: (i, j, 0)),
                  pl.BlockSpec((bb, 1, d), lambda i, j: (i, 0, 0)),
                  pl.BlockSpec((bb, 1, d), lambda i, j: (i, 0, 1)),
                  pl.BlockSpec((1, d), lambda i, j: (0, 0)),
                  pl.BlockSpec(w_in_bf.shape, lambda i, j: (0, 0)),
                  pl.BlockSpec((1, dh), lambda i, j: (0, 0))],
        out_specs=[row_spec] * 6,
        out_shape=[out] * 6,
        compiler_params=_cparams(("parallel", "parallel")),
        name="inproj",
    )(x, ada3, ada3, gain, w_in_bf, lb)


def _hgrn_kernel(q_ref, k_ref, g_ref, v_ref, gs_ref, s0_ref, hn_ref, o_ref, sf_ref, st_ref,
                 *, chunk, nt):
    j = pl.program_id(1)
    ct, dh = q_ref.shape
    dk = dh // HGRN_HEADS
    c = chunk

    @pl.when(j == 0)
    def _():
        st_ref[...] = s0_ref[0]

    r = lax.broadcasted_iota(jnp.int32, (c, c), 0)
    s = lax.broadcasted_iota(jnp.int32, (c, c), 1)
    causal = r >= s
    tri = causal.astype(F32)
    ones = jnp.ones((c, dk), F32)
    mid = c // 2 - 1

    for ci in range(ct // c):
        rows = slice(ci * c, (ci + 1) * c)
        for h in range(HGRN_HEADS):
            lanes = slice(h * dk, (h + 1) * dk)
            g = g_ref[rows, lanes]
            q = q_ref[rows, lanes]
            k = k_ref[rows, lanes]
            v = v_ref[rows, lanes].astype(BF16)
            a = jnp.dot(tri, g, precision=HIGHEST, preferred_element_type=F32)
            a_mid = a[mid:mid + 1]
            a_end = a[c - 1:c]
            qt = (q * jnp.exp(a - a_mid)).astype(BF16)
            kt = (k * jnp.exp(a_mid - a)).astype(BF16)
            sc = jnp.where(causal, _dot_nt(qt, kt), 0.0).astype(BF16)
            state = st_ref[h]
            o = _dot(sc, v) + _dot((q * jnp.exp(a)).astype(BF16), state.astype(BF16))
            decay = jnp.exp(_dot_tn(g, ones, precision=HIGHEST))
            st_ref[h] = decay * state + _dot_tn((k * jnp.exp(a_end - a)).astype(BF16), v)
            o_ref[rows, lanes] = _rms(o) * hn_ref[:, lanes] * gs_ref[rows, lanes]

    @pl.when(j == nt - 1)
    def _():
        sf_ref[0] = st_ref[...]


def _hgrn(q, k, g, v, gs, s0, hnorm, b, t, ct, chunk):
    n, dh = q.shape
    nt = t // ct
    dk = dh // HGRN_HEADS
    row_spec = pl.BlockSpec((ct, dh), lambda i, j: (i * nt + j, 0))
    st_spec = pl.BlockSpec((1, HGRN_HEADS, dk, dk), lambda i, j: (i, 0, 0, 0))
    return pl.pallas_call(
        functools.partial(_hgrn_kernel, chunk=chunk, nt=nt),
        grid=(b, nt),
        in_specs=[row_spec] * 5 + [st_spec, pl.BlockSpec((1, dh), lambda i, j: (0, 0))],
        out_specs=[row_spec, st_spec],
        out_shape=[jax.ShapeDtypeStruct((n, dh), F32),
                   jax.ShapeDtypeStruct((b, HGRN_HEADS, dk, dk), F32)],
        scratch_shapes=[pltpu.VMEM((HGRN_HEADS, dk, dk), F32)],
        compiler_params=_cparams(("parallel", "arbitrary")),
        name="hgrn",
    )(q, k, g, v, gs, s0, hnorm)


def _s5_prepare(a_re, a_im, log_dt, b_re, b_im, c_re, c_im, d_skip):
    ng, npp = a_re.shape
    nh = b_re.shape[-1]
    L = SSM_CHUNK
    gs = ng // SSM_SETS
    lam_re = jnp.minimum(a_re, MAX_REAL)
    lam_im = a_im
    dt = jnp.exp(log_dt)
    mag = jnp.exp(lam_re * dt)
    ab_re = mag * jnp.cos(lam_im * dt)
    ab_im = mag * jnp.sin(lam_im * dt)
    den = lam_re * lam_re + lam_im * lam_im
    co_re = ((ab_re - 1.0) * lam_re + ab_im * lam_im) / den
    co_im = (ab_im * lam_re - (ab_re - 1.0) * lam_im) / den
    bb_re = co_re[..., None] * b_re - co_im[..., None] * b_im
    bb_im = co_re[..., None] * b_im + co_im[..., None] * b_re
    pw_re = [jnp.ones_like(ab_re)]
    pw_im = [jnp.zeros_like(ab_im)]
    for _ in range(L):
        pr, pi = pw_re[-1], pw_im[-1]
        pw_re.append(pr * ab_re - pi * ab_im)
        pw_im.append(pr * ab_im + pi * ab_re)
    pw_re = jnp.stack(pw_re)
    pw_im = jnp.stack(pw_im)
    ab_b_re = pw_re[:L, :, :, None] * bb_re - pw_im[:L, :, :, None] * bb_im
    ab_b_im = pw_re[:L, :, :, None] * bb_im + pw_im[:L, :, :, None] * bb_re
    kern = (jnp.einsum('gkp,lgph->lghk', c_re, ab_b_re, precision=HIGHEST)
            - jnp.einsum('gkp,lgph->lghk', c_im, ab_b_im, precision=HIGHEST))
    kern = kern.at[0].add(d_skip[:, :, None] * jnp.eye(nh, dtype=F32))
    lag = jnp.arange(L)[None, :] - jnp.arange(L)[:, None]
    toep = jnp.where((lag >= 0)[:, :, None, None, None], kern[jnp.clip(lag, 0, L - 1)], 0.0)
    eye = jnp.eye(gs, dtype=F32)
    toep = toep.reshape(L, L, SSM_SETS, gs, nh, nh)
    w_t = jnp.einsum('stGghk,gq->Gsghtqk', toep, eye).reshape(SSM_SETS, L * gs * nh, L * gs * nh)
    inc_re = ab_b_re[::-1].reshape(L, SSM_SETS, gs, npp, nh)
    inc_im = ab_b_im[::-1].reshape(L, SSM_SETS, gs, npp, nh)
    n_re = jnp.einsum('sGgph,gq->Gsghqp', inc_re, eye).reshape(SSM_SETS, L * gs * nh, gs * npp)
    n_im = jnp.einsum('sGgph,gq->Gsghqp', inc_im, eye).reshape(SSM_SETS, L * gs * nh, gs * npp)
    w1 = jnp.concatenate([w_t, n_re, n_im], axis=-1).astype(BF16)
    ca_re = c_re[None] * pw_re[1:, :, None, :] - c_im[None] * pw_im[1:, :, None, :]
    ca_im = c_re[None] * pw_im[1:, :, None, :] + c_im[None] * pw_re[1:, :, None, :]
    ca_re = ca_re.reshape(L, SSM_SETS, gs, nh, npp)
    ca_im = ca_im.reshape(L, SSM_SETS, gs, nh, npp)
    m_re = jnp.einsum('tGgkp,gq->Ggptqk', ca_re, eye).reshape(SSM_SETS, gs * npp, L * gs * nh)
    m_im = jnp.einsum('tGgkp,gq->Ggptqk', -ca_im, eye).reshape(SSM_SETS, gs * npp, L * gs * nh)
    m = jnp.concatenate([m_re, m_im], axis=1).astype(BF16)
    a8 = jnp.concatenate([pw_re[L].reshape(SSM_SETS, 1, gs * npp),
                          pw_im[L].reshape(SSM_SETS, 1, gs * npp)], axis=-1)
    return w1, m, a8


def _s5_kernel(u_ref, xr_ref, xi_ref, w1_ref, m_ref, a8_ref, y_ref, fr_ref, fi_ref, *, sequential):
    n = u_ref.shape[1]
    ns = xr_ref.shape[-1]
    ny = y_ref.shape[-1]
    res = _dot(u_ref[0].astype(BF16), w1_ref[0])
    d_re = res[:, ny:ny + ns]
    d_im = res[:, ny + ns:]
    a_re = a8_ref[0][:, :ns]
    a_im = a8_ref[0][:, ns:]
    x0_re = xr_ref[0]
    x0_im = xi_ref[0]
    if sequential:
        row = lax.broadcasted_iota(jnp.int32, (n, ns), 0)
        first = row == 0
        x_re = d_re + jnp.where(first, a_re * x0_re - a_im * x0_im, 0.0)
        x_im = d_im + jnp.where(first, a_re * x0_im + a_im * x0_re, 0.0)
        p_re, p_im = a_re, a_im
        step = 1
        while step < n:
            s_re = jnp.where(row >= step, pltpu.roll(x_re, step, 0), 0.0)
            s_im = jnp.where(row >= step, pltpu.roll(x_im, step, 0), 0.0)
            x_re, x_im = x_re + p_re * s_re - p_im * s_im, x_im + p_re * s_im + p_im * s_re
            p_re, p_im = p_re * p_re - p_im * p_im, 2.0 * p_re * p_im
            step *= 2
        fr_ref[0] = x_re[n - 1:n]
        fi_ref[0] = x_im[n - 1:n]
        xc_re = jnp.where(first, x0_re, pltpu.roll(x_re, 1, 0))
        xc_im = jnp.where(first, x0_im, pltpu.roll(x_im, 1, 0))
    else:
        xc_re, xc_im = x0_re, x0_im
        fr_ref[0] = a_re * x0_re - a_im * x0_im + d_re
        fi_ref[0] = a_re * x0_im + a_im * x0_re + d_im
    xc = jnp.concatenate([xc_re, xc_im], axis=-1).astype(BF16)
    y_ref[0] = res[:, :ny] + _dot(xc, m_ref[0])


def _s5(u8, x_re, x_im, w1, m, a8, n_rows, sequential):
    sets, r, nu = u8.shape
    nb, rb, _ = x_re.shape
    ns = m.shape[1] // 2
    st_spec = pl.BlockSpec((1, rb, ns), lambda gi, i: (i, 0, gi))
    st_shape = jax.ShapeDtypeStruct(x_re.shape, F32)
    return pl.pallas_call(
        functools.partial(_s5_kernel, sequential=sequential),
        grid=(sets, nb),
        in_specs=[pl.BlockSpec((1, n_rows, nu), lambda gi, i: (gi, i, 0)),
                  st_spec, st_spec,
                  pl.BlockSpec((1,) + w1.shape[1:], lambda gi, i: (gi, 0, 0)),
                  pl.BlockSpec((1,) + m.shape[1:], lambda gi, i: (gi, 0, 0)),
                  pl.BlockSpec((1, 1, 2 * ns), lambda gi, i: (gi, 0, 0))],
        out_specs=[pl.BlockSpec((1, n_rows, nu), lambda gi, i: (gi, i, 0)), st_spec, st_spec],
        out_shape=[jax.ShapeDtypeStruct(u8.shape, F32), st_shape, st_shape],
        compiler_params=_cparams(("parallel", "parallel")),
        name="s5",
    )(u8, x_re, x_im, w1, m, a8)


def _to_chunk_rows(u, n):
    return u.reshape(n // SSM_CHUNK, SSM_CHUNK, SSM_SETS, -1).transpose(2, 0, 1, 3).reshape(
        SSM_SETS, n // SSM_CHUNK, -1)


def _from_chunk_rows(y8, n):
    return y8.reshape(SSM_SETS, n // SSM_CHUNK, SSM_CHUNK, -1).transpose(1, 2, 0, 3).reshape(n, -1)


def _gelu_tanh(x):
    return 0.5 * x * (1.0 + jnp.tanh(math.sqrt(2.0 / math.pi) * (x + 0.044715 * (x * x * x))))


def _outproj_kernel(x_ref, oh_ref, ys_ref, gate_ref, shift_ref, scale_ref, wglu_ref, bglu_ref, sn_ref,
                    wo_h_ref, wo_s_ref, nf_ref, wr_ref, br_ref, cnt_in_ref,
                    x1_ref, h2_ref, route_ref, rw_ref, cnt_ref):
    bb, tt, d = x_ref.shape
    rows = bb * tt
    y = _gelu_tanh(ys_ref[...])
    y = y * jax.nn.sigmoid(_dot(y.astype(BF16), wglu_ref[...]) + bglu_ref[...])
    o_s = _rms(y) * sn_ref[...]
    mix = _dot(oh_ref[...].astype(BF16), wo_h_ref[...]) + _dot(o_s.astype(BF16), wo_s_ref[...])
    x1 = x_ref[...] + gate_ref[...] * mix.reshape(bb, tt, d)
    x1_ref[...] = x1
    h2 = _rms(x1) * nf_ref[...]
    h2 = (h2 * (1.0 + scale_ref[...]) + shift_ref[...]).reshape(rows, d)
    _store_token_tiles(h2_ref, h2)

    logits = _dot(h2.astype(BF16), wr_ref[...]) + br_ref[...]
    lane = lax.broadcasted_iota(jnp.int32, logits.shape, 1)
    neg = -jnp.inf
    gl = jnp.where(lane < MOE_GROUPS, logits, neg)
    gmax = jnp.max(gl, axis=-1, keepdims=True)
    gidx = jnp.min(jnp.where(gl == gmax, lane, ROUTER_LANES), axis=-1, keepdims=True)
    grp_w = 1.0 / jnp.sum(jnp.exp(gl - gmax), axis=-1, keepdims=True)
    e0 = EXPERT_LANE0 + gidx * MOE_PER_GROUP
    sel = jnp.where((lane >= e0) & (lane < e0 + MOE_PER_GROUP), logits, neg)
    m1 = jnp.max(sel, axis=-1, keepdims=True)
    i1 = jnp.min(jnp.where(sel == m1, lane, ROUTER_LANES), axis=-1, keepdims=True)
    sel2 = jnp.where(lane == i1, neg, sel)
    m2 = jnp.max(sel2, axis=-1, keepdims=True)
    i2 = jnp.min(jnp.where(sel2 == m2, lane, ROUTER_LANES), axis=-1, keepdims=True)
    e2 = jnp.exp(m2 - m1)
    w1 = 1.0 / (1.0 + e2)
    w2 = e2 / (1.0 + e2)
    rw_ref[...] = grp_w * (jnp.where(lane == 0, w1, 0.0) + jnp.where(lane == 1, w2, 0.0))

    @pl.when((pl.program_id(0) == 0) & (pl.program_id(1) == 0))
    def _():
        cnt_ref[...] = cnt_in_ref[...]

    picked = (lane == i1) | (lane == i2)
    earlier = (lax.broadcasted_iota(jnp.int32, (rows, rows), 0)
               > lax.broadcasted_iota(jnp.int32, (rows, rows), 1))
    base = cnt_ref[...]
    before = _dot(earlier.astype(BF16), picked.astype(BF16)) + base
    rank1 = jnp.sum(jnp.where(lane == i1, before, 0.0), axis=-1, keepdims=True).astype(jnp.int32)
    rank2 = jnp.sum(jnp.where(lane == i2, before, 0.0), axis=-1, keepdims=True).astype(jnp.int32)
    cnt_ref[...] = base + jnp.sum(picked.astype(F32), axis=0, keepdims=True)
    info = jnp.where(lane == 0, i1 - EXPERT_LANE0,
                     jnp.where(lane == 1, i2 - EXPERT_LANE0,
                               jnp.where(lane == 2, rank1, jnp.where(lane == 3, rank2, 0))))
    route_ref[...] = info.T[:ROUTE_ROWS]


def _outproj(x, oh, ys, ada3, wglu_bf, bglu, snorm, wo_h, wo_s, nffn, wr, br, cnt_in, bb, tt):
    b, t, d = x.shape
    n = b * t
    dh = oh.shape[-1]
    nt = t // tt
    rows = bb * tt
    x_spec = pl.BlockSpec((bb, tt, d), lambda i, j: (i, j, 0))
    half_spec = pl.BlockSpec((rows, dh), lambda i, j: (i * nt + j, 0))

    def ada_spec(col):
        return pl.BlockSpec((bb, 1, d), lambda i, j: (i, 0, col))

    def full(a):
        return pl.BlockSpec(a.shape, lambda i, j: (0,) * a.ndim)

    return pl.pallas_call(
        _outproj_kernel,
        grid=(b // bb, nt),
        in_specs=[x_spec, half_spec, half_spec, ada_spec(2), ada_spec(3), ada_spec(4),
                  full(wglu_bf), full(bglu), full(snorm), full(wo_h), full(wo_s), full(nffn),
                  full(wr), full(br), full(cnt_in)],
        out_specs=[x_spec,
                   pl.BlockSpec((rows * TOKEN_TILE_ROWS, LANES), lambda i, j: (i * nt + j, 0)),
                   pl.BlockSpec((ROUTE_ROWS, rows), lambda i, j: (0, i * nt + j)),
                   pl.BlockSpec((rows, ROUTER_LANES), lambda i, j: (i * nt + j, 0)),
                   pl.BlockSpec((1, ROUTER_LANES), lambda i, j: (0, 0))],
        out_shape=[jax.ShapeDtypeStruct((b, t, d), F32),
                   jax.ShapeDtypeStruct((n * TOKEN_TILE_ROWS, LANES), F32),
                   jax.ShapeDtypeStruct((ROUTE_ROWS, n), jnp.int32),
                   jax.ShapeDtypeStruct((n, ROUTER_LANES), F32),
                   jax.ShapeDtypeStruct((1, ROUTER_LANES), F32)],
        compiler_params=_cparams(("arbitrary", "arbitrary")),
        name="outproj",
    )(x, oh, ys, ada3, ada3, ada3, wglu_bf, bglu, snorm, wo_h, wo_s, nffn, wr, br, cnt_in)


def _moe_schedule(cnt, n_slots):
    c = cnt[0, EXPERT_LANE0:EXPERT_LANE0 + N_EXPERTS].astype(jnp.int32)
    seg_end = jnp.cumsum(c)
    seg_start = seg_end - c
    first_tile = seg_start // MOE_TILE
    tiles = jnp.where(c > 0, (seg_end - 1) // MOE_TILE - first_tile + 1, 0)
    cum = jnp.cumsum(tiles)
    n_items = cum[-1]
    item = jnp.minimum(jnp.arange(n_slots, dtype=jnp.int32), n_items - 1)
    item_exp = jnp.sum(item[:, None] >= cum[None, :], axis=1).astype(jnp.int32)
    item_tile = first_tile[item_exp] + item - (cum - tiles)[item_exp]
    return (seg_start.astype(jnp.int32), seg_end.astype(jnp.int32), item_tile.astype(jnp.int32), item_exp,
            n_items.reshape(1).astype(jnp.int32))


def _token_tile(ref, t):
    return ref.at[pl.ds(pl.multiple_of(t * TOKEN_TILE_ROWS, TOKEN_TILE_ROWS), TOKEN_TILE_ROWS)]


def _store_token_tiles(ref, x):
    rows = x.shape[0]
    for c in range(TOKEN_TILE_ROWS):
        ref[pl.ds(c, rows, stride=TOKEN_TILE_ROWS), :] = x[:, c * LANES:(c + 1) * LANES]


def _load_token_tiles(ref, rows):
    return [ref[pl.ds(c, rows, stride=TOKEN_TILE_ROWS), :] for c in range(TOKEN_TILE_ROWS)]


def _pair_positions(route, seg):
    return seg[route[0]] + route[2], seg[route[1]] + route[3]


def _scatter_kernel(p1_ref, p2_ref, hp_ref, hs_ref, xs_ref, sem, *, prompt_tiles):
    ts = p1_ref.shape[0]

    def scatter_from(h_ref):
        def row_copies(t):
            src = _token_tile(h_ref, t)
            return [pltpu.make_async_copy(src, _token_tile(xs_ref, p_ref[t]), sem) for p_ref in (p1_ref, p2_ref)]

        def start(t, carry):
            for cp in row_copies(t):
                cp.start()
            return carry

        def wait(t, carry):
            for cp in row_copies(t):
                cp.wait()
            return carry

        lax.fori_loop(0, ts, start, 0, unroll=DMA_UNROLL)
        lax.fori_loop(0, ts, wait, 0, unroll=DMA_UNROLL)

    @pl.when(pl.program_id(0) < prompt_tiles)
    def _():
        scatter_from(hp_ref)

    @pl.when(pl.program_id(0) >= prompt_tiles)
    def _():
        scatter_from(hs_ref)


def _scatter(pos1, pos2, h_p, h_s, ts):
    n_p = h_p.shape[0] // TOKEN_TILE_ROWS
    n_s = h_s.shape[0] // TOKEN_TILE_ROWS
    tiles_p = n_p // ts
    idx_spec = pl.BlockSpec((ts,), lambda i: (i,), memory_space=pltpu.SMEM)
    return pl.pallas_call(
        functools.partial(_scatter_kernel, prompt_tiles=tiles_p),
        grid=((n_p + n_s) // ts,),
        in_specs=[idx_spec, idx_spec,
                  pl.BlockSpec((ts * TOKEN_TILE_ROWS, LANES), lambda i: (jnp.minimum(i, tiles_p - 1), 0)),
                  pl.BlockSpec((ts * TOKEN_TILE_ROWS, LANES), lambda i: (jnp.maximum(i - tiles_p, 0), 0))],
        out_specs=pl.BlockSpec(memory_space=pltpu.HBM),
        scratch_shapes=[pltpu.SemaphoreType.DMA],
        out_shape=jax.ShapeDtypeStruct((2 * (n_p + n_s) * TOKEN_TILE_ROWS, LANES), F32),
        compiler_params=_cparams(("arbitrary",)),
        name="moe_scatter",
    )(pos1, pos2, h_p, h_s)


def _experts_kernel(lo_ref, hi_ref, tile_ref, exp_ref, items_ref, xs_ref, wg_ref, wu_ref, wd_ref, ys_ref,
                    wg_s, wu_s, wd_s):
    i = pl.program_id(0)
    prev = jnp.maximum(i - 1, 0)
    e = exp_ref[i]

    @pl.when((i == 0) | (e != exp_ref[prev]))
    def _():
        wg_s[...] = wg_ref[0].astype(BF16)
        wu_s[...] = wu_ref[0].astype(BF16)
        wd_s[...] = wd_ref[0].astype(BF16)

    @pl.when(i < items_ref[0])
    def _():
        x = jnp.concatenate(_load_token_tiles(xs_ref, MOE_TILE), axis=-1).astype(BF16)
        act = _silu(_dot(x, wg_s[...])) * _dot(x, wu_s[...])
        out = _dot(act.astype(BF16), wd_s[...])
        row = tile_ref[i] * MOE_TILE + lax.broadcasted_iota(jnp.int32, (MOE_TILE, 1), 0)
        mine = (row >= lo_ref[e]) & (row < hi_ref[e])
        first_visit = (i == 0) | (tile_ref[i] != tile_ref[prev])

        @pl.when(first_visit)
        def _():
            _store_token_tiles(ys_ref, jnp.where(mine, out, 0.0))

        @pl.when(jnp.logical_not(first_visit))
        def _():
            old = jnp.concatenate(_load_token_tiles(ys_ref, MOE_TILE), axis=-1)
            _store_token_tiles(ys_ref, jnp.where(mine, out, old))


def _experts(seg_lo, seg_hi, item_tile, item_exp, n_items, xs, wg, wu, wd):
    n_rows = xs.shape[0] // TOKEN_TILE_ROWS
    _, d, de = wg.shape
    row_spec = pl.BlockSpec((MOE_TILE * TOKEN_TILE_ROWS, LANES), lambda i, lo, hi, tile, ex, items: (tile[i], 0))

    def w_spec(shape):
        return pl.BlockSpec((1,) + shape, lambda i, lo, hi, tile, ex, items: (ex[i], 0, 0))

    return pl.pallas_call(
        _experts_kernel,
        grid_spec=pltpu.PrefetchScalarGridSpec(
            num_scalar_prefetch=5, grid=(item_tile.shape[0],),
            in_specs=[row_spec, w_spec((d, de)), w_spec((d, de)), w_spec((de, d))],
            out_specs=row_spec,
            scratch_shapes=[pltpu.VMEM((d, de), BF16), pltpu.VMEM((d, de), BF16), pltpu.VMEM((de, d), BF16)]),
        out_shape=jax.ShapeDtypeStruct(xs.shape, F32),
        compiler_params=_cparams(("arbitrary",)),
        name="moe_experts",
    )(seg_lo, seg_hi, item_tile, item_exp, n_items, xs, wg, wu, wd)


def _combine_kernel(p1_ref, p2_ref, rw_ref, x1_ref, gate_ref, shift_ref, scale_ref, nfin_ref, ys_ref,
                    y_ref, r1_ref, r2_ref, sem):
    bb, tt, d = x1_ref.shape
    rows = bb * tt

    def row_copies(t):
        return [pltpu.make_async_copy(_token_tile(ys_ref, p_ref[t]), _token_tile(r_ref, t), sem)
                for p_ref, r_ref in ((p1_ref, r1_ref), (p2_ref, r2_ref))]

    def start(t, carry):
        for cp in row_copies(t):
            cp.start()
        return carry

    def wait(t, carry):
        for cp in row_copies(t):
            cp.wait()
        return carry

    lax.fori_loop(0, rows, start, 0, unroll=DMA_UNROLL)
    lax.fori_loop(0, rows, wait, 0, unroll=DMA_UNROLL)
    rw = rw_ref[...]
    w1 = rw[:, 0:1]
    w2 = rw[:, 1:2]
    moe = jnp.concatenate([w1 * a + w2 * b for a, b in zip(_load_token_tiles(r1_ref, rows),
                                                          _load_token_tiles(r2_ref, rows))], axis=-1)
    x2 = x1_ref[...] + gate_ref[...] * moe.reshape(bb, tt, d)
    hf = _rms(x2) * nfin_ref[...]
    y_ref[...] = hf * (1.0 + scale_ref[...]) + shift_ref[...]


def _combine(pos1, pos2, rw, x1, ada3, adaf3, nfin, ys, bb, tt):
    b, t, d = x1.shape
    nt = t // tt
    rows = bb * tt
    x_spec = pl.BlockSpec((bb, tt, d), lambda i, j: (i, j, 0))
    idx_spec = pl.BlockSpec((rows,), lambda i, j: (i * nt + j,), memory_space=pltpu.SMEM)

    def ada_spec(col):
        return pl.BlockSpec((bb, 1, d), lambda i, j: (i, 0, col))

    return pl.pallas_call(
        _combine_kernel,
        grid=(b // bb, nt),
        in_specs=[idx_spec, idx_spec,
                  pl.BlockSpec((rows, ROUTER_LANES), lambda i, j: (i * nt + j, 0)),
                  x_spec, ada_spec(5), ada_spec(0), ada_spec(1),
                  pl.BlockSpec((1, d), lambda i, j: (0, 0)),
                  pl.BlockSpec(memory_space=pltpu.HBM)],
        out_specs=x_spec,
        scratch_shapes=[pltpu.VMEM((rows * TOKEN_TILE_ROWS, LANES), F32),
                        pltpu.VMEM((rows * TOKEN_TILE_ROWS, LANES), F32),
                        pltpu.SemaphoreType.DMA],
        out_shape=jax.ShapeDtypeStruct((b, t, d), F32),
        compiler_params=_cparams(("arbitrary", "arbitrary")),
        name="moe_combine",
    )(pos1, pos2, rw, x1, ada3, adaf3, adaf3, nfin, ys)


def _mixer(x, ada3, cnt_in, s_h, s_re, s_im, p, *, bb, tt, hgrn_tile, hgrn_chunk, sequential):
    b, t, d = x.shape
    n = b * t
    q, k, g, v, gs, u = _inproj(x, ada3, p['norm_mix'], p['w_in'], p['lb'], bb, tt)
    oh, s_h_new = _hgrn(q, k, g, v, gs, s_h, p['hgrn_norm'], b, t, hgrn_tile, hgrn_chunk)
    u8 = _to_chunk_rows(u, n)
    if sequential:
        xr, xi = s_re.reshape(b, 1, -1), s_im.reshape(b, 1, -1)
        n_rows = t // SSM_CHUNK
    else:
        xr, xi = s_re.reshape(1, b, -1), s_im.reshape(1, b, -1)
        n_rows = b
    y8, fr, fi = _s5(u8, xr, xi, p['ssm_w1'], p['ssm_m'], p['ssm_a8'], n_rows, sequential)
    ys = _from_chunk_rows(y8, n)
    x1, h2, route, rw, cnt = _outproj(x, oh, ys, ada3, p['w_glu'], p['b_glu'], p['ssm_norm'], p['wo_h'],
                                      p['wo_s'], p['norm_ffn'], p['w_rt'], p['b_rt'], cnt_in, bb, tt)
    states = (s_h_new[None], fr.reshape(1, b, s_re.shape[-2], s_re.shape[-1]),
              fi.reshape(1, b, s_re.shape[-2], s_re.shape[-1]))
    return (x1, h2, route, rw), cnt, states


def kernel(x_prompt, x_sample, c_prompt, c_sample, state_hgrn, state_ssm_re, state_ssm_im, hgrn_lb_logits, w_ada, b_ada, norm_mix, w_in, hgrn_norm, ssm_a_re, ssm_a_im, ssm_log_dt, ssm_b_re, ssm_b_im, ssm_c_re, ssm_c_im, ssm_d, ssm_w_glu, ssm_b_glu, ssm_norm, w_out, norm_ffn, moe_w_group, moe_b_group, moe_w_router, moe_b_router, moe_w_gate, moe_w_up, moe_w_down, w_ada_final, b_ada_final, norm_final):
    depth = w_ada.shape[0]
    assert depth == 1
    d = x_prompt.shape[-1]
    bp = x_prompt.shape[0]
    dh = hgrn_norm.shape[-1]
    dk = dh // HGRN_HEADS
    de = moe_w_gate.shape[-1]
    n_exp = MOE_GROUPS * MOE_PER_GROUP

    lb = jax.nn.softmax(hgrn_lb_logits.astype(F32), axis=0)[0].reshape(1, dh)
    w1, m, a8 = _s5_prepare(ssm_a_re[0], ssm_a_im[0], ssm_log_dt[0], ssm_b_re[0], ssm_b_im[0],
                            ssm_c_re[0], ssm_c_im[0], ssm_d[0])
    w_rt = jnp.zeros((d, ROUTER_LANES), F32)
    w_rt = w_rt.at[:, :MOE_GROUPS].set(moe_w_group[0])
    w_rt = w_rt.at[:, EXPERT_LANE0:EXPERT_LANE0 + n_exp].set(
        moe_w_router[0].transpose(1, 0, 2).reshape(d, n_exp))
    b_rt = jnp.zeros((1, ROUTER_LANES), F32)
    b_rt = b_rt.at[0, :MOE_GROUPS].set(moe_b_group[0])
    b_rt = b_rt.at[0, EXPERT_LANE0:EXPERT_LANE0 + n_exp].set(moe_b_router[0].reshape(n_exp))
    p = dict(
        lb=lb, norm_mix=norm_mix[0].reshape(1, d), w_in=w_in[0].astype(BF16),
        hgrn_norm=hgrn_norm[0].reshape(1, dh),
        ssm_w1=w1, ssm_m=m, ssm_a8=a8,
        w_glu=ssm_w_glu[0].astype(BF16), b_glu=ssm_b_glu[0].reshape(1, -1), ssm_norm=ssm_norm[0].reshape(1, -1),
        wo_h=w_out[0, :dh].astype(BF16), wo_s=w_out[0, dh:].astype(BF16),
        norm_ffn=norm_ffn[0].reshape(1, d), w_rt=w_rt.astype(BF16), b_rt=b_rt,
    )
    wg = moe_w_gate[0].reshape(n_exp, d, de)
    wu = moe_w_up[0].reshape(n_exp, d, de)
    wd = moe_w_down[0].reshape(n_exp, de, d)
    nfin = norm_final.reshape(1, d)

    c_all = jnp.concatenate([c_prompt, c_sample], axis=0)
    ada = _silu_linear(c_all, w_ada[0], b_ada[0])
    adaf = _silu_linear(c_all, w_ada_final, b_ada_final)
    ada_p, ada_s = ada[:bp].reshape(bp, 1, -1), ada[bp:].reshape(x_sample.shape[0], 1, -1)
    adaf_p, adaf_s = adaf[:bp].reshape(bp, 1, -1), adaf[bp:].reshape(x_sample.shape[0], 1, -1)

    zeros_h = jnp.zeros((bp, HGRN_HEADS, dk, dk), F32)
    zeros_s = jnp.zeros((bp,) + state_ssm_re.shape[2:], F32)
    cnt0 = jnp.zeros((1, ROUTER_LANES), F32)
    tok_p, cnt_p, st_p = _mixer(x_prompt, ada_p, cnt0, zeros_h, zeros_s, zeros_s, p,
                                bb=1, tt=256, hgrn_tile=256, hgrn_chunk=64, sequential=True)
    tok_s, cnt, st_s = _mixer(x_sample, ada_s, cnt_p, state_hgrn[0], state_ssm_re[0], state_ssm_im[0], p,
                              bb=32, tt=8, hgrn_tile=8, hgrn_chunk=8, sequential=False)

    n_pairs = 2 * (tok_p[1].shape[0] + tok_s[1].shape[0])
    n_slots = n_pairs // MOE_TILE + N_EXPERTS
    seg, seg_end, item_tile, item_exp, n_items = _moe_schedule(cnt, n_slots)
    pos_p = _pair_positions(tok_p[2], seg)
    pos_s = _pair_positions(tok_s[2], seg)
    xs = _scatter(jnp.concatenate([pos_p[0], pos_s[0]]), jnp.concatenate([pos_p[1], pos_s[1]]),
                  tok_p[1], tok_s[1], 256)
    ys = _experts(seg, seg_end, item_tile, item_exp, n_items, xs, wg, wu, wd)
    y_p = _combine(pos_p[0], pos_p[1], tok_p[3], tok_p[0], ada_p, adaf_p, nfin, ys, 1, 256)
    y_s = _combine(pos_s[0], pos_s[1], tok_s[3], tok_s[0], ada_s, adaf_s, nfin, ys, 32, 8)
    return (y_p, y_s) + st_p + st_s
```

```python
import functools
import math

import jax
import jax.numpy as jnp
from jax import lax
from jax.experimental import pallas as pl
from jax.experimental.pallas import tpu as pltpu

F32 = jnp.float32
BF16 = jnp.bfloat16
HIGHEST = lax.Precision.HIGHEST

EPS = 1e-6
MAX_REAL = -1e-4
HGRN_HEADS = 4
SSM_GROUP = 16
SSM_STATE = 64
SSM_CHUNK = 8
SSM_SETS = 4
MOE_GROUPS = 4
MOE_PER_GROUP = 8
N_EXPERTS = MOE_GROUPS * MOE_PER_GROUP
ROUTER_LANES = 128
EXPERT_LANE0 = 32
ROUTE_ROWS = 8
MOE_TILE = 256
LANES = 128
TOKEN_TILE_ROWS = 8
DMA_UNROLL = 8
VMEM_LIMIT = 56 * 1024 * 1024


def _cparams(sem):
    return pltpu.CompilerParams(dimension_semantics=sem, vmem_limit_bytes=VMEM_LIMIT)


def _silu(x):
    return x * jax.nn.sigmoid(x)


def _rms(x):
    return x * lax.rsqrt(jnp.mean(x * x, axis=-1, keepdims=True) + EPS)


def _dot(a, b):
    return jnp.dot(a, b, preferred_element_type=F32)


def _dot_nt(a, b):
    return lax.dot_general(a, b, (((1,), (1,)), ((), ())), preferred_element_type=F32)


def _dot_tn(a, b, precision=None):
    return lax.dot_general(a, b, (((0,), (0,)), ((), ())), preferred_element_type=F32,
                           precision=precision)


def _silu_linear_kernel(c_ref, w_ref, b_ref, o_ref):
    a = _silu(c_ref[...]).astype(BF16)
    o_ref[...] = _dot(a, w_ref[...].astype(BF16)) + b_ref[...]


def _silu_linear(c, w, b):
    m, d = c.shape
    n = w.shape[1]
    tn = 1024
    return pl.pallas_call(
        _silu_linear_kernel,
        grid=(n // tn,),
        in_specs=[pl.BlockSpec((m, d), lambda j: (0, 0)),
                  pl.BlockSpec((d, tn), lambda j: (0, j)),
                  pl.BlockSpec((1, tn), lambda j: (0, j))],
        out_specs=pl.BlockSpec((m, tn), lambda j: (0, j)),
        out_shape=jax.ShapeDtypeStruct((m, n), F32),
        compiler_params=_cparams(("parallel",)),
        name="silu_linear",
    )(c, w, b.reshape(1, n))


def _inproj_kernel(x_ref, shift_ref, scale_ref, gain_ref, w_ref, lb_ref,
                   q_ref, k_ref, g_ref, v_ref, gs_ref, u_ref, *, dh):
    bb, tt, d = x_ref.shape
    h = _rms(x_ref[...]) * gain_ref[...]
    h = h * (1.0 + scale_ref[...]) + shift_ref[...]
    proj = _dot(h.reshape(bb * tt, d).astype(BF16), w_ref[...])
    lb = lb_ref[...]
    f = lb + (1.0 - lb) * jax.nn.sigmoid(proj[:, dh:2 * dh])
    q_ref[...] = _silu(proj[:, :dh]) * (float(dh // HGRN_HEADS) ** -0.5)
    k_ref[...] = 1.0 - f
    g_ref[...] = jnp.log(f)
    v_ref[...] = proj[:, 2 * dh:3 * dh]
    gs_ref[...] = _silu(proj[:, 3 * dh:4 * dh])
    for s in range(SSM_SETS):
        u_ref[s] = proj[:, 4 * dh + s * LANES:4 * dh + (s + 1) * LANES]


def _inproj(x, ada3, gain, w_in_bf, lb, bb, tt):
    b, t, d = x.shape
    dh = lb.shape[-1]
    nt = t // tt
    rows = bb * tt
    n = b * t
    row_spec = pl.BlockSpec((rows, dh), lambda i, j: (i * nt + j, 0))
    out = jax.ShapeDtypeStruct((n, dh), F32)
    return pl.pallas_call(
        functools.partial(_inproj_kernel, dh=dh),
        grid=(b // bb, nt),
        in_specs=[pl.BlockSpec((bb, tt, d), lambda i, j: (i, j, 0)),
                  pl.BlockSpec((bb, 1, d), lambda i, j: (i, 0, 0)),
                  pl.BlockSpec((bb, 1, d), lambda i, j: (i, 0, 1)),
                  pl.BlockSpec((1, d), lambda i, j: (0, 0)),
                  pl.BlockSpec(w_in_bf.shape, lambda i, j: (0, 0)),
                  pl.BlockSpec((1, dh), lambda i, j: (0, 0))],
        out_specs=[row_spec] * 5 + [pl.BlockSpec((SSM_SETS, rows, LANES), lambda i, j: (0, i * nt + j, 0))],
        out_shape=[out] * 5 + [jax.ShapeDtypeStruct((SSM_SETS, n, LANES), F32)],
        compiler_params=_cparams(("parallel", "parallel")),
        name="inproj",
    )(x, ada3, ada3, gain, w_in_bf, lb)


def _hgrn_kernel(q_ref, k_ref, g_ref, v_ref, gs_ref, s0_ref, hn_ref, o_ref, sf_ref, st_ref,
                 *, chunk, nt):
    j = pl.program_id(1)
    ct, dh = q_ref.shape
    dk = dh // HGRN_HEADS
    c = chunk

    @pl.when(j == 0)
    def _():
        st_ref[...] = s0_ref[0]

    r = lax.broadcasted_iota(jnp.int32, (c, c), 0)
    s = lax.broadcasted_iota(jnp.int32, (c, c), 1)
    causal = r >= s
    tri = causal.astype(F32)
    ones = jnp.ones((c, dk), F32)
    mid = c // 2 - 1

    for ci in range(ct // c):
        rows = slice(ci * c, (ci + 1) * c)
        for h in range(HGRN_HEADS):
            lanes = slice(h * dk, (h + 1) * dk)
            g = g_ref[rows, lanes]
            q = q_ref[rows, lanes]
            k = k_ref[rows, lanes]
            v = v_ref[rows, lanes].astype(BF16)
            a = jnp.dot(tri, g, precision=HIGHEST, preferred_element_type=F32)
            a_mid = a[mid:mid + 1]
            a_end = a[c - 1:c]
            qt = (q * jnp.exp(a - a_mid)).astype(BF16)
            kt = (k * jnp.exp(a_mid - a)).astype(BF16)
            sc = jnp.where(causal, _dot_nt(qt, kt), 0.0).astype(BF16)
            state = st_ref[h]
            o = _dot(sc, v) + _dot((q * jnp.exp(a)).astype(BF16), state.astype(BF16))
            decay = jnp.exp(_dot_tn(g, ones, precision=HIGHEST))
            st_ref[h] = decay * state + _dot_tn((k * jnp.exp(a_end - a)).astype(BF16), v)
            o_ref[rows, lanes] = _rms(o) * hn_ref[:, lanes] * gs_ref[rows, lanes]

    @pl.when(j == nt - 1)
    def _():
        sf_ref[0] = st_ref[...]


def _hgrn(q, k, g, v, gs, s0, hnorm, b, t, ct, chunk):
    n, dh = q.shape
    nt = t // ct
    dk = dh // HGRN_HEADS
    row_spec = pl.BlockSpec((ct, dh), lambda i, j: (i * nt + j, 0))
    st_spec = pl.BlockSpec((1, HGRN_HEADS, dk, dk), lambda i, j: (i, 0, 0, 0))
    return pl.pallas_call(
        functools.partial(_hgrn_kernel, chunk=chunk, nt=nt),
        grid=(b, nt),
        in_specs=[row_spec] * 5 + [st_spec, pl.BlockSpec((1, dh), lambda i, j: (0, 0))],
        out_specs=[row_spec, st_spec],
        out_shape=[jax.ShapeDtypeStruct((n, dh), F32),
                   jax.ShapeDtypeStruct((b, HGRN_HEADS, dk, dk), F32)],
        scratch_shapes=[pltpu.VMEM((HGRN_HEADS, dk, dk), F32)],
        compiler_params=_cparams(("parallel", "arbitrary")),
        name="hgrn",
    )(q, k, g, v, gs, s0, hnorm)


def _s5_prepare(a_re, a_im, log_dt, b_re, b_im, c_re, c_im, d_skip):
    ng, npp = a_re.shape
    nh = b_re.shape[-1]
    L = SSM_CHUNK
    gs = ng // SSM_SETS
    lam_re = jnp.minimum(a_re, MAX_REAL)
    lam_im = a_im
    dt = jnp.exp(log_dt)
    mag = jnp.exp(lam_re * dt)
    ab_re = mag * jnp.cos(lam_im * dt)
    ab_im = mag * jnp.sin(lam_im * dt)
    den = lam_re * lam_re + lam_im * lam_im
    co_re = ((ab_re - 1.0) * lam_re + ab_im * lam_im) / den
    co_im = (ab_im * lam_re - (ab_re - 1.0) * lam_im) / den
    bb_re = co_re[..., None] * b_re - co_im[..., None] * b_im
    bb_im = co_re[..., None] * b_im + co_im[..., None] * b_re
    pw_re = [jnp.ones_like(ab_re)]
    pw_im = [jnp.zeros_like(ab_im)]
    for _ in range(L):
        pr, pi = pw_re[-1], pw_im[-1]
        pw_re.append(pr * ab_re - pi * ab_im)
        pw_im.append(pr * ab_im + pi * ab_re)
    pw_re = jnp.stack(pw_re)
    pw_im = jnp.stack(pw_im)
    ab_b_re = pw_re[:L, :, :, None] * bb_re - pw_im[:L, :, :, None] * bb_im
    ab_b_im = pw_re[:L, :, :, None] * bb_im + pw_im[:L, :, :, None] * bb_re
    kern = (jnp.einsum('gkp,lgph->lghk', c_re, ab_b_re, precision=HIGHEST)
            - jnp.einsum('gkp,lgph->lghk', c_im, ab_b_im, precision=HIGHEST))
    kern = kern.at[0].add(d_skip[:, :, None] * jnp.eye(nh, dtype=F32))
    zero = jnp.zeros_like(kern[0])
    toep = jnp.stack([jnp.stack([kern[t - s] if t >= s else zero for t in range(L)]) for s in range(L)])
    same = jnp.eye(gs, dtype=bool)

    def block_diag(a, axes, shape):
        mask = same.reshape([gs if i in axes else 1 for i in range(len(shape))])
        return jnp.where(mask, jnp.expand_dims(a, axes[1]), 0.0).astype(BF16)

    toep = toep.reshape(L, L, SSM_SETS, gs, nh, nh).transpose(2, 0, 3, 4, 1, 5)
    w_t = block_diag(toep, (2, 5), (SSM_SETS, L, gs, nh, L, gs, nh)).reshape(
        SSM_SETS, L * gs * nh, L * gs * nh)
    inc_re = ab_b_re[::-1].reshape(L, SSM_SETS, gs, npp, nh).transpose(1, 0, 2, 4, 3)
    inc_im = ab_b_im[::-1].reshape(L, SSM_SETS, gs, npp, nh).transpose(1, 0, 2, 4, 3)
    n_shape = (SSM_SETS, L, gs, nh, gs, npp)
    n_re = block_diag(inc_re, (2, 4), n_shape).reshape(SSM_SETS, L * gs * nh, gs * npp)
    n_im = block_diag(inc_im, (2, 4), n_shape).reshape(SSM_SETS, L * gs * nh, gs * npp)
    w1 = jnp.concatenate([w_t, n_re, n_im], axis=-1)
    ca_re = c_re[None] * pw_re[1:, :, None, :] - c_im[None] * pw_im[1:, :, None, :]
    ca_im = c_re[None] * pw_im[1:, :, None, :] + c_im[None] * pw_re[1:, :, None, :]
    ca_re = ca_re.reshape(L, SSM_SETS, gs, nh, npp).transpose(1, 2, 4, 0, 3)
    ca_im = ca_im.reshape(L, SSM_SETS, gs, nh, npp).transpose(1, 2, 4, 0, 3)
    m_shape = (SSM_SETS, gs, npp, L, gs, nh)
    m_re = block_diag(ca_re, (1, 4), m_shape).reshape(SSM_SETS, gs * npp, L * gs * nh)
    m_im = block_diag(-ca_im, (1, 4), m_shape).reshape(SSM_SETS, gs * npp, L * gs * nh)
    m = jnp.concatenate([m_re, m_im], axis=1)
    a8 = jnp.concatenate([pw_re[L].reshape(SSM_SETS, 1, gs * npp),
                          pw_im[L].reshape(SSM_SETS, 1, gs * npp)], axis=-1)
    return w1, m, a8


def _s5_kernel(u_ref, xr_ref, xi_ref, w1_ref, m_ref, a8_ref, y_ref, fr_ref, fi_ref, *, sequential):
    n = u_ref.shape[1] // SSM_CHUNK
    ns = xr_ref.shape[-1]
    ny = SSM_CHUNK * LANES
    u = jnp.concatenate([u_ref[0, pl.ds(s, n, stride=SSM_CHUNK), :] for s in range(SSM_CHUNK)], axis=-1)
    res = _dot(u.astype(BF16), w1_ref[0])
    d_re = res[:, ny:ny + ns]
    d_im = res[:, ny + ns:]
    a_re = a8_ref[0][:, :ns]
    a_im = a8_ref[0][:, ns:]
    x0_re = xr_ref[0]
    x0_im = xi_ref[0]
    if sequential:
        row = lax.broadcasted_iota(jnp.int32, (n, ns), 0)
        first = row == 0
        x_re = d_re + jnp.where(first, a_re * x0_re - a_im * x0_im, 0.0)
        x_im = d_im + jnp.where(first, a_re * x0_im + a_im * x0_re, 0.0)
        p_re, p_im = a_re, a_im
        step = 1
        while step < n:
            s_re = jnp.where(row >= step, pltpu.roll(x_re, step, 0), 0.0)
            s_im = jnp.where(row >= step, pltpu.roll(x_im, step, 0), 0.0)
            x_re, x_im = x_re + p_re * s_re - p_im * s_im, x_im + p_re * s_im + p_im * s_re
            p_re, p_im = p_re * p_re - p_im * p_im, 2.0 * p_re * p_im
            step *= 2
        fr_ref[0] = x_re[n - 1:n]
        fi_ref[0] = x_im[n - 1:n]
        xc_re = jnp.where(first, x0_re, pltpu.roll(x_re, 1, 0))
        xc_im = jnp.where(first, x0_im, pltpu.roll(x_im, 1, 0))
    else:
        xc_re, xc_im = x0_re, x0_im
        fr_ref[0] = a_re * x0_re - a_im * x0_im + d_re
        fi_ref[0] = a_re * x0_im + a_im * x0_re + d_im
    xc = jnp.concatenate([xc_re, xc_im], axis=-1).astype(BF16)
    y = res[:, :ny] + _dot(xc, m_ref[0])
    for t in range(SSM_CHUNK):
        y_ref[0, pl.ds(t, n, stride=SSM_CHUNK), :] = y[:, t * LANES:(t + 1) * LANES]


def _s5(u, x_re, x_im, w1, m, a8, n_tokens, sequential):
    sets = u.shape[0]
    nb, rb, _ = x_re.shape
    ns = m.shape[1] // 2
    st_spec = pl.BlockSpec((1, rb, ns), lambda gi, i: (i, 0, gi))
    st_shape = jax.ShapeDtypeStruct(x_re.shape, F32)
    tok_spec = pl.BlockSpec((1, n_tokens, LANES), lambda gi, i: (gi, i, 0))
    return pl.pallas_call(
        functools.partial(_s5_kernel, sequential=sequential),
        grid=(sets, nb),
        in_specs=[tok_spec, st_spec, st_spec,
                  pl.BlockSpec((1,) + w1.shape[1:], lambda gi, i: (gi, 0, 0)),
                  pl.BlockSpec((1,) + m.shape[1:], lambda gi, i: (gi, 0, 0)),
                  pl.BlockSpec((1, 1, 2 * ns), lambda gi, i: (gi, 0, 0))],
        out_specs=[tok_spec, st_spec, st_spec],
        out_shape=[jax.ShapeDtypeStruct(u.shape, F32), st_shape, st_shape],
        compiler_params=_cparams(("parallel", "parallel")),
        name="s5",
    )(u, x_re, x_im, w1, m, a8)


def _gelu_tanh(x):
    return 0.5 * x * (1.0 + jnp.tanh(math.sqrt(2.0 / math.pi) * (x + 0.044715 * (x * x * x))))


def _outproj_kernel(x_ref, oh_ref, ys_ref, gate_ref, shift_ref, scale_ref, wglu_ref, bglu_ref, sn_ref,
                    wo_h_ref, wo_s_ref, nf_ref, wr_ref, br_ref, cnt_in_ref,
                    x1_ref, h2_ref, route_ref, rw_ref, cnt_ref):
    bb, tt, d = x_ref.shape
    rows = bb * tt
    y = _gelu_tanh(jnp.concatenate([ys_ref[s] for s in range(SSM_SETS)], axis=-1))
    y = y * jax.nn.sigmoid(_dot(y.astype(BF16), wglu_ref[...]) + bglu_ref[...])
    o_s = _rms(y) * sn_ref[...]
    mix = _dot(oh_ref[...].astype(BF16), wo_h_ref[...]) + _dot(o_s.astype(BF16), wo_s_ref[...])
    x1 = x_ref[...] + gate_ref[...] * mix.reshape(bb, tt, d)
    x1_ref[...] = x1
    h2 = _rms(x1) * nf_ref[...]
    h2 = (h2 * (1.0 + scale_ref[...]) + shift_ref[...]).reshape(rows, d)
    _store_token_tiles(h2_ref, h2)

    logits = _dot(h2.astype(BF16), wr_ref[...]) + br_ref[...]
    lane = lax.broadcasted_iota(jnp.int32, logits.shape, 1)
    neg = -jnp.inf
    gl = jnp.where(lane < MOE_GROUPS, logits, neg)
    gmax = jnp.max(gl, axis=-1, keepdims=True)
    gidx = jnp.min(jnp.where(gl == gmax, lane, ROUTER_LANES), axis=-1, keepdims=True)
    grp_w = 1.0 / jnp.sum(jnp.exp(gl - gmax), axis=-1, keepdims=True)
    e0 = EXPERT_LANE0 + gidx * MOE_PER_GROUP
    sel = jnp.where((lane >= e0) & (lane < e0 + MOE_PER_GROUP), logits, neg)
    m1 = jnp.max(sel, axis=-1, keepdims=True)
    i1 = jnp.min(jnp.where(sel == m1, lane, ROUTER_LANES), axis=-1, keepdims=True)
    sel2 = jnp.where(lane == i1, neg, sel)
    m2 = jnp.max(sel2, axis=-1, keepdims=True)
    i2 = jnp.min(jnp.where(sel2 == m2, lane, ROUTER_LANES), axis=-1, keepdims=True)
    e2 = jnp.exp(m2 - m1)
    w1 = 1.0 / (1.0 + e2)
    w2 = e2 / (1.0 + e2)
    rw_ref[...] = grp_w * (jnp.where(lane == 0, w1, 0.0) + jnp.where(lane == 1, w2, 0.0))

    @pl.when((pl.program_id(0) == 0) & (pl.program_id(1) == 0))
    def _():
        cnt_ref[...] = cnt_in_ref[...]

    picked = (lane == i1) | (lane == i2)
    earlier = (lax.broadcasted_iota(jnp.int32, (rows, rows), 0)
               > lax.broadcasted_iota(jnp.int32, (rows, rows), 1))
    base = cnt_ref[...]
    before = _dot(earlier.astype(BF16), picked.astype(BF16)) + base
    rank1 = jnp.sum(jnp.where(lane == i1, before, 0.0), axis=-1, keepdims=True).astype(jnp.int32)
    rank2 = jnp.sum(jnp.where(lane == i2, before, 0.0), axis=-1, keepdims=True).astype(jnp.int32)
    cnt_ref[...] = base + jnp.sum(picked.astype(F32), axis=0, keepdims=True)
    info = jnp.where(lane == 0, i1 - EXPERT_LANE0,
                     jnp.where(lane == 1, i2 - EXPERT_LANE0,
                               jnp.where(lane == 2, rank1, jnp.where(lane == 3, rank2, 0))))
    route_ref[...] = info.T[:ROUTE_ROWS]


def _outproj(x, oh, ys, ada3, wglu_bf, bglu, snorm, wo_h, wo_s, nffn, wr, br, cnt_in, bb, tt):
    b, t, d = x.shape
    n = b * t
    dh = oh.shape[-1]
    nt = t // tt
    rows = bb * tt
    x_spec = pl.BlockSpec((bb, tt, d), lambda i, j: (i, j, 0))
    half_spec = pl.BlockSpec((rows, dh), lambda i, j: (i * nt + j, 0))

    def ada_spec(col):
        return pl.BlockSpec((bb, 1, d), lambda i, j: (i, 0, col))

    def full(a):
        return pl.BlockSpec(a.shape, lambda i, j: (0,) * a.ndim)

    return pl.pallas_call(
        _outproj_kernel,
        grid=(b // bb, nt),
        in_specs=[x_spec, half_spec,
                  pl.BlockSpec((SSM_SETS, rows, LANES), lambda i, j: (0, i * nt + j, 0)),
                  ada_spec(2), ada_spec(3), ada_spec(4),
                  full(wglu_bf), full(bglu), full(snorm), full(wo_h), full(wo_s), full(nffn),
                  full(wr), full(br), full(cnt_in)],
        out_specs=[x_spec,
                   pl.BlockSpec((rows * TOKEN_TILE_ROWS, LANES), lambda i, j: (i * nt + j, 0)),
                   pl.BlockSpec((ROUTE_ROWS, rows), lambda i, j: (0, i * nt + j)),
                   pl.BlockSpec((rows, ROUTER_LANES), lambda i, j: (i * nt + j, 0)),
                   pl.BlockSpec((1, ROUTER_LANES), lambda i, j: (0, 0))],
        out_shape=[jax.ShapeDtypeStruct((b, t, d), F32),
                   jax.ShapeDtypeStruct((n * TOKEN_TILE_ROWS, LANES), F32),
                   jax.ShapeDtypeStruct((ROUTE_ROWS, n), jnp.int32),
                   jax.ShapeDtypeStruct((n, ROUTER_LANES), F32),
                   jax.ShapeDtypeStruct((1, ROUTER_LANES), F32)],
        compiler_params=_cparams(("arbitrary", "arbitrary")),
        name="outproj",
    )(x, oh, ys, ada3, ada3, ada3, wglu_bf, bglu, snorm, wo_h, wo_s, nffn, wr, br, cnt_in)


def _moe_schedule(cnt, n_slots):
    c = cnt[0, EXPERT_LANE0:EXPERT_LANE0 + N_EXPERTS].astype(jnp.int32)
    seg_end = jnp.cumsum(c)
    seg_start = seg_end - c
    first_tile = seg_start // MOE_TILE
    tiles = jnp.where(c > 0, (seg_end - 1) // MOE_TILE - first_tile + 1, 0)
    cum = jnp.cumsum(tiles)
    n_items = cum[-1]
    item = jnp.minimum(jnp.arange(n_slots, dtype=jnp.int32), n_items - 1)
    item_exp = jnp.sum(item[:, None] >= cum[None, :], axis=1).astype(jnp.int32)
    item_tile = first_tile[item_exp] + item - (cum - tiles)[item_exp]
    return (seg_start.astype(jnp.int32), seg_end.astype(jnp.int32), item_tile.astype(jnp.int32), item_exp,
            n_items.reshape(1).astype(jnp.int32))


def _token_tile(ref, t):
    return ref.at[pl.ds(pl.multiple_of(t * TOKEN_TILE_ROWS, TOKEN_TILE_ROWS), TOKEN_TILE_ROWS)]


def _store_token_tiles(ref, x):
    rows = x.shape[0]
    for c in range(TOKEN_TILE_ROWS):
        ref[pl.ds(c, rows, stride=TOKEN_TILE_ROWS), :] = x[:, c * LANES:(c + 1) * LANES]


def _load_token_tiles(ref, rows):
    return [ref[pl.ds(c, rows, stride=TOKEN_TILE_ROWS), :] for c in range(TOKEN_TILE_ROWS)]


def _pair_positions(route, seg):
    return seg[route[0]] + route[2], seg[route[1]] + route[3]


def _scatter_kernel(p1_ref, p2_ref, hp_ref, hs_ref, xs_ref, sem, *, prompt_tiles):
    ts = p1_ref.shape[0]

    def scatter_from(h_ref):
        def row_copies(t):
            src = _token_tile(h_ref, t)
            return [pltpu.make_async_copy(src, _token_tile(xs_ref, p_ref[t]), sem) for p_ref in (p1_ref, p2_ref)]

        def start(t, carry):
            for k, cp in enumerate(row_copies(t)):
                cp.start(priority=k)
            return carry

        def wait(t, carry):
            for cp in row_copies(t):
                cp.wait()
            return carry

        lax.fori_loop(0, ts, start, 0, unroll=DMA_UNROLL)
        lax.fori_loop(0, ts, wait, 0, unroll=DMA_UNROLL)

    @pl.when(pl.program_id(0) < prompt_tiles)
    def _():
        scatter_from(hp_ref)

    @pl.when(pl.program_id(0) >= prompt_tiles)
    def _():
        scatter_from(hs_ref)


def _scatter(pos1, pos2, h_p, h_s, ts):
    n_p = h_p.shape[0] // TOKEN_TILE_ROWS
    n_s = h_s.shape[0] // TOKEN_TILE_ROWS
    tiles_p = n_p // ts
    idx_spec = pl.BlockSpec((ts,), lambda i: (i,), memory_space=pltpu.SMEM)
    return pl.pallas_call(
        functools.partial(_scatter_kernel, prompt_tiles=tiles_p),
        grid=((n_p + n_s) // ts,),
        in_specs=[idx_spec, idx_spec,
                  pl.BlockSpec((ts * TOKEN_TILE_ROWS, LANES), lambda i: (jnp.minimum(i, tiles_p - 1), 0)),
                  pl.BlockSpec((ts * TOKEN_TILE_ROWS, LANES), lambda i: (jnp.maximum(i - tiles_p, 0), 0))],
        out_specs=pl.BlockSpec(memory_space=pltpu.HBM),
        scratch_shapes=[pltpu.SemaphoreType.DMA],
        out_shape=jax.ShapeDtypeStruct((2 * (n_p + n_s) * TOKEN_TILE_ROWS, LANES), F32),
        compiler_params=_cparams(("arbitrary",)),
        name="moe_scatter",
    )(pos1, pos2, h_p, h_s)


def _experts_kernel(lo_ref, hi_ref, tile_ref, exp_ref, items_ref, xs_ref, wg_ref, wu_ref, wd_ref, ys_ref,
                    wg_s, wu_s, wd_s):
    i = pl.program_id(0)
    prev = jnp.maximum(i - 1, 0)
    e = exp_ref[i]

    @pl.when((i == 0) | (e != exp_ref[prev]))
    def _():
        wg_s[...] = wg_ref[0].astype(BF16)
        wu_s[...] = wu_ref[0].astype(BF16)
        wd_s[...] = wd_ref[0].astype(BF16)

    @pl.when(i < items_ref[0])
    def _():
        x = jnp.concatenate(_load_token_tiles(xs_ref, MOE_TILE), axis=-1).astype(BF16)
        act = _silu(_dot(x, wg_s[...])) * _dot(x, wu_s[...])
        out = _dot(act.astype(BF16), wd_s[...])
        row = tile_ref[i] * MOE_TILE + lax.broadcasted_iota(jnp.int32, (MOE_TILE, 1), 0)
        mine = (row >= lo_ref[e]) & (row < hi_ref[e])
        first_visit = (i == 0) | (tile_ref[i] != tile_ref[prev])

        @pl.when(first_visit)
        def _():
            _store_token_tiles(ys_ref, jnp.where(mine, out, 0.0))

        @pl.when(jnp.logical_not(first_visit))
        def _():
            old = jnp.concatenate(_load_token_tiles(ys_ref, MOE_TILE), axis=-1)
            _store_token_tiles(ys_ref, jnp.where(mine, out, old))


def _experts(seg_lo, seg_hi, item_tile, item_exp, n_items, xs, wg, wu, wd):
    n_rows = xs.shape[0] // TOKEN_TILE_ROWS
    _, d, de = wg.shape
    row_spec = pl.BlockSpec((MOE_TILE * TOKEN_TILE_ROWS, LANES), lambda i, lo, hi, tile, ex, items: (tile[i], 0))

    def w_spec(shape):
        return pl.BlockSpec((1,) + shape, lambda i, lo, hi, tile, ex, items: (ex[i], 0, 0))

    return pl.pallas_call(
        _experts_kernel,
        grid_spec=pltpu.PrefetchScalarGridSpec(
            num_scalar_prefetch=5, grid=(item_tile.shape[0],),
            in_specs=[row_spec, w_spec((d, de)), w_spec((d, de)), w_spec((de, d))],
            out_specs=row_spec,
            scratch_shapes=[pltpu.VMEM((d, de), BF16), pltpu.VMEM((d, de), BF16), pltpu.VMEM((de, d), BF16)]),
        out_shape=jax.ShapeDtypeStruct(xs.shape, F32),
        compiler_params=_cparams(("arbitrary",)),
        name="moe_experts",
    )(seg_lo, seg_hi, item_tile, item_exp, n_items, xs, wg, wu, wd)


def _combine_kernel(p1_ref, p2_ref, rw_ref, x1_ref, gate_ref, shift_ref, scale_ref, nfin_ref, ys_ref,
                    y_ref, r1_ref, r2_ref, sem):
    bb, tt, d = x1_ref.shape
    rows = bb * tt

    def row_copies(t):
        return [pltpu.make_async_copy(_token_tile(ys_ref, p_ref[t]), _token_tile(r_ref, t), sem)
                for p_ref, r_ref in ((p1_ref, r1_ref), (p2_ref, r2_ref))]

    def start(t, carry):
        for k, cp in enumerate(row_copies(t)):
            cp.start(priority=k)
        return carry

    def wait(t, carry):
        for cp in row_copies(t):
            cp.wait()
        return carry

    lax.fori_loop(0, rows, start, 0, unroll=DMA_UNROLL)
    lax.fori_loop(0, rows, wait, 0, unroll=DMA_UNROLL)
    rw = rw_ref[...]
    w1 = rw[:, 0:1]
    w2 = rw[:, 1:2]
    moe = jnp.concatenate([w1 * a + w2 * b for a, b in zip(_load_token_tiles(r1_ref, rows),
                                                          _load_token_tiles(r2_ref, rows))], axis=-1)
    x2 = x1_ref[...] + gate_ref[...] * moe.reshape(bb, tt, d)
    hf = _rms(x2) * nfin_ref[...]
    y_ref[...] = hf * (1.0 + scale_ref[...]) + shift_ref[...]


def _combine(pos1, pos2, rw, x1, ada3, adaf3, nfin, ys, bb, tt):
    b, t, d = x1.shape
    nt = t // tt
    rows = bb * tt
    x_spec = pl.BlockSpec((bb, tt, d), lambda i, j: (i, j, 0))
    idx_spec = pl.BlockSpec((rows,), lambda i, j: (i * nt + j,), memory_space=pltpu.SMEM)

    def ada_spec(col):
        return pl.BlockSpec((bb, 1, d), lambda i, j: (i, 0, col))

    return pl.pallas_call(
        _combine_kernel,
        grid=(b // bb, nt),
        in_specs=[idx_spec, idx_spec,
                  pl.BlockSpec((rows, ROUTER_LANES), lambda i, j: (i * nt + j, 0)),
                  x_spec, ada_spec(5), ada_spec(0), ada_spec(1),
                  pl.BlockSpec((1, d), lambda i, j: (0, 0)),
                  pl.BlockSpec(memory_space=pltpu.HBM)],
        out_specs=x_spec,
        scratch_shapes=[pltpu.VMEM((rows * TOKEN_TILE_ROWS, LANES), F32),
                        pltpu.VMEM((rows * TOKEN_TILE_ROWS, LANES), F32),
                        pltpu.SemaphoreType.DMA],
        out_shape=jax.ShapeDtypeStruct((b, t, d), F32),
        compiler_params=_cparams(("arbitrary", "arbitrary")),
        name="moe_combine",
    )(pos1, pos2, rw, x1, ada3, adaf3, adaf3, nfin, ys)


def _mixer(x, ada3, cnt_in, s_h, s_re, s_im, p, *, bb, tt, hgrn_tile, hgrn_chunk, sequential):
    b, t, d = x.shape
    n = b * t
    q, k, g, v, gs, u = _inproj(x, ada3, p['norm_mix'], p['w_in'], p['lb'], bb, tt)
    oh, s_h_new = _hgrn(q, k, g, v, gs, s_h, p['hgrn_norm'], b, t, hgrn_tile, hgrn_chunk)
    if sequential:
        xr, xi = s_re.reshape(b, 1, -1), s_im.reshape(b, 1, -1)
        n_tokens = t
    else:
        xr, xi = s_re.reshape(1, b, -1), s_im.reshape(1, b, -1)
        n_tokens = n
    ys, fr, fi = _s5(u, xr, xi, p['ssm_w1'], p['ssm_m'], p['ssm_a8'], n_tokens, sequential)
    x1, h2, route, rw, cnt = _outproj(x, oh, ys, ada3, p['w_glu'], p['b_glu'], p['ssm_norm'], p['wo_h'],
                                      p['wo_s'], p['norm_ffn'], p['w_rt'], p['b_rt'], cnt_in, bb, tt)
    states = (s_h_new[None], fr.reshape(1, b, s_re.shape[-2], s_re.shape[-1]),
              fi.reshape(1, b, s_re.shape[-2], s_re.shape[-1]))
    return (x1, h2, route, rw), cnt, states


def kernel(x_prompt, x_sample, c_prompt, c_sample, state_hgrn, state_ssm_re, state_ssm_im, hgrn_lb_logits, w_ada, b_ada, norm_mix, w_in, hgrn_norm, ssm_a_re, ssm_a_im, ssm_log_dt, ssm_b_re, ssm_b_im, ssm_c_re, ssm_c_im, ssm_d, ssm_w_glu, ssm_b_glu, ssm_norm, w_out, norm_ffn, moe_w_group, moe_b_group, moe_w_router, moe_b_router, moe_w_gate, moe_w_up, moe_w_down, w_ada_final, b_ada_final, norm_final):
    depth = w_ada.shape[0]
    assert depth == 1
    d = x_prompt.shape[-1]
    bp = x_prompt.shape[0]
    dh = hgrn_norm.shape[-1]
    dk = dh // HGRN_HEADS
    de = moe_w_gate.shape[-1]
    n_exp = MOE_GROUPS * MOE_PER_GROUP

    lb = jax.nn.softmax(hgrn_lb_logits.astype(F32), axis=0)[0].reshape(1, dh)
    w1, m, a8 = _s5_prepare(ssm_a_re[0], ssm_a_im[0], ssm_log_dt[0], ssm_b_re[0], ssm_b_im[0],
                            ssm_c_re[0], ssm_c_im[0], ssm_d[0])
    w_rt = jnp.zeros((d, ROUTER_LANES), F32)
    w_rt = w_rt.at[:, :MOE_GROUPS].set(moe_w_group[0])
    w_rt = w_rt.at[:, EXPERT_LANE0:EXPERT_LANE0 + n_exp].set(
        moe_w_router[0].transpose(1, 0, 2).reshape(d, n_exp))
    b_rt = jnp.zeros((1, ROUTER_LANES), F32)
    b_rt = b_rt.at[0, :MOE_GROUPS].set(moe_b_group[0])
    b_rt = b_rt.at[0, EXPERT_LANE0:EXPERT_LANE0 + n_exp].set(moe_b_router[0].reshape(n_exp))
    p = dict(
        lb=lb, norm_mix=norm_mix[0].reshape(1, d), w_in=w_in[0].astype(BF16),
        hgrn_norm=hgrn_norm[0].reshape(1, dh),
        ssm_w1=w1, ssm_m=m, ssm_a8=a8,
        w_glu=ssm_w_glu[0].astype(BF16), b_glu=ssm_b_glu[0].reshape(1, -1), ssm_norm=ssm_norm[0].reshape(1, -1),
        wo_h=w_out[0, :dh].astype(BF16), wo_s=w_out[0, dh:].astype(BF16),
        norm_ffn=norm_ffn[0].reshape(1, d), w_rt=w_rt.astype(BF16), b_rt=b_rt,
    )
    wg = moe_w_gate[0].reshape(n_exp, d, de)
    wu = moe_w_up[0].reshape(n_exp, d, de)
    wd = moe_w_down[0].reshape(n_exp, de, d)
    nfin = norm_final.reshape(1, d)

    c_all = jnp.concatenate([c_prompt, c_sample], axis=0)
    ada = _silu_linear(c_all, w_ada[0], b_ada[0])
    adaf = _silu_linear(c_all, w_ada_final, b_ada_final)
    ada_p, ada_s = ada[:bp].reshape(bp, 1, -1), ada[bp:].reshape(x_sample.shape[0], 1, -1)
    adaf_p, adaf_s = adaf[:bp].reshape(bp, 1, -1), adaf[bp:].reshape(x_sample.shape[0], 1, -1)

    zeros_h = jnp.zeros((bp, HGRN_HEADS, dk, dk), F32)
    zeros_s = jnp.zeros((bp,) + state_ssm_re.shape[2:], F32)
    cnt0 = jnp.zeros((1, ROUTER_LANES), F32)
    tok_p, cnt_p, st_p = _mixer(x_prompt, ada_p, cnt0, zeros_h, zeros_s, zeros_s, p,
                                bb=1, tt=256, hgrn_tile=256, hgrn_chunk=64, sequential=True)
    tok_s, cnt, st_s = _mixer(x_sample, ada_s, cnt_p, state_hgrn[0], state_ssm_re[0], state_ssm_im[0], p,
                              bb=32, tt=8, hgrn_tile=8, hgrn_chunk=8, sequential=False)

    n_pairs = 2 * (tok_p[1].shape[0] + tok_s[1].shape[0])
    n_slots = n_pairs // MOE_TILE + N_EXPERTS
    seg, seg_end, item_tile, item_exp, n_items = _moe_schedule(cnt, n_slots)
    pos_p = _pair_positions(tok_p[2], seg)
    pos_s = _pair_positions(tok_s[2], seg)
    xs = _scatter(jnp.concatenate([pos_p[0], pos_s[0]]), jnp.concatenate([pos_p[1], pos_s[1]]),
                  tok_p[1], tok_s[1], 256)
    ys = _experts(seg, seg_end, item_tile, item_exp, n_items, xs, wg, wu, wd)
    y_p = _combine(pos_p[0], pos_p[1], tok_p[3], tok_p[0], ada_p, adaf_p, nfin, ys, 1, 256)
    y_s = _combine(pos_s[0], pos_s[1], tok_s[3], tok_s[0], ada_s, adaf_s, nfin, ys, 32, 8)
    return (y_p, y_s) + st_p + st_s
```

```python
import functools
import math

import jax
import jax.numpy as jnp
from jax import lax
from jax.experimental import pallas as pl
from jax.experimental.pallas import tpu as pltpu

F32 = jnp.float32
BF16 = jnp.bfloat16
HIGHEST = lax.Precision.HIGHEST

EPS = 1e-6
MAX_REAL = -1e-4
HGRN_HEADS = 4
SSM_GROUP = 16
SSM_STATE = 64
SSM_CHUNK = 8
SSM_SETS = 4
MOE_GROUPS = 4
MOE_PER_GROUP = 8
N_EXPERTS = MOE_GROUPS * MOE_PER_GROUP
ROUTER_LANES = 128
EXPERT_LANE0 = 32
RANK_BITS = 20
MOE_TILE = 256
LANES = 128
TOKEN_TILE_ROWS = 8
DMA_UNROLL = 8
VMEM_LIMIT = 56 * 1024 * 1024


def _cparams(sem):
    return pltpu.CompilerParams(dimension_semantics=sem, vmem_limit_bytes=VMEM_LIMIT)


def _silu(x):
    return x * jax.nn.sigmoid(x)


def _rms(x):
    return x * lax.rsqrt(jnp.mean(x * x, axis=-1, keepdims=True) + EPS)


def _dot(a, b):
    return jnp.dot(a, b, preferred_element_type=F32)


def _dot_nt(a, b):
    return lax.dot_general(a, b, (((1,), (1,)), ((), ())), preferred_element_type=F32)


def _dot_tn(a, b, precision=None):
    return lax.dot_general(a, b, (((0,), (0,)), ((), ())), preferred_element_type=F32,
                           precision=precision)


def _silu_linear_kernel(c_ref, w_ref, b_ref, o_ref):
    a = _silu(c_ref[...]).astype(BF16)
    o_ref[...] = _dot(a, w_ref[...].astype(BF16)) + b_ref[...]


def _silu_linear(c, w, b):
    m, d = c.shape
    n = w.shape[1]
    tn = 1024
    return pl.pallas_call(
        _silu_linear_kernel,
        grid=(n // tn,),
        in_specs=[pl.BlockSpec((m, d), lambda j: (0, 0)),
                  pl.BlockSpec((d, tn), lambda j: (0, j)),
                  pl.BlockSpec((1, tn), lambda j: (0, j))],
        out_specs=pl.BlockSpec((m, tn), lambda j: (0, j)),
        out_shape=jax.ShapeDtypeStruct((m, n), F32),
        compiler_params=_cparams(("parallel",)),
        name="silu_linear",
    )(c, w, b.reshape(1, n))


def _inproj_kernel(x_ref, shift_ref, scale_ref, gain_ref, w_ref, lb_ref,
                   q_ref, k_ref, g_ref, v_ref, gs_ref, u_ref, *, dh):
    bb, tt, d = x_ref.shape
    h = _rms(x_ref[...]) * gain_ref[...]
    h = h * (1.0 + scale_ref[...]) + shift_ref[...]
    proj = _dot(h.reshape(bb * tt, d).astype(BF16), w_ref[...])
    lb = lb_ref[...]
    f = lb + (1.0 - lb) * jax.nn.sigmoid(proj[:, dh:2 * dh])
    q_ref[...] = _silu(proj[:, :dh]) * (float(dh // HGRN_HEADS) ** -0.5)
    k_ref[...] = 1.0 - f
    g_ref[...] = jnp.log(f)
    v_ref[...] = proj[:, 2 * dh:3 * dh]
    gs_ref[...] = _silu(proj[:, 3 * dh:4 * dh])
    for s in range(SSM_SETS):
        u_ref[s] = proj[:, 4 * dh + s * LANES:4 * dh + (s + 1) * LANES]


def _inproj(x, ada3, gain, w_in_bf, lb, bb, tt):
    b, t, d = x.shape
    dh = lb.shape[-1]
    nt = t // tt
    rows = bb * tt
    n = b * t
    row_spec = pl.BlockSpec((rows, dh), lambda i, j: (i * nt + j, 0))
    out = jax.ShapeDtypeStruct((n, dh), F32)
    return pl.pallas_call(
        functools.partial(_inproj_kernel, dh=dh),
        grid=(b // bb, nt),
        in_specs=[pl.BlockSpec((bb, tt, d), lambda i, j: (i, j, 0)),
                  pl.BlockSpec((bb, 1, d), lambda i, j: (i, 0, 0)),
                  pl.BlockSpec((bb, 1, d), lambda i, j: (i, 0, 1)),
                  pl.BlockSpec((1, d), lambda i, j: (0, 0)),
                  pl.BlockSpec(w_in_bf.shape, lambda i, j: (0, 0)),
                  pl.BlockSpec((1, dh), lambda i, j: (0, 0))],
        out_specs=[row_spec] * 5 + [pl.BlockSpec((SSM_SETS, rows, LANES), lambda i, j: (0, i * nt + j, 0))],
        out_shape=[out] * 5 + [jax.ShapeDtypeStruct((SSM_SETS, n, LANES), F32)],
        compiler_params=_cparams(("parallel", "parallel")),
        name="inproj",
    )(x, ada3, ada3, gain, w_in_bf, lb)


def _hgrn_kernel(q_ref, k_ref, g_ref, v_ref, gs_ref, s0_ref, hn_ref, o_ref, sf_ref, st_ref,
                 *, chunk, nt):
    j = pl.program_id(1)
    ct, dh = q_ref.shape
    dk = dh // HGRN_HEADS
    c = chunk

    @pl.when(j == 0)
    def _():
        st_ref[...] = s0_ref[0]

    r = lax.broadcasted_iota(jnp.int32, (c, c), 0)
    s = lax.broadcasted_iota(jnp.int32, (c, c), 1)
    causal = r >= s
    tri = causal.astype(F32)
    ones = jnp.ones((c, dk), F32)
    mid = c // 2 - 1

    for ci in range(ct // c):
        rows = slice(ci * c, (ci + 1) * c)
        for h in range(HGRN_HEADS):
            lanes = slice(h * dk, (h + 1) * dk)
            g = g_ref[rows, lanes]
            q = q_ref[rows, lanes]
            k = k_ref[rows, lanes]
            v = v_ref[rows, lanes].astype(BF16)
            a = jnp.dot(tri, g, precision=HIGHEST, preferred_element_type=F32)
            a_mid = a[mid:mid + 1]
            a_end = a[c - 1:c]
            qt = (q * jnp.exp(a - a_mid)).astype(BF16)
            kt = (k * jnp.exp(a_mid - a)).astype(BF16)
            sc = jnp.where(causal, _dot_nt(qt, kt), 0.0).astype(BF16)
            state = st_ref[h]
            o = _dot(sc, v) + _dot((q * jnp.exp(a)).astype(BF16), state.astype(BF16))
            decay = jnp.exp(_dot_tn(g, ones, precision=HIGHEST))
            st_ref[h] = decay * state + _dot_tn((k * jnp.exp(a_end - a)).astype(BF16), v)
            o_ref[rows, lanes] = _rms(o) * hn_ref[:, lanes] * gs_ref[rows, lanes]

    @pl.when(j == nt - 1)
    def _():
        sf_ref[0] = st_ref[...]


def _hgrn(q, k, g, v, gs, s0, hnorm, b, t, ct, chunk):
    n, dh = q.shape
    nt = t // ct
    dk = dh // HGRN_HEADS
    row_spec = pl.BlockSpec((ct, dh), lambda i, j: (i * nt + j, 0))
    st_spec = pl.BlockSpec((1, HGRN_HEADS, dk, dk), lambda i, j: (i, 0, 0, 0))
    return pl.pallas_call(
        functools.partial(_hgrn_kernel, chunk=chunk, nt=nt),
        grid=(b, nt),
        in_specs=[row_spec] * 5 + [st_spec, pl.BlockSpec((1, dh), lambda i, j: (0, 0))],
        out_specs=[row_spec, st_spec],
        out_shape=[jax.ShapeDtypeStruct((n, dh), F32),
                   jax.ShapeDtypeStruct((b, HGRN_HEADS, dk, dk), F32)],
        scratch_shapes=[pltpu.VMEM((HGRN_HEADS, dk, dk), F32)],
        compiler_params=_cparams(("parallel", "arbitrary")),
        name="hgrn",
    )(q, k, g, v, gs, s0, hnorm)


def _s5_prepare(a_re, a_im, log_dt, b_re, b_im, c_re, c_im, d_skip):
    ng, npp = a_re.shape
    nh = b_re.shape[-1]
    L = SSM_CHUNK
    gs = ng // SSM_SETS
    lam_re = jnp.minimum(a_re, MAX_REAL)
    lam_im = a_im
    dt = jnp.exp(log_dt)
    mag = jnp.exp(lam_re * dt)
    ab_re = mag * jnp.cos(lam_im * dt)
    ab_im = mag * jnp.sin(lam_im * dt)
    den = lam_re * lam_re + lam_im * lam_im
    co_re = ((ab_re - 1.0) * lam_re + ab_im * lam_im) / den
    co_im = (ab_im * lam_re - (ab_re - 1.0) * lam_im) / den
    bb_re = co_re[..., None] * b_re - co_im[..., None] * b_im
    bb_im = co_re[..., None] * b_im + co_im[..., None] * b_re
    pw_re = [jnp.ones_like(ab_re)]
    pw_im = [jnp.zeros_like(ab_im)]
    for _ in range(L):
        pr, pi = pw_re[-1], pw_im[-1]
        pw_re.append(pr * ab_re - pi * ab_im)
        pw_im.append(pr * ab_im + pi * ab_re)
    pw_re = jnp.stack(pw_re)
    pw_im = jnp.stack(pw_im)
    ab_b_re = pw_re[:L, :, :, None] * bb_re - pw_im[:L, :, :, None] * bb_im
    ab_b_im = pw_re[:L, :, :, None] * bb_im + pw_im[:L, :, :, None] * bb_re
    kern = (jnp.einsum('gkp,lgph->lghk', c_re, ab_b_re, precision=HIGHEST)
            - jnp.einsum('gkp,lgph->lghk', c_im, ab_b_im, precision=HIGHEST))
    kern = kern.at[0].add(d_skip[:, :, None] * jnp.eye(nh, dtype=F32))
    def group_block_diag(c):
        rows, w = c.shape[-2:]
        tiled = jnp.concatenate([c] * gs, axis=-1)
        rg = jnp.arange(rows)[:, None] // (rows // gs)
        cq = jnp.arange(gs * w)[None, :] // w
        return jnp.where(rg == cq, tiled, 0.0).astype(BF16).transpose(1, 0, 2, 3).reshape(
            SSM_SETS, L * rows, gs * w)

    bd = group_block_diag(kern.reshape(L, SSM_SETS, gs * nh, nh)).reshape(SSM_SETS, L, gs * nh, gs * nh)
    zero = jnp.zeros_like(bd[:, 0])
    w_t = jnp.concatenate([jnp.concatenate([bd[:, t - s] if t >= s else zero for t in range(L)], axis=-1)
                           for s in range(L)], axis=-2)
    n_re = group_block_diag(ab_b_re[::-1].transpose(0, 1, 3, 2).reshape(L, SSM_SETS, gs * nh, npp))
    n_im = group_block_diag(ab_b_im[::-1].transpose(0, 1, 3, 2).reshape(L, SSM_SETS, gs * nh, npp))
    w1 = jnp.concatenate([w_t, n_re, n_im], axis=-1)
    ca_re = c_re[None] * pw_re[1:, :, None, :] - c_im[None] * pw_im[1:, :, None, :]
    ca_im = c_re[None] * pw_im[1:, :, None, :] + c_im[None] * pw_re[1:, :, None, :]
    m = jnp.concatenate([group_block_diag(ca_re.reshape(L, SSM_SETS, gs * nh, npp)),
                         group_block_diag(-ca_im.reshape(L, SSM_SETS, gs * nh, npp))], axis=-1)
    a8 = jnp.concatenate([pw_re[L].reshape(SSM_SETS, 1, gs * npp),
                          pw_im[L].reshape(SSM_SETS, 1, gs * npp)], axis=-1)
    return w1, m, a8


def _s5_kernel(u_ref, xr_ref, xi_ref, w1_ref, m_ref, a8_ref, y_ref, fr_ref, fi_ref, *, sequential):
    n = u_ref.shape[1] // SSM_CHUNK
    ns = xr_ref.shape[-1]
    ny = SSM_CHUNK * LANES
    u = jnp.concatenate([u_ref[0, pl.ds(s, n, stride=SSM_CHUNK), :] for s in range(SSM_CHUNK)], axis=-1)
    res = _dot(u.astype(BF16), w1_ref[0])
    d_re = res[:, ny:ny + ns]
    d_im = res[:, ny + ns:]
    a_re = a8_ref[0][:, :ns]
    a_im = a8_ref[0][:, ns:]
    x0_re = xr_ref[0]
    x0_im = xi_ref[0]
    if sequential:
        row = lax.broadcasted_iota(jnp.int32, (n, ns), 0)
        first = row == 0
        x_re = d_re + jnp.where(first, a_re * x0_re - a_im * x0_im, 0.0)
        x_im = d_im + jnp.where(first, a_re * x0_im + a_im * x0_re, 0.0)
        p_re, p_im = a_re, a_im
        step = 1
        while step < n:
            s_re = jnp.where(row >= step, pltpu.roll(x_re, step, 0), 0.0)
            s_im = jnp.where(row >= step, pltpu.roll(x_im, step, 0), 0.0)
            x_re, x_im = x_re + p_re * s_re - p_im * s_im, x_im + p_re * s_im + p_im * s_re
            p_re, p_im = p_re * p_re - p_im * p_im, 2.0 * p_re * p_im
            step *= 2
        fr_ref[0] = x_re[n - 1:n]
        fi_ref[0] = x_im[n - 1:n]
        xc_re = jnp.where(first, x0_re, pltpu.roll(x_re, 1, 0))
        xc_im = jnp.where(first, x0_im, pltpu.roll(x_im, 1, 0))
    else:
        xc_re, xc_im = x0_re, x0_im
        fr_ref[0] = a_re * x0_re - a_im * x0_im + d_re
        fi_ref[0] = a_re * x0_im + a_im * x0_re + d_im
    xc = jnp.concatenate([xc_re, xc_im], axis=-1).astype(BF16)
    y = res[:, :ny] + _dot_nt(xc, m_ref[0])
    for t in range(SSM_CHUNK):
        y_ref[0, pl.ds(t, n, stride=SSM_CHUNK), :] = y[:, t * LANES:(t + 1) * LANES]


def _s5(u, x_re, x_im, w1, m, a8, n_tokens, sequential):
    sets = u.shape[0]
    nb, rb, _ = x_re.shape
    ns = m.shape[1] // 2
    st_spec = pl.BlockSpec((1, rb, ns), lambda gi, i: (i, 0, gi))
    st_shape = jax.ShapeDtypeStruct(x_re.shape, F32)
    tok_spec = pl.BlockSpec((1, n_tokens, LANES), lambda gi, i: (gi, i, 0))
    return pl.pallas_call(
        functools.partial(_s5_kernel, sequential=sequential),
        grid=(sets, nb),
        in_specs=[tok_spec, st_spec, st_spec,
                  pl.BlockSpec((1,) + w1.shape[1:], lambda gi, i: (gi, 0, 0)),
                  pl.BlockSpec((1,) + m.shape[1:], lambda gi, i: (gi, 0, 0)),
                  pl.BlockSpec((1, 1, 2 * ns), lambda gi, i: (gi, 0, 0))],
        out_specs=[tok_spec, st_spec, st_spec],
        out_shape=[jax.ShapeDtypeStruct(u.shape, F32), st_shape, st_shape],
        compiler_params=_cparams(("parallel", "parallel")),
        name="s5",
    )(u, x_re, x_im, w1, m, a8)


def _gelu_tanh(x):
    return 0.5 * x * (1.0 + jnp.tanh(math.sqrt(2.0 / math.pi) * (x + 0.044715 * (x * x * x))))


def _outproj_kernel(x_ref, oh_ref, ys_ref, gate_ref, shift_ref, scale_ref, wglu_ref, bglu_ref, sn_ref,
                    wo_h_ref, wo_s_ref, nf_ref, wr_ref, br_ref, cnt_in_ref,
                    x1_ref, h2_ref, pair1_ref, pair2_ref, rw_ref, cnt_ref):
    bb, tt, d = x_ref.shape
    rows = bb * tt
    y = _gelu_tanh(jnp.concatenate([ys_ref[s] for s in range(SSM_SETS)], axis=-1))
    y = y * jax.nn.sigmoid(_dot(y.astype(BF16), wglu_ref[...]) + bglu_ref[...])
    o_s = _rms(y) * sn_ref[...]
    mix = _dot(oh_ref[...].astype(BF16), wo_h_ref[...]) + _dot(o_s.astype(BF16), wo_s_ref[...])
    x1 = x_ref[...] + gate_ref[...] * mix.reshape(bb, tt, d)
    x1_ref[...] = x1
    h2 = _rms(x1) * nf_ref[...]
    h2 = (h2 * (1.0 + scale_ref[...]) + shift_ref[...]).reshape(rows, d)
    _store_token_tiles(h2_ref, h2)

    logits = _dot(h2.astype(BF16), wr_ref[...]) + br_ref[...]
    lane = lax.broadcasted_iota(jnp.int32, logits.shape, 1)
    neg = -jnp.inf
    gl = jnp.where(lane < MOE_GROUPS, logits, neg)
    gmax = jnp.max(gl, axis=-1, keepdims=True)
    gidx = jnp.min(jnp.where(gl == gmax, lane, ROUTER_LANES), axis=-1, keepdims=True)
    grp_w = 1.0 / jnp.sum(jnp.exp(gl - gmax), axis=-1, keepdims=True)
    e0 = EXPERT_LANE0 + gidx * MOE_PER_GROUP
    sel = jnp.where((lane >= e0) & (lane < e0 + MOE_PER_GROUP), logits, neg)
    m1 = jnp.max(sel, axis=-1, keepdims=True)
    i1 = jnp.min(jnp.where(sel == m1, lane, ROUTER_LANES), axis=-1, keepdims=True)
    sel2 = jnp.where(lane == i1, neg, sel)
    m2 = jnp.max(sel2, axis=-1, keepdims=True)
    i2 = jnp.min(jnp.where(sel2 == m2, lane, ROUTER_LANES), axis=-1, keepdims=True)
    e2 = jnp.exp(m2 - m1)
    w1 = 1.0 / (1.0 + e2)
    w2 = e2 / (1.0 + e2)
    rw_ref[...] = grp_w * (jnp.where(lane == 0, w1, 0.0) + jnp.where(lane == 1, w2, 0.0))

    @pl.when((pl.program_id(0) == 0) & (pl.program_id(1) == 0))
    def _():
        cnt_ref[...] = cnt_in_ref[...]

    picked = (lane == i1) | (lane == i2)
    earlier = (lax.broadcasted_iota(jnp.int32, (rows, rows), 0)
               > lax.broadcasted_iota(jnp.int32, (rows, rows), 1))
    base = cnt_ref[...]
    before = _dot(earlier.astype(BF16), picked.astype(BF16)) + base
    rank1 = jnp.sum(jnp.where(lane == i1, before, 0.0), axis=-1, keepdims=True).astype(jnp.int32)
    rank2 = jnp.sum(jnp.where(lane == i2, before, 0.0), axis=-1, keepdims=True).astype(jnp.int32)
    cnt_ref[...] = base + jnp.sum(picked.astype(F32), axis=0, keepdims=True)
    info = jnp.where(lane == 0, ((i1 - EXPERT_LANE0) << RANK_BITS) | rank1,
                     jnp.where(lane == 1, ((i2 - EXPERT_LANE0) << RANK_BITS) | rank2, 0)).T
    pair1_ref[...] = info[0]
    pair2_ref[...] = info[1]


def _outproj(x, oh, ys, ada3, wglu_bf, bglu, snorm, wo_h, wo_s, nffn, wr, br, cnt_in, bb, tt):
    b, t, d = x.shape
    n = b * t
    dh = oh.shape[-1]
    nt = t // tt
    rows = bb * tt
    x_spec = pl.BlockSpec((bb, tt, d), lambda i, j: (i, j, 0))
    half_spec = pl.BlockSpec((rows, dh), lambda i, j: (i * nt + j, 0))

    def ada_spec(col):
        return pl.BlockSpec((bb, 1, d), lambda i, j: (i, 0, col))

    def full(a):
        return pl.BlockSpec(a.shape, lambda i, j: (0,) * a.ndim)

    return pl.pallas_call(
        _outproj_kernel,
        grid=(b // bb, nt),
        in_specs=[x_spec, half_spec,
                  pl.BlockSpec((SSM_SETS, rows, LANES), lambda i, j: (0, i * nt + j, 0)),
                  ada_spec(2), ada_spec(3), ada_spec(4),
                  full(wglu_bf), full(bglu), full(snorm), full(wo_h), full(wo_s), full(nffn),
                  full(wr), full(br), full(cnt_in)],
        out_specs=[x_spec,
                   pl.BlockSpec((rows * TOKEN_TILE_ROWS, LANES), lambda i, j: (i * nt + j, 0)),
                   pl.BlockSpec((rows,), lambda i, j: (i * nt + j,)),
                   pl.BlockSpec((rows,), lambda i, j: (i * nt + j,)),
                   pl.BlockSpec((rows, ROUTER_LANES), lambda i, j: (i * nt + j, 0)),
                   pl.BlockSpec((1, ROUTER_LANES), lambda i, j: (0, 0))],
        out_shape=[jax.ShapeDtypeStruct((b, t, d), F32),
                   jax.ShapeDtypeStruct((n * TOKEN_TILE_ROWS, LANES), F32),
                   jax.ShapeDtypeStruct((n,), jnp.int32),
                   jax.ShapeDtypeStruct((n,), jnp.int32),
                   jax.ShapeDtypeStruct((n, ROUTER_LANES), F32),
                   jax.ShapeDtypeStruct((1, ROUTER_LANES), F32)],
        compiler_params=_cparams(("arbitrary", "arbitrary")),
        name="outproj",
    )(x, oh, ys, ada3, ada3, ada3, wglu_bf, bglu, snorm, wo_h, wo_s, nffn, wr, br, cnt_in)


def _moe_schedule(cnt, n_slots):
    c = cnt[0, EXPERT_LANE0:EXPERT_LANE0 + N_EXPERTS].astype(jnp.int32)
    seg_end = jnp.cumsum(c)
    seg_start = seg_end - c
    first_tile = seg_start // MOE_TILE
    tiles = jnp.where(c > 0, (seg_end - 1) // MOE_TILE - first_tile + 1, 0)
    cum = jnp.cumsum(tiles)
    n_items = cum[-1]
    item = jnp.minimum(jnp.arange(n_slots, dtype=jnp.int32), n_items - 1)
    item_exp = jnp.sum(item[:, None] >= cum[None, :], axis=1).astype(jnp.int32)
    shares = ((c > 0) & (seg_start % MOE_TILE != 0)).astype(jnp.int32)
    item_tile = item - jnp.sum((item[:, None] >= (cum - tiles)[None, :]) * shares[None, :], axis=1)
    return (seg_start.astype(jnp.int32), seg_end.astype(jnp.int32), item_tile.astype(jnp.int32), item_exp,
            n_items.reshape(1).astype(jnp.int32))


def _token_tile(ref, t):
    return ref.at[pl.ds(pl.multiple_of(t * TOKEN_TILE_ROWS, TOKEN_TILE_ROWS), TOKEN_TILE_ROWS)]


def _store_token_tiles(ref, x):
    rows = x.shape[0]
    for c in range(TOKEN_TILE_ROWS):
        ref[pl.ds(c, rows, stride=TOKEN_TILE_ROWS), :] = x[:, c * LANES:(c + 1) * LANES]


def _load_token_tiles(ref, rows):
    return [ref[pl.ds(c, rows, stride=TOKEN_TILE_ROWS), :] for c in range(TOKEN_TILE_ROWS)]


def _pair_row(seg_ref, pair):
    return seg_ref[pair >> RANK_BITS] + (pair & ((1 << RANK_BITS) - 1))


def _scatter_kernel(seg_ref, pp1_ref, pp2_ref, ps1_ref, ps2_ref, hp_ref, hs_ref, xs_ref, sem, *, prompt_tiles):
    ts = pp1_ref.shape[0]

    def scatter_from(h_ref, pair_refs):
        def row_copies(t):
            src = _token_tile(h_ref, t)
            return [pltpu.make_async_copy(src, _token_tile(xs_ref, _pair_row(seg_ref, p_ref[t])), sem)
                    for p_ref in pair_refs]

        def start(t, carry):
            for k, cp in enumerate(row_copies(t)):
                cp.start(priority=k)
            return carry

        def wait(t, carry):
            for cp in row_copies(t):
                cp.wait()
            return carry

        lax.fori_loop(0, ts, start, 0, unroll=DMA_UNROLL)
        lax.fori_loop(0, ts, wait, 0, unroll=DMA_UNROLL)

    @pl.when(pl.program_id(0) < prompt_tiles)
    def _():
        scatter_from(hp_ref, (pp1_ref, pp2_ref))

    @pl.when(pl.program_id(0) >= prompt_tiles)
    def _():
        scatter_from(hs_ref, (ps1_ref, ps2_ref))


def _scatter(seg, pairs_p, pairs_s, h_p, h_s, ts):
    n_p = h_p.shape[0] // TOKEN_TILE_ROWS
    n_s = h_s.shape[0] // TOKEN_TILE_ROWS
    tiles_p = n_p // ts

    def prompt_block(i):
        return jnp.minimum(i, tiles_p - 1)

    def sample_block(i):
        return jnp.maximum(i - tiles_p, 0)

    def idx_spec(block):
        return pl.BlockSpec((ts,), lambda i: (block(i),), memory_space=pltpu.SMEM)

    def row_spec(block):
        return pl.BlockSpec((ts * TOKEN_TILE_ROWS, LANES), lambda i: (block(i), 0))

    return pl.pallas_call(
        functools.partial(_scatter_kernel, prompt_tiles=tiles_p),
        grid=((n_p + n_s) // ts,),
        in_specs=[pl.BlockSpec(memory_space=pltpu.SMEM),
                  idx_spec(prompt_block), idx_spec(prompt_block), idx_spec(sample_block), idx_spec(sample_block),
                  row_spec(prompt_block), row_spec(sample_block)],
        out_specs=pl.BlockSpec(memory_space=pltpu.HBM),
        scratch_shapes=[pltpu.SemaphoreType.DMA],
        out_shape=jax.ShapeDtypeStruct((2 * (n_p + n_s) * TOKEN_TILE_ROWS, LANES), F32),
        compiler_params=_cparams(("arbitrary",)),
        name="moe_scatter",
    )(seg, *pairs_p, *pairs_s, h_p, h_s)


def _experts_kernel(lo_ref, hi_ref, tile_ref, exp_ref, items_ref, xs_ref, wg_ref, wu_ref, wd_ref, ys_ref,
                    wg_s, wu_s, wd_s):
    i = pl.program_id(0)
    prev = jnp.maximum(i - 1, 0)
    e = exp_ref[i]

    @pl.when((i == 0) | (e != exp_ref[prev]))
    def _():
        wg_s[...] = wg_ref[0].astype(BF16)
        wu_s[...] = wu_ref[0].astype(BF16)
        wd_s[...] = wd_ref[0].astype(BF16)

    @pl.when(i < items_ref[0])
    def _():
        x = jnp.concatenate(_load_token_tiles(xs_ref, MOE_TILE), axis=-1).astype(BF16)
        act = _silu(_dot(x, wg_s[...])) * _dot(x, wu_s[...])
        out = _dot(act.astype(BF16), wd_s[...])
        row = tile_ref[i] * MOE_TILE + lax.broadcasted_iota(jnp.int32, (MOE_TILE, 1), 0)
        mine = (row >= lo_ref[e]) & (row < hi_ref[e])
        first_visit = (i == 0) | (tile_ref[i] != tile_ref[prev])

        @pl.when(first_visit)
        def _():
            _store_token_tiles(ys_ref, jnp.where(mine, out, 0.0))

        @pl.when(jnp.logical_not(first_visit))
        def _():
            old = jnp.concatenate(_load_token_tiles(ys_ref, MOE_TILE), axis=-1)
            _store_token_tiles(ys_ref, jnp.where(mine, out, old))


def _experts(seg_lo, seg_hi, item_tile, item_exp, n_items, xs, wg, wu, wd):
    n_rows = xs.shape[0] // TOKEN_TILE_ROWS
    _, d, de = wg.shape
    row_spec = pl.BlockSpec((MOE_TILE * TOKEN_TILE_ROWS, LANES), lambda i, lo, hi, tile, ex, items: (tile[i], 0))

    def w_spec(shape):
        return pl.BlockSpec((1,) + shape, lambda i, lo, hi, tile, ex, items: (ex[i], 0, 0))

    return pl.pallas_call(
        _experts_kernel,
        grid_spec=pltpu.PrefetchScalarGridSpec(
            num_scalar_prefetch=5, grid=(item_tile.shape[0],),
            in_specs=[row_spec, w_spec((d, de)), w_spec((d, de)), w_spec((de, d))],
            out_specs=row_spec,
            scratch_shapes=[pltpu.VMEM((d, de), BF16), pltpu.VMEM((d, de), BF16), pltpu.VMEM((de, d), BF16)]),
        out_shape=jax.ShapeDtypeStruct(xs.shape, F32),
        compiler_params=_cparams(("arbitrary",)),
        name="moe_experts",
    )(seg_lo, seg_hi, item_tile, item_exp, n_items, xs, wg, wu, wd)


def _combine_kernel(seg_ref, p1_ref, p2_ref, rw_ref, x1_ref, gate_ref, shift_ref, scale_ref, nfin_ref, ys_ref,
                    y_ref, r1_ref, r2_ref, sem):
    bb, tt, d = x1_ref.shape
    rows = bb * tt

    def row_copies(t):
        return [pltpu.make_async_copy(_token_tile(ys_ref, _pair_row(seg_ref, p_ref[t])), _token_tile(r_ref, t), sem)
                for p_ref, r_ref in ((p1_ref, r1_ref), (p2_ref, r2_ref))]

    def start(t, carry):
        for k, cp in enumerate(row_copies(t)):
            cp.start(priority=k)
        return carry

    def wait(t, carry):
        for cp in row_copies(t):
            cp.wait()
        return carry

    lax.fori_loop(0, rows, start, 0, unroll=DMA_UNROLL)
    lax.fori_loop(0, rows, wait, 0, unroll=DMA_UNROLL)
    rw = rw_ref[...]
    w1 = rw[:, 0:1]
    w2 = rw[:, 1:2]
    moe = jnp.concatenate([w1 * a + w2 * b for a, b in zip(_load_token_tiles(r1_ref, rows),
                                                          _load_token_tiles(r2_ref, rows))], axis=-1)
    x2 = x1_ref[...] + gate_ref[...] * moe.reshape(bb, tt, d)
    hf = _rms(x2) * nfin_ref[...]
    y_ref[...] = hf * (1.0 + scale_ref[...]) + shift_ref[...]


def _combine(seg, pairs, rw, x1, ada3, adaf3, nfin, ys, bb, tt):
    b, t, d = x1.shape
    nt = t // tt
    rows = bb * tt
    x_spec = pl.BlockSpec((bb, tt, d), lambda i, j: (i, j, 0))
    idx_spec = pl.BlockSpec((rows,), lambda i, j: (i * nt + j,), memory_space=pltpu.SMEM)

    def ada_spec(col):
        return pl.BlockSpec((bb, 1, d), lambda i, j: (i, 0, col))

    return pl.pallas_call(
        _combine_kernel,
        grid=(b // bb, nt),
        in_specs=[pl.BlockSpec(memory_space=pltpu.SMEM), idx_spec, idx_spec,
                  pl.BlockSpec((rows, ROUTER_LANES), lambda i, j: (i * nt + j, 0)),
                  x_spec, ada_spec(5), ada_spec(0), ada_spec(1),
                  pl.BlockSpec((1, d), lambda i, j: (0, 0)),
                  pl.BlockSpec(memory_space=pltpu.HBM)],
        out_specs=x_spec,
        scratch_shapes=[pltpu.VMEM((rows * TOKEN_TILE_ROWS, LANES), F32),
                        pltpu.VMEM((rows * TOKEN_TILE_ROWS, LANES), F32),
                        pltpu.SemaphoreType.DMA],
        out_shape=jax.ShapeDtypeStruct((b, t, d), F32),
        compiler_params=_cparams(("arbitrary", "arbitrary")),
        name="moe_combine",
    )(seg, *pairs, rw, x1, ada3, adaf3, adaf3, nfin, ys)


def _mixer(x, ada3, cnt_in, s_h, s_re, s_im, p, *, bb, tt, hgrn_tile, hgrn_chunk, sequential):
    b, t, d = x.shape
    n = b * t
    q, k, g, v, gs, u = _inproj(x, ada3, p['norm_mix'], p['w_in'], p['lb'], bb, tt)
    oh, s_h_new = _hgrn(q, k, g, v, gs, s_h, p['hgrn_norm'], b, t, hgrn_tile, hgrn_chunk)
    if sequential:
        xr, xi = s_re.reshape(b, 1, -1), s_im.reshape(b, 1, -1)
        n_tokens = t
    else:
        xr, xi = s_re.reshape(1, b, -1), s_im.reshape(1, b, -1)
        n_tokens = n
    ys, fr, fi = _s5(u, xr, xi, p['ssm_w1'], p['ssm_m'], p['ssm_a8'], n_tokens, sequential)
    x1, h2, pair1, pair2, rw, cnt = _outproj(x, oh, ys, ada3, p['w_glu'], p['b_glu'], p['ssm_norm'], p['wo_h'],
                                             p['wo_s'], p['norm_ffn'], p['w_rt'], p['b_rt'], cnt_in, bb, tt)
    states = (s_h_new[None], fr.reshape(1, b, s_re.shape[-2], s_re.shape[-1]),
              fi.reshape(1, b, s_re.shape[-2], s_re.shape[-1]))
    return (x1, h2, (pair1, pair2), rw), cnt, states


def kernel(x_prompt, x_sample, c_prompt, c_sample, state_hgrn, state_ssm_re, state_ssm_im, hgrn_lb_logits, w_ada, b_ada, norm_mix, w_in, hgrn_norm, ssm_a_re, ssm_a_im, ssm_log_dt, ssm_b_re, ssm_b_im, ssm_c_re, ssm_c_im, ssm_d, ssm_w_glu, ssm_b_glu, ssm_norm, w_out, norm_ffn, moe_w_group, moe_b_group, moe_w_router, moe_b_router, moe_w_gate, moe_w_up, moe_w_down, w_ada_final, b_ada_final, norm_final):
    depth = w_ada.shape[0]
    assert depth == 1
    d = x_prompt.shape[-1]
    bp = x_prompt.shape[0]
    dh = hgrn_norm.shape[-1]
    dk = dh // HGRN_HEADS
    de = moe_w_gate.shape[-1]
    n_exp = MOE_GROUPS * MOE_PER_GROUP

    lb = jax.nn.softmax(hgrn_lb_logits.astype(F32), axis=0)[0].reshape(1, dh)
    w1, m, a8 = _s5_prepare(ssm_a_re[0], ssm_a_im[0], ssm_log_dt[0], ssm_b_re[0], ssm_b_im[0],
                            ssm_c_re[0], ssm_c_im[0], ssm_d[0])
    w_rt = jnp.zeros((d, ROUTER_LANES), F32)
    w_rt = w_rt.at[:, :MOE_GROUPS].set(moe_w_group[0])
    w_rt = w_rt.at[:, EXPERT_LANE0:EXPERT_LANE0 + n_exp].set(
        moe_w_router[0].transpose(1, 0, 2).reshape(d, n_exp))
    b_rt = jnp.zeros((1, ROUTER_LANES), F32)
    b_rt = b_rt.at[0, :MOE_GROUPS].set(moe_b_group[0])
    b_rt = b_rt.at[0, EXPERT_LANE0:EXPERT_LANE0 + n_exp].set(moe_b_router[0].reshape(n_exp))
    p = dict(
        lb=lb, norm_mix=norm_mix[0].reshape(1, d), w_in=w_in[0].astype(BF16),
        hgrn_norm=hgrn_norm[0].reshape(1, dh),
        ssm_w1=w1, ssm_m=m, ssm_a8=a8,
        w_glu=ssm_w_glu[0].astype(BF16), b_glu=ssm_b_glu[0].reshape(1, -1), ssm_norm=ssm_norm[0].reshape(1, -1),
        wo_h=w_out[0, :dh].astype(BF16), wo_s=w_out[0, dh:].astype(BF16),
        norm_ffn=norm_ffn[0].reshape(1, d), w_rt=w_rt.astype(BF16), b_rt=b_rt,
    )
    wg = moe_w_gate[0].reshape(n_exp, d, de)
    wu = moe_w_up[0].reshape(n_exp, d, de)
    wd = moe_w_down[0].reshape(n_exp, de, d)
    nfin = norm_final.reshape(1, d)

    c_all = jnp.concatenate([c_prompt, c_sample], axis=0)
    ada = _silu_linear(c_all, w_ada[0], b_ada[0])
    adaf = _silu_linear(c_all, w_ada_final, b_ada_final)
    ada_p, ada_s = ada[:bp].reshape(bp, 1, -1), ada[bp:].reshape(x_sample.shape[0], 1, -1)
    adaf_p, adaf_s = adaf[:bp].reshape(bp, 1, -1), adaf[bp:].reshape(x_sample.shape[0], 1, -1)

    zeros_h = jnp.zeros((bp, HGRN_HEADS, dk, dk), F32)
    zeros_s = jnp.zeros((bp,) + state_ssm_re.shape[2:], F32)
    cnt0 = jnp.zeros((1, ROUTER_LANES), F32)
    tok_p, cnt_p, st_p = _mixer(x_prompt, ada_p, cnt0, zeros_h, zeros_s, zeros_s, p,
                                bb=1, tt=256, hgrn_tile=256, hgrn_chunk=64, sequential=True)
    tok_s, cnt, st_s = _mixer(x_sample, ada_s, cnt_p, state_hgrn[0], state_ssm_re[0], state_ssm_im[0], p,
                              bb=32, tt=8, hgrn_tile=8, hgrn_chunk=8, sequential=False)

    n_pairs = 2 * (tok_p[3].shape[0] + tok_s[3].shape[0])
    n_slots = n_pairs // MOE_TILE + N_EXPERTS
    seg, seg_end, item_tile, item_exp, n_items = _moe_schedule(cnt, n_slots)
    xs = _scatter(seg, tok_p[2], tok_s[2], tok_p[1], tok_s[1], 256)
    ys = _experts(seg, seg_end, item_tile, item_exp, n_items, xs, wg, wu, wd)
    y_p = _combine(seg, tok_p[2], tok_p[3], tok_p[0], ada_p, adaf_p, nfin, ys, 1, 256)
    y_s = _combine(seg, tok_s[2], tok_s[3], tok_s[0], ada_s, adaf_s, nfin, ys, 32, 8)
    return (y_p, y_s) + st_p + st_s
```

```python
import functools
import math

import jax
import jax.numpy as jnp
from jax import lax
from jax.experimental import pallas as pl
from jax.experimental.pallas import tpu as pltpu

F32 = jnp.float32
BF16 = jnp.bfloat16
HIGHEST = lax.Precision.HIGHEST

EPS = 1e-6
MAX_REAL = -1e-4
HGRN_HEADS = 4
HGRN_CHUNK = 128
HGRN_SAFE_EXPONENT = 80.0
HGRN_EXACT_BLOCK = 8
SSM_GROUP = 16
SSM_STATE = 64
SSM_CHUNK = 8
SSM_SETS = 4
MOE_GROUPS = 4
MOE_PER_GROUP = 8
N_EXPERTS = MOE_GROUPS * MOE_PER_GROUP
ROUTER_LANES = 128
EXPERT_LANE0 = 32
RANK_BITS = 20
MOE_TILE = 256
LANES = 128
TOKEN_TILE_ROWS = 8
DMA_UNROLL = 8
TOKEN_ROWS = 256
VMEM_LIMIT = 56 * 1024 * 1024


def _cparams(sem):
    return pltpu.CompilerParams(dimension_semantics=sem, vmem_limit_bytes=VMEM_LIMIT)


def _silu(x):
    return x * jax.nn.sigmoid(x)


def _rms(x):
    return x * lax.rsqrt(jnp.mean(x * x, axis=-1, keepdims=True) + EPS)


def _dot(a, b):
    return jnp.dot(a, b, preferred_element_type=F32)


def _dot_nt(a, b):
    return lax.dot_general(a, b, (((1,), (1,)), ((), ())), preferred_element_type=F32)


def _dot_tn(a, b, precision=None):
    return lax.dot_general(a, b, (((0,), (0,)), ((), ())), preferred_element_type=F32,
                           precision=precision)


def _silu_linear_kernel(c_ref, w_ref, b_ref, o_ref):
    a = _silu(c_ref[...]).astype(BF16)
    o_ref[...] = _dot(a, w_ref[...].astype(BF16)) + b_ref[...]


def _silu_linear(c, w, b):
    m, d = c.shape
    n = w.shape[1]
    tn = 1024
    return pl.pallas_call(
        _silu_linear_kernel,
        grid=(n // tn,),
        in_specs=[pl.BlockSpec((m, d), lambda j: (0, 0)),
                  pl.BlockSpec((d, tn), lambda j: (0, j)),
                  pl.BlockSpec((1, tn), lambda j: (0, j))],
        out_specs=pl.BlockSpec((m, tn), lambda j: (0, j)),
        out_shape=jax.ShapeDtypeStruct((m, n), F32),
        compiler_params=_cparams(("parallel",)),
        name="silu_linear",
    )(c, w, b.reshape(1, n))


def _inproj_kernel(x_ref, shift_ref, scale_ref, gain_ref, w_ref, lb_ref,
                   q_ref, k_ref, g_ref, v_ref, gs_ref, u_ref, *, dh):
    bb, tt, d = x_ref.shape
    h = _rms(x_ref[...]) * gain_ref[...]
    h = h * (1.0 + scale_ref[...]) + shift_ref[...]
    proj = _dot(h.reshape(bb * tt, d).astype(BF16), w_ref[...])
    lb = lb_ref[...]
    f = lb + (1.0 - lb) * jax.nn.sigmoid(proj[:, dh:2 * dh])
    q_ref[...] = _silu(proj[:, :dh]) * (float(dh // HGRN_HEADS) ** -0.5)
    k_ref[...] = 1.0 - f
    g_ref[...] = jnp.log(f)
    v_ref[...] = proj[:, 2 * dh:3 * dh]
    gs_ref[...] = _silu(proj[:, 3 * dh:4 * dh])
    for s in range(SSM_SETS):
        u_ref[s] = proj[:, 4 * dh + s * LANES:4 * dh + (s + 1) * LANES]


def _inproj(x, ada3, gain, w_in_bf, lb, bb, tt):
    b, t, d = x.shape
    dh = lb.shape[-1]
    nt = t // tt
    rows = bb * tt
    n = b * t
    row_spec = pl.BlockSpec((rows, dh), lambda i, j: (i * nt + j, 0))
    out = jax.ShapeDtypeStruct((n, dh), F32)
    return pl.pallas_call(
        functools.partial(_inproj_kernel, dh=dh),
        grid=(b // bb, nt),
        in_specs=[pl.BlockSpec((bb, tt, d), lambda i, j: (i, j, 0)),
                  pl.BlockSpec((bb, 1, d), lambda i, j: (i, 0, 0)),
                  pl.BlockSpec((bb, 1, d), lambda i, j: (i, 0, 1)),
                  pl.BlockSpec((1, d), lambda i, j: (0, 0)),
                  pl.BlockSpec(w_in_bf.shape, lambda i, j: (0, 0)),
                  pl.BlockSpec((1, dh), lambda i, j: (0, 0))],
        out_specs=[row_spec] * 5 + [pl.BlockSpec((SSM_SETS, rows, LANES), lambda i, j: (0, i * nt + j, 0))],
        out_shape=[out] * 5 + [jax.ShapeDtypeStruct((SSM_SETS, n, LANES), F32)],
        compiler_params=_cparams(("parallel", "parallel")),
        name="inproj",
    )(x, ada3, ada3, gain, w_in_bf, lb)


def _split3(x):
    hi = x.astype(BF16)
    r1 = x - hi.astype(F32)
    mid = r1.astype(BF16)
    lo = (r1 - mid.astype(F32)).astype(BF16)
    return hi, mid, lo


def _hgrn_kernel(q_ref, k_ref, g_ref, v_ref, gs_ref, s0_ref, hn_ref, o_ref, sf_ref,
                 st_ref, intra_ref, qh_ref, kh_ref, ea_ref, sums_ref, *, tl, nt):
    j = pl.program_id(1)
    rows_total, dh = q_ref.shape
    dk = dh // HGRN_HEADS
    c = HGRN_CHUNK
    seqs = c // tl
    n_chunks = rows_total // c

    @pl.when(j == 0)
    def _():
        st_ref[...] = s0_ref[...]

    r = lax.broadcasted_iota(jnp.int32, (c, c), 0)
    s = lax.broadcasted_iota(jnp.int32, (c, c), 1)
    same_seq = (r // tl) == (s // tl)
    causal = same_seq & (r >= s)
    upto_mid = same_seq & ((s % tl) < tl // 2)
    sums = jnp.concatenate([causal, upto_mid, same_seq], axis=0).astype(BF16)
    eye3 = (lax.broadcasted_iota(jnp.int32, (dk, 3 * dk), 1) % dk
            == lax.broadcasted_iota(jnp.int32, (dk, 3 * dk), 0)).astype(BF16)

    def decay_matrix(e_row):
        parts = jnp.concatenate(_split3(e_row), axis=-1)
        return _dot_nt(eye3, jnp.broadcast_to(parts, (dk, 3 * dk)))

    worst = jnp.float32(0.0)
    for ci in range(n_chunks):
        rows = slice(ci * c, (ci + 1) * c)
        g_parts = _split3(g_ref[rows, :])
        acc = _dot(sums, g_parts[0]) + _dot(sums, g_parts[1]) + _dot(sums, g_parts[2])
        sums_ref[ci * 3 * c:(ci + 1) * 3 * c, :] = acc
        a_mid = acc[c:2 * c]
        worst = jnp.maximum(worst, jnp.max(jnp.maximum(jnp.abs(acc[:c] - a_mid), jnp.abs(acc[2 * c:] - a_mid))))
    factorised_is_safe = worst < HGRN_SAFE_EXPONENT

    @pl.when(factorised_is_safe)
    def _():
        for ci in range(n_chunks):
            rows = slice(ci * c, (ci + 1) * c)
            a = sums_ref[ci * 3 * c:ci * 3 * c + c, :]
            a_mid = sums_ref[ci * 3 * c + c:ci * 3 * c + 2 * c, :]
            a_end = sums_ref[ci * 3 * c + 2 * c:(ci + 1) * 3 * c, :]
            e_mid = jnp.exp(a_mid)
            e_tail = jnp.exp(a_end - a_mid)
            qt = q_ref[rows, :] * jnp.exp(a - a_mid)
            kt = k_ref[rows, :] * jnp.exp(a_mid - a)
            qh_ref[rows, :] = qt * e_mid
            kh_ref[rows, :] = kt * e_tail
            ea_ref[rows, :] = e_mid * e_tail
            qt = qt.astype(BF16)
            kt = kt.astype(BF16)
            v = v_ref[rows, :].astype(BF16)
            for h in range(HGRN_HEADS):
                lanes = slice(h * dk, (h + 1) * dk)
                sc = jnp.where(causal, _dot_nt(qt[:, lanes], kt[:, lanes]), 0.0).astype(BF16)
                intra_ref[rows, lanes] = _dot(sc, v[:, lanes])

            for si in range(seqs):
                seq = ci * seqs + si if tl < c else 0
                srows = slice(si * tl, (si + 1) * tl)
                orows = slice(ci * c + si * tl, ci * c + (si + 1) * tl)
                for h in range(HGRN_HEADS):
                    lanes = slice(h * dk, (h + 1) * dk)
                    state = st_ref[seq, h]
                    o = intra_ref[orows, lanes] + _dot(qh_ref[orows, lanes].astype(BF16), state.astype(BF16))
                    decay = decay_matrix(ea_ref[orows.start:orows.start + 1, lanes])
                    st_ref[seq, h] = decay * state + _dot_tn(kh_ref[orows, lanes].astype(BF16),
                                                             v[srows, lanes])
                    o_ref[orows, lanes] = _rms(o) * hn_ref[:, lanes] * gs_ref[orows, lanes]

    @pl.when(jnp.logical_not(factorised_is_safe))
    def _():
        blk = HGRN_EXACT_BLOCK
        blocks_per_seq = max(tl // blk, 1)
        tri = (lax.broadcasted_iota(jnp.int32, (blk, blk), 0)
               >= lax.broadcasted_iota(jnp.int32, (blk, blk), 1)).astype(BF16)
        sub = lax.broadcasted_iota(jnp.int32, (blk, dk), 0)

        def block(bi, carry):
            rows = pl.ds(pl.multiple_of(bi * blk, blk), blk)
            seq = bi // blocks_per_seq if tl < c else 0
            g_parts = _split3(g_ref[rows, :])
            a = _dot(tri, g_parts[0]) + _dot(tri, g_parts[1]) + _dot(tri, g_parts[2])
            a_end = a[blk - 1:blk]
            q = q_ref[rows, :]
            k = k_ref[rows, :]
            v = v_ref[rows, :]
            qh = (q * jnp.exp(a)).astype(BF16)
            kh = (k * jnp.exp(a_end - a)).astype(BF16)
            ea = jnp.exp(a_end)
            vb = v.astype(BF16)
            for h in range(HGRN_HEADS):
                lanes = slice(h * dk, (h + 1) * dk)
                state = st_ref[seq, h]
                intra = []
                for t in range(blk):
                    live = sub <= t
                    decay_t = jnp.where(live, jnp.exp(jnp.where(live, a[t:t + 1, lanes] - a[:, lanes], 0.0)), 0.0)
                    score = jnp.sum(q[t:t + 1, lanes] * k[:, lanes] * decay_t, axis=-1, keepdims=True)
                    intra.append(jnp.sum(score * v[:, lanes], axis=0, keepdims=True))
                o = jnp.concatenate(intra, axis=0) + _dot(qh[:, lanes], state.astype(BF16))
                st_ref[seq, h] = decay_matrix(ea[:, lanes]) * state + _dot_tn(kh[:, lanes], vb[:, lanes])
                o_ref[rows, lanes] = _rms(o) * hn_ref[:, lanes] * gs_ref[rows, lanes]
            return carry

        lax.fori_loop(0, rows_total // blk, block, 0)

    @pl.when(j == nt - 1)
    def _():
        sf_ref[...] = st_ref[...]


def _hgrn(q, k, g, v, gs, s0, hnorm, b, t, nseq, tt):
    n, dh = q.shape
    nt = t // tt
    dk = dh // HGRN_HEADS
    rows = nseq * tt
    tl = min(tt, HGRN_CHUNK)
    row_spec = pl.BlockSpec((rows, dh), lambda i, j: (i * nt + j, 0))
    st_spec = pl.BlockSpec((nseq, HGRN_HEADS, dk, dk), lambda i, j: (i, 0, 0, 0))
    return pl.pallas_call(
        functools.partial(_hgrn_kernel, tl=tl, nt=nt),
        grid=(b // nseq, nt),
        in_specs=[row_spec] * 5 + [st_spec, pl.BlockSpec((1, dh), lambda i, j: (0, 0))],
        out_specs=[row_spec, st_spec],
        out_shape=[jax.ShapeDtypeStruct((n, dh), F32),
                   jax.ShapeDtypeStruct((b, HGRN_HEADS, dk, dk), F32)],
        scratch_shapes=[pltpu.VMEM((nseq, HGRN_HEADS, dk, dk), F32),
                        pltpu.VMEM((rows, dh), F32), pltpu.VMEM((rows, dh), F32),
                        pltpu.VMEM((rows, dh), F32), pltpu.VMEM((rows, dh), F32),
                        pltpu.VMEM((3 * rows, dh), F32)],
        compiler_params=_cparams(("parallel", "arbitrary")),
        name="hgrn",
    )(q, k, g, v, gs, s0, hnorm)


def _s5_prepare(a_re, a_im, log_dt, b_re, b_im, c_re, c_im, d_skip):
    ng, npp = a_re.shape
    nh = b_re.shape[-1]
    L = SSM_CHUNK
    gs = ng // SSM_SETS
    lam_re = jnp.minimum(a_re, MAX_REAL)
    lam_im = a_im
    dt = jnp.exp(log_dt)
    mag = jnp.exp(lam_re * dt)
    ab_re = mag * jnp.cos(lam_im * dt)
    ab_im = mag * jnp.sin(lam_im * dt)
    den = lam_re * lam_re + lam_im * lam_im
    co_re = ((ab_re - 1.0) * lam_re + ab_im * lam_im) / den
    co_im = (ab_im * lam_re - (ab_re - 1.0) * lam_im) / den
    bb_re = co_re[..., None] * b_re - co_im[..., None] * b_im
    bb_im = co_re[..., None] * b_im + co_im[..., None] * b_re
    pw_re = [jnp.ones_like(ab_re)]
    pw_im = [jnp.zeros_like(ab_im)]
    for _ in range(L):
        pr, pi = pw_re[-1], pw_im[-1]
        pw_re.append(pr * ab_re - pi * ab_im)
        pw_im.append(pr * ab_im + pi * ab_re)
    pw_re = jnp.stack(pw_re)
    pw_im = jnp.stack(pw_im)
    ab_b_re = pw_re[:L, :, :, None] * bb_re - pw_im[:L, :, :, None] * bb_im
    ab_b_im = pw_re[:L, :, :, None] * bb_im + pw_im[:L, :, :, None] * bb_re
    kern = (jnp.einsum('gkp,lgph->lghk', c_re, ab_b_re, precision=HIGHEST)
            - jnp.einsum('gkp,lgph->lghk', c_im, ab_b_im, precision=HIGHEST))
    kern = kern.at[0].add(d_skip[:, :, None] * jnp.eye(nh, dtype=F32))
    def group_block_diag(c):
        rows, w = c.shape[-2:]
        tiled = jnp.concatenate([c] * gs, axis=-1)
        rg = jnp.arange(rows)[:, None] // (rows // gs)
        cq = jnp.arange(gs * w)[None, :] // w
        return jnp.where(rg == cq, tiled, 0.0).astype(BF16).transpose(1, 0, 2, 3).reshape(
            SSM_SETS, L * rows, gs * w)

    bd = group_block_diag(kern.reshape(L, SSM_SETS, gs * nh, nh)).reshape(SSM_SETS, L, gs * nh, gs * nh)
    zero = jnp.zeros_like(bd[:, 0])
    w_t = jnp.concatenate([jnp.concatenate([bd[:, t - s] if t >= s else zero for t in range(L)], axis=-1)
                           for s in range(L)], axis=-2)
    n_re = group_block_diag(ab_b_re[::-1].transpose(0, 1, 3, 2).reshape(L, SSM_SETS, gs * nh, npp))
    n_im = group_block_diag(ab_b_im[::-1].transpose(0, 1, 3, 2).reshape(L, SSM_SETS, gs * nh, npp))
    w1 = jnp.concatenate([w_t, n_re, n_im], axis=-1)
    ca_re = c_re[None] * pw_re[1:, :, None, :] - c_im[None] * pw_im[1:, :, None, :]
    ca_im = c_re[None] * pw_im[1:, :, None, :] + c_im[None] * pw_re[1:, :, None, :]
    m = jnp.concatenate([group_block_diag(ca_re.reshape(L, SSM_SETS, gs * nh, npp)),
                         group_block_diag(-ca_im.reshape(L, SSM_SETS, gs * nh, npp))], axis=-1)
    a8 = jnp.concatenate([pw_re[L].reshape(SSM_SETS, 1, gs * npp),
                          pw_im[L].reshape(SSM_SETS, 1, gs * npp)], axis=-1)
    return w1, m, a8


def _s5_kernel(u_ref, xr_ref, xi_ref, w1_ref, m_ref, a8_ref, y_ref, fr_ref, fi_ref, *, sequential):
    n = u_ref.shape[1] // SSM_CHUNK
    ns = xr_ref.shape[-1]
    ny = SSM_CHUNK * LANES
    u = jnp.concatenate([u_ref[0, pl.ds(s, n, stride=SSM_CHUNK), :] for s in range(SSM_CHUNK)], axis=-1)
    res = _dot(u.astype(BF16), w1_ref[0])
    d_re = res[:, ny:ny + ns]
    d_im = res[:, ny + ns:]
    a_re = a8_ref[0][:, :ns]
    a_im = a8_ref[0][:, ns:]
    x0_re = xr_ref[0]
    x0_im = xi_ref[0]
    if sequential:
        row = lax.broadcasted_iota(jnp.int32, (n, ns), 0)
        first = row == 0
        x_re = d_re + jnp.where(first, a_re * x0_re - a_im * x0_im, 0.0)
        x_im = d_im + jnp.where(first, a_re * x0_im + a_im * x0_re, 0.0)
        p_re, p_im = a_re, a_im
        step = 1
        while step < n:
            s_re = jnp.where(row >= step, pltpu.roll(x_re, step, 0), 0.0)
            s_im = jnp.where(row >= step, pltpu.roll(x_im, step, 0), 0.0)
            x_re, x_im = x_re + p_re * s_re - p_im * s_im, x_im + p_re * s_im + p_im * s_re
            p_re, p_im = p_re * p_re - p_im * p_im, 2.0 * p_re * p_im
            step *= 2
        fr_ref[0] = x_re[n - 1:n]
        fi_ref[0] = x_im[n - 1:n]
        xc_re = jnp.where(first, x0_re, pltpu.roll(x_re, 1, 0))
        xc_im = jnp.where(first, x0_im, pltpu.roll(x_im, 1, 0))
    else:
        xc_re, xc_im = x0_re, x0_im
        fr_ref[0] = a_re * x0_re - a_im * x0_im + d_re
        fi_ref[0] = a_re * x0_im + a_im * x0_re + d_im
    xc = jnp.concatenate([xc_re, xc_im], axis=-1).astype(BF16)
    y = res[:, :ny] + _dot_nt(xc, m_ref[0])
    for t in range(SSM_CHUNK):
        y_ref[0, pl.ds(t, n, stride=SSM_CHUNK), :] = y[:, t * LANES:(t + 1) * LANES]


def _s5(u, x_re, x_im, w1, m, a8, n_tokens, sequential):
    sets = u.shape[0]
    nb, rb, _ = x_re.shape
    ns = m.shape[1] // 2
    st_spec = pl.BlockSpec((1, rb, ns), lambda gi, i: (i, 0, gi))
    st_shape = jax.ShapeDtypeStruct(x_re.shape, F32)
    tok_spec = pl.BlockSpec((1, n_tokens, LANES), lambda gi, i: (gi, i, 0))
    return pl.pallas_call(
        functools.partial(_s5_kernel, sequential=sequential),
        grid=(sets, nb),
        in_specs=[tok_spec, st_spec, st_spec,
                  pl.BlockSpec((1,) + w1.shape[1:], lambda gi, i: (gi, 0, 0)),
                  pl.BlockSpec((1,) + m.shape[1:], lambda gi, i: (gi, 0, 0)),
                  pl.BlockSpec((1, 1, 2 * ns), lambda gi, i: (gi, 0, 0))],
        out_specs=[tok_spec, st_spec, st_spec],
        out_shape=[jax.ShapeDtypeStruct(u.shape, F32), st_shape, st_shape],
        compiler_params=_cparams(("parallel", "parallel")),
        name="s5",
    )(u, x_re, x_im, w1, m, a8)


def _gelu_tanh(x):
    return 0.5 * x * (1.0 + jnp.tanh(math.sqrt(2.0 / math.pi) * (x + 0.044715 * (x * x * x))))


def _outproj_kernel(xp_ref, xs_ref, ohp_ref, ohs_ref, ysp_ref, yss_ref,
                    gate_p_ref, shift_p_ref, scale_p_ref, gate_s_ref, shift_s_ref, scale_s_ref,
                    wglu_ref, bglu_ref, sn_ref, wo_h_ref, wo_s_ref, nf_ref, wr_ref, br_ref,
                    x1p_ref, x1s_ref, h2_ref, pair1_ref, pair2_ref, rw_ref, cnt_ref, *, prompt_tiles):
    is_prompt = pl.program_id(0) < prompt_tiles
    _, rows, d = xp_ref.shape

    def per_token(p_ref, s_ref):
        per_seq = s_ref[0]
        seqs = per_seq.shape[0]
        rep = jnp.broadcast_to(per_seq[:, None, :], (seqs, rows // seqs, d)).reshape(rows, d)
        return jnp.where(is_prompt, p_ref[0], rep)

    x = jnp.where(is_prompt, xp_ref[0], xs_ref[0])
    oh = jnp.where(is_prompt, ohp_ref[...], ohs_ref[...])
    ys = jnp.concatenate([jnp.where(is_prompt, ysp_ref[s], yss_ref[s]) for s in range(SSM_SETS)], axis=-1)
    y = _gelu_tanh(ys)
    y = y * jax.nn.sigmoid(_dot(y.astype(BF16), wglu_ref[...]) + bglu_ref[...])
    o_s = _rms(y) * sn_ref[...]
    mix = _dot(oh.astype(BF16), wo_h_ref[...]) + _dot(o_s.astype(BF16), wo_s_ref[...])
    x1 = x + per_token(gate_p_ref, gate_s_ref) * mix

    @pl.when(is_prompt)
    def _():
        x1p_ref[0] = x1

    @pl.when(jnp.logical_not(is_prompt))
    def _():
        x1s_ref[0] = x1

    h2 = _rms(x1) * nf_ref[...]
    h2 = h2 * (1.0 + per_token(scale_p_ref, scale_s_ref)) + per_token(shift_p_ref, shift_s_ref)
    _store_token_tiles(h2_ref, h2)

    logits = _dot(h2.astype(BF16), wr_ref[...]) + br_ref[...]
    lane = lax.broadcasted_iota(jnp.int32, logits.shape, 1)
    neg = -jnp.inf
    gl = jnp.where(lane < MOE_GROUPS, logits, neg)
    gmax = jnp.max(gl, axis=-1, keepdims=True)
    gidx = jnp.min(jnp.where(gl == gmax, lane, ROUTER_LANES), axis=-1, keepdims=True)
    grp_w = 1.0 / jnp.sum(jnp.exp(gl - gmax), axis=-1, keepdims=True)
    e0 = EXPERT_LANE0 + gidx * MOE_PER_GROUP
    sel = jnp.where((lane >= e0) & (lane < e0 + MOE_PER_GROUP), logits, neg)
    m1 = jnp.max(sel, axis=-1, keepdims=True)
    i1 = jnp.min(jnp.where(sel == m1, lane, ROUTER_LANES), axis=-1, keepdims=True)
    sel2 = jnp.where(lane == i1, neg, sel)
    m2 = jnp.max(sel2, axis=-1, keepdims=True)
    i2 = jnp.min(jnp.where(sel2 == m2, lane, ROUTER_LANES), axis=-1, keepdims=True)
    e2 = jnp.exp(m2 - m1)
    w1 = 1.0 / (1.0 + e2)
    w2 = e2 / (1.0 + e2)
    rw_ref[...] = grp_w * (jnp.where(lane == 0, w1, 0.0) + jnp.where(lane == 1, w2, 0.0))

    @pl.when(pl.program_id(0) == 0)
    def _():
        cnt_ref[...] = jnp.zeros_like(cnt_ref)

    picked = (lane == i1) | (lane == i2)
    earlier = (lax.broadcasted_iota(jnp.int32, (rows, rows), 0)
               > lax.broadcasted_iota(jnp.int32, (rows, rows), 1))
    base = cnt_ref[...]
    before = _dot(earlier.astype(BF16), picked.astype(BF16)) + base
    rank1 = jnp.sum(jnp.where(lane == i1, before, 0.0), axis=-1, keepdims=True).astype(jnp.int32)
    rank2 = jnp.sum(jnp.where(lane == i2, before, 0.0), axis=-1, keepdims=True).astype(jnp.int32)
    cnt_ref[...] = base + jnp.sum(picked.astype(F32), axis=0, keepdims=True)
    info = jnp.where(lane == 0, ((i1 - EXPERT_LANE0) << RANK_BITS) | rank1,
                     jnp.where(lane == 1, ((i2 - EXPERT_LANE0) << RANK_BITS) | rank2, 0)).T
    pair1_ref[...] = info[0]
    pair2_ref[...] = info[1]


def _outproj(x_p, x_s, oh_p, oh_s, ys_p, ys_s, ada_p, ada_s, wglu_bf, bglu, snorm, wo_h, wo_s, nffn, wr, br, rows):
    d = x_p.shape[-1]
    dh = oh_p.shape[-1]
    n_p = x_p.shape[0] * x_p.shape[1]
    n_s = x_s.shape[0] * x_s.shape[1]
    n = n_p + n_s
    tiles_p = n_p // rows
    seqs = rows // x_s.shape[1]

    def pt(i):
        return jnp.minimum(i, tiles_p - 1)

    def st(i):
        return jnp.maximum(i - tiles_p, 0)

    tiles_per_seq = x_p.shape[1] // rows

    def ada_p_spec(col):
        return pl.BlockSpec((1, 1, d), lambda i: (pt(i) // tiles_per_seq, 0, col))

    def ada_s_spec(col):
        return pl.BlockSpec((1, seqs, d), lambda i: (st(i), 0, col))

    def full(a):
        return pl.BlockSpec(a.shape, lambda i: (0,) * a.ndim)

    xp3 = x_p.reshape(tiles_p, rows, d)
    xs3 = x_s.reshape(n_s // rows, rows, d)
    ada_s3 = ada_s.reshape(n_s // rows, seqs, -1)
    outs = pl.pallas_call(
        functools.partial(_outproj_kernel, prompt_tiles=tiles_p),
        grid=(n // rows,),
        in_specs=[pl.BlockSpec((1, rows, d), lambda i: (pt(i), 0, 0)),
                  pl.BlockSpec((1, rows, d), lambda i: (st(i), 0, 0)),
                  pl.BlockSpec((rows, dh), lambda i: (pt(i), 0)),
                  pl.BlockSpec((rows, dh), lambda i: (st(i), 0)),
                  pl.BlockSpec((SSM_SETS, rows, LANES), lambda i: (0, pt(i), 0)),
                  pl.BlockSpec((SSM_SETS, rows, LANES), lambda i: (0, st(i), 0)),
                  ada_p_spec(2), ada_p_spec(3), ada_p_spec(4), ada_s_spec(2), ada_s_spec(3), ada_s_spec(4),
                  full(wglu_bf), full(bglu), full(snorm), full(wo_h), full(wo_s), full(nffn),
                  full(wr), full(br)],
        out_specs=[pl.BlockSpec((1, rows, d), lambda i: (pt(i), 0, 0)),
                   pl.BlockSpec((1, rows, d), lambda i: (st(i), 0, 0)),
                   pl.BlockSpec((rows * TOKEN_TILE_ROWS, LANES), lambda i: (i, 0)),
                   pl.BlockSpec((rows,), lambda i: (i,)),
                   pl.BlockSpec((rows,), lambda i: (i,)),
                   pl.BlockSpec((rows, ROUTER_LANES), lambda i: (i, 0)),
                   pl.BlockSpec((1, ROUTER_LANES), lambda i: (0, 0))],
        out_shape=[jax.ShapeDtypeStruct(xp3.shape, F32),
                   jax.ShapeDtypeStruct(xs3.shape, F32),
                   jax.ShapeDtypeStruct((n * TOKEN_TILE_ROWS, LANES), F32),
                   jax.ShapeDtypeStruct((n,), jnp.int32),
                   jax.ShapeDtypeStruct((n,), jnp.int32),
                   jax.ShapeDtypeStruct((n, ROUTER_LANES), F32),
                   jax.ShapeDtypeStruct((1, ROUTER_LANES), F32)],
        compiler_params=_cparams(("arbitrary",)),
        name="outproj",
    )(xp3, xs3, oh_p, oh_s, ys_p, ys_s, ada_p, ada_p, ada_p, ada_s3, ada_s3, ada_s3,
      wglu_bf, bglu, snorm, wo_h, wo_s, nffn, wr, br)
    x1_p, x1_s = outs[0].reshape(x_p.shape), outs[1].reshape(x_s.shape)
    return (x1_p, x1_s) + tuple(outs[2:])


def _moe_schedule(cnt, n_slots):
    c = cnt[0, EXPERT_LANE0:EXPERT_LANE0 + N_EXPERTS].astype(jnp.int32)
    seg_end = jnp.cumsum(c)
    seg_start = seg_end - c
    first_tile = seg_start // MOE_TILE
    tiles = jnp.where(c > 0, (seg_end - 1) // MOE_TILE - first_tile + 1, 0)
    cum = jnp.cumsum(tiles)
    n_items = cum[-1]
    item = jnp.minimum(jnp.arange(n_slots, dtype=jnp.int32), n_items - 1)
    item_exp = jnp.sum(item[:, None] >= cum[None, :], axis=1).astype(jnp.int32)
    shares = ((c > 0) & (seg_start % MOE_TILE != 0)).astype(jnp.int32)
    item_tile = item - jnp.sum((item[:, None] >= (cum - tiles)[None, :]) * shares[None, :], axis=1)
    return (seg_start.astype(jnp.int32), seg_end.astype(jnp.int32), item_tile.astype(jnp.int32), item_exp,
            n_items.reshape(1).astype(jnp.int32))


def _token_tile(ref, t):
    return ref.at[pl.ds(pl.multiple_of(t * TOKEN_TILE_ROWS, TOKEN_TILE_ROWS), TOKEN_TILE_ROWS)]


def _store_token_tiles(ref, x):
    rows = x.shape[0]
    for c in range(TOKEN_TILE_ROWS):
        ref[pl.ds(c, rows, stride=TOKEN_TILE_ROWS), :] = x[:, c * LANES:(c + 1) * LANES]


def _load_token_tiles(ref, rows):
    return [ref[pl.ds(c, rows, stride=TOKEN_TILE_ROWS), :] for c in range(TOKEN_TILE_ROWS)]


def _pair_row(seg_ref, pair):
    return seg_ref[pair >> RANK_BITS] + (pair & ((1 << RANK_BITS) - 1))


def _inverse_kernel(seg_ref, p1_ref, p2_ref, inv_ref):
    ts = p1_ref.shape[0]
    base = pl.program_id(0) * ts

    def body(t, carry):
        inv_ref[_pair_row(seg_ref, p1_ref[t])] = base + t
        inv_ref[_pair_row(seg_ref, p2_ref[t])] = base + t
        return carry

    lax.fori_loop(0, ts, body, 0, unroll=DMA_UNROLL)


def _inverse(seg, pair1, pair2, ts):
    n = pair1.shape[0]
    idx_spec = pl.BlockSpec((ts,), lambda i: (i,), memory_space=pltpu.SMEM)
    return pl.pallas_call(
        _inverse_kernel,
        grid=(n // ts,),
        in_specs=[pl.BlockSpec(memory_space=pltpu.SMEM), idx_spec, idx_spec],
        out_specs=pl.BlockSpec(memory_space=pltpu.SMEM),
        out_shape=jax.ShapeDtypeStruct((2 * n,), jnp.int32),
        compiler_params=_cparams(("arbitrary",)),
        name="moe_inverse",
    )(seg, pair1, pair2)


def _row_gather(src_ref, idx_of, dst_ref, dst_row0, sem, n_rows):
    def copy(t):
        return pltpu.make_async_copy(_token_tile(src_ref, idx_of(t)), _token_tile(dst_ref, dst_row0 + t), sem)

    def start():
        def body(t, carry):
            copy(2 * t).start(priority=0)
            copy(2 * t + 1).start(priority=1)
            return carry
        lax.fori_loop(0, n_rows // 2, body, 0, unroll=DMA_UNROLL // 2)

    def wait():
        def body(t, carry):
            copy(t).wait()
            return carry
        lax.fori_loop(0, n_rows, body, 0, unroll=DMA_UNROLL)

    return start, wait


def _experts_kernel(lo_ref, hi_ref, tile_ref, exp_ref, items_ref, inv_ref, inv_next_ref, h_ref,
                    wg_ref, wu_ref, wd_ref, ys_ref, xbuf_ref, sems, wg_s, wu_s, wd_s, *, n_tiles):
    i = pl.program_id(0)
    prev = jnp.maximum(i - 1, 0)
    e = exp_ref[i]
    tile = tile_ref[i]
    slot = tile % 2
    active = i < items_ref[0]
    first_visit = (i == 0) | (tile != tile_ref[prev])

    @pl.when((i == 0) | (e != exp_ref[prev]))
    def _():
        wg_s[...] = wg_ref[0].astype(BF16)
        wu_s[...] = wu_ref[0].astype(BF16)
        wd_s[...] = wd_ref[0].astype(BF16)

    start_this, wait_this = _row_gather(h_ref, lambda t: inv_ref[t], xbuf_ref, slot * MOE_TILE,
                                        sems.at[slot], MOE_TILE)
    start_next, _ = _row_gather(h_ref, lambda t: inv_next_ref[t], xbuf_ref, (1 - slot) * MOE_TILE,
                                sems.at[1 - slot], MOE_TILE)

    @pl.when(active & first_visit)
    def _():
        @pl.when(i == 0)
        def _():
            start_this()

        @pl.when(tile + 1 < n_tiles)
        def _():
            start_next()

        wait_this()

    @pl.when(active)
    def _():
        base = pl.multiple_of(slot * (MOE_TILE * TOKEN_TILE_ROWS), MOE_TILE * TOKEN_TILE_ROWS)
        x = jnp.concatenate([xbuf_ref[pl.ds(base + c, MOE_TILE, stride=TOKEN_TILE_ROWS), :]
                             for c in range(TOKEN_TILE_ROWS)], axis=-1).astype(BF16)
        act = _silu(_dot(x, wg_s[...])) * _dot(x, wu_s[...])
        out = _dot(act.astype(BF16), wd_s[...])
        row = tile_ref[i] * MOE_TILE + lax.broadcasted_iota(jnp.int32, (MOE_TILE, 1), 0)
        mine = (row >= lo_ref[e]) & (row < hi_ref[e])
        first_visit = (i == 0) | (tile_ref[i] != tile_ref[prev])

        @pl.when(first_visit)
        def _():
            _store_token_tiles(ys_ref, jnp.where(mine, out, 0.0))

        @pl.when(jnp.logical_not(first_visit))
        def _():
            old = jnp.concatenate(_load_token_tiles(ys_ref, MOE_TILE), axis=-1)
            _store_token_tiles(ys_ref, jnp.where(mine, out, old))


def _experts(seg_lo, seg_hi, item_tile, item_exp, n_items, inv, h2, wg, wu, wd):
    n_tiles = inv.shape[0] // MOE_TILE
    _, d, de = wg.shape

    def w_spec(shape):
        return pl.BlockSpec((1,) + shape, lambda i, lo, hi, tile, ex, items: (ex[i], 0, 0))

    return pl.pallas_call(
        functools.partial(_experts_kernel, n_tiles=n_tiles),
        grid_spec=pltpu.PrefetchScalarGridSpec(
            num_scalar_prefetch=5, grid=(item_tile.shape[0],),
            in_specs=[pl.BlockSpec((MOE_TILE,), lambda i, lo, hi, tile, ex, items: (tile[i],),
                                   memory_space=pltpu.SMEM),
                      pl.BlockSpec((MOE_TILE,), lambda i, lo, hi, tile, ex, items: (
                          jnp.minimum(tile[i] + 1, n_tiles - 1),), memory_space=pltpu.SMEM),
                      pl.BlockSpec(memory_space=pltpu.HBM),
                      w_spec((d, de)), w_spec((d, de)), w_spec((de, d))],
            out_specs=pl.BlockSpec((MOE_TILE * TOKEN_TILE_ROWS, LANES),
                                   lambda i, lo, hi, tile, ex, items: (tile[i], 0)),
            scratch_shapes=[pltpu.VMEM((2 * MOE_TILE * TOKEN_TILE_ROWS, LANES), F32),
                            pltpu.SemaphoreType.DMA((2,)),
                            pltpu.VMEM((d, de), BF16), pltpu.VMEM((d, de), BF16), pltpu.VMEM((de, d), BF16)]),
        out_shape=jax.ShapeDtypeStruct((inv.shape[0] * TOKEN_TILE_ROWS, LANES), F32),
        compiler_params=_cparams(("arbitrary",)),
        name="moe_experts",
    )(seg_lo, seg_hi, item_tile, item_exp, n_items, inv, inv, h2, wg, wu, wd)


def _combine_kernel(seg_ref, p1_ref, p2_ref, p1n_ref, p2n_ref, rw_ref, x1_ref, gate_ref, shift_ref, scale_ref,
                    nfin_ref, ys_ref, y_ref, r1_ref, r2_ref, sems, *, n_steps):
    _, rows, d = x1_ref.shape
    s = pl.program_id(0)
    slot = s % 2

    def gathers(pa_ref, pb_ref, slot):
        g1 = _row_gather(ys_ref, lambda t: _pair_row(seg_ref, pa_ref[t]), r1_ref, slot * rows, sems.at[slot], rows)
        g2 = _row_gather(ys_ref, lambda t: _pair_row(seg_ref, pb_ref[t]), r2_ref, slot * rows, sems.at[slot], rows)
        return g1, g2

    this = gathers(p1_ref, p2_ref, slot)
    nxt = gathers(p1n_ref, p2n_ref, 1 - slot)

    @pl.when(s == 0)
    def _():
        this[0][0]()
        this[1][0]()

    @pl.when(s + 1 < n_steps)
    def _():
        nxt[0][0]()
        nxt[1][0]()

    this[0][1]()
    this[1][1]()

    def per_token(ref):
        per_seq = ref[0]
        seqs = per_seq.shape[0]
        return jnp.broadcast_to(per_seq[:, None, :], (seqs, rows // seqs, d)).reshape(rows, d)

    rw = rw_ref[...]
    w1 = rw[:, 0:1]
    w2 = rw[:, 1:2]
    base = pl.multiple_of(slot * (rows * TOKEN_TILE_ROWS), rows * TOKEN_TILE_ROWS)
    moe = jnp.concatenate(
        [w1 * r1_ref[pl.ds(base + c, rows, stride=TOKEN_TILE_ROWS), :]
         + w2 * r2_ref[pl.ds(base + c, rows, stride=TOKEN_TILE_ROWS), :] for c in range(TOKEN_TILE_ROWS)], axis=-1)
    x2 = x1_ref[0] + per_token(gate_ref) * moe
    hf = _rms(x2) * nfin_ref[...]
    y_ref[0] = hf * (1.0 + per_token(scale_ref)) + per_token(shift_ref)


def _combine(seg, pair1, pair2, rw, x1, ada3, adaf3, nfin, ys, rows, tile0):
    b, t, d = x1.shape
    n_steps = b * t // rows
    seqs = max(rows // t, 1)
    tiles_per_seq = max(t // rows, 1)
    x3 = x1.reshape(n_steps, rows, d)
    ada_v = ada3.reshape(b // seqs, seqs, -1)
    adaf_v = adaf3.reshape(b // seqs, seqs, -1)
    x_spec = pl.BlockSpec((1, rows, d), lambda i: (i, 0, 0))

    def idx_spec(step):
        return pl.BlockSpec((rows,), lambda i: (tile0 + step(i),), memory_space=pltpu.SMEM)

    def ada_spec(col):
        return pl.BlockSpec((1, seqs, d), lambda i: (i // tiles_per_seq, 0, col))

    def cur(i):
        return i

    def nxt(i):
        return jnp.minimum(i + 1, n_steps - 1)

    return pl.pallas_call(
        functools.partial(_combine_kernel, n_steps=n_steps),
        grid=(n_steps,),
        in_specs=[pl.BlockSpec(memory_space=pltpu.SMEM), idx_spec(cur), idx_spec(cur), idx_spec(nxt), idx_spec(nxt),
                  pl.BlockSpec((rows, ROUTER_LANES), lambda i: (tile0 + i, 0)),
                  x_spec, ada_spec(5), ada_spec(0), ada_spec(1),
                  pl.BlockSpec((1, d), lambda i: (0, 0)),
                  pl.BlockSpec(memory_space=pltpu.HBM)],
        out_specs=x_spec,
        scratch_shapes=[pltpu.VMEM((2 * rows * TOKEN_TILE_ROWS, LANES), F32),
                        pltpu.VMEM((2 * rows * TOKEN_TILE_ROWS, LANES), F32),
                        pltpu.SemaphoreType.DMA((2,))],
        out_shape=jax.ShapeDtypeStruct(x3.shape, F32),
        compiler_params=_cparams(("arbitrary",)),
        name="moe_combine",
    )(seg, pair1, pair2, pair1, pair2, rw, x3, ada_v, adaf_v, adaf_v, nfin, ys).reshape(x1.shape)


def _mixer(x, ada3, s_h, s_re, s_im, p, *, bb, tt, hgrn_seqs, hgrn_tokens, sequential):
    b, t, d = x.shape
    n = b * t
    q, k, g, v, gs, u = _inproj(x, ada3, p['norm_mix'], p['w_in'], p['lb'], bb, tt)
    oh, s_h_new = _hgrn(q, k, g, v, gs, s_h, p['hgrn_norm'], b, t, hgrn_seqs, hgrn_tokens)
    if sequential:
        xr, xi = s_re.reshape(b, 1, -1), s_im.reshape(b, 1, -1)
        n_tokens = t
    else:
        xr, xi = s_re.reshape(1, b, -1), s_im.reshape(1, b, -1)
        n_tokens = n
    ys, fr, fi = _s5(u, xr, xi, p['ssm_w1'], p['ssm_m'], p['ssm_a8'], n_tokens, sequential)
    states = (s_h_new[None], fr.reshape(1, b, s_re.shape[-2], s_re.shape[-1]),
              fi.reshape(1, b, s_re.shape[-2], s_re.shape[-1]))
    return oh, ys, states


def kernel(x_prompt, x_sample, c_prompt, c_sample, state_hgrn, state_ssm_re, state_ssm_im, hgrn_lb_logits, w_ada, b_ada, norm_mix, w_in, hgrn_norm, ssm_a_re, ssm_a_im, ssm_log_dt, ssm_b_re, ssm_b_im, ssm_c_re, ssm_c_im, ssm_d, ssm_w_glu, ssm_b_glu, ssm_norm, w_out, norm_ffn, moe_w_group, moe_b_group, moe_w_router, moe_b_router, moe_w_gate, moe_w_up, moe_w_down, w_ada_final, b_ada_final, norm_final):
    depth = w_ada.shape[0]
    assert depth == 1
    d = x_prompt.shape[-1]
    bp = x_prompt.shape[0]
    dh = hgrn_norm.shape[-1]
    dk = dh // HGRN_HEADS
    de = moe_w_gate.shape[-1]
    n_exp = MOE_GROUPS * MOE_PER_GROUP

    lb = jax.nn.softmax(hgrn_lb_logits.astype(F32), axis=0)[0].reshape(1, dh)
    w1, m, a8 = _s5_prepare(ssm_a_re[0], ssm_a_im[0], ssm_log_dt[0], ssm_b_re[0], ssm_b_im[0],
                            ssm_c_re[0], ssm_c_im[0], ssm_d[0])
    w_rt = jnp.zeros((d, ROUTER_LANES), F32)
    w_rt = w_rt.at[:, :MOE_GROUPS].set(moe_w_group[0])
    w_rt = w_rt.at[:, EXPERT_LANE0:EXPERT_LANE0 + n_exp].set(
        moe_w_router[0].transpose(1, 0, 2).reshape(d, n_exp))
    b_rt = jnp.zeros((1, ROUTER_LANES), F32)
    b_rt = b_rt.at[0, :MOE_GROUPS].set(moe_b_group[0])
    b_rt = b_rt.at[0, EXPERT_LANE0:EXPERT_LANE0 + n_exp].set(moe_b_router[0].reshape(n_exp))
    p = dict(
        lb=lb, norm_mix=norm_mix[0].reshape(1, d), w_in=w_in[0].astype(BF16),
        hgrn_norm=hgrn_norm[0].reshape(1, dh),
        ssm_w1=w1, ssm_m=m, ssm_a8=a8,
        w_glu=ssm_w_glu[0].astype(BF16), b_glu=ssm_b_glu[0].reshape(1, -1), ssm_norm=ssm_norm[0].reshape(1, -1),
        wo_h=w_out[0, :dh].astype(BF16), wo_s=w_out[0, dh:].astype(BF16),
        norm_ffn=norm_ffn[0].reshape(1, d), w_rt=w_rt.astype(BF16), b_rt=b_rt,
    )
    wg = moe_w_gate[0].reshape(n_exp, d, de)
    wu = moe_w_up[0].reshape(n_exp, d, de)
    wd = moe_w_down[0].reshape(n_exp, de, d)
    nfin = norm_final.reshape(1, d)

    c_all = jnp.concatenate([c_prompt, c_sample], axis=0)
    ada = _silu_linear(c_all, w_ada[0], b_ada[0])
    adaf = _silu_linear(c_all, w_ada_final, b_ada_final)
    ada_p, ada_s = ada[:bp].reshape(bp, 1, -1), ada[bp:].reshape(x_sample.shape[0], 1, -1)
    adaf_p, adaf_s = adaf[:bp].reshape(bp, 1, -1), adaf[bp:].reshape(x_sample.shape[0], 1, -1)

    zeros_h = jnp.zeros((bp, HGRN_HEADS, dk, dk), F32)
    zeros_s = jnp.zeros((bp,) + state_ssm_re.shape[2:], F32)
    oh_p, ys_p, st_p = _mixer(x_prompt, ada_p, zeros_h, zeros_s, zeros_s, p,
                              bb=1, tt=256, hgrn_seqs=1, hgrn_tokens=512, sequential=True)
    oh_s, ys_s, st_s = _mixer(x_sample, ada_s, state_hgrn[0], state_ssm_re[0], state_ssm_im[0], p,
                              bb=32, tt=8, hgrn_seqs=16, hgrn_tokens=8, sequential=False)
    x1_p, x1_s, h2, pair1, pair2, rw, cnt = _outproj(
        x_prompt, x_sample, oh_p, oh_s, ys_p, ys_s, ada_p, ada_s, p['w_glu'], p['b_glu'], p['ssm_norm'],
        p['wo_h'], p['wo_s'], p['norm_ffn'], p['w_rt'], p['b_rt'], TOKEN_ROWS)

    n_pairs = 2 * pair1.shape[0]
    n_slots = n_pairs // MOE_TILE + N_EXPERTS
    seg, seg_end, item_tile, item_exp, n_items = _moe_schedule(cnt, n_slots)
    inv = _inverse(seg, pair1, pair2, 1024)
    ys = _experts(seg, seg_end, item_tile, item_exp, n_items, inv, h2, wg, wu, wd)
    tiles_p = x_prompt.shape[0] * x_prompt.shape[1] // TOKEN_ROWS
    y_p = _combine(seg, pair1, pair2, rw, x1_p, ada_p, adaf_p, nfin, ys, TOKEN_ROWS, 0)
    y_s = _combine(seg, pair1, pair2, rw, x1_s, ada_s, adaf_s, nfin, ys, TOKEN_ROWS, tiles_p)
    return (y_p, y_s) + st_p + st_s
```

```python
import functools
import math

import jax
import jax.numpy as jnp
from jax import lax
from jax.experimental import pallas as pl
from jax.experimental.pallas import tpu as pltpu

F32 = jnp.float32
BF16 = jnp.bfloat16
HIGHEST = lax.Precision.HIGHEST

EPS = 1e-6
MAX_REAL = -1e-4
HGRN_HEADS = 4
HGRN_CHUNK = 128
HGRN_SAFE_EXPONENT = 80.0
HGRN_EXACT_BLOCK = 8
SSM_GROUP = 16
SSM_STATE = 64
SSM_CHUNK = 8
SSM_SETS = 4
MOE_GROUPS = 4
MOE_PER_GROUP = 8
N_EXPERTS = MOE_GROUPS * MOE_PER_GROUP
ROUTER_LANES = 128
EXPERT_LANE0 = 32
RANK_BITS = 20
MOE_TILE = 256
LANES = 128
TOKEN_TILE_ROWS = 8
DMA_UNROLL = 8
TOKEN_ROWS = 256
PIECE = 4
LOCAL_ROWS = -(-(2 * TOKEN_ROWS + N_EXPERTS * (PIECE - 1)) // LANES) * LANES
VMEM_LIMIT = 56 * 1024 * 1024


def _cparams(sem):
    return pltpu.CompilerParams(dimension_semantics=sem, vmem_limit_bytes=VMEM_LIMIT)


def _silu(x):
    return x * jax.nn.sigmoid(x)


def _rms(x):
    return x * lax.rsqrt(jnp.mean(x * x, axis=-1, keepdims=True) + EPS)


def _dot(a, b):
    return jnp.dot(a, b, preferred_element_type=F32)


def _dot_nt(a, b):
    return lax.dot_general(a, b, (((1,), (1,)), ((), ())), preferred_element_type=F32)


def _dot_tn(a, b, precision=None):
    return lax.dot_general(a, b, (((0,), (0,)), ((), ())), preferred_element_type=F32,
                           precision=precision)


def _silu_linear_kernel(c_ref, w_ref, b_ref, o_ref):
    a = _silu(c_ref[...]).astype(BF16)
    o_ref[...] = _dot(a, w_ref[...].astype(BF16)) + b_ref[...]


def _silu_linear(c, w, b):
    m, d = c.shape
    n = w.shape[1]
    tn = 1024
    return pl.pallas_call(
        _silu_linear_kernel,
        grid=(n // tn,),
        in_specs=[pl.BlockSpec((m, d), lambda j: (0, 0)),
                  pl.BlockSpec((d, tn), lambda j: (0, j)),
                  pl.BlockSpec((1, tn), lambda j: (0, j))],
        out_specs=pl.BlockSpec((m, tn), lambda j: (0, j)),
        out_shape=jax.ShapeDtypeStruct((m, n), F32),
        compiler_params=_cparams(("parallel",)),
        name="silu_linear",
    )(c, w, b.reshape(1, n))


def _inproj_kernel(x_ref, shift_ref, scale_ref, gain_ref, w_ref, lb_ref,
                   q_ref, k_ref, g_ref, v_ref, gs_ref, u_ref, *, dh):
    bb, tt, d = x_ref.shape
    h = _rms(x_ref[...]) * gain_ref[...]
    h = h * (1.0 + scale_ref[...]) + shift_ref[...]
    proj = _dot(h.reshape(bb * tt, d).astype(BF16), w_ref[...])
    lb = lb_ref[...]
    f = lb + (1.0 - lb) * jax.nn.sigmoid(proj[:, dh:2 * dh])
    q_ref[...] = _silu(proj[:, :dh]) * (float(dh // HGRN_HEADS) ** -0.5)
    k_ref[...] = 1.0 - f
    g_ref[...] = jnp.log(f)
    v_ref[...] = proj[:, 2 * dh:3 * dh]
    gs_ref[...] = _silu(proj[:, 3 * dh:4 * dh])
    for s in range(SSM_SETS):
        u_ref[s] = proj[:, 4 * dh + s * LANES:4 * dh + (s + 1) * LANES]


def _inproj(x, ada3, gain, w_in_bf, lb, bb, tt):
    b, t, d = x.shape
    dh = lb.shape[-1]
    nt = t // tt
    rows = bb * tt
    n = b * t
    row_spec = pl.BlockSpec((rows, dh), lambda i, j: (i * nt + j, 0))
    out = jax.ShapeDtypeStruct((n, dh), F32)
    return pl.pallas_call(
        functools.partial(_inproj_kernel, dh=dh),
        grid=(b // bb, nt),
        in_specs=[pl.BlockSpec((bb, tt, d), lambda i, j: (i, j, 0)),
                  pl.BlockSpec((bb, 1, d), lambda i, j: (i, 0, 0)),
                  pl.BlockSpec((bb, 1, d), lambda i, j: (i, 0, 1)),
                  pl.BlockSpec((1, d), lambda i, j: (0, 0)),
                  pl.BlockSpec(w_in_bf.shape, lambda i, j: (0, 0)),
                  pl.BlockSpec((1, dh), lambda i, j: (0, 0))],
        out_specs=[row_spec] * 5 + [pl.BlockSpec((SSM_SETS, rows, LANES), lambda i, j: (0, i * nt + j, 0))],
        out_shape=[out] * 5 + [jax.ShapeDtypeStruct((SSM_SETS, n, LANES), F32)],
        compiler_params=_cparams(("parallel", "parallel")),
        name="inproj",
    )(x, ada3, ada3, gain, w_in_bf, lb)


def _split3(x):
    hi = x.astype(BF16)
    r1 = x - hi.astype(F32)
    mid = r1.astype(BF16)
    lo = (r1 - mid.astype(F32)).astype(BF16)
    return hi, mid, lo


def _hgrn_kernel(q_ref, k_ref, g_ref, v_ref, gs_ref, s0_ref, hn_ref, o_ref, sf_ref,
                 st_ref, intra_ref, qh_ref, kh_ref, ea_ref, sums_ref, *, tl, nt):
    j = pl.program_id(1)
    rows_total, dh = q_ref.shape
    dk = dh // HGRN_HEADS
    c = HGRN_CHUNK
    seqs = c // tl
    n_chunks = rows_total // c

    @pl.when(j == 0)
    def _():
        st_ref[...] = s0_ref[...]

    r = lax.broadcasted_iota(jnp.int32, (c, c), 0)
    s = lax.broadcasted_iota(jnp.int32, (c, c), 1)
    same_seq = (r // tl) == (s // tl)
    causal = same_seq & (r >= s)
    upto_mid = same_seq & ((s % tl) < tl // 2)
    sums = jnp.concatenate([causal, upto_mid, same_seq], axis=0).astype(BF16)
    eye3 = (lax.broadcasted_iota(jnp.int32, (dk, 3 * dk), 1) % dk
            == lax.broadcasted_iota(jnp.int32, (dk, 3 * dk), 0)).astype(BF16)

    def decay_matrix(e_row):
        parts = jnp.concatenate(_split3(e_row), axis=-1)
        return _dot_nt(eye3, jnp.broadcast_to(parts, (dk, 3 * dk)))

    worst = jnp.float32(0.0)
    for ci in range(n_chunks):
        rows = slice(ci * c, (ci + 1) * c)
        g_parts = _split3(g_ref[rows, :])
        acc = _dot(sums, g_parts[0]) + _dot(sums, g_parts[1]) + _dot(sums, g_parts[2])
        sums_ref[ci * 3 * c:(ci + 1) * 3 * c, :] = acc
        a_mid = acc[c:2 * c]
        worst = jnp.maximum(worst, jnp.max(jnp.maximum(jnp.abs(acc[:c] - a_mid), jnp.abs(acc[2 * c:] - a_mid))))
    factorised_is_safe = worst < HGRN_SAFE_EXPONENT

    @pl.when(factorised_is_safe)
    def _():
        for ci in range(n_chunks):
            rows = slice(ci * c, (ci + 1) * c)
            a = sums_ref[ci * 3 * c:ci * 3 * c + c, :]
            a_mid = sums_ref[ci * 3 * c + c:ci * 3 * c + 2 * c, :]
            a_end = sums_ref[ci * 3 * c + 2 * c:(ci + 1) * 3 * c, :]
            e_mid = jnp.exp(a_mid)
            e_tail = jnp.exp(a_end - a_mid)
            qt = q_ref[rows, :] * jnp.exp(a - a_mid)
            kt = k_ref[rows, :] * jnp.exp(a_mid - a)
            qh_ref[rows, :] = qt * e_mid
            kh_ref[rows, :] = kt * e_tail
            ea_ref[rows, :] = e_mid * e_tail
            qt = qt.astype(BF16)
            kt = kt.astype(BF16)
            v = v_ref[rows, :].astype(BF16)
            for h in range(HGRN_HEADS):
                lanes = slice(h * dk, (h + 1) * dk)
                sc = jnp.where(causal, _dot_nt(qt[:, lanes], kt[:, lanes]), 0.0).astype(BF16)
                intra_ref[rows, lanes] = _dot(sc, v[:, lanes])

            for si in range(seqs):
                seq = ci * seqs + si if tl < c else 0
                srows = slice(si * tl, (si + 1) * tl)
                orows = slice(ci * c + si * tl, ci * c + (si + 1) * tl)
                for h in range(HGRN_HEADS):
                    lanes = slice(h * dk, (h + 1) * dk)
                    state = st_ref[seq, h]
                    o = intra_ref[orows, lanes] + _dot(qh_ref[orows, lanes].astype(BF16), state.astype(BF16))
                    decay = decay_matrix(ea_ref[orows.start:orows.start + 1, lanes])
                    st_ref[seq, h] = decay * state + _dot_tn(kh_ref[orows, lanes].astype(BF16),
                                                             v[srows, lanes])
                    o_ref[orows, lanes] = _rms(o) * hn_ref[:, lanes] * gs_ref[orows, lanes]

    @pl.when(jnp.logical_not(factorised_is_safe))
    def _():
        blk = HGRN_EXACT_BLOCK
        blocks_per_seq = max(tl // blk, 1)
        tri = (lax.broadcasted_iota(jnp.int32, (blk, blk), 0)
               >= lax.broadcasted_iota(jnp.int32, (blk, blk), 1)).astype(BF16)
        sub = lax.broadcasted_iota(jnp.int32, (blk, dk), 0)

        def block(bi, carry):
            rows = pl.ds(pl.multiple_of(bi * blk, blk), blk)
            seq = bi // blocks_per_seq if tl < c else 0
            g_parts = _split3(g_ref[rows, :])
            a = _dot(tri, g_parts[0]) + _dot(tri, g_parts[1]) + _dot(tri, g_parts[2])
            a_end = a[blk - 1:blk]
            q = q_ref[rows, :]
            k = k_ref[rows, :]
            v = v_ref[rows, :]
            qh = (q * jnp.exp(a)).astype(BF16)
            kh = (k * jnp.exp(a_end - a)).astype(BF16)
            ea = jnp.exp(a_end)
            vb = v.astype(BF16)
            for h in range(HGRN_HEADS):
                lanes = slice(h * dk, (h + 1) * dk)
                state = st_ref[seq, h]
                intra = []
                for t in range(blk):
                    live = sub <= t
                    decay_t = jnp.where(live, jnp.exp(jnp.where(live, a[t:t + 1, lanes] - a[:, lanes], 0.0)), 0.0)
                    score = jnp.sum(q[t:t + 1, lanes] * k[:, lanes] * decay_t, axis=-1, keepdims=True)
                    intra.append(jnp.sum(score * v[:, lanes], axis=0, keepdims=True))
                o = jnp.concatenate(intra, axis=0) + _dot(qh[:, lanes], state.astype(BF16))
                st_ref[seq, h] = decay_matrix(ea[:, lanes]) * state + _dot_tn(kh[:, lanes], vb[:, lanes])
                o_ref[rows, lanes] = _rms(o) * hn_ref[:, lanes] * gs_ref[rows, lanes]
            return carry

        lax.fori_loop(0, rows_total // blk, block, 0)

    @pl.when(j == nt - 1)
    def _():
        sf_ref[...] = st_ref[...]


def _hgrn(q, k, g, v, gs, s0, hnorm, b, t, nseq, tt):
    n, dh = q.shape
    nt = t // tt
    dk = dh // HGRN_HEADS
    rows = nseq * tt
    tl = min(tt, HGRN_CHUNK)
    row_spec = pl.BlockSpec((rows, dh), lambda i, j: (i * nt + j, 0))
    st_spec = pl.BlockSpec((nseq, HGRN_HEADS, dk, dk), lambda i, j: (i, 0, 0, 0))
    return pl.pallas_call(
        functools.partial(_hgrn_kernel, tl=tl, nt=nt),
        grid=(b // nseq, nt),
        in_specs=[row_spec] * 5 + [st_spec, pl.BlockSpec((1, dh), lambda i, j: (0, 0))],
        out_specs=[row_spec, st_spec],
        out_shape=[jax.ShapeDtypeStruct((n, dh), F32),
                   jax.ShapeDtypeStruct((b, HGRN_HEADS, dk, dk), F32)],
        scratch_shapes=[pltpu.VMEM((nseq, HGRN_HEADS, dk, dk), F32),
                        pltpu.VMEM((rows, dh), F32), pltpu.VMEM((rows, dh), F32),
                        pltpu.VMEM((rows, dh), F32), pltpu.VMEM((rows, dh), F32),
                        pltpu.VMEM((3 * rows, dh), F32)],
        compiler_params=_cparams(("parallel", "arbitrary")),
        name="hgrn",
    )(q, k, g, v, gs, s0, hnorm)


def _s5_prepare(a_re, a_im, log_dt, b_re, b_im, c_re, c_im, d_skip):
    ng, npp = a_re.shape
    nh = b_re.shape[-1]
    L = SSM_CHUNK
    gs = ng // SSM_SETS
    lam_re = jnp.minimum(a_re, MAX_REAL)
    lam_im = a_im
    dt = jnp.exp(log_dt)
    mag = jnp.exp(lam_re * dt)
    ab_re = mag * jnp.cos(lam_im * dt)
    ab_im = mag * jnp.sin(lam_im * dt)
    den = lam_re * lam_re + lam_im * lam_im
    co_re = ((ab_re - 1.0) * lam_re + ab_im * lam_im) / den
    co_im = (ab_im * lam_re - (ab_re - 1.0) * lam_im) / den
    bb_re = co_re[..., None] * b_re - co_im[..., None] * b_im
    bb_im = co_re[..., None] * b_im + co_im[..., None] * b_re
    pw_re = [jnp.ones_like(ab_re)]
    pw_im = [jnp.zeros_like(ab_im)]
    for _ in range(L):
        pr, pi = pw_re[-1], pw_im[-1]
        pw_re.append(pr * ab_re - pi * ab_im)
        pw_im.append(pr * ab_im + pi * ab_re)
    pw_re = jnp.stack(pw_re)
    pw_im = jnp.stack(pw_im)
    ab_b_re = pw_re[:L, :, :, None] * bb_re - pw_im[:L, :, :, None] * bb_im
    ab_b_im = pw_re[:L, :, :, None] * bb_im + pw_im[:L, :, :, None] * bb_re
    kern = (jnp.einsum('gkp,lgph->lghk', c_re, ab_b_re, precision=HIGHEST)
            - jnp.einsum('gkp,lgph->lghk', c_im, ab_b_im, precision=HIGHEST))
    kern = kern.at[0].add(d_skip[:, :, None] * jnp.eye(nh, dtype=F32))
    def group_block_diag(c):
        rows, w = c.shape[-2:]
        tiled = jnp.concatenate([c] * gs, axis=-1)
        rg = jnp.arange(rows)[:, None] // (rows // gs)
        cq = jnp.arange(gs * w)[None, :] // w
        return jnp.where(rg == cq, tiled, 0.0).astype(BF16).transpose(1, 0, 2, 3).reshape(
            SSM_SETS, L * rows, gs * w)

    bd = group_block_diag(kern.reshape(L, SSM_SETS, gs * nh, nh)).reshape(SSM_SETS, L, gs * nh, gs * nh)
    zero = jnp.zeros_like(bd[:, 0])
    w_t = jnp.concatenate([jnp.concatenate([bd[:, t - s] if t >= s else zero for t in range(L)], axis=-1)
                           for s in range(L)], axis=-2)
    n_re = group_block_diag(ab_b_re[::-1].transpose(0, 1, 3, 2).reshape(L, SSM_SETS, gs * nh, npp))
    n_im = group_block_diag(ab_b_im[::-1].transpose(0, 1, 3, 2).reshape(L, SSM_SETS, gs * nh, npp))
    w1 = jnp.concatenate([w_t, n_re, n_im], axis=-1)
    ca_re = c_re[None] * pw_re[1:, :, None, :] - c_im[None] * pw_im[1:, :, None, :]
    ca_im = c_re[None] * pw_im[1:, :, None, :] + c_im[None] * pw_re[1:, :, None, :]
    m = jnp.concatenate([group_block_diag(ca_re.reshape(L, SSM_SETS, gs * nh, npp)),
                         group_block_diag(-ca_im.reshape(L, SSM_SETS, gs * nh, npp))], axis=-1)
    a8 = jnp.concatenate([pw_re[L].reshape(SSM_SETS, 1, gs * npp),
                          pw_im[L].reshape(SSM_SETS, 1, gs * npp)], axis=-1)
    return w1, m, a8


def _s5_kernel(u_ref, xr_ref, xi_ref, w1_ref, m_ref, a8_ref, y_ref, fr_ref, fi_ref, *, sequential):
    n = u_ref.shape[1] // SSM_CHUNK
    ns = xr_ref.shape[-1]
    ny = SSM_CHUNK * LANES
    u = jnp.concatenate([u_ref[0, pl.ds(s, n, stride=SSM_CHUNK), :] for s in range(SSM_CHUNK)], axis=-1)
    res = _dot(u.astype(BF16), w1_ref[0])
    d_re = res[:, ny:ny + ns]
    d_im = res[:, ny + ns:]
    a_re = a8_ref[0][:, :ns]
    a_im = a8_ref[0][:, ns:]
    x0_re = xr_ref[0]
    x0_im = xi_ref[0]
    if sequential:
        row = lax.broadcasted_iota(jnp.int32, (n, ns), 0)
        first = row == 0
        x_re = d_re + jnp.where(first, a_re * x0_re - a_im * x0_im, 0.0)
        x_im = d_im + jnp.where(first, a_re * x0_im + a_im * x0_re, 0.0)
        p_re, p_im = a_re, a_im
        step = 1
        while step < n:
            s_re = jnp.where(row >= step, pltpu.roll(x_re, step, 0), 0.0)
            s_im = jnp.where(row >= step, pltpu.roll(x_im, step, 0), 0.0)
            x_re, x_im = x_re + p_re * s_re - p_im * s_im, x_im + p_re * s_im + p_im * s_re
            p_re, p_im = p_re * p_re - p_im * p_im, 2.0 * p_re * p_im
            step *= 2
        fr_ref[0] = x_re[n - 1:n]
        fi_ref[0] = x_im[n - 1:n]
        xc_re = jnp.where(first, x0_re, pltpu.roll(x_re, 1, 0))
        xc_im = jnp.where(first, x0_im, pltpu.roll(x_im, 1, 0))
    else:
        xc_re, xc_im = x0_re, x0_im
        fr_ref[0] = a_re * x0_re - a_im * x0_im + d_re
        fi_ref[0] = a_re * x0_im + a_im * x0_re + d_im
    xc = jnp.concatenate([xc_re, xc_im], axis=-1).astype(BF16)
    y = res[:, :ny] + _dot_nt(xc, m_ref[0])
    for t in range(SSM_CHUNK):
        y_ref[0, pl.ds(t, n, stride=SSM_CHUNK), :] = y[:, t * LANES:(t + 1) * LANES]


def _s5(u, x_re, x_im, w1, m, a8, n_tokens, sequential):
    sets = u.shape[0]
    nb, rb, _ = x_re.shape
    ns = m.shape[1] // 2
    st_spec = pl.BlockSpec((1, rb, ns), lambda gi, i: (i, 0, gi))
    st_shape = jax.ShapeDtypeStruct(x_re.shape, F32)
    tok_spec = pl.BlockSpec((1, n_tokens, LANES), lambda gi, i: (gi, i, 0))
    return pl.pallas_call(
        functools.partial(_s5_kernel, sequential=sequential),
        grid=(sets, nb),
        in_specs=[tok_spec, st_spec, st_spec,
                  pl.BlockSpec((1,) + w1.shape[1:], lambda gi, i: (gi, 0, 0)),
                  pl.BlockSpec((1,) + m.shape[1:], lambda gi, i: (gi, 0, 0)),
                  pl.BlockSpec((1, 1, 2 * ns), lambda gi, i: (gi, 0, 0))],
        out_specs=[tok_spec, st_spec, st_spec],
        out_shape=[jax.ShapeDtypeStruct(u.shape, F32), st_shape, st_shape],
        compiler_params=_cparams(("parallel", "parallel")),
        name="s5",
    )(u, x_re, x_im, w1, m, a8)


def _gelu_tanh(x):
    return 0.5 * x * (1.0 + jnp.tanh(math.sqrt(2.0 / math.pi) * (x + 0.044715 * (x * x * x))))


def _outproj_kernel(xp_ref, xs_ref, ohp_ref, ohs_ref, ysp_ref, yss_ref,
                    gate_p_ref, shift_p_ref, scale_p_ref, gate_s_ref, shift_s_ref, scale_s_ref,
                    wglu_ref, bglu_ref, sn_ref, wo_h_ref, wo_s_ref, nf_ref, wr_ref, br_ref,
                    x1p_ref, x1s_ref, xl_ref, pair1_ref, pair2_ref, rw_ref, cnt_ref, npad_ref, *, prompt_tiles):
    is_prompt = pl.program_id(0) < prompt_tiles
    _, rows, d = xp_ref.shape

    def per_token(p_ref, s_ref):
        per_seq = s_ref[0]
        seqs = per_seq.shape[0]
        rep = jnp.broadcast_to(per_seq[:, None, :], (seqs, rows // seqs, d)).reshape(rows, d)
        return jnp.where(is_prompt, p_ref[0], rep)

    x = jnp.where(is_prompt, xp_ref[0], xs_ref[0])
    oh = jnp.where(is_prompt, ohp_ref[...], ohs_ref[...])
    ys = jnp.concatenate([jnp.where(is_prompt, ysp_ref[s], yss_ref[s]) for s in range(SSM_SETS)], axis=-1)
    y = _gelu_tanh(ys)
    y = y * jax.nn.sigmoid(_dot(y.astype(BF16), wglu_ref[...]) + bglu_ref[...])
    o_s = _rms(y) * sn_ref[...]
    mix = _dot(oh.astype(BF16), wo_h_ref[...]) + _dot(o_s.astype(BF16), wo_s_ref[...])
    x1 = x + per_token(gate_p_ref, gate_s_ref) * mix

    @pl.when(is_prompt)
    def _():
        x1p_ref[0] = x1

    @pl.when(jnp.logical_not(is_prompt))
    def _():
        x1s_ref[0] = x1

    h2 = _rms(x1) * nf_ref[...]
    h2 = h2 * (1.0 + per_token(scale_p_ref, scale_s_ref)) + per_token(shift_p_ref, shift_s_ref)
    logits = _dot(h2.astype(BF16), wr_ref[...]) + br_ref[...]
    lane = lax.broadcasted_iota(jnp.int32, logits.shape, 1)
    neg = -jnp.inf
    gl = jnp.where(lane < MOE_GROUPS, logits, neg)
    gmax = jnp.max(gl, axis=-1, keepdims=True)
    gidx = jnp.min(jnp.where(gl == gmax, lane, ROUTER_LANES), axis=-1, keepdims=True)
    grp_w = 1.0 / jnp.sum(jnp.exp(gl - gmax), axis=-1, keepdims=True)
    e0 = EXPERT_LANE0 + gidx * MOE_PER_GROUP
    sel = jnp.where((lane >= e0) & (lane < e0 + MOE_PER_GROUP), logits, neg)
    m1 = jnp.max(sel, axis=-1, keepdims=True)
    i1 = jnp.min(jnp.where(sel == m1, lane, ROUTER_LANES), axis=-1, keepdims=True)
    sel2 = jnp.where(lane == i1, neg, sel)
    m2 = jnp.max(sel2, axis=-1, keepdims=True)
    i2 = jnp.min(jnp.where(sel2 == m2, lane, ROUTER_LANES), axis=-1, keepdims=True)
    e2 = jnp.exp(m2 - m1)
    w1 = 1.0 / (1.0 + e2)
    w2 = e2 / (1.0 + e2)
    rw_ref[...] = grp_w * (jnp.where(lane == 0, w1, 0.0) + jnp.where(lane == 1, w2, 0.0))

    @pl.when(pl.program_id(0) == 0)
    def _():
        cnt_ref[...] = jnp.zeros_like(cnt_ref)

    picked = (lane == i1) | (lane == i2)
    earlier = (lax.broadcasted_iota(jnp.int32, (rows, rows), 0)
               > lax.broadcasted_iota(jnp.int32, (rows, rows), 1))
    before = _dot(earlier.astype(BF16), picked.astype(BF16))
    n_pad = jnp.floor((jnp.sum(picked.astype(F32), axis=0, keepdims=True) + (PIECE - 1)) * (1.0 / PIECE)) * PIECE
    lane1 = lax.broadcasted_iota(jnp.int32, n_pad.shape, 1)
    local_off = n_pad
    shift = 1
    while shift < ROUTER_LANES:
        local_off = local_off + jnp.where(lane1 >= shift, pltpu.roll(local_off, shift, 1), 0.0)
        shift *= 2
    local_off = local_off - n_pad
    base = cnt_ref[...]
    cnt_ref[...] = base + n_pad
    npad_ref[0] = n_pad.astype(jnp.int32)

    def at_expert(idx, table):
        return jnp.sum(jnp.where(lane == idx, table, 0.0), axis=-1, keepdims=True).astype(jnp.int32)

    rank1 = at_expert(i1, before + base)
    rank2 = at_expert(i2, before + base)
    local1 = at_expert(i1, before + local_off)
    local2 = at_expert(i2, before + local_off)
    info = jnp.where(lane == 0, ((i1 - EXPERT_LANE0) << RANK_BITS) | rank1,
                     jnp.where(lane == 1, ((i2 - EXPERT_LANE0) << RANK_BITS) | rank2,
                               jnp.where(lane == 2, local1, jnp.where(lane == 3, local2, 0)))).T
    pair1_ref[...] = info[0]
    pair2_ref[...] = info[1]
    slot_row = lax.broadcasted_iota(jnp.int32, (LOCAL_ROWS, rows), 0)
    perm = ((slot_row == info[2:3]) | (slot_row == info[3:4])).astype(BF16)
    _store_token_tiles(xl_ref, _dot(perm, h2.astype(BF16)))


def _outproj(x_p, x_s, oh_p, oh_s, ys_p, ys_s, ada_p, ada_s, wglu_bf, bglu, snorm, wo_h, wo_s, nffn, wr, br, rows):
    d = x_p.shape[-1]
    dh = oh_p.shape[-1]
    n_p = x_p.shape[0] * x_p.shape[1]
    n_s = x_s.shape[0] * x_s.shape[1]
    n = n_p + n_s
    tiles_p = n_p // rows
    seqs = rows // x_s.shape[1]

    def pt(i):
        return jnp.minimum(i, tiles_p - 1)

    def st(i):
        return jnp.maximum(i - tiles_p, 0)

    tiles_per_seq = x_p.shape[1] // rows

    def ada_p_spec(col):
        return pl.BlockSpec((1, 1, d), lambda i: (pt(i) // tiles_per_seq, 0, col))

    def ada_s_spec(col):
        return pl.BlockSpec((1, seqs, d), lambda i: (st(i), 0, col))

    def full(a):
        return pl.BlockSpec(a.shape, lambda i: (0,) * a.ndim)

    xp3 = x_p.reshape(tiles_p, rows, d)
    xs3 = x_s.reshape(n_s // rows, rows, d)
    ada_s3 = ada_s.reshape(n_s // rows, seqs, -1)
    outs = pl.pallas_call(
        functools.partial(_outproj_kernel, prompt_tiles=tiles_p),
        grid=(n // rows,),
        in_specs=[pl.BlockSpec((1, rows, d), lambda i: (pt(i), 0, 0)),
                  pl.BlockSpec((1, rows, d), lambda i: (st(i), 0, 0)),
                  pl.BlockSpec((rows, dh), lambda i: (pt(i), 0)),
                  pl.BlockSpec((rows, dh), lambda i: (st(i), 0)),
                  pl.BlockSpec((SSM_SETS, rows, LANES), lambda i: (0, pt(i), 0)),
                  pl.BlockSpec((SSM_SETS, rows, LANES), lambda i: (0, st(i), 0)),
                  ada_p_spec(2), ada_p_spec(3), ada_p_spec(4), ada_s_spec(2), ada_s_spec(3), ada_s_spec(4),
                  full(wglu_bf), full(bglu), full(snorm), full(wo_h), full(wo_s), full(nffn),
                  full(wr), full(br)],
        out_specs=[pl.BlockSpec((1, rows, d), lambda i: (pt(i), 0, 0)),
                   pl.BlockSpec((1, rows, d), lambda i: (st(i), 0, 0)),
                   pl.BlockSpec((LOCAL_ROWS * TOKEN_TILE_ROWS, LANES), lambda i: (i, 0)),
                   pl.BlockSpec((rows,), lambda i: (i,)),
                   pl.BlockSpec((rows,), lambda i: (i,)),
                   pl.BlockSpec((rows, ROUTER_LANES), lambda i: (i, 0)),
                   pl.BlockSpec((1, ROUTER_LANES), lambda i: (0, 0)),
                   pl.BlockSpec((1, 1, ROUTER_LANES), lambda i: (i, 0, 0))],
        out_shape=[jax.ShapeDtypeStruct(xp3.shape, F32),
                   jax.ShapeDtypeStruct(xs3.shape, F32),
                   jax.ShapeDtypeStruct((n // rows * LOCAL_ROWS * TOKEN_TILE_ROWS, LANES), F32),
                   jax.ShapeDtypeStruct((n,), jnp.int32),
                   jax.ShapeDtypeStruct((n,), jnp.int32),
                   jax.ShapeDtypeStruct((n, ROUTER_LANES), F32),
                   jax.ShapeDtypeStruct((1, ROUTER_LANES), F32),
                   jax.ShapeDtypeStruct((n // rows, 1, ROUTER_LANES), jnp.int32)],
        compiler_params=_cparams(("arbitrary",)),
        name="outproj",
    )(xp3, xs3, oh_p, oh_s, ys_p, ys_s, ada_p, ada_p, ada_p, ada_s3, ada_s3, ada_s3,
      wglu_bf, bglu, snorm, wo_h, wo_s, nffn, wr, br)
    x1_p, x1_s = outs[0].reshape(x_p.shape), outs[1].reshape(x_s.shape)
    return (x1_p, x1_s) + tuple(outs[2:])


def _moe_schedule(cnt, n_slots):
    c = cnt[0, EXPERT_LANE0:EXPERT_LANE0 + N_EXPERTS].astype(jnp.int32)
    seg_end = jnp.cumsum(c)
    seg_start = seg_end - c
    first_tile = seg_start // MOE_TILE
    tiles = jnp.where(c > 0, (seg_end - 1) // MOE_TILE - first_tile + 1, 0)
    cum = jnp.cumsum(tiles)
    n_items = cum[-1]
    item = jnp.minimum(jnp.arange(n_slots, dtype=jnp.int32), n_items - 1)
    item_exp = jnp.sum(item[:, None] >= cum[None, :], axis=1).astype(jnp.int32)
    shares = ((c > 0) & (seg_start % MOE_TILE != 0)).astype(jnp.int32)
    item_tile = item - jnp.sum((item[:, None] >= (cum - tiles)[None, :]) * shares[None, :], axis=1)
    idle = jnp.maximum(jnp.arange(n_slots, dtype=jnp.int32) - (n_items - 1), 0)
    item_tile = jnp.minimum(item_tile + idle, n_slots - N_EXPERTS - 1)
    return (seg_start.astype(jnp.int32), seg_end.astype(jnp.int32), item_tile.astype(jnp.int32), item_exp,
            n_items.reshape(1).astype(jnp.int32))


def _token_tile(ref, t):
    return ref.at[pl.ds(pl.multiple_of(t * TOKEN_TILE_ROWS, TOKEN_TILE_ROWS), TOKEN_TILE_ROWS)]


def _store_token_tiles(ref, x):
    rows = x.shape[0]
    for c in range(TOKEN_TILE_ROWS):
        ref[pl.ds(c, rows, stride=TOKEN_TILE_ROWS), :] = x[:, c * LANES:(c + 1) * LANES]


def _load_token_tiles(ref, rows):
    return [ref[pl.ds(c, rows, stride=TOKEN_TILE_ROWS), :] for c in range(TOKEN_TILE_ROWS)]


def _pair_row(seg_ref, pair):
    return seg_ref[pair >> RANK_BITS] + (pair & ((1 << RANK_BITS) - 1))


def _pieces_kernel(seg_ref, npad_ref, src_ref, base_ref):
    n_tiles = npad_ref.shape[0]
    n_pieces = src_ref.shape[0]

    def clear(g, carry):
        src_ref[g] = 0
        return carry

    lax.fori_loop(0, n_pieces, clear, 0, unroll=DMA_UNROLL)
    for e in range(N_EXPERTS):
        base_ref[e] = seg_ref[e] // PIECE

    def tile(t, carry):
        def expert(e, local):
            k = npad_ref[t, EXPERT_LANE0 + e] // PIECE
            first = base_ref[e]

            def piece(j, c):
                src_ref[first + j] = t * (LOCAL_ROWS // PIECE) + local + j
                return c

            lax.fori_loop(0, k, piece, 0)
            base_ref[e] = first + k
            return local + k

        lax.fori_loop(0, N_EXPERTS, expert, 0)
        return carry

    lax.fori_loop(0, n_tiles, tile, 0)


def _pieces(seg, npad, n_sorted_rows):
    return pl.pallas_call(
        _pieces_kernel,
        in_specs=[pl.BlockSpec(memory_space=pltpu.SMEM), pl.BlockSpec(memory_space=pltpu.SMEM)],
        out_specs=pl.BlockSpec(memory_space=pltpu.SMEM),
        out_shape=jax.ShapeDtypeStruct((n_sorted_rows // PIECE,), jnp.int32),
        scratch_shapes=[pltpu.SMEM((N_EXPERTS,), jnp.int32)],
        compiler_params=pltpu.CompilerParams(vmem_limit_bytes=VMEM_LIMIT),
        name="moe_pieces",
    )(seg, npad)


def _row_gather(src_ref, idx_of, dst_ref, dst_row0, sem, n_rows, rows_per_copy=1):
    span = rows_per_copy * TOKEN_TILE_ROWS

    def slab(ref, i):
        return ref.at[pl.ds(pl.multiple_of(i * span, span), span)]

    def copy(t):
        return pltpu.make_async_copy(slab(src_ref, idx_of(t)), slab(dst_ref, dst_row0 + t), sem)

    def start():
        def body(t, carry):
            copy(2 * t).start(priority=0)
            copy(2 * t + 1).start(priority=1)
            return carry
        lax.fori_loop(0, n_rows // 2, body, 0, unroll=DMA_UNROLL // 2)

    def wait():
        def body(t, carry):
            copy(t).wait()
            return carry
        lax.fori_loop(0, n_rows, body, 0, unroll=DMA_UNROLL)

    return start, wait


def _experts_kernel(lo_ref, hi_ref, tile_ref, exp_ref, items_ref, src_ref, h_ref,
                    wg_ref, wu_ref, wd_ref, ys_ref, xbuf_ref, sems, wg_s, wu_s, wd_s, *, n_tiles):
    i = pl.program_id(0)
    prev = jnp.maximum(i - 1, 0)
    e = exp_ref[i]
    tile = tile_ref[i]
    slot = tile % 2
    active = i < items_ref[0]
    last_tile = tile_ref[items_ref[0] - 1]
    first_visit = (i == 0) | (tile != tile_ref[prev])

    @pl.when((i == 0) | (e != exp_ref[prev]))
    def _():
        wg_s[...] = wg_ref[0].astype(BF16)
        wu_s[...] = wu_ref[0].astype(BF16)
        wd_s[...] = wd_ref[0].astype(BF16)

    tile_pieces = MOE_TILE // PIECE
    next_tile = jnp.minimum(tile + 1, n_tiles - 1)
    start_this, wait_this = _row_gather(h_ref, lambda t: src_ref[tile * tile_pieces + t], xbuf_ref,
                                        slot * tile_pieces, sems.at[slot], tile_pieces, PIECE)
    start_next, _ = _row_gather(h_ref, lambda t: src_ref[next_tile * tile_pieces + t], xbuf_ref,
                                (1 - slot) * tile_pieces, sems.at[1 - slot], tile_pieces, PIECE)

    @pl.when(active & first_visit)
    def _():
        @pl.when(i == 0)
        def _():
            start_this()

        @pl.when(tile < last_tile)
        def _():
            start_next()

        wait_this()

    @pl.when(jnp.logical_not(active) & (tile > last_tile))
    def _():
        ys_ref[...] = jnp.zeros_like(ys_ref)

    @pl.when(active)
    def _():
        base = pl.multiple_of(slot * (MOE_TILE * TOKEN_TILE_ROWS), MOE_TILE * TOKEN_TILE_ROWS)
        x = jnp.concatenate([xbuf_ref[pl.ds(base + c, MOE_TILE, stride=TOKEN_TILE_ROWS), :]
                             for c in range(TOKEN_TILE_ROWS)], axis=-1).astype(BF16)
        act = _silu(_dot(x, wg_s[...])) * _dot(x, wu_s[...])
        out = _dot(act.astype(BF16), wd_s[...])
        row = tile_ref[i] * MOE_TILE + lax.broadcasted_iota(jnp.int32, (MOE_TILE, 1), 0)
        mine = (row >= lo_ref[e]) & (row < hi_ref[e])
        first_visit = (i == 0) | (tile_ref[i] != tile_ref[prev])

        @pl.when(first_visit)
        def _():
            _store_token_tiles(ys_ref, jnp.where(mine, out, 0.0))

        @pl.when(jnp.logical_not(first_visit))
        def _():
            old = jnp.concatenate(_load_token_tiles(ys_ref, MOE_TILE), axis=-1)
            _store_token_tiles(ys_ref, jnp.where(mine, out, old))


def _experts(seg_lo, seg_hi, item_tile, item_exp, n_items, src, xl, wg, wu, wd):
    tile_pieces = MOE_TILE // PIECE
    n_tiles = src.shape[0] // tile_pieces
    _, d, de = wg.shape

    def w_spec(shape):
        return pl.BlockSpec((1,) + shape, lambda i, lo, hi, tile, ex, items, pieces: (ex[i], 0, 0))

    return pl.pallas_call(
        functools.partial(_experts_kernel, n_tiles=n_tiles),
        grid_spec=pltpu.PrefetchScalarGridSpec(
            num_scalar_prefetch=6, grid=(item_tile.shape[0],),
            in_specs=[pl.BlockSpec(memory_space=pltpu.HBM),
                      w_spec((d, de)), w_spec((d, de)), w_spec((de, d))],
            out_specs=pl.BlockSpec((MOE_TILE * TOKEN_TILE_ROWS, LANES),
                                   lambda i, lo, hi, tile, ex, items, pieces: (tile[i], 0)),
            scratch_shapes=[pltpu.VMEM((2 * MOE_TILE * TOKEN_TILE_ROWS, LANES), F32),
                            pltpu.SemaphoreType.DMA((2,)),
                            pltpu.VMEM((d, de), BF16), pltpu.VMEM((d, de), BF16), pltpu.VMEM((de, d), BF16)]),
        out_shape=jax.ShapeDtypeStruct((n_tiles * MOE_TILE * TOKEN_TILE_ROWS, LANES), F32),
        compiler_params=_cparams(("arbitrary",)),
        name="moe_experts",
    )(seg_lo, seg_hi, item_tile, item_exp, n_items, src, xl, wg, wu, wd)


def _combine_kernel(seg_ref, p1_ref, p2_ref, p1n_ref, p2n_ref, rw_ref, x1_ref, gate_ref, shift_ref, scale_ref,
                    nfin_ref, ys_ref, y_ref, r1_ref, r2_ref, sems, *, n_steps):
    _, rows, d = x1_ref.shape
    s = pl.program_id(0)
    slot = s % 2

    def gathers(pa_ref, pb_ref, slot):
        g1 = _row_gather(ys_ref, lambda t: _pair_row(seg_ref, pa_ref[t]), r1_ref, slot * rows, sems.at[slot], rows)
        g2 = _row_gather(ys_ref, lambda t: _pair_row(seg_ref, pb_ref[t]), r2_ref, slot * rows, sems.at[slot], rows)
        return g1, g2

    this = gathers(p1_ref, p2_ref, slot)
    nxt = gathers(p1n_ref, p2n_ref, 1 - slot)

    @pl.when(s == 0)
    def _():
        this[0][0]()
        this[1][0]()

    @pl.when(s + 1 < n_steps)
    def _():
        nxt[0][0]()
        nxt[1][0]()

    this[0][1]()
    this[1][1]()

    def per_token(ref):
        per_seq = ref[0]
        seqs = per_seq.shape[0]
        return jnp.broadcast_to(per_seq[:, None, :], (seqs, rows // seqs, d)).reshape(rows, d)

    rw = rw_ref[...]
    w1 = rw[:, 0:1]
    w2 = rw[:, 1:2]
    base = pl.multiple_of(slot * (rows * TOKEN_TILE_ROWS), rows * TOKEN_TILE_ROWS)
    moe = jnp.concatenate(
        [w1 * r1_ref[pl.ds(base + c, rows, stride=TOKEN_TILE_ROWS), :]
         + w2 * r2_ref[pl.ds(base + c, rows, stride=TOKEN_TILE_ROWS), :] for c in range(TOKEN_TILE_ROWS)], axis=-1)
    x2 = x1_ref[0] + per_token(gate_ref) * moe
    hf = _rms(x2) * nfin_ref[...]
    y_ref[0] = hf * (1.0 + per_token(scale_ref)) + per_token(shift_ref)


def _combine(seg, pair1, pair2, rw, x1, ada3, adaf3, nfin, ys, rows, tile0):
    b, t, d = x1.shape
    n_steps = b * t // rows
    seqs = max(rows // t, 1)
    tiles_per_seq = max(t // rows, 1)
    x3 = x1.reshape(n_steps, rows, d)
    ada_v = ada3.reshape(b // seqs, seqs, -1)
    adaf_v = adaf3.reshape(b // seqs, seqs, -1)
    x_spec = pl.BlockSpec((1, rows, d), lambda i: (i, 0, 0))

    def idx_spec(step):
        return pl.BlockSpec((rows,), lambda i: (tile0 + step(i),), memory_space=pltpu.SMEM)

    def ada_spec(col):
        return pl.BlockSpec((1, seqs, d), lambda i: (i // tiles_per_seq, 0, col))

    def cur(i):
        return i

    def nxt(i):
        return jnp.minimum(i + 1, n_steps - 1)

    return pl.pallas_call(
        functools.partial(_combine_kernel, n_steps=n_steps),
        grid=(n_steps,),
        in_specs=[pl.BlockSpec(memory_space=pltpu.SMEM), idx_spec(cur), idx_spec(cur), idx_spec(nxt), idx_spec(nxt),
                  pl.BlockSpec((rows, ROUTER_LANES), lambda i: (tile0 + i, 0)),
                  x_spec, ada_spec(5), ada_spec(0), ada_spec(1),
                  pl.BlockSpec((1, d), lambda i: (0, 0)),
                  pl.BlockSpec(memory_space=pltpu.HBM)],
        out_specs=x_spec,
        scratch_shapes=[pltpu.VMEM((2 * rows * TOKEN_TILE_ROWS, LANES), F32),
                        pltpu.VMEM((2 * rows * TOKEN_TILE_ROWS, LANES), F32),
                        pltpu.SemaphoreType.DMA((2,))],
        out_shape=jax.ShapeDtypeStruct(x3.shape, F32),
        compiler_params=_cparams(("arbitrary",)),
        name="moe_combine",
    )(seg, pair1, pair2, pair1, pair2, rw, x3, ada_v, adaf_v, adaf_v, nfin, ys).reshape(x1.shape)


def _mixer(x, ada3, s_h, s_re, s_im, p, *, bb, tt, hgrn_seqs, hgrn_tokens, sequential):
    b, t, d = x.shape
    n = b * t
    q, k, g, v, gs, u = _inproj(x, ada3, p['norm_mix'], p['w_in'], p['lb'], bb, tt)
    oh, s_h_new = _hgrn(q, k, g, v, gs, s_h, p['hgrn_norm'], b, t, hgrn_seqs, hgrn_tokens)
    if sequential:
        xr, xi = s_re.reshape(b, 1, -1), s_im.reshape(b, 1, -1)
        n_tokens = t
    else:
        xr, xi = s_re.reshape(1, b, -1), s_im.reshape(1, b, -1)
        n_tokens = n
    ys, fr, fi = _s5(u, xr, xi, p['ssm_w1'], p['ssm_m'], p['ssm_a8'], n_tokens, sequential)
    states = (s_h_new[None], fr.reshape(1, b, s_re.shape[-2], s_re.shape[-1]),
              fi.reshape(1, b, s_re.shape[-2], s_re.shape[-1]))
    return oh, ys, states


def kernel(x_prompt, x_sample, c_prompt, c_sample, state_hgrn, state_ssm_re, state_ssm_im, hgrn_lb_logits, w_ada, b_ada, norm_mix, w_in, hgrn_norm, ssm_a_re, ssm_a_im, ssm_log_dt, ssm_b_re, ssm_b_im, ssm_c_re, ssm_c_im, ssm_d, ssm_w_glu, ssm_b_glu, ssm_norm, w_out, norm_ffn, moe_w_group, moe_b_group, moe_w_router, moe_b_router, moe_w_gate, moe_w_up, moe_w_down, w_ada_final, b_ada_final, norm_final):
    depth = w_ada.shape[0]
    assert depth == 1
    d = x_prompt.shape[-1]
    bp = x_prompt.shape[0]
    dh = hgrn_norm.shape[-1]
    dk = dh // HGRN_HEADS
    de = moe_w_gate.shape[-1]
    n_exp = MOE_GROUPS * MOE_PER_GROUP

    lb = jax.nn.softmax(hgrn_lb_logits.astype(F32), axis=0)[0].reshape(1, dh)
    w1, m, a8 = _s5_prepare(ssm_a_re[0], ssm_a_im[0], ssm_log_dt[0], ssm_b_re[0], ssm_b_im[0],
                            ssm_c_re[0], ssm_c_im[0], ssm_d[0])
    w_rt = jnp.zeros((d, ROUTER_LANES), F32)
    w_rt = w_rt.at[:, :MOE_GROUPS].set(moe_w_group[0])
    w_rt = w_rt.at[:, EXPERT_LANE0:EXPERT_LANE0 + n_exp].set(
        moe_w_router[0].transpose(1, 0, 2).reshape(d, n_exp))
    b_rt = jnp.zeros((1, ROUTER_LANES), F32)
    b_rt = b_rt.at[0, :MOE_GROUPS].set(moe_b_group[0])
    b_rt = b_rt.at[0, EXPERT_LANE0:EXPERT_LANE0 + n_exp].set(moe_b_router[0].reshape(n_exp))
    p = dict(
        lb=lb, norm_mix=norm_mix[0].reshape(1, d), w_in=w_in[0].astype(BF16),
        hgrn_norm=hgrn_norm[0].reshape(1, dh),
        ssm_w1=w1, ssm_m=m, ssm_a8=a8,
        w_glu=ssm_w_glu[0].astype(BF16), b_glu=ssm_b_glu[0].reshape(1, -1), ssm_norm=ssm_norm[0].reshape(1, -1),
        wo_h=w_out[0, :dh].astype(BF16), wo_s=w_out[0, dh:].astype(BF16),
        norm_ffn=norm_ffn[0].reshape(1, d), w_rt=w_rt.astype(BF16), b_rt=b_rt,
    )
    wg = moe_w_gate[0].reshape(n_exp, d, de)
    wu = moe_w_up[0].reshape(n_exp, d, de)
    wd = moe_w_down[0].reshape(n_exp, de, d)
    nfin = norm_final.reshape(1, d)

    c_all = jnp.concatenate([c_prompt, c_sample], axis=0)
    ada = _silu_linear(c_all, w_ada[0], b_ada[0])
    adaf = _silu_linear(c_all, w_ada_final, b_ada_final)
    ada_p, ada_s = ada[:bp].reshape(bp, 1, -1), ada[bp:].reshape(x_sample.shape[0], 1, -1)
    adaf_p, adaf_s = adaf[:bp].reshape(bp, 1, -1), adaf[bp:].reshape(x_sample.shape[0], 1, -1)

    zeros_h = jnp.zeros((bp, HGRN_HEADS, dk, dk), F32)
    zeros_s = jnp.zeros((bp,) + state_ssm_re.shape[2:], F32)
    oh_p, ys_p, st_p = _mixer(x_prompt, ada_p, zeros_h, zeros_s, zeros_s, p,
                              bb=1, tt=256, hgrn_seqs=1, hgrn_tokens=512, sequential=True)
    oh_s, ys_s, st_s = _mixer(x_sample, ada_s, state_hgrn[0], state_ssm_re[0], state_ssm_im[0], p,
                              bb=32, tt=8, hgrn_seqs=16, hgrn_tokens=8, sequential=False)
    x1_p, x1_s, xl, pair1, pair2, rw, cnt, npad = _outproj(
        x_prompt, x_sample, oh_p, oh_s, ys_p, ys_s, ada_p, ada_s, p['w_glu'], p['b_glu'], p['ssm_norm'],
        p['wo_h'], p['wo_s'], p['norm_ffn'], p['w_rt'], p['b_rt'], TOKEN_ROWS)

    n_tok_tiles = pair1.shape[0] // TOKEN_ROWS
    max_rows = 2 * pair1.shape[0] + n_tok_tiles * N_EXPERTS * (PIECE - 1)
    n_sorted_rows = -(-max_rows // MOE_TILE) * MOE_TILE
    n_slots = n_sorted_rows // MOE_TILE + N_EXPERTS
    seg, seg_end, item_tile, item_exp, n_items = _moe_schedule(cnt, n_slots)
    src = _pieces(seg, npad.reshape(n_tok_tiles, ROUTER_LANES), n_sorted_rows)
    ys = _experts(seg, seg_end, item_tile, item_exp, n_items, src, xl, wg, wu, wd)
    tiles_p = x_prompt.shape[0] * x_prompt.shape[1] // TOKEN_ROWS
    y_p = _combine(seg, pair1, pair2, rw, x1_p, ada_p, adaf_p, nfin, ys, TOKEN_ROWS, 0)
    y_s = _combine(seg, pair1, pair2, rw, x1_s, ada_s, adaf_s, nfin, ys, TOKEN_ROWS, tiles_p)
    return (y_p, y_s) + st_p + st_s
```

```python
import functools
import math

import jax
import jax.numpy as jnp
from jax import lax
from jax.experimental import pallas as pl
from jax.experimental.pallas import tpu as pltpu

F32 = jnp.float32
BF16 = jnp.bfloat16
HIGHEST = lax.Precision.HIGHEST

EPS = 1e-6
MAX_REAL = -1e-4
HGRN_HEADS = 4
HGRN_CHUNK = 128
HGRN_SAFE_EXPONENT = 80.0
HGRN_EXACT_BLOCK = 8
SSM_GROUP = 16
SSM_STATE = 64
SSM_CHUNK = 8
SSM_SETS = 4
MOE_GROUPS = 4
MOE_PER_GROUP = 8
N_EXPERTS = MOE_GROUPS * MOE_PER_GROUP
ROUTER_LANES = 128
EXPERT_LANE0 = 32
RANK_BITS = 20
MOE_TILE = 256
LANES = 128
TOKEN_TILE_ROWS = 8
DMA_UNROLL = 8
TOKEN_ROWS = 256
VMEM_LIMIT = 56 * 1024 * 1024


def _cparams(sem):
    return pltpu.CompilerParams(dimension_semantics=sem, vmem_limit_bytes=VMEM_LIMIT)


def _silu(x):
    return x * jax.nn.sigmoid(x)


def _rms(x):
    return x * lax.rsqrt(jnp.mean(x * x, axis=-1, keepdims=True) + EPS)


def _dot(a, b):
    return jnp.dot(a, b, preferred_element_type=F32)


def _dot_nt(a, b):
    return lax.dot_general(a, b, (((1,), (1,)), ((), ())), preferred_element_type=F32)


def _dot_tn(a, b, precision=None):
    return lax.dot_general(a, b, (((0,), (0,)), ((), ())), preferred_element_type=F32,
                           precision=precision)


def _silu_linear_kernel(c_ref, w_ref, b_ref, o_ref):
    a = _silu(c_ref[...]).astype(BF16)
    o_ref[...] = _dot(a, w_ref[...].astype(BF16)) + b_ref[...]


def _silu_linear(c, w, b):
    m, d = c.shape
    n = w.shape[1]
    tn = 1024
    return pl.pallas_call(
        _silu_linear_kernel,
        grid=(n // tn,),
        in_specs=[pl.BlockSpec((m, d), lambda j: (0, 0)),
                  pl.BlockSpec((d, tn), lambda j: (0, j)),
                  pl.BlockSpec((1, tn), lambda j: (0, j))],
        out_specs=pl.BlockSpec((m, tn), lambda j: (0, j)),
        out_shape=jax.ShapeDtypeStruct((m, n), F32),
        compiler_params=_cparams(("parallel",)),
        name="silu_linear",
    )(c, w, b.reshape(1, n))


def _inproj_kernel(x_ref, shift_ref, scale_ref, gain_ref, w_ref, lb_ref,
                   q_ref, k_ref, g_ref, v_ref, gs_ref, u_ref, *, dh):
    bb, tt, d = x_ref.shape
    h = _rms(x_ref[...]) * gain_ref[...]
    h = h * (1.0 + scale_ref[...]) + shift_ref[...]
    proj = _dot(h.reshape(bb * tt, d).astype(BF16), w_ref[...])
    lb = lb_ref[...]
    f = lb + (1.0 - lb) * jax.nn.sigmoid(proj[:, dh:2 * dh])
    q_ref[...] = _silu(proj[:, :dh]) * (float(dh // HGRN_HEADS) ** -0.5)
    k_ref[...] = 1.0 - f
    g_ref[...] = jnp.log(f)
    v_ref[...] = proj[:, 2 * dh:3 * dh]
    gs_ref[...] = _silu(proj[:, 3 * dh:4 * dh])
    for s in range(SSM_SETS):
        u_ref[s] = proj[:, 4 * dh + s * LANES:4 * dh + (s + 1) * LANES]


def _inproj(x, ada3, gain, w_in_bf, lb, bb, tt):
    b, t, d = x.shape
    dh = lb.shape[-1]
    nt = t // tt
    rows = bb * tt
    n = b * t
    row_spec = pl.BlockSpec((rows, dh), lambda i, j: (i * nt + j, 0))
    out = jax.ShapeDtypeStruct((n, dh), F32)
    return pl.pallas_call(
        functools.partial(_inproj_kernel, dh=dh),
        grid=(b // bb, nt),
        in_specs=[pl.BlockSpec((bb, tt, d), lambda i, j: (i, j, 0)),
                  pl.BlockSpec((bb, 1, d), lambda i, j: (i, 0, 0)),
                  pl.BlockSpec((bb, 1, d), lambda i, j: (i, 0, 1)),
                  pl.BlockSpec((1, d), lambda i, j: (0, 0)),
                  pl.BlockSpec(w_in_bf.shape, lambda i, j: (0, 0)),
                  pl.BlockSpec((1, dh), lambda i, j: (0, 0))],
        out_specs=[row_spec] * 5 + [pl.BlockSpec((SSM_SETS, rows, LANES), lambda i, j: (0, i * nt + j, 0))],
        out_shape=[out] * 5 + [jax.ShapeDtypeStruct((SSM_SETS, n, LANES), F32)],
        compiler_params=_cparams(("parallel", "parallel")),
        name="inproj",
    )(x, ada3, ada3, gain, w_in_bf, lb)


def _split3(x):
    hi = x.astype(BF16)
    r1 = x - hi.astype(F32)
    mid = r1.astype(BF16)
    lo = (r1 - mid.astype(F32)).astype(BF16)
    return hi, mid, lo


def _hgrn_kernel(q_ref, k_ref, g_ref, v_ref, gs_ref, s0_ref, hn_ref, o_ref, sf_ref,
                 st_ref, intra_ref, qh_ref, kh_ref, ea_ref, sums_ref, *, tl, nt):
    j = pl.program_id(1)
    rows_total, dh = q_ref.shape
    dk = dh // HGRN_HEADS
    c = HGRN_CHUNK
    seqs = c // tl
    n_chunks = rows_total // c

    @pl.when(j == 0)
    def _():
        st_ref[...] = s0_ref[...]

    r = lax.broadcasted_iota(jnp.int32, (c, c), 0)
    s = lax.broadcasted_iota(jnp.int32, (c, c), 1)
    same_seq = (r // tl) == (s // tl)
    causal = same_seq & (r >= s)
    upto_mid = same_seq & ((s % tl) < tl // 2)
    sums = jnp.concatenate([causal, upto_mid, same_seq], axis=0).astype(BF16)
    eye3 = (lax.broadcasted_iota(jnp.int32, (dk, 3 * dk), 1) % dk
            == lax.broadcasted_iota(jnp.int32, (dk, 3 * dk), 0)).astype(BF16)

    def decay_matrix(e_row):
        parts = jnp.concatenate(_split3(e_row), axis=-1)
        return _dot_nt(eye3, jnp.broadcast_to(parts, (dk, 3 * dk)))

    worst = jnp.float32(0.0)
    for ci in range(n_chunks):
        rows = slice(ci * c, (ci + 1) * c)
        g_parts = _split3(g_ref[rows, :])
        acc = _dot(sums, g_parts[0]) + _dot(sums, g_parts[1]) + _dot(sums, g_parts[2])
        sums_ref[ci * 3 * c:(ci + 1) * 3 * c, :] = acc
        a_mid = acc[c:2 * c]
        worst = jnp.maximum(worst, jnp.max(jnp.maximum(jnp.abs(acc[:c] - a_mid), jnp.abs(acc[2 * c:] - a_mid))))
    factorised_is_safe = worst < HGRN_SAFE_EXPONENT

    @pl.when(factorised_is_safe)
    def _():
        for ci in range(n_chunks):
            rows = slice(ci * c, (ci + 1) * c)
            a = sums_ref[ci * 3 * c:ci * 3 * c + c, :]
            a_mid = sums_ref[ci * 3 * c + c:ci * 3 * c + 2 * c, :]
            a_end = sums_ref[ci * 3 * c + 2 * c:(ci + 1) * 3 * c, :]
            e_mid = jnp.exp(a_mid)
            e_tail = jnp.exp(a_end - a_mid)
            qt = q_ref[rows, :] * jnp.exp(a - a_mid)
            kt = k_ref[rows, :] * jnp.exp(a_mid - a)
            qh_ref[rows, :] = qt * e_mid
            kh_ref[rows, :] = kt * e_tail
            ea_ref[rows, :] = e_mid * e_tail
            qt = qt.astype(BF16)
            kt = kt.astype(BF16)
            v = v_ref[rows, :].astype(BF16)
            for h in range(HGRN_HEADS):
                lanes = slice(h * dk, (h + 1) * dk)
                sc = jnp.where(causal, _dot_nt(qt[:, lanes], kt[:, lanes]), 0.0).astype(BF16)
                intra_ref[rows, lanes] = _dot(sc, v[:, lanes])

            for si in range(seqs):
                seq = ci * seqs + si if tl < c else 0
                srows = slice(si * tl, (si + 1) * tl)
                orows = slice(ci * c + si * tl, ci * c + (si + 1) * tl)
                for h in range(HGRN_HEADS):
                    lanes = slice(h * dk, (h + 1) * dk)
                    state = st_ref[seq, h]
                    o = intra_ref[orows, lanes] + _dot(qh_ref[orows, lanes].astype(BF16), state.astype(BF16))
                    decay = decay_matrix(ea_ref[orows.start:orows.start + 1, lanes])
                    st_ref[seq, h] = decay * state + _dot_tn(kh_ref[orows, lanes].astype(BF16),
                                                             v[srows, lanes])
                    o_ref[orows, lanes] = _rms(o) * hn_ref[:, lanes] * gs_ref[orows, lanes]

    @pl.when(jnp.logical_not(factorised_is_safe))
    def _():
        blk = HGRN_EXACT_BLOCK
        blocks_per_seq = max(tl // blk, 1)
        tri = (lax.broadcasted_iota(jnp.int32, (blk, blk), 0)
               >= lax.broadcasted_iota(jnp.int32, (blk, blk), 1)).astype(BF16)
        sub = lax.broadcasted_iota(jnp.int32, (blk, dk), 0)

        def block(bi, carry):
            rows = pl.ds(pl.multiple_of(bi * blk, blk), blk)
            seq = bi // blocks_per_seq if tl < c else 0
            g_parts = _split3(g_ref[rows, :])
            a = _dot(tri, g_parts[0]) + _dot(tri, g_parts[1]) + _dot(tri, g_parts[2])
            a_end = a[blk - 1:blk]
            q = q_ref[rows, :]
            k = k_ref[rows, :]
            v = v_ref[rows, :]
            qh = (q * jnp.exp(a)).astype(BF16)
            kh = (k * jnp.exp(a_end - a)).astype(BF16)
            ea = jnp.exp(a_end)
            vb = v.astype(BF16)
            for h in range(HGRN_HEADS):
                lanes = slice(h * dk, (h + 1) * dk)
                state = st_ref[seq, h]
                intra = []
                for t in range(blk):
                    live = sub <= t
                    decay_t = jnp.where(live, jnp.exp(jnp.where(live, a[t:t + 1, lanes] - a[:, lanes], 0.0)), 0.0)
                    score = jnp.sum(q[t:t + 1, lanes] * k[:, lanes] * decay_t, axis=-1, keepdims=True)
                    intra.append(jnp.sum(score * v[:, lanes], axis=0, keepdims=True))
                o = jnp.concatenate(intra, axis=0) + _dot(qh[:, lanes], state.astype(BF16))
                st_ref[seq, h] = decay_matrix(ea[:, lanes]) * state + _dot_tn(kh[:, lanes], vb[:, lanes])
                o_ref[rows, lanes] = _rms(o) * hn_ref[:, lanes] * gs_ref[rows, lanes]
            return carry

        lax.fori_loop(0, rows_total // blk, block, 0)

    @pl.when(j == nt - 1)
    def _():
        sf_ref[...] = st_ref[...]


def _hgrn(q, k, g, v, gs, s0, hnorm, b, t, nseq, tt):
    n, dh = q.shape
    nt = t // tt
    dk = dh // HGRN_HEADS
    rows = nseq * tt
    tl = min(tt, HGRN_CHUNK)
    row_spec = pl.BlockSpec((rows, dh), lambda i, j: (i * nt + j, 0))
    st_spec = pl.BlockSpec((nseq, HGRN_HEADS, dk, dk), lambda i, j: (i, 0, 0, 0))
    return pl.pallas_call(
        functools.partial(_hgrn_kernel, tl=tl, nt=nt),
        grid=(b // nseq, nt),
        in_specs=[row_spec] * 5 + [st_spec, pl.BlockSpec((1, dh), lambda i, j: (0, 0))],
        out_specs=[row_spec, st_spec],
        out_shape=[jax.ShapeDtypeStruct((n, dh), F32),
                   jax.ShapeDtypeStruct((b, HGRN_HEADS, dk, dk), F32)],
        scratch_shapes=[pltpu.VMEM((nseq, HGRN_HEADS, dk, dk), F32),
                        pltpu.VMEM((rows, dh), F32), pltpu.VMEM((rows, dh), F32),
                        pltpu.VMEM((rows, dh), F32), pltpu.VMEM((rows, dh), F32),
                        pltpu.VMEM((3 * rows, dh), F32)],
        compiler_params=_cparams(("parallel", "arbitrary")),
        name="hgrn",
    )(q, k, g, v, gs, s0, hnorm)


def _s5_prepare(a_re, a_im, log_dt, b_re, b_im, c_re, c_im, d_skip):
    ng, npp = a_re.shape
    nh = b_re.shape[-1]
    L = SSM_CHUNK
    gs = ng // SSM_SETS
    lam_re = jnp.minimum(a_re, MAX_REAL)
    lam_im = a_im
    dt = jnp.exp(log_dt)
    mag = jnp.exp(lam_re * dt)
    ab_re = mag * jnp.cos(lam_im * dt)
    ab_im = mag * jnp.sin(lam_im * dt)
    den = lam_re * lam_re + lam_im * lam_im
    co_re = ((ab_re - 1.0) * lam_re + ab_im * lam_im) / den
    co_im = (ab_im * lam_re - (ab_re - 1.0) * lam_im) / den
    bb_re = co_re[..., None] * b_re - co_im[..., None] * b_im
    bb_im = co_re[..., None] * b_im + co_im[..., None] * b_re
    tau = jnp.arange(L + 1, dtype=F32)[:, None, None]
    pw_mag = jnp.exp(tau * (lam_re * dt))
    pw_re = pw_mag * jnp.cos(tau * (lam_im * dt))
    pw_im = pw_mag * jnp.sin(tau * (lam_im * dt))
    ab_b_re = pw_re[:L, :, :, None] * bb_re - pw_im[:L, :, :, None] * bb_im
    ab_b_im = pw_re[:L, :, :, None] * bb_im + pw_im[:L, :, :, None] * bb_re
    kern = (jnp.einsum('gkp,lgph->lghk', c_re, ab_b_re, precision=HIGHEST)
            - jnp.einsum('gkp,lgph->lghk', c_im, ab_b_im, precision=HIGHEST))
    skip = d_skip[None, :, :, None] * jnp.eye(nh, dtype=F32)
    kern = kern + jnp.where(jnp.arange(L)[:, None, None, None] == 0, skip, 0.0)

    def group_block_diag(c):
        rows, w = c.shape[-2:]
        tiled = jnp.concatenate([c] * gs, axis=-1)
        rg = jnp.arange(rows)[:, None] // (rows // gs)
        cq = jnp.arange(gs * w)[None, :] // w
        return jnp.where(rg == cq, tiled, 0.0).astype(BF16).transpose(1, 0, 2, 3).reshape(
            SSM_SETS, L * rows, gs * w)

    bd = group_block_diag(kern.reshape(L, SSM_SETS, gs * nh, nh)).reshape(SSM_SETS, L, gs * nh, gs * nh)
    n_re = group_block_diag(ab_b_re[::-1].transpose(0, 1, 3, 2).reshape(L, SSM_SETS, gs * nh, npp))
    n_im = group_block_diag(ab_b_im[::-1].transpose(0, 1, 3, 2).reshape(L, SSM_SETS, gs * nh, npp))
    inc = jnp.concatenate([n_re, n_im], axis=-1)
    ca_re = c_re[None] * pw_re[1:, :, None, :] - c_im[None] * pw_im[1:, :, None, :]
    ca_im = c_re[None] * pw_im[1:, :, None, :] + c_im[None] * pw_re[1:, :, None, :]
    m = jnp.concatenate([group_block_diag(ca_re.reshape(L, SSM_SETS, gs * nh, npp)),
                         group_block_diag(-ca_im.reshape(L, SSM_SETS, gs * nh, npp))], axis=-1)
    a8 = jnp.concatenate([pw_re[L].reshape(SSM_SETS, 1, gs * npp),
                          pw_im[L].reshape(SSM_SETS, 1, gs * npp)], axis=-1)
    return bd, inc, m, a8


def _s5_kernel(u_ref, xr_ref, xi_ref, bd_ref, inc_ref, m_ref, a8_ref, y_ref, fr_ref, fi_ref, wt_ref, *, sequential):
    n = u_ref.shape[1] // SSM_CHUNK
    ns = xr_ref.shape[-1]
    u = jnp.concatenate([u_ref[0, pl.ds(s, n, stride=SSM_CHUNK), :] for s in range(SSM_CHUNK)], axis=-1)

    @pl.when(pl.program_id(1) == 0)
    def _():
        wt_ref[...] = jnp.zeros_like(wt_ref)
        for s in range(SSM_CHUNK):
            for t in range(s, SSM_CHUNK):
                wt_ref[s * LANES:(s + 1) * LANES, t * LANES:(t + 1) * LANES] = bd_ref[0, t - s]

    u = u.astype(BF16)
    y_local = _dot(u, wt_ref[...])
    inc = _dot(u, inc_ref[0])
    d_re = inc[:, :ns]
    d_im = inc[:, ns:]
    a_re = a8_ref[0][:, :ns]
    a_im = a8_ref[0][:, ns:]
    x0_re = xr_ref[0]
    x0_im = xi_ref[0]
    if sequential:
        row = lax.broadcasted_iota(jnp.int32, (n, ns), 0)
        first = row == 0
        x_re = d_re + jnp.where(first, a_re * x0_re - a_im * x0_im, 0.0)
        x_im = d_im + jnp.where(first, a_re * x0_im + a_im * x0_re, 0.0)
        p_re, p_im = a_re, a_im
        step = 1
        while step < n:
            s_re = jnp.where(row >= step, pltpu.roll(x_re, step, 0), 0.0)
            s_im = jnp.where(row >= step, pltpu.roll(x_im, step, 0), 0.0)
            x_re, x_im = x_re + p_re * s_re - p_im * s_im, x_im + p_re * s_im + p_im * s_re
            p_re, p_im = p_re * p_re - p_im * p_im, 2.0 * p_re * p_im
            step *= 2
        fr_ref[0] = x_re[n - 1:n]
        fi_ref[0] = x_im[n - 1:n]
        xc_re = jnp.where(first, x0_re, pltpu.roll(x_re, 1, 0))
        xc_im = jnp.where(first, x0_im, pltpu.roll(x_im, 1, 0))
    else:
        xc_re, xc_im = x0_re, x0_im
        fr_ref[0] = a_re * x0_re - a_im * x0_im + d_re
        fi_ref[0] = a_re * x0_im + a_im * x0_re + d_im
    xc = jnp.concatenate([xc_re, xc_im], axis=-1).astype(BF16)
    y = y_local + _dot_nt(xc, m_ref[0])
    for t in range(SSM_CHUNK):
        y_ref[0, pl.ds(t, n, stride=SSM_CHUNK), :] = y[:, t * LANES:(t + 1) * LANES]


def _s5(u, x_re, x_im, bd, inc, m, a8, n_tokens, sequential):
    sets = u.shape[0]
    nb, rb, _ = x_re.shape
    ns = m.shape[1] // 2
    st_spec = pl.BlockSpec((1, rb, ns), lambda gi, i: (i, 0, gi))
    st_shape = jax.ShapeDtypeStruct(x_re.shape, F32)
    tok_spec = pl.BlockSpec((1, n_tokens, LANES), lambda gi, i: (gi, i, 0))
    return pl.pallas_call(
        functools.partial(_s5_kernel, sequential=sequential),
        grid=(sets, nb),
        in_specs=[tok_spec, st_spec, st_spec,
                  pl.BlockSpec((1,) + bd.shape[1:], lambda gi, i: (gi, 0, 0, 0)),
                  pl.BlockSpec((1,) + inc.shape[1:], lambda gi, i: (gi, 0, 0)),
                  pl.BlockSpec((1,) + m.shape[1:], lambda gi, i: (gi, 0, 0)),
                  pl.BlockSpec((1, 1, 2 * ns), lambda gi, i: (gi, 0, 0))],
        out_specs=[tok_spec, st_spec, st_spec],
        out_shape=[jax.ShapeDtypeStruct(u.shape, F32), st_shape, st_shape],
        scratch_shapes=[pltpu.VMEM((inc.shape[1], SSM_CHUNK * LANES), BF16)],
        compiler_params=_cparams(("parallel", "arbitrary")),
        name="s5",
    )(u, x_re, x_im, bd, inc, m, a8)


def _gelu_tanh(x):
    return 0.5 * x * (1.0 + jnp.tanh(math.sqrt(2.0 / math.pi) * (x + 0.044715 * (x * x * x))))


def _outproj_kernel(xp_ref, xs_ref, ohp_ref, ohs_ref, ysp_ref, yss_ref,
                    gate_p_ref, shift_p_ref, scale_p_ref, gate_s_ref, shift_s_ref, scale_s_ref,
                    wglu_ref, bglu_ref, sn_ref, wo_h_ref, wo_s_ref, nf_ref, wr_ref, br_ref,
                    x1p_ref, x1s_ref, h2_ref, pair1_ref, pair2_ref, rw_ref, cnt_ref, *, prompt_tiles):
    is_prompt = pl.program_id(0) < prompt_tiles
    _, rows, d = xp_ref.shape

    def per_token(p_ref, s_ref):
        per_seq = s_ref[0]
        seqs = per_seq.shape[0]
        rep = jnp.broadcast_to(per_seq[:, None, :], (seqs, rows // seqs, d)).reshape(rows, d)
        return jnp.where(is_prompt, p_ref[0], rep)

    x = jnp.where(is_prompt, xp_ref[0], xs_ref[0])
    oh = jnp.where(is_prompt, ohp_ref[...], ohs_ref[...])
    ys = jnp.concatenate([jnp.where(is_prompt, ysp_ref[s], yss_ref[s]) for s in range(SSM_SETS)], axis=-1)
    y = _gelu_tanh(ys)
    y = y * jax.nn.sigmoid(_dot(y.astype(BF16), wglu_ref[...]) + bglu_ref[...])
    o_s = _rms(y) * sn_ref[...]
    mix = _dot(oh.astype(BF16), wo_h_ref[...]) + _dot(o_s.astype(BF16), wo_s_ref[...])
    x1 = x + per_token(gate_p_ref, gate_s_ref) * mix

    @pl.when(is_prompt)
    def _():
        x1p_ref[0] = x1

    @pl.when(jnp.logical_not(is_prompt))
    def _():
        x1s_ref[0] = x1

    h2 = _rms(x1) * nf_ref[...]
    h2 = h2 * (1.0 + per_token(scale_p_ref, scale_s_ref)) + per_token(shift_p_ref, shift_s_ref)
    _store_token_tiles(h2_ref, h2)

    logits = _dot(h2.astype(BF16), wr_ref[...]) + br_ref[...]
    lane = lax.broadcasted_iota(jnp.int32, logits.shape, 1)
    neg = -jnp.inf
    gl = jnp.where(lane < MOE_GROUPS, logits, neg)
    gmax = jnp.max(gl, axis=-1, keepdims=True)
    gidx = jnp.min(jnp.where(gl == gmax, lane, ROUTER_LANES), axis=-1, keepdims=True)
    grp_w = 1.0 / jnp.sum(jnp.exp(gl - gmax), axis=-1, keepdims=True)
    e0 = EXPERT_LANE0 + gidx * MOE_PER_GROUP
    sel = jnp.where((lane >= e0) & (lane < e0 + MOE_PER_GROUP), logits, neg)
    m1 = jnp.max(sel, axis=-1, keepdims=True)
    i1 = jnp.min(jnp.where(sel == m1, lane, ROUTER_LANES), axis=-1, keepdims=True)
    sel2 = jnp.where(lane == i1, neg, sel)
    m2 = jnp.max(sel2, axis=-1, keepdims=True)
    i2 = jnp.min(jnp.where(sel2 == m2, lane, ROUTER_LANES), axis=-1, keepdims=True)
    e2 = jnp.exp(m2 - m1)
    w1 = 1.0 / (1.0 + e2)
    w2 = e2 / (1.0 + e2)
    rw_ref[...] = grp_w * (jnp.where(lane == 0, w1, 0.0) + jnp.where(lane == 1, w2, 0.0))

    @pl.when(pl.program_id(0) == 0)
    def _():
        cnt_ref[...] = jnp.zeros_like(cnt_ref)

    picked = (lane == i1) | (lane == i2)
    earlier = (lax.broadcasted_iota(jnp.int32, (rows, rows), 0)
               > lax.broadcasted_iota(jnp.int32, (rows, rows), 1))
    base = cnt_ref[...]
    before = _dot(earlier.astype(BF16), picked.astype(BF16)) + base
    rank1 = jnp.sum(jnp.where(lane == i1, before, 0.0), axis=-1, keepdims=True).astype(jnp.int32)
    rank2 = jnp.sum(jnp.where(lane == i2, before, 0.0), axis=-1, keepdims=True).astype(jnp.int32)
    cnt_ref[...] = base + jnp.sum(picked.astype(F32), axis=0, keepdims=True)
    info = jnp.where(lane == 0, ((i1 - EXPERT_LANE0) << RANK_BITS) | rank1,
                     jnp.where(lane == 1, ((i2 - EXPERT_LANE0) << RANK_BITS) | rank2, 0)).T
    pair1_ref[...] = info[0]
    pair2_ref[...] = info[1]


def _outproj(x_p, x_s, oh_p, oh_s, ys_p, ys_s, ada_p, ada_s, wglu_bf, bglu, snorm, wo_h, wo_s, nffn, wr, br, rows):
    d = x_p.shape[-1]
    dh = oh_p.shape[-1]
    n_p = x_p.shape[0] * x_p.shape[1]
    n_s = x_s.shape[0] * x_s.shape[1]
    n = n_p + n_s
    tiles_p = n_p // rows
    seqs = rows // x_s.shape[1]

    def pt(i):
        return jnp.minimum(i, tiles_p - 1)

    def st(i):
        return jnp.maximum(i - tiles_p, 0)

    tiles_per_seq = x_p.shape[1] // rows

    def ada_p_spec(col):
        return pl.BlockSpec((1, 1, d), lambda i: (pt(i) // tiles_per_seq, 0, col))

    def ada_s_spec(col):
        return pl.BlockSpec((1, seqs, d), lambda i: (st(i), 0, col))

    def full(a):
        return pl.BlockSpec(a.shape, lambda i: (0,) * a.ndim)

    xp3 = x_p.reshape(tiles_p, rows, d)
    xs3 = x_s.reshape(n_s // rows, rows, d)
    ada_s3 = ada_s.reshape(n_s // rows, seqs, -1)
    outs = pl.pallas_call(
        functools.partial(_outproj_kernel, prompt_tiles=tiles_p),
        grid=(n // rows,),
        in_specs=[pl.BlockSpec((1, rows, d), lambda i: (pt(i), 0, 0)),
                  pl.BlockSpec((1, rows, d), lambda i: (st(i), 0, 0)),
                  pl.BlockSpec((rows, dh), lambda i: (pt(i), 0)),
                  pl.BlockSpec((rows, dh), lambda i: (st(i), 0)),
                  pl.BlockSpec((SSM_SETS, rows, LANES), lambda i: (0, pt(i), 0)),
                  pl.BlockSpec((SSM_SETS, rows, LANES), lambda i: (0, st(i), 0)),
                  ada_p_spec(2), ada_p_spec(3), ada_p_spec(4), ada_s_spec(2), ada_s_spec(3), ada_s_spec(4),
                  full(wglu_bf), full(bglu), full(snorm), full(wo_h), full(wo_s), full(nffn),
                  full(wr), full(br)],
        out_specs=[pl.BlockSpec((1, rows, d), lambda i: (pt(i), 0, 0)),
                   pl.BlockSpec((1, rows, d), lambda i: (st(i), 0, 0)),
                   pl.BlockSpec((rows * TOKEN_TILE_ROWS, LANES), lambda i: (i, 0)),
                   pl.BlockSpec((rows,), lambda i: (i,)),
                   pl.BlockSpec((rows,), lambda i: (i,)),
                   pl.BlockSpec((rows, ROUTER_LANES), lambda i: (i, 0)),
                   pl.BlockSpec((1, ROUTER_LANES), lambda i: (0, 0))],
        out_shape=[jax.ShapeDtypeStruct(xp3.shape, F32),
                   jax.ShapeDtypeStruct(xs3.shape, F32),
                   jax.ShapeDtypeStruct((n * TOKEN_TILE_ROWS, LANES), F32),
                   jax.ShapeDtypeStruct((n,), jnp.int32),
                   jax.ShapeDtypeStruct((n,), jnp.int32),
                   jax.ShapeDtypeStruct((n, ROUTER_LANES), F32),
                   jax.ShapeDtypeStruct((1, ROUTER_LANES), F32)],
        compiler_params=_cparams(("arbitrary",)),
        name="outproj",
    )(xp3, xs3, oh_p, oh_s, ys_p, ys_s, ada_p, ada_p, ada_p, ada_s3, ada_s3, ada_s3,
      wglu_bf, bglu, snorm, wo_h, wo_s, nffn, wr, br)
    x1_p, x1_s = outs[0].reshape(x_p.shape), outs[1].reshape(x_s.shape)
    return (x1_p, x1_s) + tuple(outs[2:])


def _moe_schedule(cnt, n_slots):
    c = cnt[0, EXPERT_LANE0:EXPERT_LANE0 + N_EXPERTS].astype(jnp.int32)
    seg_end = jnp.cumsum(c)
    seg_start = seg_end - c
    first_tile = seg_start // MOE_TILE
    tiles = jnp.where(c > 0, (seg_end - 1) // MOE_TILE - first_tile + 1, 0)
    cum = jnp.cumsum(tiles)
    n_items = cum[-1]
    item = jnp.minimum(jnp.arange(n_slots, dtype=jnp.int32), n_items - 1)
    item_exp = jnp.sum(item[:, None] >= cum[None, :], axis=1).astype(jnp.int32)
    shares = ((c > 0) & (seg_start % MOE_TILE != 0)).astype(jnp.int32)
    item_tile = item - jnp.sum((item[:, None] >= (cum - tiles)[None, :]) * shares[None, :], axis=1)
    return (seg_start.astype(jnp.int32), seg_end.astype(jnp.int32), item_tile.astype(jnp.int32), item_exp,
            n_items.reshape(1).astype(jnp.int32))


def _token_tile(ref, t):
    return ref.at[pl.ds(pl.multiple_of(t * TOKEN_TILE_ROWS, TOKEN_TILE_ROWS), TOKEN_TILE_ROWS)]


def _store_token_tiles(ref, x):
    rows = x.shape[0]
    for c in range(TOKEN_TILE_ROWS):
        ref[pl.ds(c, rows, stride=TOKEN_TILE_ROWS), :] = x[:, c * LANES:(c + 1) * LANES]


def _load_token_tiles(ref, rows):
    return [ref[pl.ds(c, rows, stride=TOKEN_TILE_ROWS), :] for c in range(TOKEN_TILE_ROWS)]


def _pair_row(seg_ref, pair):
    return seg_ref[pair >> RANK_BITS] + (pair & ((1 << RANK_BITS) - 1))


def _inverse_kernel(seg_ref, p1_ref, p2_ref, inv_ref):
    ts = p1_ref.shape[0]
    base = pl.program_id(0) * ts

    def body(t, carry):
        inv_ref[_pair_row(seg_ref, p1_ref[t])] = base + t
        inv_ref[_pair_row(seg_ref, p2_ref[t])] = base + t
        return carry

    lax.fori_loop(0, ts, body, 0, unroll=DMA_UNROLL)


def _inverse(seg, pair1, pair2, ts):
    n = pair1.shape[0]
    idx_spec = pl.BlockSpec((ts,), lambda i: (i,), memory_space=pltpu.SMEM)
    return pl.pallas_call(
        _inverse_kernel,
        grid=(n // ts,),
        in_specs=[pl.BlockSpec(memory_space=pltpu.SMEM), idx_spec, idx_spec],
        out_specs=pl.BlockSpec(memory_space=pltpu.SMEM),
        out_shape=jax.ShapeDtypeStruct((2 * n,), jnp.int32),
        compiler_params=_cparams(("arbitrary",)),
        name="moe_inverse",
    )(seg, pair1, pair2)


def _row_gather(src_ref, idx_of, dst_ref, dst_row0, sem, n_rows):
    def copy(t):
        return pltpu.make_async_copy(_token_tile(src_ref, idx_of(t)), _token_tile(dst_ref, dst_row0 + t), sem)

    def start():
        def body(t, carry):
            copy(2 * t).start(priority=0)
            copy(2 * t + 1).start(priority=1)
            return carry
        lax.fori_loop(0, n_rows // 2, body, 0, unroll=DMA_UNROLL // 2)

    def wait():
        def body(t, carry):
            copy(t).wait()
            return carry
        lax.fori_loop(0, n_rows, body, 0, unroll=DMA_UNROLL)

    return start, wait


def _experts_kernel(lo_ref, hi_ref, tile_ref, exp_ref, items_ref, inv_ref, inv_next_ref, h_ref,
                    wg_ref, wu_ref, wd_ref, ys_ref, xbuf_ref, sems, wg_s, wu_s, wd_s, *, n_tiles):
    i = pl.program_id(0)
    prev = jnp.maximum(i - 1, 0)
    e = exp_ref[i]
    tile = tile_ref[i]
    slot = tile % 2
    active = i < items_ref[0]
    first_visit = (i == 0) | (tile != tile_ref[prev])

    @pl.when((i == 0) | (e != exp_ref[prev]))
    def _():
        wg_s[...] = wg_ref[0].astype(BF16)
        wu_s[...] = wu_ref[0].astype(BF16)
        wd_s[...] = wd_ref[0].astype(BF16)

    start_this, wait_this = _row_gather(h_ref, lambda t: inv_ref[t], xbuf_ref, slot * MOE_TILE,
                                        sems.at[slot], MOE_TILE)
    start_next, _ = _row_gather(h_ref, lambda t: inv_next_ref[t], xbuf_ref, (1 - slot) * MOE_TILE,
                                sems.at[1 - slot], MOE_TILE)

    @pl.when(active & first_visit)
    def _():
        @pl.when(i == 0)
        def _():
            start_this()

        @pl.when(tile + 1 < n_tiles)
        def _():
            start_next()

        wait_this()

    @pl.when(active)
    def _():
        base = pl.multiple_of(slot * (MOE_TILE * TOKEN_TILE_ROWS), MOE_TILE * TOKEN_TILE_ROWS)
        x = jnp.concatenate([xbuf_ref[pl.ds(base + c, MOE_TILE, stride=TOKEN_TILE_ROWS), :]
                             for c in range(TOKEN_TILE_ROWS)], axis=-1).astype(BF16)
        act = _silu(_dot(x, wg_s[...])) * _dot(x, wu_s[...])
        out = _dot(act.astype(BF16), wd_s[...])
        row = tile_ref[i] * MOE_TILE + lax.broadcasted_iota(jnp.int32, (MOE_TILE, 1), 0)
        mine = (row >= lo_ref[e]) & (row < hi_ref[e])
        first_visit = (i == 0) | (tile_ref[i] != tile_ref[prev])

        @pl.when(first_visit)
        def _():
            _store_token_tiles(ys_ref, jnp.where(mine, out, 0.0))

        @pl.when(jnp.logical_not(first_visit))
        def _():
            old = jnp.concatenate(_load_token_tiles(ys_ref, MOE_TILE), axis=-1)
            _store_token_tiles(ys_ref, jnp.where(mine, out, old))


def _experts(seg_lo, seg_hi, item_tile, item_exp, n_items, inv, h2, wg, wu, wd):
    n_tiles = inv.shape[0] // MOE_TILE
    _, d, de = wg.shape

    def w_spec(shape):
        return pl.BlockSpec((1,) + shape, lambda i, lo, hi, tile, ex, items: (ex[i], 0, 0))

    return pl.pallas_call(
        functools.partial(_experts_kernel, n_tiles=n_tiles),
        grid_spec=pltpu.PrefetchScalarGridSpec(
            num_scalar_prefetch=5, grid=(item_tile.shape[0],),
            in_specs=[pl.BlockSpec((MOE_TILE,), lambda i, lo, hi, tile, ex, items: (tile[i],),
                                   memory_space=pltpu.SMEM),
                      pl.BlockSpec((MOE_TILE,), lambda i, lo, hi, tile, ex, items: (
                          jnp.minimum(tile[i] + 1, n_tiles - 1),), memory_space=pltpu.SMEM),
                      pl.BlockSpec(memory_space=pltpu.HBM),
                      w_spec((d, de)), w_spec((d, de)), w_spec((de, d))],
            out_specs=pl.BlockSpec((MOE_TILE * TOKEN_TILE_ROWS, LANES),
                                   lambda i, lo, hi, tile, ex, items: (tile[i], 0)),
            scratch_shapes=[pltpu.VMEM((2 * MOE_TILE * TOKEN_TILE_ROWS, LANES), F32),
                            pltpu.SemaphoreType.DMA((2,)),
                            pltpu.VMEM((d, de), BF16), pltpu.VMEM((d, de), BF16), pltpu.VMEM((de, d), BF16)]),
        out_shape=jax.ShapeDtypeStruct((inv.shape[0] * TOKEN_TILE_ROWS, LANES), F32),
        compiler_params=_cparams(("arbitrary",)),
        name="moe_experts",
    )(seg_lo, seg_hi, item_tile, item_exp, n_items, inv, inv, h2, wg, wu, wd)


def _combine_kernel(seg_ref, p1_ref, p2_ref, p1n_ref, p2n_ref, rw_ref, x1_ref, gate_ref, shift_ref, scale_ref,
                    nfin_ref, ys_ref, y_ref, r1_ref, r2_ref, sems, *, n_steps):
    _, rows, d = x1_ref.shape
    s = pl.program_id(0)
    slot = s % 2

    def gathers(pa_ref, pb_ref, slot):
        g1 = _row_gather(ys_ref, lambda t: _pair_row(seg_ref, pa_ref[t]), r1_ref, slot * rows, sems.at[slot], rows)
        g2 = _row_gather(ys_ref, lambda t: _pair_row(seg_ref, pb_ref[t]), r2_ref, slot * rows, sems.at[slot], rows)
        return g1, g2

    this = gathers(p1_ref, p2_ref, slot)
    nxt = gathers(p1n_ref, p2n_ref, 1 - slot)

    @pl.when(s == 0)
    def _():
        this[0][0]()
        this[1][0]()

    @pl.when(s + 1 < n_steps)
    def _():
        nxt[0][0]()
        nxt[1][0]()

    this[0][1]()
    this[1][1]()

    def per_token(ref):
        per_seq = ref[0]
        seqs = per_seq.shape[0]
        return jnp.broadcast_to(per_seq[:, None, :], (seqs, rows // seqs, d)).reshape(rows, d)

    rw = rw_ref[...]
    w1 = rw[:, 0:1]
    w2 = rw[:, 1:2]
    base = pl.multiple_of(slot * (rows * TOKEN_TILE_ROWS), rows * TOKEN_TILE_ROWS)
    moe = jnp.concatenate(
        [w1 * r1_ref[pl.ds(base + c, rows, stride=TOKEN_TILE_ROWS), :]
         + w2 * r2_ref[pl.ds(base + c, rows, stride=TOKEN_TILE_ROWS), :] for c in range(TOKEN_TILE_ROWS)], axis=-1)
    x2 = x1_ref[0] + per_token(gate_ref) * moe
    hf = _rms(x2) * nfin_ref[...]
    y_ref[0] = hf * (1.0 + per_token(scale_ref)) + per_token(shift_ref)


def _combine(seg, pair1, pair2, rw, x1, ada3, adaf3, nfin, ys, rows, tile0):
    b, t, d = x1.shape
    n_steps = b * t // rows
    seqs = max(rows // t, 1)
    tiles_per_seq = max(t // rows, 1)
    x3 = x1.reshape(n_steps, rows, d)
    ada_v = ada3.reshape(b // seqs, seqs, -1)
    adaf_v = adaf3.reshape(b // seqs, seqs, -1)
    x_spec = pl.BlockSpec((1, rows, d), lambda i: (i, 0, 0))

    def idx_spec(step):
        return pl.BlockSpec((rows,), lambda i: (tile0 + step(i),), memory_space=pltpu.SMEM)

    def ada_spec(col):
        return pl.BlockSpec((1, seqs, d), lambda i: (i // tiles_per_seq, 0, col))

    def cur(i):
        return i

    def nxt(i):
        return jnp.minimum(i + 1, n_steps - 1)

    return pl.pallas_call(
        functools.partial(_combine_kernel, n_steps=n_steps),
        grid=(n_steps,),
        in_specs=[pl.BlockSpec(memory_space=pltpu.SMEM), idx_spec(cur), idx_spec(cur), idx_spec(nxt), idx_spec(nxt),
                  pl.BlockSpec((rows, ROUTER_LANES), lambda i: (tile0 + i, 0)),
                  x_spec, ada_spec(5), ada_spec(0), ada_spec(1),
                  pl.BlockSpec((1, d), lambda i: (0, 0)),
                  pl.BlockSpec(memory_space=pltpu.HBM)],
        out_specs=x_spec,
        scratch_shapes=[pltpu.VMEM((2 * rows * TOKEN_TILE_ROWS, LANES), F32),
                        pltpu.VMEM((2 * rows * TOKEN_TILE_ROWS, LANES), F32),
                        pltpu.SemaphoreType.DMA((2,))],
        out_shape=jax.ShapeDtypeStruct(x3.shape, F32),
        compiler_params=_cparams(("arbitrary",)),
        name="moe_combine",
    )(seg, pair1, pair2, pair1, pair2, rw, x3, ada_v, adaf_v, adaf_v, nfin, ys).reshape(x1.shape)


def _mixer(x, ada3, s_h, s_re, s_im, p, *, bb, tt, hgrn_seqs, hgrn_tokens, sequential):
    b, t, d = x.shape
    n = b * t
    q, k, g, v, gs, u = _inproj(x, ada3, p['norm_mix'], p['w_in'], p['lb'], bb, tt)
    oh, s_h_new = _hgrn(q, k, g, v, gs, s_h, p['hgrn_norm'], b, t, hgrn_seqs, hgrn_tokens)
    if sequential:
        xr, xi = s_re.reshape(b, 1, -1), s_im.reshape(b, 1, -1)
        n_tokens = t
    else:
        xr, xi = s_re.reshape(1, b, -1), s_im.reshape(1, b, -1)
        n_tokens = n
    ys, fr, fi = _s5(u, xr, xi, p['ssm_bd'], p['ssm_inc'], p['ssm_m'], p['ssm_a8'], n_tokens, sequential)
    states = (s_h_new[None], fr.reshape(1, b, s_re.shape[-2], s_re.shape[-1]),
              fi.reshape(1, b, s_re.shape[-2], s_re.shape[-1]))
    return oh, ys, states


def kernel(x_prompt, x_sample, c_prompt, c_sample, state_hgrn, state_ssm_re, state_ssm_im, hgrn_lb_logits, w_ada, b_ada, norm_mix, w_in, hgrn_norm, ssm_a_re, ssm_a_im, ssm_log_dt, ssm_b_re, ssm_b_im, ssm_c_re, ssm_c_im, ssm_d, ssm_w_glu, ssm_b_glu, ssm_norm, w_out, norm_ffn, moe_w_group, moe_b_group, moe_w_router, moe_b_router, moe_w_gate, moe_w_up, moe_w_down, w_ada_final, b_ada_final, norm_final):
    depth = w_ada.shape[0]
    assert depth == 1
    d = x_prompt.shape[-1]
    bp = x_prompt.shape[0]
    dh = hgrn_norm.shape[-1]
    dk = dh // HGRN_HEADS
    de = moe_w_gate.shape[-1]
    n_exp = MOE_GROUPS * MOE_PER_GROUP

    lb = jax.nn.softmax(hgrn_lb_logits.astype(F32), axis=0)[0].reshape(1, dh)
    bd, inc, m, a8 = _s5_prepare(ssm_a_re[0], ssm_a_im[0], ssm_log_dt[0], ssm_b_re[0], ssm_b_im[0],
                                 ssm_c_re[0], ssm_c_im[0], ssm_d[0])

    def router_lanes(group_part, expert_part):
        rows = group_part.shape[0]
        return jnp.concatenate(
            [group_part, jnp.zeros((rows, EXPERT_LANE0 - MOE_GROUPS), F32), expert_part,
             jnp.zeros((rows, ROUTER_LANES - EXPERT_LANE0 - n_exp), F32)], axis=1)

    w_rt = router_lanes(moe_w_group[0], moe_w_router[0].transpose(1, 0, 2).reshape(d, n_exp))
    b_rt = router_lanes(moe_b_group[0].reshape(1, MOE_GROUPS), moe_b_router[0].reshape(1, n_exp))
    p = dict(
        lb=lb, norm_mix=norm_mix[0].reshape(1, d), w_in=w_in[0].astype(BF16),
        hgrn_norm=hgrn_norm[0].reshape(1, dh),
        ssm_bd=bd, ssm_inc=inc, ssm_m=m, ssm_a8=a8,
        w_glu=ssm_w_glu[0].astype(BF16), b_glu=ssm_b_glu[0].reshape(1, -1), ssm_norm=ssm_norm[0].reshape(1, -1),
        wo_h=w_out[0, :dh].astype(BF16), wo_s=w_out[0, dh:].astype(BF16),
        norm_ffn=norm_ffn[0].reshape(1, d), w_rt=w_rt.astype(BF16), b_rt=b_rt,
    )
    wg = moe_w_gate[0].reshape(n_exp, d, de)
    wu = moe_w_up[0].reshape(n_exp, d, de)
    wd = moe_w_down[0].reshape(n_exp, de, d)
    nfin = norm_final.reshape(1, d)

    c_all = jnp.concatenate([c_prompt, c_sample], axis=0)
    ada = _silu_linear(c_all, w_ada[0], b_ada[0])
    adaf = _silu_linear(c_all, w_ada_final, b_ada_final)
    ada_p, ada_s = ada[:bp].reshape(bp, 1, -1), ada[bp:].reshape(x_sample.shape[0], 1, -1)
    adaf_p, adaf_s = adaf[:bp].reshape(bp, 1, -1), adaf[bp:].reshape(x_sample.shape[0], 1, -1)

    zeros_h = jnp.zeros((bp, HGRN_HEADS, dk, dk), F32)
    zeros_s = jnp.zeros((bp,) + state_ssm_re.shape[2:], F32)
    oh_p, ys_p, st_p = _mixer(x_prompt, ada_p, zeros_h, zeros_s, zeros_s, p,
                              bb=1, tt=256, hgrn_seqs=1, hgrn_tokens=512, sequential=True)
    oh_s, ys_s, st_s = _mixer(x_sample, ada_s, state_hgrn[0], state_ssm_re[0], state_ssm_im[0], p,
                              bb=32, tt=8, hgrn_seqs=16, hgrn_tokens=8, sequential=False)
    x1_p, x1_s, h2, pair1, pair2, rw, cnt = _outproj(
        x_prompt, x_sample, oh_p, oh_s, ys_p, ys_s, ada_p, ada_s, p['w_glu'], p['b_glu'], p['ssm_norm'],
        p['wo_h'], p['wo_s'], p['norm_ffn'], p['w_rt'], p['b_rt'], TOKEN_ROWS)

    n_pairs = 2 * pair1.shape[0]
    n_slots = n_pairs // MOE_TILE + N_EXPERTS
    seg, seg_end, item_tile, item_exp, n_items = _moe_schedule(cnt, n_slots)
    inv = _inverse(seg, pair1, pair2, 1024)
    ys = _experts(seg, seg_end, item_tile, item_exp, n_items, inv, h2, wg, wu, wd)
    tiles_p = x_prompt.shape[0] * x_prompt.shape[1] // TOKEN_ROWS
    y_p = _combine(seg, pair1, pair2, rw, x1_p, ada_p, adaf_p, nfin, ys, TOKEN_ROWS, 0)
    y_s = _combine(seg, pair1, pair2, rw, x1_s, ada_s, adaf_s, nfin, ys, TOKEN_ROWS, tiles_p)
    return (y_p, y_s) + st_p + st_s
```

```python
import functools
import math

import jax
import jax.numpy as jnp
from jax import lax
from jax.experimental import pallas as pl
from jax.experimental.pallas import tpu as pltpu

F32 = jnp.float32
BF16 = jnp.bfloat16
HIGHEST = lax.Precision.HIGHEST

EPS = 1e-6
MAX_REAL = -1e-4
HGRN_HEADS = 4
HGRN_CHUNK = 128
HGRN_SAFE_EXPONENT = 80.0
HGRN_EXACT_BLOCK = 8
SSM_GROUP = 16
SSM_STATE = 64
SSM_CHUNK = 8
SSM_SETS = 4
MOE_GROUPS = 4
MOE_PER_GROUP = 8
N_EXPERTS = MOE_GROUPS * MOE_PER_GROUP
ROUTER_LANES = 128
EXPERT_LANE0 = 32
RANK_BITS = 20
MOE_TILE = 256
LANES = 128
TOKEN_TILE_ROWS = 8
DMA_UNROLL = 8
TOKEN_ROWS = 512
VMEM_LIMIT = 56 * 1024 * 1024


def _cparams(sem):
    return pltpu.CompilerParams(dimension_semantics=sem, vmem_limit_bytes=VMEM_LIMIT)


def _silu(x):
    return x * jax.nn.sigmoid(x)


def _rms(x):
    return x * lax.rsqrt(jnp.mean(x * x, axis=-1, keepdims=True) + EPS)


def _dot(a, b):
    return jnp.dot(a, b, preferred_element_type=F32)


def _dot_nt(a, b):
    return lax.dot_general(a, b, (((1,), (1,)), ((), ())), preferred_element_type=F32)


def _dot_tn(a, b, precision=None):
    return lax.dot_general(a, b, (((0,), (0,)), ((), ())), preferred_element_type=F32,
                           precision=precision)


def _silu_linear_kernel(c_ref, w_ref, b_ref, o_ref):
    a = _silu(c_ref[...]).astype(BF16)
    o_ref[...] = _dot(a, w_ref[...].astype(BF16)) + b_ref[...]


def _silu_linear(c, w, b):
    m, d = c.shape
    n = w.shape[1]
    tn = 1024
    return pl.pallas_call(
        _silu_linear_kernel,
        grid=(n // tn,),
        in_specs=[pl.BlockSpec((m, d), lambda j: (0, 0)),
                  pl.BlockSpec((d, tn), lambda j: (0, j)),
                  pl.BlockSpec((1, tn), lambda j: (0, j))],
        out_specs=pl.BlockSpec((m, tn), lambda j: (0, j)),
        out_shape=jax.ShapeDtypeStruct((m, n), F32),
        compiler_params=_cparams(("parallel",)),
        name="silu_linear",
    )(c, w, b.reshape(1, n))


def _inproj_kernel(x_ref, shift_ref, scale_ref, gain_ref, w_ref, lb_ref,
                   q_ref, k_ref, g_ref, v_ref, gs_ref, u_ref, *, dh):
    bb, tt, d = x_ref.shape
    h = _rms(x_ref[...]) * gain_ref[...]
    h = h * (1.0 + scale_ref[...]) + shift_ref[...]
    proj = _dot(h.reshape(bb * tt, d).astype(BF16), w_ref[...])
    lb = lb_ref[...]
    f = lb + (1.0 - lb) * jax.nn.sigmoid(proj[:, dh:2 * dh])
    q_ref[...] = _silu(proj[:, :dh]) * (float(dh // HGRN_HEADS) ** -0.5)
    k_ref[...] = 1.0 - f
    g_ref[...] = jnp.log(f)
    v_ref[...] = proj[:, 2 * dh:3 * dh]
    gs_ref[...] = _silu(proj[:, 3 * dh:4 * dh])
    for s in range(SSM_SETS):
        u_ref[s] = proj[:, 4 * dh + s * LANES:4 * dh + (s + 1) * LANES]


def _inproj(x, ada3, gain, w_in_bf, lb, bb, tt):
    b, t, d = x.shape
    dh = lb.shape[-1]
    nt = t // tt
    rows = bb * tt
    n = b * t
    row_spec = pl.BlockSpec((rows, dh), lambda i, j: (i * nt + j, 0))
    out = jax.ShapeDtypeStruct((n, dh), F32)
    return pl.pallas_call(
        functools.partial(_inproj_kernel, dh=dh),
        grid=(b // bb, nt),
        in_specs=[pl.BlockSpec((bb, tt, d), lambda i, j: (i, j, 0)),
                  pl.BlockSpec((bb, 1, d), lambda i, j: (i, 0, 0)),
                  pl.BlockSpec((bb, 1, d), lambda i, j: (i, 0, 1)),
                  pl.BlockSpec((1, d), lambda i, j: (0, 0)),
                  pl.BlockSpec(w_in_bf.shape, lambda i, j: (0, 0)),
                  pl.BlockSpec((1, dh), lambda i, j: (0, 0))],
        out_specs=[row_spec] * 5 + [pl.BlockSpec((SSM_SETS, rows, LANES), lambda i, j: (0, i * nt + j, 0))],
        out_shape=[out] * 5 + [jax.ShapeDtypeStruct((SSM_SETS, n, LANES), F32)],
        compiler_params=_cparams(("parallel", "parallel")),
        name="inproj",
    )(x, ada3, ada3, gain, w_in_bf, lb)


def _split3(x):
    hi = x.astype(BF16)
    r1 = x - hi.astype(F32)
    mid = r1.astype(BF16)
    lo = (r1 - mid.astype(F32)).astype(BF16)
    return hi, mid, lo


def _hgrn_kernel(q_ref, k_ref, g_ref, v_ref, gs_ref, s0_ref, hn_ref, o_ref, sf_ref,
                 st_ref, intra_ref, qh_ref, kh_ref, ea_ref, sums_ref, *, tl, nt):
    j = pl.program_id(1)
    rows_total, dh = q_ref.shape
    dk = dh // HGRN_HEADS
    c = HGRN_CHUNK
    seqs = c // tl
    n_chunks = rows_total // c

    @pl.when(j == 0)
    def _():
        st_ref[...] = s0_ref[...]

    r = lax.broadcasted_iota(jnp.int32, (c, c), 0)
    s = lax.broadcasted_iota(jnp.int32, (c, c), 1)
    same_seq = (r // tl) == (s // tl)
    causal = same_seq & (r >= s)
    upto_mid = same_seq & ((s % tl) < tl // 2)
    sums = jnp.concatenate([causal, upto_mid, same_seq], axis=0).astype(BF16)
    eye3 = (lax.broadcasted_iota(jnp.int32, (dk, 3 * dk), 1) % dk
            == lax.broadcasted_iota(jnp.int32, (dk, 3 * dk), 0)).astype(BF16)

    def decay_matrix(e_row):
        parts = jnp.concatenate(_split3(e_row), axis=-1)
        return _dot_nt(eye3, jnp.broadcast_to(parts, (dk, 3 * dk)))

    worst = jnp.float32(0.0)
    for ci in range(n_chunks):
        rows = slice(ci * c, (ci + 1) * c)
        g_parts = _split3(g_ref[rows, :])
        acc = _dot(sums, g_parts[0]) + _dot(sums, g_parts[1]) + _dot(sums, g_parts[2])
        sums_ref[ci * 3 * c:(ci + 1) * 3 * c, :] = acc
        a_mid = acc[c:2 * c]
        worst = jnp.maximum(worst, jnp.max(jnp.maximum(jnp.abs(a_mid), jnp.abs(acc[2 * c:] - a_mid))))
    factorised_is_safe = worst < HGRN_SAFE_EXPONENT

    @pl.when(factorised_is_safe)
    def _():
        for ci in range(n_chunks):
            rows = slice(ci * c, (ci + 1) * c)
            a = sums_ref[ci * 3 * c:ci * 3 * c + c, :]
            a_mid = sums_ref[ci * 3 * c + c:ci * 3 * c + 2 * c, :]
            a_end = sums_ref[ci * 3 * c + 2 * c:(ci + 1) * 3 * c, :]
            e_mid = jnp.exp(a_mid)
            e_tail = jnp.exp(a_end - a_mid)
            qt = q_ref[rows, :] * jnp.exp(a - a_mid)
            kt = k_ref[rows, :] * jnp.exp(a_mid - a)
            qh_ref[rows, :] = qt * e_mid
            kh_ref[rows, :] = kt * e_tail
            ea_ref[rows, :] = e_mid * e_tail
            qt = qt.astype(BF16)
            kt = kt.astype(BF16)
            v = v_ref[rows, :].astype(BF16)
            for h in range(HGRN_HEADS):
                lanes = slice(h * dk, (h + 1) * dk)
                sc = jnp.where(causal, _dot_nt(qt[:, lanes], kt[:, lanes]), 0.0).astype(BF16)
                intra_ref[rows, lanes] = _dot(sc, v[:, lanes])

            for si in range(seqs):
                seq = ci * seqs + si if tl < c else 0
                srows = slice(si * tl, (si + 1) * tl)
                orows = slice(ci * c + si * tl, ci * c + (si + 1) * tl)
                for h in range(HGRN_HEADS):
                    lanes = slice(h * dk, (h + 1) * dk)
                    state = st_ref[seq, h]
                    o = intra_ref[orows, lanes] + _dot(qh_ref[orows, lanes].astype(BF16), state.astype(BF16))
                    decay = decay_matrix(ea_ref[orows.start:orows.start + 1, lanes])
                    st_ref[seq, h] = decay * state + _dot_tn(kh_ref[orows, lanes].astype(BF16),
                                                             v[srows, lanes])
                    o_ref[orows, lanes] = _rms(o) * hn_ref[:, lanes] * gs_ref[orows, lanes]

    @pl.when(jnp.logical_not(factorised_is_safe))
    def _():
        blk = HGRN_EXACT_BLOCK
        blocks_per_seq = max(tl // blk, 1)
        tri = (lax.broadcasted_iota(jnp.int32, (blk, blk), 0)
               >= lax.broadcasted_iota(jnp.int32, (blk, blk), 1)).astype(BF16)
        sub = lax.broadcasted_iota(jnp.int32, (blk, dk), 0)

        def block(bi, carry):
            rows = pl.ds(pl.multiple_of(bi * blk, blk), blk)
            seq = bi // blocks_per_seq if tl < c else 0
            g_parts = _split3(g_ref[rows, :])
            a = _dot(tri, g_parts[0]) + _dot(tri, g_parts[1]) + _dot(tri, g_parts[2])
            a_end = a[blk - 1:blk]
            q = q_ref[rows, :]
            k = k_ref[rows, :]
            v = v_ref[rows, :]
            qh = (q * jnp.exp(a)).astype(BF16)
            kh = (k * jnp.exp(a_end - a)).astype(BF16)
            ea = jnp.exp(a_end)
            vb = v.astype(BF16)
            for h in range(HGRN_HEADS):
                lanes = slice(h * dk, (h + 1) * dk)
                state = st_ref[seq, h]
                intra = []
                for t in range(blk):
                    live = sub <= t
                    decay_t = jnp.where(live, jnp.exp(jnp.where(live, a[t:t + 1, lanes] - a[:, lanes], 0.0)), 0.0)
                    score = jnp.sum(q[t:t + 1, lanes] * k[:, lanes] * decay_t, axis=-1, keepdims=True)
                    intra.append(jnp.sum(score * v[:, lanes], axis=0, keepdims=True))
                o = jnp.concatenate(intra, axis=0) + _dot(qh[:, lanes], state.astype(BF16))
                st_ref[seq, h] = decay_matrix(ea[:, lanes]) * state + _dot_tn(kh[:, lanes], vb[:, lanes])
                o_ref[rows, lanes] = _rms(o) * hn_ref[:, lanes] * gs_ref[rows, lanes]
            return carry

        lax.fori_loop(0, rows_total // blk, block, 0)

    @pl.when(j == nt - 1)
    def _():
        sf_ref[...] = st_ref[...]


def _hgrn(q, k, g, v, gs, s0, hnorm, b, t, nseq, tt):
    n, dh = q.shape
    nt = t // tt
    dk = dh // HGRN_HEADS
    rows = nseq * tt
    tl = min(tt, HGRN_CHUNK)
    row_spec = pl.BlockSpec((rows, dh), lambda i, j: (i * nt + j, 0))
    st_spec = pl.BlockSpec((nseq, HGRN_HEADS, dk, dk), lambda i, j: (i, 0, 0, 0))
    return pl.pallas_call(
        functools.partial(_hgrn_kernel, tl=tl, nt=nt),
        grid=(b // nseq, nt),
        in_specs=[row_spec] * 5 + [st_spec, pl.BlockSpec((1, dh), lambda i, j: (0, 0))],
        out_specs=[row_spec, st_spec],
        out_shape=[jax.ShapeDtypeStruct((n, dh), F32),
                   jax.ShapeDtypeStruct((b, HGRN_HEADS, dk, dk), F32)],
        scratch_shapes=[pltpu.VMEM((nseq, HGRN_HEADS, dk, dk), F32),
                        pltpu.VMEM((rows, dh), F32), pltpu.VMEM((rows, dh), F32),
                        pltpu.VMEM((rows, dh), F32), pltpu.VMEM((rows, dh), F32),
                        pltpu.VMEM((3 * rows, dh), F32)],
        compiler_params=_cparams(("parallel", "arbitrary")),
        name="hgrn",
    )(q, k, g, v, gs, s0, hnorm)


def _s5_prepare(a_re, a_im, log_dt, b_re, b_im, c_re, c_im, d_skip):
    ng, npp = a_re.shape
    nh = b_re.shape[-1]
    L = SSM_CHUNK
    gs = ng // SSM_SETS
    lam_re = jnp.minimum(a_re, MAX_REAL)
    lam_im = a_im
    dt = jnp.exp(log_dt)
    mag = jnp.exp(lam_re * dt)
    ab_re = mag * jnp.cos(lam_im * dt)
    ab_im = mag * jnp.sin(lam_im * dt)
    den = lam_re * lam_re + lam_im * lam_im
    co_re = ((ab_re - 1.0) * lam_re + ab_im * lam_im) / den
    co_im = (ab_im * lam_re - (ab_re - 1.0) * lam_im) / den
    bb_re = co_re[..., None] * b_re - co_im[..., None] * b_im
    bb_im = co_re[..., None] * b_im + co_im[..., None] * b_re
    tau = jnp.arange(L + 1, dtype=F32)[:, None, None]
    pw_mag = jnp.exp(tau * (lam_re * dt))
    pw_re = pw_mag * jnp.cos(tau * (lam_im * dt))
    pw_im = pw_mag * jnp.sin(tau * (lam_im * dt))
    ab_b_re = pw_re[:L, :, :, None] * bb_re - pw_im[:L, :, :, None] * bb_im
    ab_b_im = pw_re[:L, :, :, None] * bb_im + pw_im[:L, :, :, None] * bb_re
    kern = (jnp.einsum('gkp,lgph->lghk', c_re, ab_b_re, precision=HIGHEST)
            - jnp.einsum('gkp,lgph->lghk', c_im, ab_b_im, precision=HIGHEST))
    skip = d_skip[None, :, :, None] * jnp.eye(nh, dtype=F32)
    kern = kern + jnp.where(jnp.arange(L)[:, None, None, None] == 0, skip, 0.0)

    def group_block_diag(c):
        rows, w = c.shape[-2:]
        repeat = (jnp.arange(gs * w)[None, :] % w == jnp.arange(w)[:, None]).astype(F32)
        tiled = jnp.dot(c, repeat, precision=HIGHEST)
        rg = jnp.arange(rows)[:, None] // (rows // gs)
        cq = jnp.arange(gs * w)[None, :] // w
        return jnp.where(rg == cq, tiled, 0.0).astype(BF16).transpose(1, 0, 2, 3).reshape(
            SSM_SETS, L * rows, gs * w)

    bd = group_block_diag(kern.reshape(L, SSM_SETS, gs * nh, nh)).reshape(SSM_SETS, L, gs * nh, gs * nh)
    n_re = group_block_diag(ab_b_re[::-1].transpose(0, 1, 3, 2).reshape(L, SSM_SETS, gs * nh, npp))
    n_im = group_block_diag(ab_b_im[::-1].transpose(0, 1, 3, 2).reshape(L, SSM_SETS, gs * nh, npp))
    inc = jnp.concatenate([n_re, n_im], axis=-1)
    ca_re = c_re[None] * pw_re[1:, :, None, :] - c_im[None] * pw_im[1:, :, None, :]
    ca_im = c_re[None] * pw_im[1:, :, None, :] + c_im[None] * pw_re[1:, :, None, :]
    m = jnp.concatenate([group_block_diag(ca_re.reshape(L, SSM_SETS, gs * nh, npp)),
                         group_block_diag(-ca_im.reshape(L, SSM_SETS, gs * nh, npp))], axis=-1)
    a8 = jnp.concatenate([pw_re[L].reshape(SSM_SETS, 1, gs * npp),
                          pw_im[L].reshape(SSM_SETS, 1, gs * npp)], axis=-1)
    return bd, inc, m, a8


def _s5_kernel(u_ref, xr_ref, xi_ref, bd_ref, inc_ref, m_ref, a8_ref, y_ref, fr_ref, fi_ref, wt_ref, *, sequential):
    n = u_ref.shape[1] // SSM_CHUNK
    ns = xr_ref.shape[-1]
    u = jnp.concatenate([u_ref[0, pl.ds(s, n, stride=SSM_CHUNK), :] for s in range(SSM_CHUNK)], axis=-1)

    @pl.when(pl.program_id(1) == 0)
    def _():
        wt_ref[...] = jnp.zeros_like(wt_ref)
        for s in range(SSM_CHUNK):
            for t in range(s, SSM_CHUNK):
                wt_ref[s * LANES:(s + 1) * LANES, t * LANES:(t + 1) * LANES] = bd_ref[0, t - s]

    u = u.astype(BF16)
    y_local = _dot(u, wt_ref[...])
    inc = _dot(u, inc_ref[0])
    d_re = inc[:, :ns]
    d_im = inc[:, ns:]
    a_re = a8_ref[0][:, :ns]
    a_im = a8_ref[0][:, ns:]
    x0_re = xr_ref[0]
    x0_im = xi_ref[0]
    if sequential:
        row = lax.broadcasted_iota(jnp.int32, (n, ns), 0)
        first = row == 0
        x_re = d_re + jnp.where(first, a_re * x0_re - a_im * x0_im, 0.0)
        x_im = d_im + jnp.where(first, a_re * x0_im + a_im * x0_re, 0.0)
        p_re, p_im = a_re, a_im
        step = 1
        while step < n:
            s_re = jnp.where(row >= step, pltpu.roll(x_re, step, 0), 0.0)
            s_im = jnp.where(row >= step, pltpu.roll(x_im, step, 0), 0.0)
            x_re, x_im = x_re + p_re * s_re - p_im * s_im, x_im + p_re * s_im + p_im * s_re
            p_re, p_im = p_re * p_re - p_im * p_im, 2.0 * p_re * p_im
            step *= 2
        fr_ref[0] = x_re[n - 1:n]
        fi_ref[0] = x_im[n - 1:n]
        xc_re = jnp.where(first, x0_re, pltpu.roll(x_re, 1, 0))
        xc_im = jnp.where(first, x0_im, pltpu.roll(x_im, 1, 0))
    else:
        xc_re, xc_im = x0_re, x0_im
        fr_ref[0] = a_re * x0_re - a_im * x0_im + d_re
        fi_ref[0] = a_re * x0_im + a_im * x0_re + d_im
    xc = jnp.concatenate([xc_re, xc_im], axis=-1).astype(BF16)
    y = y_local + _dot_nt(xc, m_ref[0])
    for t in range(SSM_CHUNK):
        y_ref[0, pl.ds(t, n, stride=SSM_CHUNK), :] = y[:, t * LANES:(t + 1) * LANES]


def _s5(u, x_re, x_im, bd, inc, m, a8, n_tokens, sequential):
    sets = u.shape[0]
    nb, rb, _ = x_re.shape
    ns = m.shape[1] // 2
    st_spec = pl.BlockSpec((1, rb, ns), lambda gi, i: (i, 0, gi))
    st_shape = jax.ShapeDtypeStruct(x_re.shape, F32)
    tok_spec = pl.BlockSpec((1, n_tokens, LANES), lambda gi, i: (gi, i, 0))
    return pl.pallas_call(
        functools.partial(_s5_kernel, sequential=sequential),
        grid=(sets, nb),
        in_specs=[tok_spec, st_spec, st_spec,
                  pl.BlockSpec((1,) + bd.shape[1:], lambda gi, i: (gi, 0, 0, 0)),
                  pl.BlockSpec((1,) + inc.shape[1:], lambda gi, i: (gi, 0, 0)),
                  pl.BlockSpec((1,) + m.shape[1:], lambda gi, i: (gi, 0, 0)),
                  pl.BlockSpec((1, 1, 2 * ns), lambda gi, i: (gi, 0, 0))],
        out_specs=[tok_spec, st_spec, st_spec],
        out_shape=[jax.ShapeDtypeStruct(u.shape, F32), st_shape, st_shape],
        scratch_shapes=[pltpu.VMEM((inc.shape[1], SSM_CHUNK * LANES), BF16)],
        compiler_params=_cparams(("parallel", "arbitrary")),
        name="s5",
    )(u, x_re, x_im, bd, inc, m, a8)


def _gelu_tanh(x):
    return 0.5 * x * (1.0 + jnp.tanh(math.sqrt(2.0 / math.pi) * (x + 0.044715 * (x * x * x))))


def _outproj_kernel(xp_ref, xs_ref, ohp_ref, ohs_ref, ysp_ref, yss_ref,
                    gate_p_ref, shift_p_ref, scale_p_ref, gate_s_ref, shift_s_ref, scale_s_ref,
                    wglu_ref, bglu_ref, sn_ref, wo_h_ref, wo_s_ref, nf_ref, wr_ref, br_ref,
                    x1p_ref, x1s_ref, h2_ref, pair1_ref, pair2_ref, rw_ref, cnt_ref, *, prompt_tiles):
    is_prompt = pl.program_id(0) < prompt_tiles
    _, rows, d = xp_ref.shape

    def per_token(p_ref, s_ref):
        per_seq = s_ref[0]
        seqs = per_seq.shape[0]
        rep = jnp.broadcast_to(per_seq[:, None, :], (seqs, rows // seqs, d)).reshape(rows, d)
        return jnp.where(is_prompt, p_ref[0], rep)

    x = jnp.where(is_prompt, xp_ref[0], xs_ref[0])
    oh = jnp.where(is_prompt, ohp_ref[...], ohs_ref[...])
    ys = jnp.concatenate([jnp.where(is_prompt, ysp_ref[s], yss_ref[s]) for s in range(SSM_SETS)], axis=-1)
    y = _gelu_tanh(ys)
    y = y * jax.nn.sigmoid(_dot(y.astype(BF16), wglu_ref[...]) + bglu_ref[...])
    o_s = _rms(y) * sn_ref[...]
    mix = _dot(oh.astype(BF16), wo_h_ref[...]) + _dot(o_s.astype(BF16), wo_s_ref[...])
    x1 = x + per_token(gate_p_ref, gate_s_ref) * mix

    @pl.when(is_prompt)
    def _():
        x1p_ref[0] = x1

    @pl.when(jnp.logical_not(is_prompt))
    def _():
        x1s_ref[0] = x1

    h2 = _rms(x1) * nf_ref[...]
    h2 = h2 * (1.0 + per_token(scale_p_ref, scale_s_ref)) + per_token(shift_p_ref, shift_s_ref)
    _store_token_tiles(h2_ref, h2)

    logits = _dot(h2.astype(BF16), wr_ref[...]) + br_ref[...]
    lane = lax.broadcasted_iota(jnp.int32, logits.shape, 1)
    neg = -jnp.inf
    gl = jnp.where(lane < MOE_GROUPS, logits, neg)
    gmax = jnp.max(gl, axis=-1, keepdims=True)
    gidx = jnp.min(jnp.where(gl == gmax, lane, ROUTER_LANES), axis=-1, keepdims=True)
    grp_w = 1.0 / jnp.sum(jnp.exp(gl - gmax), axis=-1, keepdims=True)
    e0 = EXPERT_LANE0 + gidx * MOE_PER_GROUP
    sel = jnp.where((lane >= e0) & (lane < e0 + MOE_PER_GROUP), logits, neg)
    m1 = jnp.max(sel, axis=-1, keepdims=True)
    i1 = jnp.min(jnp.where(sel == m1, lane, ROUTER_LANES), axis=-1, keepdims=True)
    sel2 = jnp.where(lane == i1, neg, sel)
    m2 = jnp.max(sel2, axis=-1, keepdims=True)
    i2 = jnp.min(jnp.where(sel2 == m2, lane, ROUTER_LANES), axis=-1, keepdims=True)
    e2 = jnp.exp(m2 - m1)
    w1 = 1.0 / (1.0 + e2)
    w2 = e2 / (1.0 + e2)
    rw_ref[...] = grp_w * (jnp.where(lane == 0, w1, 0.0) + jnp.where(lane == 1, w2, 0.0))

    @pl.when(pl.program_id(0) == 0)
    def _():
        cnt_ref[...] = jnp.zeros_like(cnt_ref)

    picked = (lane == i1) | (lane == i2)
    earlier = (lax.broadcasted_iota(jnp.int32, (rows, rows), 0)
               > lax.broadcasted_iota(jnp.int32, (rows, rows), 1))
    base = cnt_ref[...]
    before = _dot(earlier.astype(BF16), picked.astype(BF16)) + base
    rank1 = jnp.sum(jnp.where(lane == i1, before, 0.0), axis=-1, keepdims=True).astype(jnp.int32)
    rank2 = jnp.sum(jnp.where(lane == i2, before, 0.0), axis=-1, keepdims=True).astype(jnp.int32)
    cnt_ref[...] = base + jnp.sum(picked.astype(F32), axis=0, keepdims=True)
    info = jnp.where(lane == 0, ((i1 - EXPERT_LANE0) << RANK_BITS) | rank1,
                     jnp.where(lane == 1, ((i2 - EXPERT_LANE0) << RANK_BITS) | rank2, 0)).T
    pair1_ref[...] = info[0]
    pair2_ref[...] = info[1]


def _outproj(x_p, x_s, oh_p, oh_s, ys_p, ys_s, ada_p, ada_s, wglu_bf, bglu, snorm, wo_h, wo_s, nffn, wr, br, rows):
    d = x_p.shape[-1]
    dh = oh_p.shape[-1]
    n_p = x_p.shape[0] * x_p.shape[1]
    n_s = x_s.shape[0] * x_s.shape[1]
    n = n_p + n_s
    tiles_p = n_p // rows
    seqs = rows // x_s.shape[1]

    def pt(i):
        return jnp.minimum(i, tiles_p - 1)

    def st(i):
        return jnp.maximum(i - tiles_p, 0)

    tiles_per_seq = x_p.shape[1] // rows

    def ada_p_spec(col):
        return pl.BlockSpec((1, 1, d), lambda i: (pt(i) // tiles_per_seq, 0, col))

    def ada_s_spec(col):
        return pl.BlockSpec((1, seqs, d), lambda i: (st(i), 0, col))

    def full(a):
        return pl.BlockSpec(a.shape, lambda i: (0,) * a.ndim)

    xp3 = x_p.reshape(tiles_p, rows, d)
    xs3 = x_s.reshape(n_s // rows, rows, d)
    ada_s3 = ada_s.reshape(n_s // rows, seqs, -1)
    outs = pl.pallas_call(
        functools.partial(_outproj_kernel, prompt_tiles=tiles_p),
        grid=(n // rows,),
        in_specs=[pl.BlockSpec((1, rows, d), lambda i: (pt(i), 0, 0)),
                  pl.BlockSpec((1, rows, d), lambda i: (st(i), 0, 0)),
                  pl.BlockSpec((rows, dh), lambda i: (pt(i), 0)),
                  pl.BlockSpec((rows, dh), lambda i: (st(i), 0)),
                  pl.BlockSpec((SSM_SETS, rows, LANES), lambda i: (0, pt(i), 0)),
                  pl.BlockSpec((SSM_SETS, rows, LANES), lambda i: (0, st(i), 0)),
                  ada_p_spec(2), ada_p_spec(3), ada_p_spec(4), ada_s_spec(2), ada_s_spec(3), ada_s_spec(4),
                  full(wglu_bf), full(bglu), full(snorm), full(wo_h), full(wo_s), full(nffn),
                  full(wr), full(br)],
        out_specs=[pl.BlockSpec((1, rows, d), lambda i: (pt(i), 0, 0)),
                   pl.BlockSpec((1, rows, d), lambda i: (st(i), 0, 0)),
                   pl.BlockSpec((rows * TOKEN_TILE_ROWS, LANES), lambda i: (i, 0)),
                   pl.BlockSpec((rows,), lambda i: (i,)),
                   pl.BlockSpec((rows,), lambda i: (i,)),
                   pl.BlockSpec((rows, ROUTER_LANES), lambda i: (i, 0)),
                   pl.BlockSpec((1, ROUTER_LANES), lambda i: (0, 0))],
        out_shape=[jax.ShapeDtypeStruct(xp3.shape, F32),
                   jax.ShapeDtypeStruct(xs3.shape, F32),
                   jax.ShapeDtypeStruct((n * TOKEN_TILE_ROWS, LANES), F32),
                   jax.ShapeDtypeStruct((n,), jnp.int32),
                   jax.ShapeDtypeStruct((n,), jnp.int32),
                   jax.ShapeDtypeStruct((n, ROUTER_LANES), F32),
                   jax.ShapeDtypeStruct((1, ROUTER_LANES), F32)],
        compiler_params=_cparams(("arbitrary",)),
        name="outproj",
    )(xp3, xs3, oh_p, oh_s, ys_p, ys_s, ada_p, ada_p, ada_p, ada_s3, ada_s3, ada_s3,
      wglu_bf, bglu, snorm, wo_h, wo_s, nffn, wr, br)
    x1_p, x1_s = outs[0].reshape(x_p.shape), outs[1].reshape(x_s.shape)
    return (x1_p, x1_s) + tuple(outs[2:])


def _moe_schedule(cnt, n_slots):
    c = cnt[0, EXPERT_LANE0:EXPERT_LANE0 + N_EXPERTS].astype(jnp.int32)
    seg_end = jnp.cumsum(c)
    seg_start = seg_end - c
    first_tile = seg_start // MOE_TILE
    tiles = jnp.where(c > 0, (seg_end - 1) // MOE_TILE - first_tile + 1, 0)
    cum = jnp.cumsum(tiles)
    n_items = cum[-1]
    item = jnp.minimum(jnp.arange(n_slots, dtype=jnp.int32), n_items - 1)
    item_exp = jnp.sum(item[:, None] >= cum[None, :], axis=1).astype(jnp.int32)
    shares = ((c > 0) & (seg_start % MOE_TILE != 0)).astype(jnp.int32)
    item_tile = item - jnp.sum((item[:, None] >= (cum - tiles)[None, :]) * shares[None, :], axis=1)
    return (seg_start.astype(jnp.int32), seg_end.astype(jnp.int32), item_tile.astype(jnp.int32), item_exp,
            n_items.reshape(1).astype(jnp.int32))


def _token_tile(ref, t):
    return ref.at[pl.ds(pl.multiple_of(t * TOKEN_TILE_ROWS, TOKEN_TILE_ROWS), TOKEN_TILE_ROWS)]


def _store_token_tiles(ref, x):
    rows = x.shape[0]
    for c in range(TOKEN_TILE_ROWS):
        ref[pl.ds(c, rows, stride=TOKEN_TILE_ROWS), :] = x[:, c * LANES:(c + 1) * LANES]


def _load_token_tiles(ref, rows):
    return [ref[pl.ds(c, rows, stride=TOKEN_TILE_ROWS), :] for c in range(TOKEN_TILE_ROWS)]


def _pair_row(seg_ref, pair):
    return seg_ref[pair >> RANK_BITS] + (pair & ((1 << RANK_BITS) - 1))


def _inverse_kernel(seg_ref, p1_ref, p2_ref, inv_ref):
    ts = p1_ref.shape[0]
    base = pl.program_id(0) * ts

    def body(t, carry):
        inv_ref[_pair_row(seg_ref, p1_ref[t])] = base + t
        inv_ref[_pair_row(seg_ref, p2_ref[t])] = base + t
        return carry

    lax.fori_loop(0, ts, body, 0, unroll=DMA_UNROLL)


def _inverse(seg, pair1, pair2, ts):
    n = pair1.shape[0]
    idx_spec = pl.BlockSpec((ts,), lambda i: (i,), memory_space=pltpu.SMEM)
    return pl.pallas_call(
        _inverse_kernel,
        grid=(n // ts,),
        in_specs=[pl.BlockSpec(memory_space=pltpu.SMEM), idx_spec, idx_spec],
        out_specs=pl.BlockSpec(memory_space=pltpu.SMEM),
        out_shape=jax.ShapeDtypeStruct((2 * n,), jnp.int32),
        compiler_params=_cparams(("arbitrary",)),
        name="moe_inverse",
    )(seg, pair1, pair2)


def _row_gather(src_ref, idx_of, dst_ref, dst_row0, sem, n_rows):
    def copy(t):
        return pltpu.make_async_copy(_token_tile(src_ref, idx_of(t)), _token_tile(dst_ref, dst_row0 + t), sem)

    def start():
        def body(t, carry):
            copy(2 * t).start(priority=0)
            copy(2 * t + 1).start(priority=1)
            return carry
        lax.fori_loop(0, n_rows // 2, body, 0, unroll=DMA_UNROLL // 2)

    def wait():
        def body(t, carry):
            copy(t).wait()
            return carry
        lax.fori_loop(0, n_rows, body, 0, unroll=DMA_UNROLL)

    return start, wait


def _experts_kernel(lo_ref, hi_ref, tile_ref, exp_ref, items_ref, inv_ref, inv_next_ref, h_ref,
                    wg_ref, wu_ref, wd_ref, ys_ref, xbuf_ref, sems, wg_s, wu_s, wd_s, *, n_tiles):
    i = pl.program_id(0)
    prev = jnp.maximum(i - 1, 0)
    e = exp_ref[i]
    tile = tile_ref[i]
    slot = tile % 2
    active = i < items_ref[0]
    first_visit = (i == 0) | (tile != tile_ref[prev])

    @pl.when((i == 0) | (e != exp_ref[prev]))
    def _():
        wg_s[...] = wg_ref[0].astype(BF16)
        wu_s[...] = wu_ref[0].astype(BF16)
        wd_s[...] = wd_ref[0].astype(BF16)

    start_this, wait_this = _row_gather(h_ref, lambda t: inv_ref[t], xbuf_ref, slot * MOE_TILE,
                                        sems.at[slot], MOE_TILE)
    start_next, _ = _row_gather(h_ref, lambda t: inv_next_ref[t], xbuf_ref, (1 - slot) * MOE_TILE,
                                sems.at[1 - slot], MOE_TILE)

    @pl.when(active & first_visit)
    def _():
        @pl.when(i == 0)
        def _():
            start_this()

        @pl.when(tile + 1 < n_tiles)
        def _():
            start_next()

        wait_this()

    @pl.when(active)
    def _():
        base = pl.multiple_of(slot * (MOE_TILE * TOKEN_TILE_ROWS), MOE_TILE * TOKEN_TILE_ROWS)
        x = jnp.concatenate([xbuf_ref[pl.ds(base + c, MOE_TILE, stride=TOKEN_TILE_ROWS), :]
                             for c in range(TOKEN_TILE_ROWS)], axis=-1).astype(BF16)
        act = _silu(_dot(x, wg_s[...])) * _dot(x, wu_s[...])
        out = _dot(act.astype(BF16), wd_s[...])
        row = tile_ref[i] * MOE_TILE + lax.broadcasted_iota(jnp.int32, (MOE_TILE, 1), 0)
        mine = (row >= lo_ref[e]) & (row < hi_ref[e])
        first_visit = (i == 0) | (tile_ref[i] != tile_ref[prev])

        @pl.when(first_visit)
        def _():
            _store_token_tiles(ys_ref, jnp.where(mine, out, 0.0))

        @pl.when(jnp.logical_not(first_visit))
        def _():
            old = jnp.concatenate(_load_token_tiles(ys_ref, MOE_TILE), axis=-1)
            _store_token_tiles(ys_ref, jnp.where(mine, out, old))


def _experts(seg_lo, seg_hi, item_tile, item_exp, n_items, inv, h2, wg, wu, wd):
    n_tiles = inv.shape[0] // MOE_TILE
    _, d, de = wg.shape

    def w_spec(shape):
        return pl.BlockSpec((1,) + shape, lambda i, lo, hi, tile, ex, items: (ex[i], 0, 0))

    return pl.pallas_call(
        functools.partial(_experts_kernel, n_tiles=n_tiles),
        grid_spec=pltpu.PrefetchScalarGridSpec(
            num_scalar_prefetch=5, grid=(item_tile.shape[0],),
            in_specs=[pl.BlockSpec((MOE_TILE,), lambda i, lo, hi, tile, ex, items: (tile[i],),
                                   memory_space=pltpu.SMEM),
                      pl.BlockSpec((MOE_TILE,), lambda i, lo, hi, tile, ex, items: (
                          jnp.minimum(tile[i] + 1, n_tiles - 1),), memory_space=pltpu.SMEM),
                      pl.BlockSpec(memory_space=pltpu.HBM),
                      w_spec((d, de)), w_spec((d, de)), w_spec((de, d))],
            out_specs=pl.BlockSpec((MOE_TILE * TOKEN_TILE_ROWS, LANES),
                                   lambda i, lo, hi, tile, ex, items: (tile[i], 0)),
            scratch_shapes=[pltpu.VMEM((2 * MOE_TILE * TOKEN_TILE_ROWS, LANES), F32),
                            pltpu.SemaphoreType.DMA((2,)),
                            pltpu.VMEM((d, de), BF16), pltpu.VMEM((d, de), BF16), pltpu.VMEM((de, d), BF16)]),
        out_shape=jax.ShapeDtypeStruct((inv.shape[0] * TOKEN_TILE_ROWS, LANES), F32),
        compiler_params=_cparams(("arbitrary",)),
        name="moe_experts",
    )(seg_lo, seg_hi, item_tile, item_exp, n_items, inv, inv, h2, wg, wu, wd)


def _combine_kernel(seg_ref, p1_ref, p2_ref, p1n_ref, p2n_ref, rw_ref, x1_ref, gate_ref, shift_ref, scale_ref,
                    nfin_ref, ys_ref, y_ref, r1_ref, r2_ref, sems, *, n_steps):
    _, rows, d = x1_ref.shape
    s = pl.program_id(0)
    slot = s % 2

    def gathers(pa_ref, pb_ref, slot):
        g1 = _row_gather(ys_ref, lambda t: _pair_row(seg_ref, pa_ref[t]), r1_ref, slot * rows, sems.at[slot], rows)
        g2 = _row_gather(ys_ref, lambda t: _pair_row(seg_ref, pb_ref[t]), r2_ref, slot * rows, sems.at[slot], rows)
        return g1, g2

    this = gathers(p1_ref, p2_ref, slot)
    nxt = gathers(p1n_ref, p2n_ref, 1 - slot)

    @pl.when(s == 0)
    def _():
        this[0][0]()
        this[1][0]()

    @pl.when(s + 1 < n_steps)
    def _():
        nxt[0][0]()
        nxt[1][0]()

    this[0][1]()
    this[1][1]()

    def per_token(ref):
        per_seq = ref[0]
        seqs = per_seq.shape[0]
        return jnp.broadcast_to(per_seq[:, None, :], (seqs, rows // seqs, d)).reshape(rows, d)

    rw = rw_ref[...]
    w1 = rw[:, 0:1]
    w2 = rw[:, 1:2]
    base = pl.multiple_of(slot * (rows * TOKEN_TILE_ROWS), rows * TOKEN_TILE_ROWS)
    moe = jnp.concatenate(
        [w1 * r1_ref[pl.ds(base + c, rows, stride=TOKEN_TILE_ROWS), :]
         + w2 * r2_ref[pl.ds(base + c, rows, stride=TOKEN_TILE_ROWS), :] for c in range(TOKEN_TILE_ROWS)], axis=-1)
    x2 = x1_ref[0] + per_token(gate_ref) * moe
    hf = _rms(x2) * nfin_ref[...]
    y_ref[0] = hf * (1.0 + per_token(scale_ref)) + per_token(shift_ref)


def _combine(seg, pair1, pair2, rw, x1, ada3, adaf3, nfin, ys, rows, tile0):
    b, t, d = x1.shape
    n_steps = b * t // rows
    seqs = max(rows // t, 1)
    tiles_per_seq = max(t // rows, 1)
    x3 = x1.reshape(n_steps, rows, d)
    ada_v = ada3.reshape(b // seqs, seqs, -1)
    adaf_v = adaf3.reshape(b // seqs, seqs, -1)
    x_spec = pl.BlockSpec((1, rows, d), lambda i: (i, 0, 0))

    def idx_spec(step):
        return pl.BlockSpec((rows,), lambda i: (tile0 + step(i),), memory_space=pltpu.SMEM)

    def ada_spec(col):
        return pl.BlockSpec((1, seqs, d), lambda i: (i // tiles_per_seq, 0, col))

    def cur(i):
        return i

    def nxt(i):
        return jnp.minimum(i + 1, n_steps - 1)

    return pl.pallas_call(
        functools.partial(_combine_kernel, n_steps=n_steps),
        grid=(n_steps,),
        in_specs=[pl.BlockSpec(memory_space=pltpu.SMEM), idx_spec(cur), idx_spec(cur), idx_spec(nxt), idx_spec(nxt),
                  pl.BlockSpec((rows, ROUTER_LANES), lambda i: (tile0 + i, 0)),
                  x_spec, ada_spec(5), ada_spec(0), ada_spec(1),
                  pl.BlockSpec((1, d), lambda i: (0, 0)),
                  pl.BlockSpec(memory_space=pltpu.HBM)],
        out_specs=x_spec,
        scratch_shapes=[pltpu.VMEM((2 * rows * TOKEN_TILE_ROWS, LANES), F32),
                        pltpu.VMEM((2 * rows * TOKEN_TILE_ROWS, LANES), F32),
                        pltpu.SemaphoreType.DMA((2,))],
        out_shape=jax.ShapeDtypeStruct(x3.shape, F32),
        compiler_params=_cparams(("arbitrary",)),
        name="moe_combine",
    )(seg, pair1, pair2, pair1, pair2, rw, x3, ada_v, adaf_v, adaf_v, nfin, ys).reshape(x1.shape)


def _mixer(x, ada3, s_h, s_re, s_im, p, *, bb, tt, hgrn_seqs, hgrn_tokens, sequential):
    b, t, d = x.shape
    n = b * t
    q, k, g, v, gs, u = _inproj(x, ada3, p['norm_mix'], p['w_in'], p['lb'], bb, tt)
    oh, s_h_new = _hgrn(q, k, g, v, gs, s_h, p['hgrn_norm'], b, t, hgrn_seqs, hgrn_tokens)
    if sequential:
        xr, xi = s_re.reshape(b, 1, -1), s_im.reshape(b, 1, -1)
        n_tokens = t
    else:
        xr, xi = s_re.reshape(1, b, -1), s_im.reshape(1, b, -1)
        n_tokens = n
    ys, fr, fi = _s5(u, xr, xi, p['ssm_bd'], p['ssm_inc'], p['ssm_m'], p['ssm_a8'], n_tokens, sequential)
    states = (s_h_new[None], fr.reshape(1, b, s_re.shape[-2], s_re.shape[-1]),
              fi.reshape(1, b, s_re.shape[-2], s_re.shape[-1]))
    return oh, ys, states


def kernel(x_prompt, x_sample, c_prompt, c_sample, state_hgrn, state_ssm_re, state_ssm_im, hgrn_lb_logits, w_ada, b_ada, norm_mix, w_in, hgrn_norm, ssm_a_re, ssm_a_im, ssm_log_dt, ssm_b_re, ssm_b_im, ssm_c_re, ssm_c_im, ssm_d, ssm_w_glu, ssm_b_glu, ssm_norm, w_out, norm_ffn, moe_w_group, moe_b_group, moe_w_router, moe_b_router, moe_w_gate, moe_w_up, moe_w_down, w_ada_final, b_ada_final, norm_final):
    depth = w_ada.shape[0]
    assert depth == 1
    d = x_prompt.shape[-1]
    bp = x_prompt.shape[0]
    dh = hgrn_norm.shape[-1]
    dk = dh // HGRN_HEADS
    de = moe_w_gate.shape[-1]
    n_exp = MOE_GROUPS * MOE_PER_GROUP

    lb = jax.nn.softmax(hgrn_lb_logits.astype(F32), axis=0)[0].reshape(1, dh)
    bd, inc, m, a8 = _s5_prepare(ssm_a_re[0], ssm_a_im[0], ssm_log_dt[0], ssm_b_re[0], ssm_b_im[0],
                                 ssm_c_re[0], ssm_c_im[0], ssm_d[0])

    def router_lanes(group_part, expert_part):
        rows = group_part.shape[0]
        return jnp.concatenate(
            [group_part, jnp.zeros((rows, EXPERT_LANE0 - MOE_GROUPS), F32), expert_part,
             jnp.zeros((rows, ROUTER_LANES - EXPERT_LANE0 - n_exp), F32)], axis=1)

    w_rt = router_lanes(moe_w_group[0], moe_w_router[0].transpose(1, 0, 2).reshape(d, n_exp))
    b_rt = router_lanes(moe_b_group[0].reshape(1, MOE_GROUPS), moe_b_router[0].reshape(1, n_exp))
    p = dict(
        lb=lb, norm_mix=norm_mix[0].reshape(1, d), w_in=w_in[0].astype(BF16),
        hgrn_norm=hgrn_norm[0].reshape(1, dh),
        ssm_bd=bd, ssm_inc=inc, ssm_m=m, ssm_a8=a8,
        w_glu=ssm_w_glu[0].astype(BF16), b_glu=ssm_b_glu[0].reshape(1, -1), ssm_norm=ssm_norm[0].reshape(1, -1),
        wo_h=w_out[0, :dh].astype(BF16), wo_s=w_out[0, dh:].astype(BF16),
        norm_ffn=norm_ffn[0].reshape(1, d), w_rt=w_rt.astype(BF16), b_rt=b_rt,
    )
    wg = moe_w_gate[0].reshape(n_exp, d, de)
    wu = moe_w_up[0].reshape(n_exp, d, de)
    wd = moe_w_down[0].reshape(n_exp, de, d)
    nfin = norm_final.reshape(1, d)

    c_all = jnp.concatenate([c_prompt, c_sample], axis=0)
    ada = _silu_linear(c_all, w_ada[0], b_ada[0])
    adaf = _silu_linear(c_all, w_ada_final, b_ada_final)
    ada_p, ada_s = ada[:bp].reshape(bp, 1, -1), ada[bp:].reshape(x_sample.shape[0], 1, -1)
    adaf_p, adaf_s = adaf[:bp].reshape(bp, 1, -1), adaf[bp:].reshape(x_sample.shape[0], 1, -1)

    zeros_h = jnp.zeros((bp, HGRN_HEADS, dk, dk), F32)
    zeros_s = jnp.zeros((bp,) + state_ssm_re.shape[2:], F32)
    oh_p, ys_p, st_p = _mixer(x_prompt, ada_p, zeros_h, zeros_s, zeros_s, p,
                              bb=1, tt=256, hgrn_seqs=1, hgrn_tokens=512, sequential=True)
    oh_s, ys_s, st_s = _mixer(x_sample, ada_s, state_hgrn[0], state_ssm_re[0], state_ssm_im[0], p,
                              bb=32, tt=8, hgrn_seqs=16, hgrn_tokens=8, sequential=False)
    x1_p, x1_s, h2, pair1, pair2, rw, cnt = _outproj(
        x_prompt, x_sample, oh_p, oh_s, ys_p, ys_s, ada_p, ada_s, p['w_glu'], p['b_glu'], p['ssm_norm'],
        p['wo_h'], p['wo_s'], p['norm_ffn'], p['w_rt'], p['b_rt'], TOKEN_ROWS)

    n_pairs = 2 * pair1.shape[0]
    n_slots = n_pairs // MOE_TILE + N_EXPERTS
    seg, seg_end, item_tile, item_exp, n_items = _moe_schedule(cnt, n_slots)
    inv = _inverse(seg, pair1, pair2, 1024)
    ys = _experts(seg, seg_end, item_tile, item_exp, n_items, inv, h2, wg, wu, wd)
    tiles_p = x_prompt.shape[0] * x_prompt.shape[1] // TOKEN_ROWS
    y_p = _combine(seg, pair1, pair2, rw, x1_p, ada_p, adaf_p, nfin, ys, TOKEN_ROWS, 0)
    y_s = _combine(seg, pair1, pair2, rw, x1_s, ada_s, adaf_s, nfin, ys, TOKEN_ROWS, tiles_p)
    return (y_p, y_s) + st_p + st_s
```

```python
import functools
import math

import jax
import jax.numpy as jnp
from jax import lax
from jax.experimental import pallas as pl
from jax.experimental.pallas import tpu as pltpu

F32 = jnp.float32
BF16 = jnp.bfloat16
HIGHEST = lax.Precision.HIGHEST

EPS = 1e-6
MAX_REAL = -1e-4
HGRN_HEADS = 4
HGRN_CHUNK = 128
HGRN_SAFE_EXPONENT = 80.0
HGRN_EXACT_BLOCK = 8
SSM_GROUP = 16
SSM_STATE = 64
SSM_CHUNK = 8
SSM_SETS = 4
MOE_GROUPS = 4
MOE_PER_GROUP = 8
N_EXPERTS = MOE_GROUPS * MOE_PER_GROUP
ROUTER_LANES = 128
EXPERT_LANE0 = 32
RANK_BITS = 20
MOE_TILE = 256
LANES = 128
TOKEN_TILE_ROWS = 8
DMA_UNROLL = 8
TOKEN_ROWS = 512
VMEM_LIMIT = 56 * 1024 * 1024


def _cparams(sem):
    return pltpu.CompilerParams(dimension_semantics=sem, vmem_limit_bytes=VMEM_LIMIT)


def _silu(x):
    return x * jax.nn.sigmoid(x)


def _rms(x):
    return x * lax.rsqrt(jnp.mean(x * x, axis=-1, keepdims=True) + EPS)


def _dot(a, b):
    return jnp.dot(a, b, preferred_element_type=F32)


def _dot_nt(a, b):
    return lax.dot_general(a, b, (((1,), (1,)), ((), ())), preferred_element_type=F32)


def _dot_tn(a, b, precision=None):
    return lax.dot_general(a, b, (((0,), (0,)), ((), ())), preferred_element_type=F32,
                           precision=precision)


def _silu_linear_kernel(c_ref, w_ref, b_ref, o_ref):
    a = _silu(c_ref[...]).astype(BF16)
    o_ref[...] = _dot(a, w_ref[...].astype(BF16)) + b_ref[...]


def _silu_linear(c, w, b):
    m, d = c.shape
    n = w.shape[1]
    tn = 1024
    return pl.pallas_call(
        _silu_linear_kernel,
        grid=(n // tn,),
        in_specs=[pl.BlockSpec((m, d), lambda j: (0, 0)),
                  pl.BlockSpec((d, tn), lambda j: (0, j)),
                  pl.BlockSpec((1, tn), lambda j: (0, j))],
        out_specs=pl.BlockSpec((m, tn), lambda j: (0, j)),
        out_shape=jax.ShapeDtypeStruct((m, n), F32),
        compiler_params=_cparams(("parallel",)),
        name="silu_linear",
    )(c, w, b.reshape(1, n))


def _inproj_kernel(x_ref, shift_ref, scale_ref, gain_ref, w_ref, lb_ref,
                   q_ref, k_ref, g_ref, v_ref, gs_ref, u_ref, *, dh):
    bb, tt, d = x_ref.shape
    h = _rms(x_ref[...]) * gain_ref[...]
    h = h * (1.0 + scale_ref[...]) + shift_ref[...]
    proj = _dot(h.reshape(bb * tt, d).astype(BF16), w_ref[...])
    lb = lb_ref[...]
    f = lb + (1.0 - lb) * jax.nn.sigmoid(proj[:, dh:2 * dh])
    q_ref[...] = _silu(proj[:, :dh]) * (float(dh // HGRN_HEADS) ** -0.5)
    k_ref[...] = 1.0 - f
    g_ref[...] = jnp.log(f)
    v_ref[...] = proj[:, 2 * dh:3 * dh]
    gs_ref[...] = _silu(proj[:, 3 * dh:4 * dh])
    for s in range(SSM_SETS):
        u_ref[s] = proj[:, 4 * dh + s * LANES:4 * dh + (s + 1) * LANES]


def _inproj(x, ada3, gain, w_in_bf, lb, bb, tt):
    b, t, d = x.shape
    dh = lb.shape[-1]
    nt = t // tt
    rows = bb * tt
    n = b * t
    row_spec = pl.BlockSpec((rows, dh), lambda i, j: (i * nt + j, 0))
    out = jax.ShapeDtypeStruct((n, dh), F32)
    return pl.pallas_call(
        functools.partial(_inproj_kernel, dh=dh),
        grid=(b // bb, nt),
        in_specs=[pl.BlockSpec((bb, tt, d), lambda i, j: (i, j, 0)),
                  pl.BlockSpec((bb, 1, d), lambda i, j: (i, 0, 0)),
                  pl.BlockSpec((bb, 1, d), lambda i, j: (i, 0, 1)),
                  pl.BlockSpec((1, d), lambda i, j: (0, 0)),
                  pl.BlockSpec(w_in_bf.shape, lambda i, j: (0, 0)),
                  pl.BlockSpec((1, dh), lambda i, j: (0, 0))],
        out_specs=[row_spec] * 5 + [pl.BlockSpec((SSM_SETS, rows, LANES), lambda i, j: (0, i * nt + j, 0))],
        out_shape=[out] * 5 + [jax.ShapeDtypeStruct((SSM_SETS, n, LANES), F32)],
        compiler_params=_cparams(("parallel", "parallel")),
        name="inproj",
    )(x, ada3, ada3, gain, w_in_bf, lb)


def _split3(x):
    hi = x.astype(BF16)
    r1 = x - hi.astype(F32)
    mid = r1.astype(BF16)
    lo = (r1 - mid.astype(F32)).astype(BF16)
    return hi, mid, lo


def _hgrn_kernel(q_ref, k_ref, g_ref, v_ref, gs_ref, s0_ref, hn_ref, o_ref, sf_ref,
                 st_ref, intra_ref, qh_ref, kh_ref, ea_ref, sums_ref, *, tl, nt):
    j = pl.program_id(1)
    rows_total, dh = q_ref.shape
    dk = dh // HGRN_HEADS
    c = HGRN_CHUNK
    seqs = c // tl
    n_chunks = rows_total // c

    @pl.when(j == 0)
    def _():
        st_ref[...] = s0_ref[...]

    r = lax.broadcasted_iota(jnp.int32, (c, c), 0)
    s = lax.broadcasted_iota(jnp.int32, (c, c), 1)
    same_seq = (r // tl) == (s // tl)
    causal = same_seq & (r >= s)
    upto_mid = same_seq & ((s % tl) < tl // 2)
    one_seq = tl == c
    sums = (causal if one_seq else jnp.concatenate([causal, upto_mid, same_seq], axis=0)).astype(BF16)
    ref_rows = 1 if one_seq else c
    eye3 = (lax.broadcasted_iota(jnp.int32, (dk, 3 * dk), 1) % dk
            == lax.broadcasted_iota(jnp.int32, (dk, 3 * dk), 0)).astype(BF16)

    def decay_matrix(e_row):
        parts = jnp.concatenate(_split3(e_row), axis=-1)
        return _dot_nt(eye3, jnp.broadcast_to(parts, (dk, 3 * dk)))

    worst = jnp.float32(0.0)
    for ci in range(n_chunks):
        rows = slice(ci * c, (ci + 1) * c)
        g_parts = _split3(g_ref[rows, :])
        acc = _dot(sums, g_parts[0]) + _dot(sums, g_parts[1]) + _dot(sums, g_parts[2])
        if one_seq:
            a_mid, a_end = acc[c // 2 - 1:c // 2], acc[c - 1:c]
            sums_ref[ci * 3 * c:ci * 3 * c + c, :] = acc
            sums_ref[ci * 3 * c + c:ci * 3 * c + c + 1, :] = a_mid
            sums_ref[ci * 3 * c + 2 * c:ci * 3 * c + 2 * c + 1, :] = a_end
        else:
            a_mid, a_end = acc[c:2 * c], acc[2 * c:]
            sums_ref[ci * 3 * c:(ci + 1) * 3 * c, :] = acc
        worst = jnp.maximum(worst, jnp.max(jnp.maximum(jnp.abs(a_mid), jnp.abs(a_end - a_mid))))
    factorised_is_safe = worst < HGRN_SAFE_EXPONENT

    @pl.when(factorised_is_safe)
    def _():
        for ci in range(n_chunks):
            rows = slice(ci * c, (ci + 1) * c)
            a = sums_ref[ci * 3 * c:ci * 3 * c + c, :]
            a_mid = sums_ref[ci * 3 * c + c:ci * 3 * c + c + ref_rows, :]
            a_end = sums_ref[ci * 3 * c + 2 * c:ci * 3 * c + 2 * c + ref_rows, :]
            e_mid = jnp.exp(a_mid)
            e_tail = jnp.exp(a_end - a_mid)
            qt = q_ref[rows, :] * jnp.exp(a - a_mid)
            kt = k_ref[rows, :] * jnp.exp(a_mid - a)
            qh_ref[rows, :] = qt * e_mid
            kh_ref[rows, :] = kt * e_tail
            ea_ref[rows.start:rows.start + ref_rows, :] = e_mid * e_tail
            qt = qt.astype(BF16)
            kt = kt.astype(BF16)
            v = v_ref[rows, :].astype(BF16)
            for h in range(HGRN_HEADS):
                lanes = slice(h * dk, (h + 1) * dk)
                sc = jnp.where(causal, _dot_nt(qt[:, lanes], kt[:, lanes]), 0.0).astype(BF16)
                intra_ref[rows, lanes] = _dot(sc, v[:, lanes])

            for si in range(seqs):
                seq = ci * seqs + si if tl < c else 0
                srows = slice(si * tl, (si + 1) * tl)
                orows = slice(ci * c + si * tl, ci * c + (si + 1) * tl)
                for h in range(HGRN_HEADS):
                    lanes = slice(h * dk, (h + 1) * dk)
                    state = st_ref[seq, h]
                    o = intra_ref[orows, lanes] + _dot(qh_ref[orows, lanes].astype(BF16), state.astype(BF16))
                    decay = decay_matrix(ea_ref[orows.start:orows.start + 1, lanes])
                    st_ref[seq, h] = decay * state + _dot_tn(kh_ref[orows, lanes].astype(BF16),
                                                             v[srows, lanes])
                    o_ref[orows, lanes] = _rms(o) * hn_ref[:, lanes] * gs_ref[orows, lanes]

    @pl.when(jnp.logical_not(factorised_is_safe))
    def _():
        blk = HGRN_EXACT_BLOCK
        blocks_per_seq = max(tl // blk, 1)
        tri = (lax.broadcasted_iota(jnp.int32, (blk, blk), 0)
               >= lax.broadcasted_iota(jnp.int32, (blk, blk), 1)).astype(BF16)
        sub = lax.broadcasted_iota(jnp.int32, (blk, dk), 0)

        def block(bi, carry):
            rows = pl.ds(pl.multiple_of(bi * blk, blk), blk)
            seq = bi // blocks_per_seq if tl < c else 0
            g_parts = _split3(g_ref[rows, :])
            a = _dot(tri, g_parts[0]) + _dot(tri, g_parts[1]) + _dot(tri, g_parts[2])
            a_end = a[blk - 1:blk]
            q = q_ref[rows, :]
            k = k_ref[rows, :]
            v = v_ref[rows, :]
            qh = (q * jnp.exp(a)).astype(BF16)
            kh = (k * jnp.exp(a_end - a)).astype(BF16)
            ea = jnp.exp(a_end)
            vb = v.astype(BF16)
            for h in range(HGRN_HEADS):
                lanes = slice(h * dk, (h + 1) * dk)
                state = st_ref[seq, h]
                intra = []
                for t in range(blk):
                    live = sub <= t
                    decay_t = jnp.where(live, jnp.exp(jnp.where(live, a[t:t + 1, lanes] - a[:, lanes], 0.0)), 0.0)
                    score = jnp.sum(q[t:t + 1, lanes] * k[:, lanes] * decay_t, axis=-1, keepdims=True)
                    intra.append(jnp.sum(score * v[:, lanes], axis=0, keepdims=True))
                o = jnp.concatenate(intra, axis=0) + _dot(qh[:, lanes], state.astype(BF16))
                st_ref[seq, h] = decay_matrix(ea[:, lanes]) * state + _dot_tn(kh[:, lanes], vb[:, lanes])
                o_ref[rows, lanes] = _rms(o) * hn_ref[:, lanes] * gs_ref[rows, lanes]
            return carry

        lax.fori_loop(0, rows_total // blk, block, 0)

    @pl.when(j == nt - 1)
    def _():
        sf_ref[...] = st_ref[...]


def _hgrn(q, k, g, v, gs, s0, hnorm, b, t, nseq, tt):
    n, dh = q.shape
    nt = t // tt
    dk = dh // HGRN_HEADS
    rows = nseq * tt
    tl = min(tt, HGRN_CHUNK)
    row_spec = pl.BlockSpec((rows, dh), lambda i, j: (i * nt + j, 0))
    st_spec = pl.BlockSpec((nseq, HGRN_HEADS, dk, dk), lambda i, j: (i, 0, 0, 0))
    return pl.pallas_call(
        functools.partial(_hgrn_kernel, tl=tl, nt=nt),
        grid=(b // nseq, nt),
        in_specs=[row_spec] * 5 + [st_spec, pl.BlockSpec((1, dh), lambda i, j: (0, 0))],
        out_specs=[row_spec, st_spec],
        out_shape=[jax.ShapeDtypeStruct((n, dh), F32),
                   jax.ShapeDtypeStruct((b, HGRN_HEADS, dk, dk), F32)],
        scratch_shapes=[pltpu.VMEM((nseq, HGRN_HEADS, dk, dk), F32),
                        pltpu.VMEM((rows, dh), F32), pltpu.VMEM((rows, dh), F32),
                        pltpu.VMEM((rows, dh), F32), pltpu.VMEM((rows, dh), F32),
                        pltpu.VMEM((3 * rows, dh), F32)],
        compiler_params=_cparams(("parallel", "arbitrary")),
        name="hgrn",
    )(q, k, g, v, gs, s0, hnorm)


def _s5_prepare(a_re, a_im, log_dt, b_re, b_im, c_re, c_im, d_skip):
    ng, npp = a_re.shape
    nh = b_re.shape[-1]
    L = SSM_CHUNK
    gs = ng // SSM_SETS
    lam_re = jnp.minimum(a_re, MAX_REAL)
    lam_im = a_im
    dt = jnp.exp(log_dt)
    mag = jnp.exp(lam_re * dt)
    ab_re = mag * jnp.cos(lam_im * dt)
    ab_im = mag * jnp.sin(lam_im * dt)
    den = lam_re * lam_re + lam_im * lam_im
    co_re = ((ab_re - 1.0) * lam_re + ab_im * lam_im) / den
    co_im = (ab_im * lam_re - (ab_re - 1.0) * lam_im) / den
    bb_re = co_re[..., None] * b_re - co_im[..., None] * b_im
    bb_im = co_re[..., None] * b_im + co_im[..., None] * b_re
    tau = jnp.arange(L + 1, dtype=F32)[:, None, None]
    pw_mag = jnp.exp(tau * (lam_re * dt))
    pw_re = pw_mag * jnp.cos(tau * (lam_im * dt))
    pw_im = pw_mag * jnp.sin(tau * (lam_im * dt))
    ab_b_re = pw_re[:L, :, :, None] * bb_re - pw_im[:L, :, :, None] * bb_im
    ab_b_im = pw_re[:L, :, :, None] * bb_im + pw_im[:L, :, :, None] * bb_re
    kern = (jnp.einsum('gkp,lgph->lghk', c_re, ab_b_re, precision=HIGHEST)
            - jnp.einsum('gkp,lgph->lghk', c_im, ab_b_im, precision=HIGHEST))
    skip = d_skip[None, :, :, None] * jnp.eye(nh, dtype=F32)
    kern = kern + jnp.where(jnp.arange(L)[:, None, None, None] == 0, skip, 0.0)

    def group_block_diag(c):
        rows, w = c.shape[-2:]
        repeat = (jnp.arange(gs * w)[None, :] % w == jnp.arange(w)[:, None]).astype(F32)
        tiled = jnp.dot(c, repeat, precision=HIGHEST)
        rg = jnp.arange(rows)[:, None] // (rows // gs)
        cq = jnp.arange(gs * w)[None, :] // w
        return jnp.where(rg == cq, tiled, 0.0).astype(BF16).transpose(1, 0, 2, 3).reshape(
            SSM_SETS, L * rows, gs * w)

    bd = group_block_diag(kern.reshape(L, SSM_SETS, gs * nh, nh)).reshape(SSM_SETS, L, gs * nh, gs * nh)
    n_re = group_block_diag(ab_b_re[::-1].transpose(0, 1, 3, 2).reshape(L, SSM_SETS, gs * nh, npp))
    n_im = group_block_diag(ab_b_im[::-1].transpose(0, 1, 3, 2).reshape(L, SSM_SETS, gs * nh, npp))
    inc = jnp.concatenate([n_re, n_im], axis=-1)
    ca_re = c_re[None] * pw_re[1:, :, None, :] - c_im[None] * pw_im[1:, :, None, :]
    ca_im = c_re[None] * pw_im[1:, :, None, :] + c_im[None] * pw_re[1:, :, None, :]
    m = jnp.concatenate([group_block_diag(ca_re.reshape(L, SSM_SETS, gs * nh, npp)),
                         group_block_diag(-ca_im.reshape(L, SSM_SETS, gs * nh, npp))], axis=-1)
    a8 = jnp.concatenate([pw_re[L].reshape(SSM_SETS, 1, gs * npp),
                          pw_im[L].reshape(SSM_SETS, 1, gs * npp)], axis=-1)
    return bd, inc, m, a8


def _s5_kernel(u_ref, xr_ref, xi_ref, bd_ref, inc_ref, m_ref, a8_ref, y_ref, fr_ref, fi_ref, wt_ref, *, sequential):
    n = u_ref.shape[1] // SSM_CHUNK
    ns = xr_ref.shape[-1]
    u = jnp.concatenate([u_ref[0, pl.ds(s, n, stride=SSM_CHUNK), :] for s in range(SSM_CHUNK)], axis=-1)

    @pl.when(pl.program_id(1) == 0)
    def _():
        wt_ref[...] = jnp.zeros_like(wt_ref)
        for s in range(SSM_CHUNK):
            for t in range(s, SSM_CHUNK):
                wt_ref[s * LANES:(s + 1) * LANES, t * LANES:(t + 1) * LANES] = bd_ref[0, t - s]

    u = u.astype(BF16)
    y_local = _dot(u, wt_ref[...])
    inc = _dot(u, inc_ref[0])
    d_re = inc[:, :ns]
    d_im = inc[:, ns:]
    a_re = a8_ref[0][:, :ns]
    a_im = a8_ref[0][:, ns:]
    x0_re = xr_ref[0]
    x0_im = xi_ref[0]
    if sequential:
        row = lax.broadcasted_iota(jnp.int32, (n, ns), 0)
        first = row == 0
        x_re = d_re + jnp.where(first, a_re * x0_re - a_im * x0_im, 0.0)
        x_im = d_im + jnp.where(first, a_re * x0_im + a_im * x0_re, 0.0)
        p_re, p_im = a_re, a_im
        step = 1
        while step < n:
            s_re = jnp.where(row >= step, pltpu.roll(x_re, step, 0), 0.0)
            s_im = jnp.where(row >= step, pltpu.roll(x_im, step, 0), 0.0)
            x_re, x_im = x_re + p_re * s_re - p_im * s_im, x_im + p_re * s_im + p_im * s_re
            p_re, p_im = p_re * p_re - p_im * p_im, 2.0 * p_re * p_im
            step *= 2
        fr_ref[0] = x_re[n - 1:n]
        fi_ref[0] = x_im[n - 1:n]
        xc_re = jnp.where(first, x0_re, pltpu.roll(x_re, 1, 0))
        xc_im = jnp.where(first, x0_im, pltpu.roll(x_im, 1, 0))
    else:
        xc_re, xc_im = x0_re, x0_im
        fr_ref[0] = a_re * x0_re - a_im * x0_im + d_re
        fi_ref[0] = a_re * x0_im + a_im * x0_re + d_im
    xc = jnp.concatenate([xc_re, xc_im], axis=-1).astype(BF16)
    y = y_local + _dot_nt(xc, m_ref[0])
    for t in range(SSM_CHUNK):
        y_ref[0, pl.ds(t, n, stride=SSM_CHUNK), :] = y[:, t * LANES:(t + 1) * LANES]


def _s5(u, x_re, x_im, bd, inc, m, a8, n_tokens, sequential):
    sets = u.shape[0]
    nb, rb, _ = x_re.shape
    ns = m.shape[1] // 2
    st_spec = pl.BlockSpec((1, rb, ns), lambda gi, i: (i, 0, gi))
    st_shape = jax.ShapeDtypeStruct(x_re.shape, F32)
    tok_spec = pl.BlockSpec((1, n_tokens, LANES), lambda gi, i: (gi, i, 0))
    return pl.pallas_call(
        functools.partial(_s5_kernel, sequential=sequential),
        grid=(sets, nb),
        in_specs=[tok_spec, st_spec, st_spec,
                  pl.BlockSpec((1,) + bd.shape[1:], lambda gi, i: (gi, 0, 0, 0)),
                  pl.BlockSpec((1,) + inc.shape[1:], lambda gi, i: (gi, 0, 0)),
                  pl.BlockSpec((1,) + m.shape[1:], lambda gi, i: (gi, 0, 0)),
                  pl.BlockSpec((1, 1, 2 * ns), lambda gi, i: (gi, 0, 0))],
        out_specs=[tok_spec, st_spec, st_spec],
        out_shape=[jax.ShapeDtypeStruct(u.shape, F32), st_shape, st_shape],
        scratch_shapes=[pltpu.VMEM((inc.shape[1], SSM_CHUNK * LANES), BF16)],
        compiler_params=_cparams(("parallel", "arbitrary")),
        name="s5",
    )(u, x_re, x_im, bd, inc, m, a8)


def _gelu_tanh(x):
    return 0.5 * x * (1.0 + jnp.tanh(math.sqrt(2.0 / math.pi) * (x + 0.044715 * (x * x * x))))


def _outproj_kernel(xp_ref, xs_ref, ohp_ref, ohs_ref, ysp_ref, yss_ref,
                    gate_p_ref, shift_p_ref, scale_p_ref, gate_s_ref, shift_s_ref, scale_s_ref,
                    wglu_ref, bglu_ref, sn_ref, wo_h_ref, wo_s_ref, nf_ref, wr_ref, br_ref,
                    x1p_ref, x1s_ref, h2_ref, pair1_ref, pair2_ref, rw_ref, cnt_ref, *, prompt_tiles):
    is_prompt = pl.program_id(0) < prompt_tiles
    _, rows, d = xp_ref.shape

    def per_token(p_ref, s_ref):
        per_seq = s_ref[0]
        seqs = per_seq.shape[0]
        rep = jnp.broadcast_to(per_seq[:, None, :], (seqs, rows // seqs, d)).reshape(rows, d)
        return jnp.where(is_prompt, p_ref[0], rep)

    x = jnp.where(is_prompt, xp_ref[0], xs_ref[0])
    oh = jnp.where(is_prompt, ohp_ref[...], ohs_ref[...])
    ys = jnp.concatenate([jnp.where(is_prompt, ysp_ref[s], yss_ref[s]) for s in range(SSM_SETS)], axis=-1)
    y = _gelu_tanh(ys)
    y = y * jax.nn.sigmoid(_dot(y.astype(BF16), wglu_ref[...]) + bglu_ref[...])
    o_s = _rms(y) * sn_ref[...]
    mix = _dot(oh.astype(BF16), wo_h_ref[...]) + _dot(o_s.astype(BF16), wo_s_ref[...])
    x1 = x + per_token(gate_p_ref, gate_s_ref) * mix

    @pl.when(is_prompt)
    def _():
        x1p_ref[0] = x1

    @pl.when(jnp.logical_not(is_prompt))
    def _():
        x1s_ref[0] = x1

    h2 = _rms(x1) * nf_ref[...]
    h2 = h2 * (1.0 + per_token(scale_p_ref, scale_s_ref)) + per_token(shift_p_ref, shift_s_ref)
    _store_token_tiles(h2_ref, h2)

    logits = _dot(h2.astype(BF16), wr_ref[...]) + br_ref[...]
    lane = lax.broadcasted_iota(jnp.int32, logits.shape, 1)
    neg = -jnp.inf
    gl = jnp.where(lane < MOE_GROUPS, logits, neg)
    gmax = jnp.max(gl, axis=-1, keepdims=True)
    gidx = jnp.min(jnp.where(gl == gmax, lane, ROUTER_LANES), axis=-1, keepdims=True)
    grp_w = 1.0 / jnp.sum(jnp.exp(gl - gmax), axis=-1, keepdims=True)
    e0 = EXPERT_LANE0 + gidx * MOE_PER_GROUP
    sel = jnp.where((lane >= e0) & (lane < e0 + MOE_PER_GROUP), logits, neg)
    m1 = jnp.max(sel, axis=-1, keepdims=True)
    i1 = jnp.min(jnp.where(sel == m1, lane, ROUTER_LANES), axis=-1, keepdims=True)
    sel2 = jnp.where(lane == i1, neg, sel)
    m2 = jnp.max(sel2, axis=-1, keepdims=True)
    i2 = jnp.min(jnp.where(sel2 == m2, lane, ROUTER_LANES), axis=-1, keepdims=True)
    e2 = jnp.exp(m2 - m1)
    w1 = 1.0 / (1.0 + e2)
    w2 = e2 / (1.0 + e2)
    rw_ref[...] = grp_w * (jnp.where(lane == 0, w1, 0.0) + jnp.where(lane == 1, w2, 0.0))

    @pl.when(pl.program_id(0) == 0)
    def _():
        cnt_ref[...] = jnp.zeros_like(cnt_ref)

    picked = (lane == i1) | (lane == i2)
    earlier = (lax.broadcasted_iota(jnp.int32, (rows, rows), 0)
               > lax.broadcasted_iota(jnp.int32, (rows, rows), 1))
    base = cnt_ref[...]
    before = _dot(earlier.astype(BF16), picked.astype(BF16)) + base
    rank1 = jnp.sum(jnp.where(lane == i1, before, 0.0), axis=-1, keepdims=True).astype(jnp.int32)
    rank2 = jnp.sum(jnp.where(lane == i2, before, 0.0), axis=-1, keepdims=True).astype(jnp.int32)
    cnt_ref[...] = base + jnp.sum(picked.astype(F32), axis=0, keepdims=True)
    info = jnp.where(lane == 0, ((i1 - EXPERT_LANE0) << RANK_BITS) | rank1,
                     jnp.where(lane == 1, ((i2 - EXPERT_LANE0) << RANK_BITS) | rank2, 0)).T
    pair1_ref[...] = info[0]
    pair2_ref[...] = info[1]


def _outproj(x_p, x_s, oh_p, oh_s, ys_p, ys_s, ada_p, ada_s, wglu_bf, bglu, snorm, wo_h, wo_s, nffn, wr, br, rows):
    d = x_p.shape[-1]
    dh = oh_p.shape[-1]
    n_p = x_p.shape[0] * x_p.shape[1]
    n_s = x_s.shape[0] * x_s.shape[1]
    n = n_p + n_s
    tiles_p = n_p // rows
    seqs = rows // x_s.shape[1]

    def pt(i):
        return jnp.minimum(i, tiles_p - 1)

    def st(i):
        return jnp.maximum(i - tiles_p, 0)

    tiles_per_seq = x_p.shape[1] // rows

    def ada_p_spec(col):
        return pl.BlockSpec((1, 1, d), lambda i: (pt(i) // tiles_per_seq, 0, col))

    def ada_s_spec(col):
        return pl.BlockSpec((1, seqs, d), lambda i: (st(i), 0, col))

    def full(a):
        return pl.BlockSpec(a.shape, lambda i: (0,) * a.ndim)

    xp3 = x_p.reshape(tiles_p, rows, d)
    xs3 = x_s.reshape(n_s // rows, rows, d)
    ada_s3 = ada_s.reshape(n_s // rows, seqs, -1)
    outs = pl.pallas_call(
        functools.partial(_outproj_kernel, prompt_tiles=tiles_p),
        grid=(n // rows,),
        in_specs=[pl.BlockSpec((1, rows, d), lambda i: (pt(i), 0, 0)),
                  pl.BlockSpec((1, rows, d), lambda i: (st(i), 0, 0)),
                  pl.BlockSpec((rows, dh), lambda i: (pt(i), 0)),
                  pl.BlockSpec((rows, dh), lambda i: (st(i), 0)),
                  pl.BlockSpec((SSM_SETS, rows, LANES), lambda i: (0, pt(i), 0)),
                  pl.BlockSpec((SSM_SETS, rows, LANES), lambda i: (0, st(i), 0)),
                  ada_p_spec(2), ada_p_spec(3), ada_p_spec(4), ada_s_spec(2), ada_s_spec(3), ada_s_spec(4),
                  full(wglu_bf), full(bglu), full(snorm), full(wo_h), full(wo_s), full(nffn),
                  full(wr), full(br)],
        out_specs=[pl.BlockSpec((1, rows, d), lambda i: (pt(i), 0, 0)),
                   pl.BlockSpec((1, rows, d), lambda i: (st(i), 0, 0)),
                   pl.BlockSpec((rows * TOKEN_TILE_ROWS, LANES), lambda i: (i, 0)),
                   pl.BlockSpec((rows,), lambda i: (i,)),
                   pl.BlockSpec((rows,), lambda i: (i,)),
                   pl.BlockSpec((rows, ROUTER_LANES), lambda i: (i, 0)),
                   pl.BlockSpec((1, ROUTER_LANES), lambda i: (0, 0))],
        out_shape=[jax.ShapeDtypeStruct(xp3.shape, F32),
                   jax.ShapeDtypeStruct(xs3.shape, F32),
                   jax.ShapeDtypeStruct((n * TOKEN_TILE_ROWS, LANES), F32),
                   jax.ShapeDtypeStruct((n,), jnp.int32),
                   jax.ShapeDtypeStruct((n,), jnp.int32),
                   jax.ShapeDtypeStruct((n, ROUTER_LANES), F32),
                   jax.ShapeDtypeStruct((1, ROUTER_LANES), F32)],
        compiler_params=_cparams(("arbitrary",)),
        name="outproj",
    )(xp3, xs3, oh_p, oh_s, ys_p, ys_s, ada_p, ada_p, ada_p, ada_s3, ada_s3, ada_s3,
      wglu_bf, bglu, snorm, wo_h, wo_s, nffn, wr, br)
    x1_p, x1_s = outs[0].reshape(x_p.shape), outs[1].reshape(x_s.shape)
    return (x1_p, x1_s) + tuple(outs[2:])


def _moe_schedule(cnt, n_slots):
    c = cnt[0, EXPERT_LANE0:EXPERT_LANE0 + N_EXPERTS].astype(jnp.int32)
    seg_end = jnp.cumsum(c)
    seg_start = seg_end - c
    first_tile = seg_start // MOE_TILE
    tiles = jnp.where(c > 0, (seg_end - 1) // MOE_TILE - first_tile + 1, 0)
    cum = jnp.cumsum(tiles)
    n_items = cum[-1]
    item = jnp.minimum(jnp.arange(n_slots, dtype=jnp.int32), n_items - 1)
    item_exp = jnp.sum(item[:, None] >= cum[None, :], axis=1).astype(jnp.int32)
    shares = ((c > 0) & (seg_start % MOE_TILE != 0)).astype(jnp.int32)
    item_tile = item - jnp.sum((item[:, None] >= (cum - tiles)[None, :]) * shares[None, :], axis=1)
    return (seg_start.astype(jnp.int32), seg_end.astype(jnp.int32), item_tile.astype(jnp.int32), item_exp,
            n_items.reshape(1).astype(jnp.int32))


def _token_tile(ref, t):
    return ref.at[pl.ds(pl.multiple_of(t * TOKEN_TILE_ROWS, TOKEN_TILE_ROWS), TOKEN_TILE_ROWS)]


def _store_token_tiles(ref, x):
    rows = x.shape[0]
    for c in range(TOKEN_TILE_ROWS):
        ref[pl.ds(c, rows, stride=TOKEN_TILE_ROWS), :] = x[:, c * LANES:(c + 1) * LANES]


def _load_token_tiles(ref, rows):
    return [ref[pl.ds(c, rows, stride=TOKEN_TILE_ROWS), :] for c in range(TOKEN_TILE_ROWS)]


def _positions_kernel(seg_ref, p1_ref, p2_ref, o1_ref, o2_ref):
    for p_ref, o_ref in ((p1_ref, o1_ref), (p2_ref, o2_ref)):
        pair = p_ref[...]
        expert = pair >> RANK_BITS
        pos = pair & ((1 << RANK_BITS) - 1)
        for e in range(N_EXPERTS):
            pos = pos + jnp.where(expert == e, seg_ref[e], 0)
        o_ref[...] = pos


def _positions(seg, pair1, pair2):
    n = pair1.shape[0]
    shape2 = (n // LANES, LANES)
    full = pl.BlockSpec(shape2, lambda: (0, 0))
    pos1, pos2 = pl.pallas_call(
        _positions_kernel,
        in_specs=[pl.BlockSpec(memory_space=pltpu.SMEM), full, full],
        out_specs=[full, full],
        out_shape=[jax.ShapeDtypeStruct(shape2, jnp.int32)] * 2,
        compiler_params=pltpu.CompilerParams(vmem_limit_bytes=VMEM_LIMIT),
        name="moe_positions",
    )(seg, pair1.reshape(shape2), pair2.reshape(shape2))
    return pos1.reshape(n), pos2.reshape(n)


def _inverse_kernel(p1_ref, p2_ref, inv_ref):
    ts = p1_ref.shape[0]
    base = pl.program_id(0) * ts

    def body(t, carry):
        inv_ref[p1_ref[t]] = base + t
        inv_ref[p2_ref[t]] = base + t
        return carry

    lax.fori_loop(0, ts, body, 0, unroll=2 * DMA_UNROLL)


def _inverse(pos1, pos2, ts):
    n = pos1.shape[0]
    idx_spec = pl.BlockSpec((ts,), lambda i: (i,), memory_space=pltpu.SMEM)
    return pl.pallas_call(
        _inverse_kernel,
        grid=(n // ts,),
        in_specs=[idx_spec, idx_spec],
        out_specs=pl.BlockSpec(memory_space=pltpu.SMEM),
        out_shape=jax.ShapeDtypeStruct((2 * n,), jnp.int32),
        compiler_params=_cparams(("arbitrary",)),
        name="moe_inverse",
    )(pos1, pos2)


def _row_gather(src_ref, idx_of, dst_ref, dst_row0, sem, n_rows):
    def copy(t):
        return pltpu.make_async_copy(_token_tile(src_ref, idx_of(t)), _token_tile(dst_ref, dst_row0 + t), sem)

    def start():
        def body(t, carry):
            copy(2 * t).start(priority=0)
            copy(2 * t + 1).start(priority=1)
            return carry
        lax.fori_loop(0, n_rows // 2, body, 0, unroll=DMA_UNROLL // 2)

    def wait():
        def body(t, carry):
            copy(t).wait()
            return carry
        lax.fori_loop(0, n_rows, body, 0, unroll=DMA_UNROLL)

    return start, wait


def _experts_kernel(lo_ref, hi_ref, tile_ref, exp_ref, items_ref, inv_ref, inv_next_ref, h_ref,
                    wg_ref, wu_ref, wd_ref, ys_ref, xbuf_ref, sems, wg_s, wu_s, wd_s, *, n_tiles):
    i = pl.program_id(0)
    prev = jnp.maximum(i - 1, 0)
    e = exp_ref[i]
    tile = tile_ref[i]
    slot = tile % 2
    active = i < items_ref[0]
    first_visit = (i == 0) | (tile != tile_ref[prev])

    @pl.when((i == 0) | (e != exp_ref[prev]))
    def _():
        wg_s[...] = wg_ref[0].astype(BF16)
        wu_s[...] = wu_ref[0].astype(BF16)
        wd_s[...] = wd_ref[0].astype(BF16)

    start_this, wait_this = _row_gather(h_ref, lambda t: inv_ref[t], xbuf_ref, slot * MOE_TILE,
                                        sems.at[slot], MOE_TILE)
    start_next, _ = _row_gather(h_ref, lambda t: inv_next_ref[t], xbuf_ref, (1 - slot) * MOE_TILE,
                                sems.at[1 - slot], MOE_TILE)

    @pl.when(active & first_visit)
    def _():
        @pl.when(i == 0)
        def _():
            start_this()

        @pl.when(tile + 1 < n_tiles)
        def _():
            start_next()

        wait_this()

    @pl.when(active)
    def _():
        base = pl.multiple_of(slot * (MOE_TILE * TOKEN_TILE_ROWS), MOE_TILE * TOKEN_TILE_ROWS)
        x = jnp.concatenate([xbuf_ref[pl.ds(base + c, MOE_TILE, stride=TOKEN_TILE_ROWS), :]
                             for c in range(TOKEN_TILE_ROWS)], axis=-1).astype(BF16)
        act = _silu(_dot(x, wg_s[...])) * _dot(x, wu_s[...])
        out = _dot(act.astype(BF16), wd_s[...])
        row = tile_ref[i] * MOE_TILE + lax.broadcasted_iota(jnp.int32, (MOE_TILE, 1), 0)
        mine = (row >= lo_ref[e]) & (row < hi_ref[e])
        first_visit = (i == 0) | (tile_ref[i] != tile_ref[prev])

        @pl.when(first_visit)
        def _():
            _store_token_tiles(ys_ref, jnp.where(mine, out, 0.0))

        @pl.when(jnp.logical_not(first_visit))
        def _():
            old = jnp.concatenate(_load_token_tiles(ys_ref, MOE_TILE), axis=-1)
            _store_token_tiles(ys_ref, jnp.where(mine, out, old))


def _experts(seg_lo, seg_hi, item_tile, item_exp, n_items, inv, h2, wg, wu, wd):
    n_tiles = inv.shape[0] // MOE_TILE
    _, d, de = wg.shape

    def w_spec(shape):
        return pl.BlockSpec((1,) + shape, lambda i, lo, hi, tile, ex, items: (ex[i], 0, 0))

    return pl.pallas_call(
        functools.partial(_experts_kernel, n_tiles=n_tiles),
        grid_spec=pltpu.PrefetchScalarGridSpec(
            num_scalar_prefetch=5, grid=(item_tile.shape[0],),
            in_specs=[pl.BlockSpec((MOE_TILE,), lambda i, lo, hi, tile, ex, items: (tile[i],),
                                   memory_space=pltpu.SMEM),
                      pl.BlockSpec((MOE_TILE,), lambda i, lo, hi, tile, ex, items: (
                          jnp.minimum(tile[i] + 1, n_tiles - 1),), memory_space=pltpu.SMEM),
                      pl.BlockSpec(memory_space=pltpu.HBM),
                      w_spec((d, de)), w_spec((d, de)), w_spec((de, d))],
            out_specs=pl.BlockSpec((MOE_TILE * TOKEN_TILE_ROWS, LANES),
                                   lambda i, lo, hi, tile, ex, items: (tile[i], 0)),
            scratch_shapes=[pltpu.VMEM((2 * MOE_TILE * TOKEN_TILE_ROWS, LANES), F32),
                            pltpu.SemaphoreType.DMA((2,)),
                            pltpu.VMEM((d, de), BF16), pltpu.VMEM((d, de), BF16), pltpu.VMEM((de, d), BF16)]),
        out_shape=jax.ShapeDtypeStruct((inv.shape[0] * TOKEN_TILE_ROWS, LANES), F32),
        compiler_params=_cparams(("arbitrary",)),
        name="moe_experts",
    )(seg_lo, seg_hi, item_tile, item_exp, n_items, inv, inv, h2, wg, wu, wd)


def _combine_kernel(p1_ref, p2_ref, p1n_ref, p2n_ref, rw_ref, x1_ref, gate_ref, shift_ref, scale_ref,
                    nfin_ref, ys_ref, y_ref, r1_ref, r2_ref, sems, *, n_steps):
    _, rows, d = x1_ref.shape
    s = pl.program_id(0)
    slot = s % 2

    def gathers(pa_ref, pb_ref, slot):
        g1 = _row_gather(ys_ref, lambda t: pa_ref[t], r1_ref, slot * rows, sems.at[slot], rows)
        g2 = _row_gather(ys_ref, lambda t: pb_ref[t], r2_ref, slot * rows, sems.at[slot], rows)
        return g1, g2

    this = gathers(p1_ref, p2_ref, slot)
    nxt = gathers(p1n_ref, p2n_ref, 1 - slot)

    @pl.when(s == 0)
    def _():
        this[0][0]()
        this[1][0]()

    @pl.when(s + 1 < n_steps)
    def _():
        nxt[0][0]()
        nxt[1][0]()

    this[0][1]()
    this[1][1]()

    def per_token(ref):
        per_seq = ref[0]
        seqs = per_seq.shape[0]
        return jnp.broadcast_to(per_seq[:, None, :], (seqs, rows // seqs, d)).reshape(rows, d)

    rw = rw_ref[...]
    w1 = rw[:, 0:1]
    w2 = rw[:, 1:2]
    base = pl.multiple_of(slot * (rows * TOKEN_TILE_ROWS), rows * TOKEN_TILE_ROWS)
    moe = jnp.concatenate(
        [w1 * r1_ref[pl.ds(base + c, rows, stride=TOKEN_TILE_ROWS), :]
         + w2 * r2_ref[pl.ds(base + c, rows, stride=TOKEN_TILE_ROWS), :] for c in range(TOKEN_TILE_ROWS)], axis=-1)
    x2 = x1_ref[0] + per_token(gate_ref) * moe
    hf = _rms(x2) * nfin_ref[...]
    y_ref[0] = hf * (1.0 + per_token(scale_ref)) + per_token(shift_ref)


def _combine(pos1, pos2, rw, x1, ada3, adaf3, nfin, ys, rows, tile0):
    b, t, d = x1.shape
    n_steps = b * t // rows
    seqs = max(rows // t, 1)
    tiles_per_seq = max(t // rows, 1)
    x3 = x1.reshape(n_steps, rows, d)
    ada_v = ada3.reshape(b // seqs, seqs, -1)
    adaf_v = adaf3.reshape(b // seqs, seqs, -1)
    x_spec = pl.BlockSpec((1, rows, d), lambda i: (i, 0, 0))

    def idx_spec(step):
        return pl.BlockSpec((rows,), lambda i: (tile0 + step(i),), memory_space=pltpu.SMEM)

    def ada_spec(col):
        return pl.BlockSpec((1, seqs, d), lambda i: (i // tiles_per_seq, 0, col))

    def cur(i):
        return i

    def nxt(i):
        return jnp.minimum(i + 1, n_steps - 1)

    return pl.pallas_call(
        functools.partial(_combine_kernel, n_steps=n_steps),
        grid=(n_steps,),
        in_specs=[idx_spec(cur), idx_spec(cur), idx_spec(nxt), idx_spec(nxt),
                  pl.BlockSpec((rows, ROUTER_LANES), lambda i: (tile0 + i, 0)),
                  x_spec, ada_spec(5), ada_spec(0), ada_spec(1),
                  pl.BlockSpec((1, d), lambda i: (0, 0)),
                  pl.BlockSpec(memory_space=pltpu.HBM)],
        out_specs=x_spec,
        scratch_shapes=[pltpu.VMEM((2 * rows * TOKEN_TILE_ROWS, LANES), F32),
                        pltpu.VMEM((2 * rows * TOKEN_TILE_ROWS, LANES), F32),
                        pltpu.SemaphoreType.DMA((2,))],
        out_shape=jax.ShapeDtypeStruct(x3.shape, F32),
        compiler_params=_cparams(("arbitrary",)),
        name="moe_combine",
    )(pos1, pos2, pos1, pos2, rw, x3, ada_v, adaf_v, adaf_v, nfin, ys).reshape(x1.shape)


def _mixer(x, ada3, s_h, s_re, s_im, p, *, bb, tt, hgrn_seqs, hgrn_tokens, sequential):
    b, t, d = x.shape
    n = b * t
    q, k, g, v, gs, u = _inproj(x, ada3, p['norm_mix'], p['w_in'], p['lb'], bb, tt)
    oh, s_h_new = _hgrn(q, k, g, v, gs, s_h, p['hgrn_norm'], b, t, hgrn_seqs, hgrn_tokens)
    if sequential:
        xr, xi = s_re.reshape(b, 1, -1), s_im.reshape(b, 1, -1)
        n_tokens = t
    else:
        xr, xi = s_re.reshape(1, b, -1), s_im.reshape(1, b, -1)
        n_tokens = n
    ys, fr, fi = _s5(u, xr, xi, p['ssm_bd'], p['ssm_inc'], p['ssm_m'], p['ssm_a8'], n_tokens, sequential)
    states = (s_h_new[None], fr.reshape(1, b, s_re.shape[-2], s_re.shape[-1]),
              fi.reshape(1, b, s_re.shape[-2], s_re.shape[-1]))
    return oh, ys, states


def kernel(x_prompt, x_sample, c_prompt, c_sample, state_hgrn, state_ssm_re, state_ssm_im, hgrn_lb_logits, w_ada, b_ada, norm_mix, w_in, hgrn_norm, ssm_a_re, ssm_a_im, ssm_log_dt, ssm_b_re, ssm_b_im, ssm_c_re, ssm_c_im, ssm_d, ssm_w_glu, ssm_b_glu, ssm_norm, w_out, norm_ffn, moe_w_group, moe_b_group, moe_w_router, moe_b_router, moe_w_gate, moe_w_up, moe_w_down, w_ada_final, b_ada_final, norm_final):
    depth = w_ada.shape[0]
    assert depth == 1
    d = x_prompt.shape[-1]
    bp = x_prompt.shape[0]
    dh = hgrn_norm.shape[-1]
    dk = dh // HGRN_HEADS
    de = moe_w_gate.shape[-1]
    n_exp = MOE_GROUPS * MOE_PER_GROUP

    lb = jax.nn.softmax(hgrn_lb_logits.astype(F32), axis=0)[0].reshape(1, dh)
    bd, inc, m, a8 = _s5_prepare(ssm_a_re[0], ssm_a_im[0], ssm_log_dt[0], ssm_b_re[0], ssm_b_im[0],
                                 ssm_c_re[0], ssm_c_im[0], ssm_d[0])

    def router_lanes(group_part, expert_part):
        rows = group_part.shape[0]
        return jnp.concatenate(
            [group_part, jnp.zeros((rows, EXPERT_LANE0 - MOE_GROUPS), F32), expert_part,
             jnp.zeros((rows, ROUTER_LANES - EXPERT_LANE0 - n_exp), F32)], axis=1)

    w_rt = router_lanes(moe_w_group[0], moe_w_router[0].transpose(1, 0, 2).reshape(d, n_exp))
    b_rt = router_lanes(moe_b_group[0].reshape(1, MOE_GROUPS), moe_b_router[0].reshape(1, n_exp))
    p = dict(
        lb=lb, norm_mix=norm_mix[0].reshape(1, d), w_in=w_in[0].astype(BF16),
        hgrn_norm=hgrn_norm[0].reshape(1, dh),
        ssm_bd=bd, ssm_inc=inc, ssm_m=m, ssm_a8=a8,
        w_glu=ssm_w_glu[0].astype(BF16), b_glu=ssm_b_glu[0].reshape(1, -1), ssm_norm=ssm_norm[0].reshape(1, -1),
        wo_h=w_out[0, :dh].astype(BF16), wo_s=w_out[0, dh:].astype(BF16),
        norm_ffn=norm_ffn[0].reshape(1, d), w_rt=w_rt.astype(BF16), b_rt=b_rt,
    )
    wg = moe_w_gate[0].reshape(n_exp, d, de)
    wu = moe_w_up[0].reshape(n_exp, d, de)
    wd = moe_w_down[0].reshape(n_exp, de, d)
    nfin = norm_final.reshape(1, d)

    c_all = jnp.concatenate([c_prompt, c_sample], axis=0)
    ada = _silu_linear(c_all, w_ada[0], b_ada[0])
    adaf = _silu_linear(c_all, w_ada_final, b_ada_final)
    ada_p, ada_s = ada[:bp].reshape(bp, 1, -1), ada[bp:].reshape(x_sample.shape[0], 1, -1)
    adaf_p, adaf_s = adaf[:bp].reshape(bp, 1, -1), adaf[bp:].reshape(x_sample.shape[0], 1, -1)

    zeros_h = jnp.zeros((bp, HGRN_HEADS, dk, dk), F32)
    zeros_s = jnp.zeros((bp,) + state_ssm_re.shape[2:], F32)
    oh_p, ys_p, st_p = _mixer(x_prompt, ada_p, zeros_h, zeros_s, zeros_s, p,
                              bb=1, tt=256, hgrn_seqs=1, hgrn_tokens=512, sequential=True)
    oh_s, ys_s, st_s = _mixer(x_sample, ada_s, state_hgrn[0], state_ssm_re[0], state_ssm_im[0], p,
                              bb=32, tt=8, hgrn_seqs=16, hgrn_tokens=8, sequential=False)
    x1_p, x1_s, h2, pair1, pair2, rw, cnt = _outproj(
        x_prompt, x_sample, oh_p, oh_s, ys_p, ys_s, ada_p, ada_s, p['w_glu'], p['b_glu'], p['ssm_norm'],
        p['wo_h'], p['wo_s'], p['norm_ffn'], p['w_rt'], p['b_rt'], TOKEN_ROWS)

    n_pairs = 2 * pair1.shape[0]
    n_slots = n_pairs // MOE_TILE + N_EXPERTS
    seg, seg_end, item_tile, item_exp, n_items = _moe_schedule(cnt, n_slots)
    pos1, pos2 = _positions(seg, pair1, pair2)
    inv = _inverse(pos1, pos2, 1024)
    ys = _experts(seg, seg_end, item_tile, item_exp, n_items, inv, h2, wg, wu, wd)
    tiles_p = x_prompt.shape[0] * x_prompt.shape[1] // TOKEN_ROWS
    y_p = _combine(pos1, pos2, rw, x1_p, ada_p, adaf_p, nfin, ys, TOKEN_ROWS, 0)
    y_s = _combine(pos1, pos2, rw, x1_s, ada_s, adaf_s, nfin, ys, TOKEN_ROWS, tiles_p)
    return (y_p, y_s) + st_p + st_s
```

```python
import functools
import math

import jax
import jax.numpy as jnp
from jax import lax
from jax.experimental import pallas as pl
from jax.experimental.pallas import tpu as pltpu

F32 = jnp.float32
BF16 = jnp.bfloat16
HIGHEST = lax.Precision.HIGHEST

EPS = 1e-6
MAX_REAL = -1e-4
HGRN_HEADS = 4
HGRN_CHUNK = 128
HGRN_SAFE_EXPONENT = 80.0
HGRN_EXACT_BLOCK = 8
SSM_GROUP = 16
SSM_STATE = 64
SSM_CHUNK = 8
SSM_SETS = 4
MOE_GROUPS = 4
MOE_PER_GROUP = 8
N_EXPERTS = MOE_GROUPS * MOE_PER_GROUP
ROUTER_LANES = 128
EXPERT_LANE0 = 32
RANK_BITS = 20
MOE_TILE = 256
LANES = 128
TOKEN_TILE_ROWS = 8
DMA_UNROLL = 8
TOKEN_ROWS = 512
VMEM_LIMIT = 56 * 1024 * 1024


def _cparams(sem):
    return pltpu.CompilerParams(dimension_semantics=sem, vmem_limit_bytes=VMEM_LIMIT)


def _silu(x):
    return x * jax.nn.sigmoid(x)


def _rms(x):
    return x * lax.rsqrt(jnp.mean(x * x, axis=-1, keepdims=True) + EPS)


def _dot(a, b):
    return jnp.dot(a, b, preferred_element_type=F32)


def _dot_nt(a, b):
    return lax.dot_general(a, b, (((1,), (1,)), ((), ())), preferred_element_type=F32)


def _dot_tn(a, b, precision=None):
    return lax.dot_general(a, b, (((0,), (0,)), ((), ())), preferred_element_type=F32,
                           precision=precision)


def _silu_linear_kernel(c_ref, w_ref, b_ref, o_ref):
    a = _silu(c_ref[...]).astype(BF16)
    o_ref[...] = _dot(a, w_ref[...].astype(BF16)) + b_ref[...]


def _silu_linear(c, w, b):
    m, d = c.shape
    n = w.shape[1]
    tn = 1024
    return pl.pallas_call(
        _silu_linear_kernel,
        grid=(n // tn,),
        in_specs=[pl.BlockSpec((m, d), lambda j: (0, 0)),
                  pl.BlockSpec((d, tn), lambda j: (0, j)),
                  pl.BlockSpec((1, tn), lambda j: (0, j))],
        out_specs=pl.BlockSpec((m, tn), lambda j: (0, j)),
        out_shape=jax.ShapeDtypeStruct((m, n), F32),
        compiler_params=_cparams(("parallel",)),
        name="silu_linear",
    )(c, w, b.reshape(1, n))


def _inproj_kernel(x_ref, shift_ref, scale_ref, gain_ref, w_ref, lb_ref,
                   q_ref, k_ref, g_ref, v_ref, gs_ref, u_ref, *, dh):
    bb, tt, d = x_ref.shape
    h = _rms(x_ref[...]) * gain_ref[...]
    h = h * (1.0 + scale_ref[...]) + shift_ref[...]
    proj = _dot(h.reshape(bb * tt, d).astype(BF16), w_ref[...])
    lb = lb_ref[...]
    f = lb + (1.0 - lb) * jax.nn.sigmoid(proj[:, dh:2 * dh])
    q_ref[...] = _silu(proj[:, :dh]) * (float(dh // HGRN_HEADS) ** -0.5)
    k_ref[...] = 1.0 - f
    g_ref[...] = jnp.log(f)
    v_ref[...] = proj[:, 2 * dh:3 * dh]
    gs_ref[...] = _silu(proj[:, 3 * dh:4 * dh])
    for s in range(SSM_SETS):
        u_ref[s] = proj[:, 4 * dh + s * LANES:4 * dh + (s + 1) * LANES]


def _inproj(x, ada3, gain, w_in_bf, lb, bb, tt):
    b, t, d = x.shape
    dh = lb.shape[-1]
    nt = t // tt
    rows = bb * tt
    n = b * t
    row_spec = pl.BlockSpec((rows, dh), lambda i, j: (i * nt + j, 0))
    out = jax.ShapeDtypeStruct((n, dh), F32)
    return pl.pallas_call(
        functools.partial(_inproj_kernel, dh=dh),
        grid=(b // bb, nt),
        in_specs=[pl.BlockSpec((bb, tt, d), lambda i, j: (i, j, 0)),
                  pl.BlockSpec((bb, 1, d), lambda i, j: (i, 0, 0)),
                  pl.BlockSpec((bb, 1, d), lambda i, j: (i, 0, 1)),
                  pl.BlockSpec((1, d), lambda i, j: (0, 0)),
                  pl.BlockSpec(w_in_bf.shape, lambda i, j: (0, 0)),
                  pl.BlockSpec((1, dh), lambda i, j: (0, 0))],
        out_specs=[row_spec] * 5 + [pl.BlockSpec((SSM_SETS, rows, LANES), lambda i, j: (0, i * nt + j, 0))],
        out_shape=[out] * 5 + [jax.ShapeDtypeStruct((SSM_SETS, n, LANES), F32)],
        compiler_params=_cparams(("parallel", "parallel")),
        name="inproj",
    )(x, ada3, ada3, gain, w_in_bf, lb)


def _split3(x):
    hi = x.astype(BF16)
    r1 = x - hi.astype(F32)
    mid = r1.astype(BF16)
    lo = (r1 - mid.astype(F32)).astype(BF16)
    return hi, mid, lo


def _hgrn_kernel(q_ref, k_ref, g_ref, v_ref, gs_ref, s0_ref, hn_ref, o_ref, sf_ref,
                 st_ref, intra_ref, qh_ref, kh_ref, ea_ref, sums_ref, *, tl, nt):
    j = pl.program_id(1)
    rows_total, dh = q_ref.shape
    dk = dh // HGRN_HEADS
    c = HGRN_CHUNK
    seqs = c // tl
    n_chunks = rows_total // c

    @pl.when(j == 0)
    def _():
        st_ref[...] = s0_ref[...]

    r = lax.broadcasted_iota(jnp.int32, (c, c), 0)
    s = lax.broadcasted_iota(jnp.int32, (c, c), 1)
    same_seq = (r // tl) == (s // tl)
    causal = same_seq & (r >= s)
    upto_mid = same_seq & ((s % tl) < tl // 2)
    one_seq = tl == c
    sums = (causal if one_seq else jnp.concatenate([causal, upto_mid, same_seq], axis=0)).astype(BF16)
    ref_rows = 1 if one_seq else c
    eye3 = (lax.broadcasted_iota(jnp.int32, (dk, 3 * dk), 1) % dk
            == lax.broadcasted_iota(jnp.int32, (dk, 3 * dk), 0)).astype(BF16)

    def decay_matrix(e_row):
        parts = jnp.concatenate(_split3(e_row), axis=-1)
        return _dot_nt(eye3, jnp.broadcast_to(parts, (dk, 3 * dk)))

    worst = jnp.float32(0.0)
    for ci in range(n_chunks):
        rows = slice(ci * c, (ci + 1) * c)
        g_parts = _split3(g_ref[rows, :])
        acc = _dot(sums, g_parts[0]) + _dot(sums, g_parts[1]) + _dot(sums, g_parts[2])
        if one_seq:
            a_mid, a_end = acc[c // 2 - 1:c // 2], acc[c - 1:c]
            sums_ref[ci * 3 * c:ci * 3 * c + c, :] = acc
            sums_ref[ci * 3 * c + c:ci * 3 * c + c + 1, :] = a_mid
            sums_ref[ci * 3 * c + 2 * c:ci * 3 * c + 2 * c + 1, :] = a_end
        else:
            a_mid, a_end = acc[c:2 * c], acc[2 * c:]
            sums_ref[ci * 3 * c:(ci + 1) * 3 * c, :] = acc
        worst = jnp.maximum(worst, jnp.max(jnp.maximum(jnp.abs(a_mid), jnp.abs(a_end - a_mid))))
    factorised_is_safe = worst < HGRN_SAFE_EXPONENT

    @pl.when(factorised_is_safe)
    def _():
        for ci in range(n_chunks):
            rows = slice(ci * c, (ci + 1) * c)
            a = sums_ref[ci * 3 * c:ci * 3 * c + c, :]
            a_mid = sums_ref[ci * 3 * c + c:ci * 3 * c + c + ref_rows, :]
            a_end = sums_ref[ci * 3 * c + 2 * c:ci * 3 * c + 2 * c + ref_rows, :]
            e_mid = jnp.exp(a_mid)
            e_tail = jnp.exp(a_end - a_mid)
            qt = q_ref[rows, :] * jnp.exp(a - a_mid)
            kt = k_ref[rows, :] * jnp.exp(a_mid - a)
            qh_ref[rows, :] = qt * e_mid
            kh_ref[rows, :] = kt * e_tail
            ea_ref[rows.start:rows.start + ref_rows, :] = e_mid * e_tail
            qt = qt.astype(BF16)
            kt = kt.astype(BF16)
            v = v_ref[rows, :].astype(BF16)
            for h in range(HGRN_HEADS):
                lanes = slice(h * dk, (h + 1) * dk)
                sc = jnp.where(causal, _dot_nt(qt[:, lanes], kt[:, lanes]), 0.0).astype(BF16)
                intra_ref[rows, lanes] = _dot(sc, v[:, lanes])

            for si in range(seqs):
                seq = ci * seqs + si if tl < c else 0
                srows = slice(si * tl, (si + 1) * tl)
                orows = slice(ci * c + si * tl, ci * c + (si + 1) * tl)
                for h in range(HGRN_HEADS):
                    lanes = slice(h * dk, (h + 1) * dk)
                    state = st_ref[seq, h]
                    o = intra_ref[orows, lanes] + _dot(qh_ref[orows, lanes].astype(BF16), state.astype(BF16))
                    decay = decay_matrix(ea_ref[orows.start:orows.start + 1, lanes])
                    st_ref[seq, h] = decay * state + _dot_tn(kh_ref[orows, lanes].astype(BF16),
                                                             v[srows, lanes])
                    o_ref[orows, lanes] = _rms(o) * hn_ref[:, lanes] * gs_ref[orows, lanes]

    @pl.when(jnp.logical_not(factorised_is_safe))
    def _():
        blk = HGRN_EXACT_BLOCK
        blocks_per_seq = max(tl // blk, 1)
        tri = (lax.broadcasted_iota(jnp.int32, (blk, blk), 0)
               >= lax.broadcasted_iota(jnp.int32, (blk, blk), 1)).astype(BF16)
        sub = lax.broadcasted_iota(jnp.int32, (blk, dk), 0)

        def block(bi, carry):
            rows = pl.ds(pl.multiple_of(bi * blk, blk), blk)
            seq = bi // blocks_per_seq if tl < c else 0
            g_parts = _split3(g_ref[rows, :])
            a = _dot(tri, g_parts[0]) + _dot(tri, g_parts[1]) + _dot(tri, g_parts[2])
            a_end = a[blk - 1:blk]
            q = q_ref[rows, :]
            k = k_ref[rows, :]
            v = v_ref[rows, :]
            qh = (q * jnp.exp(a)).astype(BF16)
            kh = (k * jnp.exp(a_end - a)).astype(BF16)
            ea = jnp.exp(a_end)
            vb = v.astype(BF16)
            for h in range(HGRN_HEADS):
                lanes = slice(h * dk, (h + 1) * dk)
                state = st_ref[seq, h]
                intra = []
                for t in range(blk):
                    live = sub <= t
                    decay_t = jnp.where(live, jnp.exp(jnp.where(live, a[t:t + 1, lanes] - a[:, lanes], 0.0)), 0.0)
                    score = jnp.sum(q[t:t + 1, lanes] * k[:, lanes] * decay_t, axis=-1, keepdims=True)
                    intra.append(jnp.sum(score * v[:, lanes], axis=0, keepdims=True))
                o = jnp.concatenate(intra, axis=0) + _dot(qh[:, lanes], state.astype(BF16))
                st_ref[seq, h] = decay_matrix(ea[:, lanes]) * state + _dot_tn(kh[:, lanes], vb[:, lanes])
                o_ref[rows, lanes] = _rms(o) * hn_ref[:, lanes] * gs_ref[rows, lanes]
            return carry

        lax.fori_loop(0, rows_total // blk, block, 0)

    @pl.when(j == nt - 1)
    def _():
        sf_ref[...] = st_ref[...]


def _hgrn(q, k, g, v, gs, s0, hnorm, b, t, nseq, tt):
    n, dh = q.shape
    nt = t // tt
    dk = dh // HGRN_HEADS
    rows = nseq * tt
    tl = min(tt, HGRN_CHUNK)
    row_spec = pl.BlockSpec((rows, dh), lambda i, j: (i * nt + j, 0))
    st_spec = pl.BlockSpec((nseq, HGRN_HEADS, dk, dk), lambda i, j: (i, 0, 0, 0))
    return pl.pallas_call(
        functools.partial(_hgrn_kernel, tl=tl, nt=nt),
        grid=(b // nseq, nt),
        in_specs=[row_spec] * 5 + [st_spec, pl.BlockSpec((1, dh), lambda i, j: (0, 0))],
        out_specs=[row_spec, st_spec],
        out_shape=[jax.ShapeDtypeStruct((n, dh), F32),
                   jax.ShapeDtypeStruct((b, HGRN_HEADS, dk, dk), F32)],
        scratch_shapes=[pltpu.VMEM((nseq, HGRN_HEADS, dk, dk), F32),
                        pltpu.VMEM((rows, dh), F32), pltpu.VMEM((rows, dh), F32),
                        pltpu.VMEM((rows, dh), F32), pltpu.VMEM((rows, dh), F32),
                        pltpu.VMEM((3 * rows, dh), F32)],
        compiler_params=_cparams(("parallel", "arbitrary")),
        name="hgrn",
    )(q, k, g, v, gs, s0, hnorm)


def _s5_prepare(a_re, a_im, log_dt, b_re, b_im, c_re, c_im, d_skip):
    ng, npp = a_re.shape
    nh = b_re.shape[-1]
    L = SSM_CHUNK
    gs = ng // SSM_SETS
    lam_re = jnp.minimum(a_re, MAX_REAL)
    lam_im = a_im
    dt = jnp.exp(log_dt)
    mag = jnp.exp(lam_re * dt)
    ab_re = mag * jnp.cos(lam_im * dt)
    ab_im = mag * jnp.sin(lam_im * dt)
    den = lam_re * lam_re + lam_im * lam_im
    co_re = ((ab_re - 1.0) * lam_re + ab_im * lam_im) / den
    co_im = (ab_im * lam_re - (ab_re - 1.0) * lam_im) / den
    bb_re = co_re[..., None] * b_re - co_im[..., None] * b_im
    bb_im = co_re[..., None] * b_im + co_im[..., None] * b_re
    tau = jnp.arange(L + 1, dtype=F32)[:, None, None]
    pw_mag = jnp.exp(tau * (lam_re * dt))
    pw_re = pw_mag * jnp.cos(tau * (lam_im * dt))
    pw_im = pw_mag * jnp.sin(tau * (lam_im * dt))
    ab_b_re = pw_re[:L, :, :, None] * bb_re - pw_im[:L, :, :, None] * bb_im
    ab_b_im = pw_re[:L, :, :, None] * bb_im + pw_im[:L, :, :, None] * bb_re
    kern = (jnp.einsum('gkp,lgph->lghk', c_re, ab_b_re, precision=HIGHEST)
            - jnp.einsum('gkp,lgph->lghk', c_im, ab_b_im, precision=HIGHEST))
    skip = d_skip[None, :, :, None] * jnp.eye(nh, dtype=F32)
    kern = kern + jnp.where(jnp.arange(L)[:, None, None, None] == 0, skip, 0.0)

    def group_block_diag(c):
        rows, w = c.shape[-2:]
        repeat = (jnp.arange(gs * w)[None, :] % w == jnp.arange(w)[:, None]).astype(F32)
        tiled = jnp.dot(c, repeat, precision=HIGHEST)
        rg = jnp.arange(rows)[:, None] // (rows // gs)
        cq = jnp.arange(gs * w)[None, :] // w
        return jnp.where(rg == cq, tiled, 0.0).astype(BF16).transpose(1, 0, 2, 3).reshape(
            SSM_SETS, L * rows, gs * w)

    bd = group_block_diag(kern.reshape(L, SSM_SETS, gs * nh, nh)).reshape(SSM_SETS, L, gs * nh, gs * nh)
    n_re = group_block_diag(ab_b_re[::-1].transpose(0, 1, 3, 2).reshape(L, SSM_SETS, gs * nh, npp))
    n_im = group_block_diag(ab_b_im[::-1].transpose(0, 1, 3, 2).reshape(L, SSM_SETS, gs * nh, npp))
    inc = jnp.concatenate([n_re, n_im], axis=-1)
    ca_re = c_re[None] * pw_re[1:, :, None, :] - c_im[None] * pw_im[1:, :, None, :]
    ca_im = c_re[None] * pw_im[1:, :, None, :] + c_im[None] * pw_re[1:, :, None, :]
    m = jnp.concatenate([group_block_diag(ca_re.reshape(L, SSM_SETS, gs * nh, npp)),
                         group_block_diag(-ca_im.reshape(L, SSM_SETS, gs * nh, npp))], axis=-1)
    a8 = jnp.concatenate([pw_re[L].reshape(SSM_SETS, 1, gs * npp),
                          pw_im[L].reshape(SSM_SETS, 1, gs * npp)], axis=-1)
    return bd, inc, m, a8


def _s5_kernel(u_ref, xr_ref, xi_ref, bd_ref, inc_ref, m_ref, a8_ref, y_ref, fr_ref, fi_ref, wt_ref, *, sequential):
    n = u_ref.shape[1] // SSM_CHUNK
    ns = xr_ref.shape[-1]
    u = jnp.concatenate([u_ref[0, pl.ds(s, n, stride=SSM_CHUNK), :] for s in range(SSM_CHUNK)], axis=-1)

    @pl.when(pl.program_id(1) == 0)
    def _():
        wt_ref[...] = jnp.zeros_like(wt_ref)
        for s in range(SSM_CHUNK):
            for t in range(s, SSM_CHUNK):
                wt_ref[s * LANES:(s + 1) * LANES, t * LANES:(t + 1) * LANES] = bd_ref[0, t - s]

    u = u.astype(BF16)
    y_local = _dot(u, wt_ref[...])
    inc = _dot(u, inc_ref[0])
    d_re = inc[:, :ns]
    d_im = inc[:, ns:]
    a_re = a8_ref[0][:, :ns]
    a_im = a8_ref[0][:, ns:]
    x0_re = xr_ref[0]
    x0_im = xi_ref[0]
    if sequential:
        row = lax.broadcasted_iota(jnp.int32, (n, ns), 0)
        first = row == 0
        x_re = d_re + jnp.where(first, a_re * x0_re - a_im * x0_im, 0.0)
        x_im = d_im + jnp.where(first, a_re * x0_im + a_im * x0_re, 0.0)
        p_re, p_im = a_re, a_im
        step = 1
        while step < n:
            s_re = jnp.where(row >= step, pltpu.roll(x_re, step, 0), 0.0)
            s_im = jnp.where(row >= step, pltpu.roll(x_im, step, 0), 0.0)
            x_re, x_im = x_re + p_re * s_re - p_im * s_im, x_im + p_re * s_im + p_im * s_re
            p_re, p_im = p_re * p_re - p_im * p_im, 2.0 * p_re * p_im
            step *= 2
        fr_ref[0] = x_re[n - 1:n]
        fi_ref[0] = x_im[n - 1:n]
        xc_re = jnp.where(first, x0_re, pltpu.roll(x_re, 1, 0))
        xc_im = jnp.where(first, x0_im, pltpu.roll(x_im, 1, 0))
    else:
        xc_re, xc_im = x0_re, x0_im
        fr_ref[0] = a_re * x0_re - a_im * x0_im + d_re
        fi_ref[0] = a_re * x0_im + a_im * x0_re + d_im
    xc = jnp.concatenate([xc_re, xc_im], axis=-1).astype(BF16)
    y = y_local + _dot_nt(xc, m_ref[0])
    for t in range(SSM_CHUNK):
        y_ref[0, pl.ds(t, n, stride=SSM_CHUNK), :] = y[:, t * LANES:(t + 1) * LANES]


def _s5(u, x_re, x_im, bd, inc, m, a8, n_tokens, sequential):
    sets = u.shape[0]
    nb, rb, _ = x_re.shape
    ns = m.shape[1] // 2
    st_spec = pl.BlockSpec((1, rb, ns), lambda gi, i: (i, 0, gi))
    st_shape = jax.ShapeDtypeStruct(x_re.shape, F32)
    tok_spec = pl.BlockSpec((1, n_tokens, LANES), lambda gi, i: (gi, i, 0))
    return pl.pallas_call(
        functools.partial(_s5_kernel, sequential=sequential),
        grid=(sets, nb),
        in_specs=[tok_spec, st_spec, st_spec,
                  pl.BlockSpec((1,) + bd.shape[1:], lambda gi, i: (gi, 0, 0, 0)),
                  pl.BlockSpec((1,) + inc.shape[1:], lambda gi, i: (gi, 0, 0)),
                  pl.BlockSpec((1,) + m.shape[1:], lambda gi, i: (gi, 0, 0)),
                  pl.BlockSpec((1, 1, 2 * ns), lambda gi, i: (gi, 0, 0))],
        out_specs=[tok_spec, st_spec, st_spec],
        out_shape=[jax.ShapeDtypeStruct(u.shape, F32), st_shape, st_shape],
        scratch_shapes=[pltpu.VMEM((inc.shape[1], SSM_CHUNK * LANES), BF16)],
        compiler_params=_cparams(("parallel", "arbitrary")),
        name="s5",
    )(u, x_re, x_im, bd, inc, m, a8)


def _gelu_tanh(x):
    return 0.5 * x * (1.0 + jnp.tanh(math.sqrt(2.0 / math.pi) * (x + 0.044715 * (x * x * x))))


def _outproj_kernel(xp_ref, xs_ref, ohp_ref, ohs_ref, ysp_ref, yss_ref,
                    gate_p_ref, shift_p_ref, scale_p_ref, gate_s_ref, shift_s_ref, scale_s_ref,
                    wglu_ref, bglu_ref, sn_ref, wo_h_ref, wo_s_ref, nf_ref, wr_ref, br_ref,
                    x1p_ref, x1s_ref, h2_ref, pair1_ref, pair2_ref, rw_ref, cnt_ref, *, prompt_tiles):
    is_prompt = pl.program_id(0) < prompt_tiles
    _, rows, d = xp_ref.shape

    def per_token(p_ref, s_ref):
        per_seq = s_ref[0]
        seqs = per_seq.shape[0]
        rep = jnp.broadcast_to(per_seq[:, None, :], (seqs, rows // seqs, d)).reshape(rows, d)
        return jnp.where(is_prompt, p_ref[0], rep)

    x = jnp.where(is_prompt, xp_ref[0], xs_ref[0])
    oh = jnp.where(is_prompt, ohp_ref[...], ohs_ref[...])
    ys = jnp.concatenate([jnp.where(is_prompt, ysp_ref[s], yss_ref[s]) for s in range(SSM_SETS)], axis=-1)
    y = _gelu_tanh(ys)
    y = y * jax.nn.sigmoid(_dot(y.astype(BF16), wglu_ref[...]) + bglu_ref[...])
    o_s = _rms(y) * sn_ref[...]
    mix = _dot(oh.astype(BF16), wo_h_ref[...]) + _dot(o_s.astype(BF16), wo_s_ref[...])
    x1 = x + per_token(gate_p_ref, gate_s_ref) * mix

    @pl.when(is_prompt)
    def _():
        x1p_ref[0] = x1

    @pl.when(jnp.logical_not(is_prompt))
    def _():
        x1s_ref[0] = x1

    h2 = _rms(x1) * nf_ref[...]
    h2 = h2 * (1.0 + per_token(scale_p_ref, scale_s_ref)) + per_token(shift_p_ref, shift_s_ref)
    _store_token_tiles(h2_ref, h2)

    logits = _dot(h2.astype(BF16), wr_ref[...]) + br_ref[...]
    lane = lax.broadcasted_iota(jnp.int32, logits.shape, 1)
    neg = -jnp.inf
    gl = jnp.where(lane < MOE_GROUPS, logits, neg)
    gmax = jnp.max(gl, axis=-1, keepdims=True)
    gidx = jnp.min(jnp.where(gl == gmax, lane, ROUTER_LANES), axis=-1, keepdims=True)
    grp_w = 1.0 / jnp.sum(jnp.exp(gl - gmax), axis=-1, keepdims=True)
    e0 = EXPERT_LANE0 + gidx * MOE_PER_GROUP
    sel = jnp.where((lane >= e0) & (lane < e0 + MOE_PER_GROUP), logits, neg)
    m1 = jnp.max(sel, axis=-1, keepdims=True)
    i1 = jnp.min(jnp.where(sel == m1, lane, ROUTER_LANES), axis=-1, keepdims=True)
    sel2 = jnp.where(lane == i1, neg, sel)
    m2 = jnp.max(sel2, axis=-1, keepdims=True)
    i2 = jnp.min(jnp.where(sel2 == m2, lane, ROUTER_LANES), axis=-1, keepdims=True)
    e2 = jnp.exp(m2 - m1)
    w1 = 1.0 / (1.0 + e2)
    w2 = e2 / (1.0 + e2)
    rw_ref[...] = grp_w * (jnp.where(lane == 0, w1, 0.0) + jnp.where(lane == 1, w2, 0.0))

    @pl.when(pl.program_id(0) == 0)
    def _():
        cnt_ref[...] = jnp.zeros_like(cnt_ref)

    picked = (lane == i1) | (lane == i2)
    earlier = (lax.broadcasted_iota(jnp.int32, (rows, rows), 0)
               > lax.broadcasted_iota(jnp.int32, (rows, rows), 1))
    base = cnt_ref[...]
    before = _dot(earlier.astype(BF16), picked.astype(BF16)) + base
    rank1 = jnp.sum(jnp.where(lane == i1, before, 0.0), axis=-1, keepdims=True).astype(jnp.int32)
    rank2 = jnp.sum(jnp.where(lane == i2, before, 0.0), axis=-1, keepdims=True).astype(jnp.int32)
    cnt_ref[...] = base + jnp.sum(picked.astype(F32), axis=0, keepdims=True)
    info = jnp.where(lane == 0, ((i1 - EXPERT_LANE0) << RANK_BITS) | rank1,
                     jnp.where(lane == 1, ((i2 - EXPERT_LANE0) << RANK_BITS) | rank2, 0)).T
    pair1_ref[...] = info[0]
    pair2_ref[...] = info[1]


def _outproj(x_p, x_s, oh_p, oh_s, ys_p, ys_s, ada_p, ada_s, wglu_bf, bglu, snorm, wo_h, wo_s, nffn, wr, br, rows):
    d = x_p.shape[-1]
    dh = oh_p.shape[-1]
    n_p = x_p.shape[0] * x_p.shape[1]
    n_s = x_s.shape[0] * x_s.shape[1]
    n = n_p + n_s
    tiles_p = n_p // rows
    seqs = rows // x_s.shape[1]

    def pt(i):
        return jnp.minimum(i, tiles_p - 1)

    def st(i):
        return jnp.maximum(i - tiles_p, 0)

    tiles_per_seq = x_p.shape[1] // rows

    def ada_p_spec(col):
        return pl.BlockSpec((1, 1, d), lambda i: (pt(i) // tiles_per_seq, 0, col))

    def ada_s_spec(col):
        return pl.BlockSpec((1, seqs, d), lambda i: (st(i), 0, col))

    def full(a):
        return pl.BlockSpec(a.shape, lambda i: (0,) * a.ndim)

    xp3 = x_p.reshape(tiles_p, rows, d)
    xs3 = x_s.reshape(n_s // rows, rows, d)
    ada_s3 = ada_s.reshape(n_s // rows, seqs, -1)
    outs = pl.pallas_call(
        functools.partial(_outproj_kernel, prompt_tiles=tiles_p),
        grid=(n // rows,),
        in_specs=[pl.BlockSpec((1, rows, d), lambda i: (pt(i), 0, 0)),
                  pl.BlockSpec((1, rows, d), lambda i: (st(i), 0, 0)),
                  pl.BlockSpec((rows, dh), lambda i: (pt(i), 0)),
                  pl.BlockSpec((rows, dh), lambda i: (st(i), 0)),
                  pl.BlockSpec((SSM_SETS, rows, LANES), lambda i: (0, pt(i), 0)),
                  pl.BlockSpec((SSM_SETS, rows, LANES), lambda i: (0, st(i), 0)),
                  ada_p_spec(2), ada_p_spec(3), ada_p_spec(4), ada_s_spec(2), ada_s_spec(3), ada_s_spec(4),
                  full(wglu_bf), full(bglu), full(snorm), full(wo_h), full(wo_s), full(nffn),
                  full(wr), full(br)],
        out_specs=[pl.BlockSpec((1, rows, d), lambda i: (pt(i), 0, 0)),
                   pl.BlockSpec((1, rows, d), lambda i: (st(i), 0, 0)),
                   pl.BlockSpec((rows * TOKEN_TILE_ROWS, LANES), lambda i: (i, 0)),
                   pl.BlockSpec((rows,), lambda i: (i,)),
                   pl.BlockSpec((rows,), lambda i: (i,)),
                   pl.BlockSpec((rows, ROUTER_LANES), lambda i: (i, 0)),
                   pl.BlockSpec((1, ROUTER_LANES), lambda i: (0, 0))],
        out_shape=[jax.ShapeDtypeStruct(xp3.shape, F32),
                   jax.ShapeDtypeStruct(xs3.shape, F32),
                   jax.ShapeDtypeStruct((n * TOKEN_TILE_ROWS, LANES), F32),
                   jax.ShapeDtypeStruct((n,), jnp.int32),
                   jax.ShapeDtypeStruct((n,), jnp.int32),
                   jax.ShapeDtypeStruct((n, ROUTER_LANES), F32),
                   jax.ShapeDtypeStruct((1, ROUTER_LANES), F32)],
        compiler_params=_cparams(("arbitrary",)),
        name="outproj",
    )(xp3, xs3, oh_p, oh_s, ys_p, ys_s, ada_p, ada_p, ada_p, ada_s3, ada_s3, ada_s3,
      wglu_bf, bglu, snorm, wo_h, wo_s, nffn, wr, br)
    x1_p, x1_s = outs[0].reshape(x_p.shape), outs[1].reshape(x_s.shape)
    return (x1_p, x1_s) + tuple(outs[2:])


def _moe_schedule(cnt, n_slots):
    c = cnt[0, EXPERT_LANE0:EXPERT_LANE0 + N_EXPERTS].astype(jnp.int32)
    seg_end = jnp.cumsum(c)
    seg_start = seg_end - c
    first_tile = seg_start // MOE_TILE
    tiles = jnp.where(c > 0, (seg_end - 1) // MOE_TILE - first_tile + 1, 0)
    cum = jnp.cumsum(tiles)
    n_items = cum[-1]
    item = jnp.minimum(jnp.arange(n_slots, dtype=jnp.int32), n_items - 1)
    item_exp = jnp.sum(item[:, None] >= cum[None, :], axis=1).astype(jnp.int32)
    shares = ((c > 0) & (seg_start % MOE_TILE != 0)).astype(jnp.int32)
    item_tile = item - jnp.sum((item[:, None] >= (cum - tiles)[None, :]) * shares[None, :], axis=1)
    return (seg_start.astype(jnp.int32), seg_end.astype(jnp.int32), item_tile.astype(jnp.int32), item_exp,
            n_items.reshape(1).astype(jnp.int32))


def _token_tile(ref, t):
    return ref.at[pl.ds(pl.multiple_of(t * TOKEN_TILE_ROWS, TOKEN_TILE_ROWS), TOKEN_TILE_ROWS)]


def _store_token_tiles(ref, x):
    rows = x.shape[0]
    for c in range(TOKEN_TILE_ROWS):
        ref[pl.ds(c, rows, stride=TOKEN_TILE_ROWS), :] = x[:, c * LANES:(c + 1) * LANES]


def _load_token_tiles(ref, rows):
    return [ref[pl.ds(c, rows, stride=TOKEN_TILE_ROWS), :] for c in range(TOKEN_TILE_ROWS)]


def _positions_kernel(seg_ref, p1_ref, p2_ref, o1_ref, o2_ref):
    for p_ref, o_ref in ((p1_ref, o1_ref), (p2_ref, o2_ref)):
        pair = p_ref[...]
        expert = pair >> RANK_BITS
        pos = pair & ((1 << RANK_BITS) - 1)
        for e in range(N_EXPERTS):
            pos = pos + jnp.where(expert == e, seg_ref[e], 0)
        o_ref[...] = pos


def _positions(seg, pair1, pair2):
    n = pair1.shape[0]
    shape2 = (n // LANES, LANES)
    full = pl.BlockSpec(shape2, lambda: (0, 0))
    pos1, pos2 = pl.pallas_call(
        _positions_kernel,
        in_specs=[pl.BlockSpec(memory_space=pltpu.SMEM), full, full],
        out_specs=[full, full],
        out_shape=[jax.ShapeDtypeStruct(shape2, jnp.int32)] * 2,
        compiler_params=pltpu.CompilerParams(vmem_limit_bytes=VMEM_LIMIT),
        name="moe_positions",
    )(seg, pair1.reshape(shape2), pair2.reshape(shape2))
    return pos1.reshape(n), pos2.reshape(n)


def _inverse_kernel(p1_ref, p2_ref, inv_ref):
    ts = p1_ref.shape[0]
    base = pl.program_id(0) * ts

    def body(t, carry):
        inv_ref[p1_ref[t]] = base + t
        inv_ref[p2_ref[t]] = base + t
        return carry

    lax.fori_loop(0, ts, body, 0, unroll=2 * DMA_UNROLL)


def _inverse(pos1, pos2, ts):
    n = pos1.shape[0]
    idx_spec = pl.BlockSpec((ts,), lambda i: (i,), memory_space=pltpu.SMEM)
    return pl.pallas_call(
        _inverse_kernel,
        grid=(n // ts,),
        in_specs=[idx_spec, idx_spec],
        out_specs=pl.BlockSpec(memory_space=pltpu.SMEM),
        out_shape=jax.ShapeDtypeStruct((2 * n,), jnp.int32),
        compiler_params=_cparams(("arbitrary",)),
        name="moe_inverse",
    )(pos1, pos2)


def _row_gather(src_ref, idx_of, dst_ref, dst_row0, sem, n_rows):
    def copy(t):
        return pltpu.make_async_copy(_token_tile(src_ref, idx_of(t)), _token_tile(dst_ref, dst_row0 + t), sem)

    def start():
        def body(t, carry):
            copy(2 * t).start(priority=0)
            copy(2 * t + 1).start(priority=1)
            return carry
        lax.fori_loop(0, n_rows // 2, body, 0, unroll=DMA_UNROLL // 2)

    def wait():
        def body(t, carry):
            copy(t).wait()
            return carry
        lax.fori_loop(0, n_rows, body, 0, unroll=DMA_UNROLL)

    return start, wait


def _experts_kernel(lo_ref, hi_ref, tile_ref, exp_ref, items_ref, inv_ref, inv_next_ref, h_ref,
                    wg_ref, wu_ref, wd_ref, ys_ref, xbuf_ref, sems, wg_s, wu_s, wd_s, *, n_slots):
    i = pl.program_id(0)
    prev = jnp.maximum(i - 1, 0)
    e = exp_ref[i]
    slot = i % 2

    @pl.when((i == 0) | (e != exp_ref[prev]))
    def _():
        wg_s[...] = wg_ref[0].astype(BF16)
        wu_s[...] = wu_ref[0].astype(BF16)
        wd_s[...] = wd_ref[0].astype(BF16)

    start_this, wait_this = _row_gather(h_ref, lambda t: inv_ref[t], xbuf_ref, slot * MOE_TILE,
                                        sems.at[slot], MOE_TILE)
    _, wait_next = _row_gather(h_ref, lambda t: inv_next_ref[t], xbuf_ref, (1 - slot) * MOE_TILE,
                               sems.at[1 - slot], MOE_TILE)

    def start_next(rows):
        for t in rows:
            pltpu.make_async_copy(_token_tile(h_ref, inv_next_ref[t]),
                                  _token_tile(xbuf_ref, (1 - slot) * MOE_TILE + t),
                                  sems.at[1 - slot]).start(priority=t % 2)

    @pl.when(i == 0)
    def _():
        start_this()

    wait_this()
    third = -(-MOE_TILE // 3)
    base = pl.multiple_of(slot * (MOE_TILE * TOKEN_TILE_ROWS), MOE_TILE * TOKEN_TILE_ROWS)
    x = jnp.concatenate([xbuf_ref[pl.ds(base + c, MOE_TILE, stride=TOKEN_TILE_ROWS), :]
                         for c in range(TOKEN_TILE_ROWS)], axis=-1).astype(BF16)
    start_next(range(0, third))
    gate = _dot(x, wg_s[...])
    start_next(range(third, 2 * third))
    up = _dot(x, wu_s[...])
    start_next(range(2 * third, MOE_TILE))
    out = _dot((_silu(gate) * up).astype(BF16), wd_s[...])
    row = tile_ref[i] * MOE_TILE + lax.broadcasted_iota(jnp.int32, (MOE_TILE, 1), 0)
    mine = (row >= lo_ref[e]) & (row < hi_ref[e])
    first_visit = (i == 0) | (tile_ref[i] != tile_ref[prev])

    @pl.when(first_visit)
    def _():
        _store_token_tiles(ys_ref, jnp.where(mine, out, 0.0))

    @pl.when(jnp.logical_not(first_visit))
    def _():
        old = jnp.concatenate(_load_token_tiles(ys_ref, MOE_TILE), axis=-1)
        _store_token_tiles(ys_ref, jnp.where(mine, out, old))

    @pl.when(i == n_slots - 1)
    def _():
        wait_next()


def _experts(seg_lo, seg_hi, item_tile, item_exp, n_items, inv, h2, wg, wu, wd):
    n_slots = item_tile.shape[0]
    _, d, de = wg.shape

    def w_spec(shape):
        return pl.BlockSpec((1,) + shape, lambda i, lo, hi, tile, ex, items: (ex[i], 0, 0))

    return pl.pallas_call(
        functools.partial(_experts_kernel, n_slots=n_slots),
        grid_spec=pltpu.PrefetchScalarGridSpec(
            num_scalar_prefetch=5, grid=(n_slots,),
            in_specs=[pl.BlockSpec((MOE_TILE,), lambda i, lo, hi, tile, ex, items: (tile[i],),
                                   memory_space=pltpu.SMEM),
                      pl.BlockSpec((MOE_TILE,), lambda i, lo, hi, tile, ex, items: (
                          tile[jnp.minimum(i + 1, n_slots - 1)],), memory_space=pltpu.SMEM),
                      pl.BlockSpec(memory_space=pltpu.HBM),
                      w_spec((d, de)), w_spec((d, de)), w_spec((de, d))],
            out_specs=pl.BlockSpec((MOE_TILE * TOKEN_TILE_ROWS, LANES),
                                   lambda i, lo, hi, tile, ex, items: (tile[i], 0)),
            scratch_shapes=[pltpu.VMEM((2 * MOE_TILE * TOKEN_TILE_ROWS, LANES), F32),
                            pltpu.SemaphoreType.DMA((2,)),
                            pltpu.VMEM((d, de), BF16), pltpu.VMEM((d, de), BF16), pltpu.VMEM((de, d), BF16)]),
        out_shape=jax.ShapeDtypeStruct((inv.shape[0] * TOKEN_TILE_ROWS, LANES), F32),
        compiler_params=_cparams(("arbitrary",)),
        name="moe_experts",
    )(seg_lo, seg_hi, item_tile, item_exp, n_items, inv, inv, h2, wg, wu, wd)


def _combine_kernel(p1_ref, p2_ref, p1n_ref, p2n_ref, rw_ref, x1_ref, gate_ref, shift_ref, scale_ref,
                    nfin_ref, ys_ref, y_ref, r1_ref, r2_ref, sems, *, n_steps):
    _, rows, d = x1_ref.shape
    s = pl.program_id(0)
    slot = s % 2

    def gathers(pa_ref, pb_ref, slot):
        g1 = _row_gather(ys_ref, lambda t: pa_ref[t], r1_ref, slot * rows, sems.at[slot], rows)
        g2 = _row_gather(ys_ref, lambda t: pb_ref[t], r2_ref, slot * rows, sems.at[slot], rows)
        return g1, g2

    this = gathers(p1_ref, p2_ref, slot)
    nxt = gathers(p1n_ref, p2n_ref, 1 - slot)

    @pl.when(s == 0)
    def _():
        this[0][0]()
        this[1][0]()

    @pl.when(s + 1 < n_steps)
    def _():
        nxt[0][0]()
        nxt[1][0]()

    this[0][1]()
    this[1][1]()

    def per_token(ref):
        per_seq = ref[0]
        seqs = per_seq.shape[0]
        return jnp.broadcast_to(per_seq[:, None, :], (seqs, rows // seqs, d)).reshape(rows, d)

    rw = rw_ref[...]
    w1 = rw[:, 0:1]
    w2 = rw[:, 1:2]
    base = pl.multiple_of(slot * (rows * TOKEN_TILE_ROWS), rows * TOKEN_TILE_ROWS)
    moe = jnp.concatenate(
        [w1 * r1_ref[pl.ds(base + c, rows, stride=TOKEN_TILE_ROWS), :]
         + w2 * r2_ref[pl.ds(base + c, rows, stride=TOKEN_TILE_ROWS), :] for c in range(TOKEN_TILE_ROWS)], axis=-1)
    x2 = x1_ref[0] + per_token(gate_ref) * moe
    hf = _rms(x2) * nfin_ref[...]
    y_ref[0] = hf * (1.0 + per_token(scale_ref)) + per_token(shift_ref)


def _combine(pos1, pos2, rw, x1, ada3, adaf3, nfin, ys, rows, tile0):
    b, t, d = x1.shape
    n_steps = b * t // rows
    seqs = max(rows // t, 1)
    tiles_per_seq = max(t // rows, 1)
    x3 = x1.reshape(n_steps, rows, d)
    ada_v = ada3.reshape(b // seqs, seqs, -1)
    adaf_v = adaf3.reshape(b // seqs, seqs, -1)
    x_spec = pl.BlockSpec((1, rows, d), lambda i: (i, 0, 0))

    def idx_spec(step):
        return pl.BlockSpec((rows,), lambda i: (tile0 + step(i),), memory_space=pltpu.SMEM)

    def ada_spec(col):
        return pl.BlockSpec((1, seqs, d), lambda i: (i // tiles_per_seq, 0, col))

    def cur(i):
        return i

    def nxt(i):
        return jnp.minimum(i + 1, n_steps - 1)

    return pl.pallas_call(
        functools.partial(_combine_kernel, n_steps=n_steps),
        grid=(n_steps,),
        in_specs=[idx_spec(cur), idx_spec(cur), idx_spec(nxt), idx_spec(nxt),
                  pl.BlockSpec((rows, ROUTER_LANES), lambda i: (tile0 + i, 0)),
                  x_spec, ada_spec(5), ada_spec(0), ada_spec(1),
                  pl.BlockSpec((1, d), lambda i: (0, 0)),
                  pl.BlockSpec(memory_space=pltpu.HBM)],
        out_specs=x_spec,
        scratch_shapes=[pltpu.VMEM((2 * rows * TOKEN_TILE_ROWS, LANES), F32),
                        pltpu.VMEM((2 * rows * TOKEN_TILE_ROWS, LANES), F32),
                        pltpu.SemaphoreType.DMA((2,))],
        out_shape=jax.ShapeDtypeStruct(x3.shape, F32),
        compiler_params=_cparams(("arbitrary",)),
        name="moe_combine",
    )(pos1, pos2, pos1, pos2, rw, x3, ada_v, adaf_v, adaf_v, nfin, ys).reshape(x1.shape)


def _mixer(x, ada3, s_h, s_re, s_im, p, *, bb, tt, hgrn_seqs, hgrn_tokens, sequential):
    b, t, d = x.shape
    n = b * t
    q, k, g, v, gs, u = _inproj(x, ada3, p['norm_mix'], p['w_in'], p['lb'], bb, tt)
    oh, s_h_new = _hgrn(q, k, g, v, gs, s_h, p['hgrn_norm'], b, t, hgrn_seqs, hgrn_tokens)
    if sequential:
        xr, xi = s_re.reshape(b, 1, -1), s_im.reshape(b, 1, -1)
        n_tokens = t
    else:
        xr, xi = s_re.reshape(1, b, -1), s_im.reshape(1, b, -1)
        n_tokens = n
    ys, fr, fi = _s5(u, xr, xi, p['ssm_bd'], p['ssm_inc'], p['ssm_m'], p['ssm_a8'], n_tokens, sequential)
    states = (s_h_new[None], fr.reshape(1, b, s_re.shape[-2], s_re.shape[-1]),
              fi.reshape(1, b, s_re.shape[-2], s_re.shape[-1]))
    return oh, ys, states


def kernel(x_prompt, x_sample, c_prompt, c_sample, state_hgrn, state_ssm_re, state_ssm_im, hgrn_lb_logits, w_ada, b_ada, norm_mix, w_in, hgrn_norm, ssm_a_re, ssm_a_im, ssm_log_dt, ssm_b_re, ssm_b_im, ssm_c_re, ssm_c_im, ssm_d, ssm_w_glu, ssm_b_glu, ssm_norm, w_out, norm_ffn, moe_w_group, moe_b_group, moe_w_router, moe_b_router, moe_w_gate, moe_w_up, moe_w_down, w_ada_final, b_ada_final, norm_final):
    depth = w_ada.shape[0]
    assert depth == 1
    d = x_prompt.shape[-1]
    bp = x_prompt.shape[0]
    dh = hgrn_norm.shape[-1]
    dk = dh // HGRN_HEADS
    de = moe_w_gate.shape[-1]
    n_exp = MOE_GROUPS * MOE_PER_GROUP

    lb = jax.nn.softmax(hgrn_lb_logits.astype(F32), axis=0)[0].reshape(1, dh)
    bd, inc, m, a8 = _s5_prepare(ssm_a_re[0], ssm_a_im[0], ssm_log_dt[0], ssm_b_re[0], ssm_b_im[0],
                                 ssm_c_re[0], ssm_c_im[0], ssm_d[0])

    def router_lanes(group_part, expert_part):
        rows = group_part.shape[0]
        return jnp.concatenate(
            [group_part, jnp.zeros((rows, EXPERT_LANE0 - MOE_GROUPS), F32), expert_part,
             jnp.zeros((rows, ROUTER_LANES - EXPERT_LANE0 - n_exp), F32)], axis=1)

    w_rt = router_lanes(moe_w_group[0], moe_w_router[0].transpose(1, 0, 2).reshape(d, n_exp))
    b_rt = router_lanes(moe_b_group[0].reshape(1, MOE_GROUPS), moe_b_router[0].reshape(1, n_exp))
    p = dict(
        lb=lb, norm_mix=norm_mix[0].reshape(1, d), w_in=w_in[0].astype(BF16),
        hgrn_norm=hgrn_norm[0].reshape(1, dh),
        ssm_bd=bd, ssm_inc=inc, ssm_m=m, ssm_a8=a8,
        w_glu=ssm_w_glu[0].astype(BF16), b_glu=ssm_b_glu[0].reshape(1, -1), ssm_norm=ssm_norm[0].reshape(1, -1),
        wo_h=w_out[0, :dh].astype(BF16), wo_s=w_out[0, dh:].astype(BF16),
        norm_ffn=norm_ffn[0].reshape(1, d), w_rt=w_rt.astype(BF16), b_rt=b_rt,
    )
    wg = moe_w_gate[0].reshape(n_exp, d, de)
    wu = moe_w_up[0].reshape(n_exp, d, de)
    wd = moe_w_down[0].reshape(n_exp, de, d)
    nfin = norm_final.reshape(1, d)

    c_all = jnp.concatenate([c_prompt, c_sample], axis=0)
    ada = _silu_linear(c_all, w_ada[0], b_ada[0])
    adaf = _silu_linear(c_all, w_ada_final, b_ada_final)
    ada_p, ada_s = ada[:bp].reshape(bp, 1, -1), ada[bp:].reshape(x_sample.shape[0], 1, -1)
    adaf_p, adaf_s = adaf[:bp].reshape(bp, 1, -1), adaf[bp:].reshape(x_sample.shape[0], 1, -1)

    zeros_h = jnp.zeros((bp, HGRN_HEADS, dk, dk), F32)
    zeros_s = jnp.zeros((bp,) + state_ssm_re.shape[2:], F32)
    oh_p, ys_p, st_p = _mixer(x_prompt, ada_p, zeros_h, zeros_s, zeros_s, p,
                              bb=1, tt=256, hgrn_seqs=1, hgrn_tokens=512, sequential=True)
    oh_s, ys_s, st_s = _mixer(x_sample, ada_s, state_hgrn[0], state_ssm_re[0], state_ssm_im[0], p,
                              bb=32, tt=8, hgrn_seqs=16, hgrn_tokens=8, sequential=False)
    x1_p, x1_s, h2, pair1, pair2, rw, cnt = _outproj(
        x_prompt, x_sample, oh_p, oh_s, ys_p, ys_s, ada_p, ada_s, p['w_glu'], p['b_glu'], p['ssm_norm'],
        p['wo_h'], p['wo_s'], p['norm_ffn'], p['w_rt'], p['b_rt'], TOKEN_ROWS)

    n_pairs = 2 * pair1.shape[0]
    n_slots = n_pairs // MOE_TILE + N_EXPERTS
    seg, seg_end, item_tile, item_exp, n_items = _moe_schedule(cnt, n_slots)
    pos1, pos2 = _positions(seg, pair1, pair2)
    inv = _inverse(pos1, pos2, 1024)
    ys = _experts(seg, seg_end, item_tile, item_exp, n_items, inv, h2, wg, wu, wd)
    tiles_p = x_prompt.shape[0] * x_prompt.shape[1] // TOKEN_ROWS
    y_p = _combine(pos1, pos2, rw, x1_p, ada_p, adaf_p, nfin, ys, TOKEN_ROWS, 0)
    y_s = _combine(pos1, pos2, rw, x1_s, ada_s, adaf_s, nfin, ys, TOKEN_ROWS, tiles_p)
    return (y_p, y_s) + st_p + st_s
```

```python
import functools
import math

import jax
import jax.numpy as jnp
from jax import lax
from jax.experimental import pallas as pl
from jax.experimental.pallas import tpu as pltpu

F32 = jnp.float32
BF16 = jnp.bfloat16
HIGHEST = lax.Precision.HIGHEST

EPS = 1e-6
MAX_REAL = -1e-4
HGRN_HEADS = 4
HGRN_CHUNK = 128
HGRN_SAFE_EXPONENT = 80.0
HGRN_EXACT_BLOCK = 8
SSM_GROUP = 16
SSM_STATE = 64
SSM_CHUNK = 8
SSM_SETS = 4
MOE_GROUPS = 4
MOE_PER_GROUP = 8
N_EXPERTS = MOE_GROUPS * MOE_PER_GROUP
ROUTER_LANES = 128
EXPERT_LANE0 = 32
RANK_BITS = 20
MOE_TILE = 256
LANES = 128
TOKEN_TILE_ROWS = 8
DMA_UNROLL = 8
GATHER_PRIORITY = 1
TOKEN_ROWS = 512
VMEM_LIMIT = 56 * 1024 * 1024


def _cparams(sem):
    return pltpu.CompilerParams(dimension_semantics=sem, vmem_limit_bytes=VMEM_LIMIT)


def _silu(x):
    return x * jax.nn.sigmoid(x)


def _rms(x):
    return x * lax.rsqrt(jnp.mean(x * x, axis=-1, keepdims=True) + EPS)


def _dot(a, b):
    return jnp.dot(a, b, preferred_element_type=F32)


def _dot_nt(a, b):
    return lax.dot_general(a, b, (((1,), (1,)), ((), ())), preferred_element_type=F32)


def _dot_tn(a, b, precision=None):
    return lax.dot_general(a, b, (((0,), (0,)), ((), ())), preferred_element_type=F32,
                           precision=precision)


def _silu_linear_kernel(c_ref, w_ref, b_ref, o_ref):
    a = _silu(c_ref[...]).astype(BF16)
    o_ref[...] = _dot(a, w_ref[...].astype(BF16)) + b_ref[...]


def _silu_linear(c, w, b):
    m, d = c.shape
    n = w.shape[1]
    tn = 1024
    return pl.pallas_call(
        _silu_linear_kernel,
        grid=(n // tn,),
        in_specs=[pl.BlockSpec((m, d), lambda j: (0, 0)),
                  pl.BlockSpec((d, tn), lambda j: (0, j)),
                  pl.BlockSpec((1, tn), lambda j: (0, j))],
        out_specs=pl.BlockSpec((m, tn), lambda j: (0, j)),
        out_shape=jax.ShapeDtypeStruct((m, n), F32),
        compiler_params=_cparams(("parallel",)),
        name="silu_linear",
    )(c, w, b.reshape(1, n))


def _inproj_kernel(x_ref, shift_ref, scale_ref, gain_ref, w_ref, lb_ref,
                   q_ref, k_ref, g_ref, v_ref, gs_ref, u_ref, *, dh):
    bb, tt, d = x_ref.shape
    h = _rms(x_ref[...]) * gain_ref[...]
    h = h * (1.0 + scale_ref[...]) + shift_ref[...]
    proj = _dot(h.reshape(bb * tt, d).astype(BF16), w_ref[...])
    lb = lb_ref[...]
    f = lb + (1.0 - lb) * jax.nn.sigmoid(proj[:, dh:2 * dh])
    q_ref[...] = _silu(proj[:, :dh]) * (float(dh // HGRN_HEADS) ** -0.5)
    k_ref[...] = 1.0 - f
    g_ref[...] = jnp.log(f)
    v_ref[...] = proj[:, 2 * dh:3 * dh]
    gs_ref[...] = _silu(proj[:, 3 * dh:4 * dh])
    for s in range(SSM_SETS):
        u_ref[s] = proj[:, 4 * dh + s * LANES:4 * dh + (s + 1) * LANES]


def _inproj(x, ada3, gain, w_in_bf, lb, bb, tt):
    b, t, d = x.shape
    dh = lb.shape[-1]
    nt = t // tt
    rows = bb * tt
    n = b * t
    row_spec = pl.BlockSpec((rows, dh), lambda i, j: (i * nt + j, 0))
    out = jax.ShapeDtypeStruct((n, dh), F32)
    return pl.pallas_call(
        functools.partial(_inproj_kernel, dh=dh),
        grid=(b // bb, nt),
        in_specs=[pl.BlockSpec((bb, tt, d), lambda i, j: (i, j, 0)),
                  pl.BlockSpec((bb, 1, d), lambda i, j: (i, 0, 0)),
                  pl.BlockSpec((bb, 1, d), lambda i, j: (i, 0, 1)),
                  pl.BlockSpec((1, d), lambda i, j: (0, 0)),
                  pl.BlockSpec(w_in_bf.shape, lambda i, j: (0, 0)),
                  pl.BlockSpec((1, dh), lambda i, j: (0, 0))],
        out_specs=[row_spec] * 5 + [pl.BlockSpec((SSM_SETS, rows, LANES), lambda i, j: (0, i * nt + j, 0))],
        out_shape=[out] * 5 + [jax.ShapeDtypeStruct((SSM_SETS, n, LANES), F32)],
        compiler_params=_cparams(("parallel", "parallel")),
        name="inproj",
    )(x, ada3, ada3, gain, w_in_bf, lb)


def _split3(x):
    hi = x.astype(BF16)
    r1 = x - hi.astype(F32)
    mid = r1.astype(BF16)
    lo = (r1 - mid.astype(F32)).astype(BF16)
    return hi, mid, lo


def _hgrn_kernel(q_ref, k_ref, g_ref, v_ref, gs_ref, s0_ref, hn_ref, o_ref, sf_ref,
                 st_ref, intra_ref, qh_ref, kh_ref, ea_ref, sums_ref, *, tl, nt):
    j = pl.program_id(1)
    rows_total, dh = q_ref.shape
    dk = dh // HGRN_HEADS
    c = HGRN_CHUNK
    seqs = c // tl
    n_chunks = rows_total // c

    @pl.when(j == 0)
    def _():
        st_ref[...] = s0_ref[...]

    r = lax.broadcasted_iota(jnp.int32, (c, c), 0)
    s = lax.broadcasted_iota(jnp.int32, (c, c), 1)
    same_seq = (r // tl) == (s // tl)
    causal = same_seq & (r >= s)
    upto_mid = same_seq & ((s % tl) < tl // 2)
    one_seq = tl == c
    sums = (causal if one_seq else jnp.concatenate([causal, upto_mid, same_seq], axis=0)).astype(BF16)
    ref_rows = 1 if one_seq else c
    eye3 = (lax.broadcasted_iota(jnp.int32, (dk, 3 * dk), 1) % dk
            == lax.broadcasted_iota(jnp.int32, (dk, 3 * dk), 0)).astype(BF16)

    def decay_matrix(e_row):
        parts = jnp.concatenate(_split3(e_row), axis=-1)
        return _dot_nt(eye3, jnp.broadcast_to(parts, (dk, 3 * dk)))

    worst = jnp.float32(0.0)
    for ci in range(n_chunks):
        rows = slice(ci * c, (ci + 1) * c)
        g_parts = _split3(g_ref[rows, :])
        acc = _dot(sums, g_parts[0]) + _dot(sums, g_parts[1]) + _dot(sums, g_parts[2])
        if one_seq:
            a_mid, a_end = acc[c // 2 - 1:c // 2], acc[c - 1:c]
            sums_ref[ci * 3 * c:ci * 3 * c + c, :] = acc
            sums_ref[ci * 3 * c + c:ci * 3 * c + c + 1, :] = a_mid
            sums_ref[ci * 3 * c + 2 * c:ci * 3 * c + 2 * c + 1, :] = a_end
        else:
            a_mid, a_end = acc[c:2 * c], acc[2 * c:]
            sums_ref[ci * 3 * c:(ci + 1) * 3 * c, :] = acc
        worst = jnp.maximum(worst, jnp.max(jnp.maximum(jnp.abs(a_mid), jnp.abs(a_end - a_mid))))
    factorised_is_safe = worst < HGRN_SAFE_EXPONENT

    @pl.when(factorised_is_safe)
    def _():
        for ci in range(n_chunks):
            rows = slice(ci * c, (ci + 1) * c)
            a = sums_ref[ci * 3 * c:ci * 3 * c + c, :]
            a_mid = sums_ref[ci * 3 * c + c:ci * 3 * c + c + ref_rows, :]
            a_end = sums_ref[ci * 3 * c + 2 * c:ci * 3 * c + 2 * c + ref_rows, :]
            e_mid = jnp.exp(a_mid)
            e_tail = jnp.exp(a_end - a_mid)
            qt = q_ref[rows, :] * jnp.exp(a - a_mid)
            kt = k_ref[rows, :] * jnp.exp(a_mid - a)
            qh_ref[rows, :] = qt * e_mid
            kh_ref[rows, :] = kt * e_tail
            ea_ref[rows.start:rows.start + ref_rows, :] = e_mid * e_tail
            qt = qt.astype(BF16)
            kt = kt.astype(BF16)
            v = v_ref[rows, :].astype(BF16)
            for h in range(HGRN_HEADS):
                lanes = slice(h * dk, (h + 1) * dk)
                sc = jnp.where(causal, _dot_nt(qt[:, lanes], kt[:, lanes]), 0.0).astype(BF16)
                intra_ref[rows, lanes] = _dot(sc, v[:, lanes])

            for si in range(seqs):
                seq = ci * seqs + si if tl < c else 0
                srows = slice(si * tl, (si + 1) * tl)
                orows = slice(ci * c + si * tl, ci * c + (si + 1) * tl)
                for h in range(HGRN_HEADS):
                    lanes = slice(h * dk, (h + 1) * dk)
                    state = st_ref[seq, h]
                    o = intra_ref[orows, lanes] + _dot(qh_ref[orows, lanes].astype(BF16), state.astype(BF16))
                    decay = decay_matrix(ea_ref[orows.start:orows.start + 1, lanes])
                    st_ref[seq, h] = decay * state + _dot_tn(kh_ref[orows, lanes].astype(BF16),
                                                             v[srows, lanes])
                    o_ref[orows, lanes] = _rms(o) * hn_ref[:, lanes] * gs_ref[orows, lanes]

    @pl.when(jnp.logical_not(factorised_is_safe))
    def _():
        blk = HGRN_EXACT_BLOCK
        blocks_per_seq = max(tl // blk, 1)
        tri = (lax.broadcasted_iota(jnp.int32, (blk, blk), 0)
               >= lax.broadcasted_iota(jnp.int32, (blk, blk), 1)).astype(BF16)
        sub = lax.broadcasted_iota(jnp.int32, (blk, dk), 0)

        def block(bi, carry):
            rows = pl.ds(pl.multiple_of(bi * blk, blk), blk)
            seq = bi // blocks_per_seq if tl < c else 0
            g_parts = _split3(g_ref[rows, :])
            a = _dot(tri, g_parts[0]) + _dot(tri, g_parts[1]) + _dot(tri, g_parts[2])
            a_end = a[blk - 1:blk]
            q = q_ref[rows, :]
            k = k_ref[rows, :]
            v = v_ref[rows, :]
            qh = (q * jnp.exp(a)).astype(BF16)
            kh = (k * jnp.exp(a_end - a)).astype(BF16)
            ea = jnp.exp(a_end)
            vb = v.astype(BF16)
            for h in range(HGRN_HEADS):
                lanes = slice(h * dk, (h + 1) * dk)
                state = st_ref[seq, h]
                intra = []
                for t in range(blk):
                    live = sub <= t
                    decay_t = jnp.where(live, jnp.exp(jnp.where(live, a[t:t + 1, lanes] - a[:, lanes], 0.0)), 0.0)
                    score = jnp.sum(q[t:t + 1, lanes] * k[:, lanes] * decay_t, axis=-1, keepdims=True)
                    intra.append(jnp.sum(score * v[:, lanes], axis=0, keepdims=True))
                o = jnp.concatenate(intra, axis=0) + _dot(qh[:, lanes], state.astype(BF16))
                st_ref[seq, h] = decay_matrix(ea[:, lanes]) * state + _dot_tn(kh[:, lanes], vb[:, lanes])
                o_ref[rows, lanes] = _rms(o) * hn_ref[:, lanes] * gs_ref[rows, lanes]
            return carry

        lax.fori_loop(0, rows_total // blk, block, 0)

    @pl.when(j == nt - 1)
    def _():
        sf_ref[...] = st_ref[...]


def _hgrn(q, k, g, v, gs, s0, hnorm, b, t, nseq, tt):
    n, dh = q.shape
    nt = t // tt
    dk = dh // HGRN_HEADS
    rows = nseq * tt
    tl = min(tt, HGRN_CHUNK)
    row_spec = pl.BlockSpec((rows, dh), lambda i, j: (i * nt + j, 0))
    st_spec = pl.BlockSpec((nseq, HGRN_HEADS, dk, dk), lambda i, j: (i, 0, 0, 0))
    return pl.pallas_call(
        functools.partial(_hgrn_kernel, tl=tl, nt=nt),
        grid=(b // nseq, nt),
        in_specs=[row_spec] * 5 + [st_spec, pl.BlockSpec((1, dh), lambda i, j: (0, 0))],
        out_specs=[row_spec, st_spec],
        out_shape=[jax.ShapeDtypeStruct((n, dh), F32),
                   jax.ShapeDtypeStruct((b, HGRN_HEADS, dk, dk), F32)],
        scratch_shapes=[pltpu.VMEM((nseq, HGRN_HEADS, dk, dk), F32),
                        pltpu.VMEM((rows, dh), F32), pltpu.VMEM((rows, dh), F32),
                        pltpu.VMEM((rows, dh), F32), pltpu.VMEM((rows, dh), F32),
                        pltpu.VMEM((3 * rows, dh), F32)],
        compiler_params=_cparams(("parallel", "arbitrary")),
        name="hgrn",
    )(q, k, g, v, gs, s0, hnorm)


def _s5_prepare(a_re, a_im, log_dt, b_re, b_im, c_re, c_im, d_skip):
    ng, npp = a_re.shape
    nh = b_re.shape[-1]
    L = SSM_CHUNK
    gs = ng // SSM_SETS
    lam_re = jnp.minimum(a_re, MAX_REAL)
    lam_im = a_im
    dt = jnp.exp(log_dt)
    mag = jnp.exp(lam_re * dt)
    ab_re = mag * jnp.cos(lam_im * dt)
    ab_im = mag * jnp.sin(lam_im * dt)
    den = lam_re * lam_re + lam_im * lam_im
    co_re = ((ab_re - 1.0) * lam_re + ab_im * lam_im) / den
    co_im = (ab_im * lam_re - (ab_re - 1.0) * lam_im) / den
    bb_re = co_re[..., None] * b_re - co_im[..., None] * b_im
    bb_im = co_re[..., None] * b_im + co_im[..., None] * b_re
    tau = jnp.arange(L + 1, dtype=F32)[:, None, None]
    pw_mag = jnp.exp(tau * (lam_re * dt))
    pw_re = pw_mag * jnp.cos(tau * (lam_im * dt))
    pw_im = pw_mag * jnp.sin(tau * (lam_im * dt))
    ab_b_re = pw_re[:L, :, :, None] * bb_re - pw_im[:L, :, :, None] * bb_im
    ab_b_im = pw_re[:L, :, :, None] * bb_im + pw_im[:L, :, :, None] * bb_re
    kern = (jnp.einsum('gkp,lgph->lghk', c_re, ab_b_re, precision=HIGHEST)
            - jnp.einsum('gkp,lgph->lghk', c_im, ab_b_im, precision=HIGHEST))
    skip = d_skip[None, :, :, None] * jnp.eye(nh, dtype=F32)
    kern = kern + jnp.where(jnp.arange(L)[:, None, None, None] == 0, skip, 0.0)

    def group_block_diag(c):
        rows, w = c.shape[-2:]
        repeat = (jnp.arange(gs * w)[None, :] % w == jnp.arange(w)[:, None]).astype(F32)
        tiled = jnp.dot(c, repeat, precision=HIGHEST)
        rg = jnp.arange(rows)[:, None] // (rows // gs)
        cq = jnp.arange(gs * w)[None, :] // w
        return jnp.where(rg == cq, tiled, 0.0).astype(BF16).transpose(1, 0, 2, 3).reshape(
            SSM_SETS, L * rows, gs * w)

    bd = group_block_diag(kern.reshape(L, SSM_SETS, gs * nh, nh)).reshape(SSM_SETS, L, gs * nh, gs * nh)
    n_re = group_block_diag(ab_b_re[::-1].transpose(0, 1, 3, 2).reshape(L, SSM_SETS, gs * nh, npp))
    n_im = group_block_diag(ab_b_im[::-1].transpose(0, 1, 3, 2).reshape(L, SSM_SETS, gs * nh, npp))
    inc = jnp.concatenate([n_re, n_im], axis=-1)
    ca_re = c_re[None] * pw_re[1:, :, None, :] - c_im[None] * pw_im[1:, :, None, :]
    ca_im = c_re[None] * pw_im[1:, :, None, :] + c_im[None] * pw_re[1:, :, None, :]
    m = jnp.concatenate([group_block_diag(ca_re.reshape(L, SSM_SETS, gs * nh, npp)),
                         group_block_diag(-ca_im.reshape(L, SSM_SETS, gs * nh, npp))], axis=-1)
    a8 = jnp.concatenate([pw_re[L].reshape(SSM_SETS, 1, gs * npp),
                          pw_im[L].reshape(SSM_SETS, 1, gs * npp)], axis=-1)
    return bd, inc, m, a8


def _s5_kernel(u_ref, xr_ref, xi_ref, bd_ref, inc_ref, m_ref, a8_ref, y_ref, fr_ref, fi_ref, wt_ref, *, sequential):
    n = u_ref.shape[1] // SSM_CHUNK
    ns = xr_ref.shape[-1]
    u = jnp.concatenate([u_ref[0, pl.ds(s, n, stride=SSM_CHUNK), :] for s in range(SSM_CHUNK)], axis=-1)

    @pl.when(pl.program_id(1) == 0)
    def _():
        wt_ref[...] = jnp.zeros_like(wt_ref)
        for s in range(SSM_CHUNK):
            for t in range(s, SSM_CHUNK):
                wt_ref[s * LANES:(s + 1) * LANES, t * LANES:(t + 1) * LANES] = bd_ref[0, t - s]

    u = u.astype(BF16)
    y_local = _dot(u, wt_ref[...])
    inc = _dot(u, inc_ref[0])
    d_re = inc[:, :ns]
    d_im = inc[:, ns:]
    a_re = a8_ref[0][:, :ns]
    a_im = a8_ref[0][:, ns:]
    x0_re = xr_ref[0]
    x0_im = xi_ref[0]
    if sequential:
        row = lax.broadcasted_iota(jnp.int32, (n, ns), 0)
        first = row == 0
        x_re = d_re + jnp.where(first, a_re * x0_re - a_im * x0_im, 0.0)
        x_im = d_im + jnp.where(first, a_re * x0_im + a_im * x0_re, 0.0)
        p_re, p_im = a_re, a_im
        step = 1
        while step < n:
            s_re = jnp.where(row >= step, pltpu.roll(x_re, step, 0), 0.0)
            s_im = jnp.where(row >= step, pltpu.roll(x_im, step, 0), 0.0)
            x_re, x_im = x_re + p_re * s_re - p_im * s_im, x_im + p_re * s_im + p_im * s_re
            p_re, p_im = p_re * p_re - p_im * p_im, 2.0 * p_re * p_im
            step *= 2
        fr_ref[0] = x_re[n - 1:n]
        fi_ref[0] = x_im[n - 1:n]
        xc_re = jnp.where(first, x0_re, pltpu.roll(x_re, 1, 0))
        xc_im = jnp.where(first, x0_im, pltpu.roll(x_im, 1, 0))
    else:
        xc_re, xc_im = x0_re, x0_im
        fr_ref[0] = a_re * x0_re - a_im * x0_im + d_re
        fi_ref[0] = a_re * x0_im + a_im * x0_re + d_im
    xc = jnp.concatenate([xc_re, xc_im], axis=-1).astype(BF16)
    y = y_local + _dot_nt(xc, m_ref[0])
    for t in range(SSM_CHUNK):
        y_ref[0, pl.ds(t, n, stride=SSM_CHUNK), :] = y[:, t * LANES:(t + 1) * LANES]


def _s5(u, x_re, x_im, bd, inc, m, a8, n_tokens, sequential):
    sets = u.shape[0]
    nb, rb, _ = x_re.shape
    ns = m.shape[1] // 2
    st_spec = pl.BlockSpec((1, rb, ns), lambda gi, i: (i, 0, gi))
    st_shape = jax.ShapeDtypeStruct(x_re.shape, F32)
    tok_spec = pl.BlockSpec((1, n_tokens, LANES), lambda gi, i: (gi, i, 0))
    return pl.pallas_call(
        functools.partial(_s5_kernel, sequential=sequential),
        grid=(sets, nb),
        in_specs=[tok_spec, st_spec, st_spec,
                  pl.BlockSpec((1,) + bd.shape[1:], lambda gi, i: (gi, 0, 0, 0)),
                  pl.BlockSpec((1,) + inc.shape[1:], lambda gi, i: (gi, 0, 0)),
                  pl.BlockSpec((1,) + m.shape[1:], lambda gi, i: (gi, 0, 0)),
                  pl.BlockSpec((1, 1, 2 * ns), lambda gi, i: (gi, 0, 0))],
        out_specs=[tok_spec, st_spec, st_spec],
        out_shape=[jax.ShapeDtypeStruct(u.shape, F32), st_shape, st_shape],
        scratch_shapes=[pltpu.VMEM((inc.shape[1], SSM_CHUNK * LANES), BF16)],
        compiler_params=_cparams(("parallel", "arbitrary")),
        name="s5",
    )(u, x_re, x_im, bd, inc, m, a8)


def _gelu_tanh(x):
    return 0.5 * x * (1.0 + jnp.tanh(math.sqrt(2.0 / math.pi) * (x + 0.044715 * (x * x * x))))


def _outproj_kernel(xp_ref, xs_ref, ohp_ref, ohs_ref, ysp_ref, yss_ref,
                    gate_p_ref, shift_p_ref, scale_p_ref, gate_s_ref, shift_s_ref, scale_s_ref,
                    wglu_ref, bglu_ref, sn_ref, wo_h_ref, wo_s_ref, nf_ref, wr_ref, br_ref,
                    x1p_ref, x1s_ref, h2_ref, pair1_ref, pair2_ref, rw_ref, cnt_ref, *, prompt_tiles):
    is_prompt = pl.program_id(0) < prompt_tiles
    _, rows, d = xp_ref.shape

    def per_token(p_ref, s_ref):
        per_seq = s_ref[0]
        seqs = per_seq.shape[0]
        rep = jnp.broadcast_to(per_seq[:, None, :], (seqs, rows // seqs, d)).reshape(rows, d)
        return jnp.where(is_prompt, p_ref[0], rep)

    x = jnp.where(is_prompt, xp_ref[0], xs_ref[0])
    oh = jnp.where(is_prompt, ohp_ref[...], ohs_ref[...])
    ys = jnp.concatenate([jnp.where(is_prompt, ysp_ref[s], yss_ref[s]) for s in range(SSM_SETS)], axis=-1)
    y = _gelu_tanh(ys)
    y = y * jax.nn.sigmoid(_dot(y.astype(BF16), wglu_ref[...]) + bglu_ref[...])
    o_s = _rms(y) * sn_ref[...]
    mix = _dot(oh.astype(BF16), wo_h_ref[...]) + _dot(o_s.astype(BF16), wo_s_ref[...])
    x1 = x + per_token(gate_p_ref, gate_s_ref) * mix

    @pl.when(is_prompt)
    def _():
        x1p_ref[0] = x1

    @pl.when(jnp.logical_not(is_prompt))
    def _():
        x1s_ref[0] = x1

    h2 = _rms(x1) * nf_ref[...]
    h2 = h2 * (1.0 + per_token(scale_p_ref, scale_s_ref)) + per_token(shift_p_ref, shift_s_ref)
    _store_token_tiles(h2_ref, h2)

    logits = _dot(h2.astype(BF16), wr_ref[...]) + br_ref[...]
    lane = lax.broadcasted_iota(jnp.int32, logits.shape, 1)
    neg = -jnp.inf
    gl = jnp.where(lane < MOE_GROUPS, logits, neg)
    gmax = jnp.max(gl, axis=-1, keepdims=True)
    gidx = jnp.min(jnp.where(gl == gmax, lane, ROUTER_LANES), axis=-1, keepdims=True)
    grp_w = 1.0 / jnp.sum(jnp.exp(gl - gmax), axis=-1, keepdims=True)
    e0 = EXPERT_LANE0 + gidx * MOE_PER_GROUP
    sel = jnp.where((lane >= e0) & (lane < e0 + MOE_PER_GROUP), logits, neg)
    m1 = jnp.max(sel, axis=-1, keepdims=True)
    i1 = jnp.min(jnp.where(sel == m1, lane, ROUTER_LANES), axis=-1, keepdims=True)
    sel2 = jnp.where(lane == i1, neg, sel)
    m2 = jnp.max(sel2, axis=-1, keepdims=True)
    i2 = jnp.min(jnp.where(sel2 == m2, lane, ROUTER_LANES), axis=-1, keepdims=True)
    e2 = jnp.exp(m2 - m1)
    w1 = 1.0 / (1.0 + e2)
    w2 = e2 / (1.0 + e2)
    rw_ref[...] = grp_w * (jnp.where(lane == 0, w1, 0.0) + jnp.where(lane == 1, w2, 0.0))

    @pl.when(pl.program_id(0) == 0)
    def _():
        cnt_ref[...] = jnp.zeros_like(cnt_ref)

    picked = (lane == i1) | (lane == i2)
    earlier = (lax.broadcasted_iota(jnp.int32, (rows, rows), 0)
               > lax.broadcasted_iota(jnp.int32, (rows, rows), 1))
    base = cnt_ref[...]
    before = _dot(earlier.astype(BF16), picked.astype(BF16)) + base
    rank1 = jnp.sum(jnp.where(lane == i1, before, 0.0), axis=-1, keepdims=True).astype(jnp.int32)
    rank2 = jnp.sum(jnp.where(lane == i2, before, 0.0), axis=-1, keepdims=True).astype(jnp.int32)
    cnt_ref[...] = base + jnp.sum(picked.astype(F32), axis=0, keepdims=True)
    info = jnp.where(lane == 0, ((i1 - EXPERT_LANE0) << RANK_BITS) | rank1,
                     jnp.where(lane == 1, ((i2 - EXPERT_LANE0) << RANK_BITS) | rank2, 0)).T
    pair1_ref[...] = info[0]
    pair2_ref[...] = info[1]


def _outproj(x_p, x_s, oh_p, oh_s, ys_p, ys_s, ada_p, ada_s, wglu_bf, bglu, snorm, wo_h, wo_s, nffn, wr, br, rows):
    d = x_p.shape[-1]
    dh = oh_p.shape[-1]
    n_p = x_p.shape[0] * x_p.shape[1]
    n_s = x_s.shape[0] * x_s.shape[1]
    n = n_p + n_s
    tiles_p = n_p // rows
    seqs = rows // x_s.shape[1]

    def pt(i):
        return jnp.minimum(i, tiles_p - 1)

    def st(i):
        return jnp.maximum(i - tiles_p, 0)

    tiles_per_seq = x_p.shape[1] // rows

    def ada_p_spec(col):
        return pl.BlockSpec((1, 1, d), lambda i: (pt(i) // tiles_per_seq, 0, col))

    def ada_s_spec(col):
        return pl.BlockSpec((1, seqs, d), lambda i: (st(i), 0, col))

    def full(a):
        return pl.BlockSpec(a.shape, lambda i: (0,) * a.ndim)

    xp3 = x_p.reshape(tiles_p, rows, d)
    xs3 = x_s.reshape(n_s // rows, rows, d)
    ada_s3 = ada_s.reshape(n_s // rows, seqs, -1)
    outs = pl.pallas_call(
        functools.partial(_outproj_kernel, prompt_tiles=tiles_p),
        grid=(n // rows,),
        in_specs=[pl.BlockSpec((1, rows, d), lambda i: (pt(i), 0, 0)),
                  pl.BlockSpec((1, rows, d), lambda i: (st(i), 0, 0)),
                  pl.BlockSpec((rows, dh), lambda i: (pt(i), 0)),
                  pl.BlockSpec((rows, dh), lambda i: (st(i), 0)),
                  pl.BlockSpec((SSM_SETS, rows, LANES), lambda i: (0, pt(i), 0)),
                  pl.BlockSpec((SSM_SETS, rows, LANES), lambda i: (0, st(i), 0)),
                  ada_p_spec(2), ada_p_spec(3), ada_p_spec(4), ada_s_spec(2), ada_s_spec(3), ada_s_spec(4),
                  full(wglu_bf), full(bglu), full(snorm), full(wo_h), full(wo_s), full(nffn),
                  full(wr), full(br)],
        out_specs=[pl.BlockSpec((1, rows, d), lambda i: (pt(i), 0, 0)),
                   pl.BlockSpec((1, rows, d), lambda i: (st(i), 0, 0)),
                   pl.BlockSpec((rows * TOKEN_TILE_ROWS, LANES), lambda i: (i, 0)),
                   pl.BlockSpec((rows,), lambda i: (i,)),
                   pl.BlockSpec((rows,), lambda i: (i,)),
                   pl.BlockSpec((rows, ROUTER_LANES), lambda i: (i, 0)),
                   pl.BlockSpec((1, ROUTER_LANES), lambda i: (0, 0))],
        out_shape=[jax.ShapeDtypeStruct(xp3.shape, F32),
                   jax.ShapeDtypeStruct(xs3.shape, F32),
                   jax.ShapeDtypeStruct((n * TOKEN_TILE_ROWS, LANES), F32),
                   jax.ShapeDtypeStruct((n,), jnp.int32),
                   jax.ShapeDtypeStruct((n,), jnp.int32),
                   jax.ShapeDtypeStruct((n, ROUTER_LANES), F32),
                   jax.ShapeDtypeStruct((1, ROUTER_LANES), F32)],
        compiler_params=_cparams(("arbitrary",)),
        name="outproj",
    )(xp3, xs3, oh_p, oh_s, ys_p, ys_s, ada_p, ada_p, ada_p, ada_s3, ada_s3, ada_s3,
      wglu_bf, bglu, snorm, wo_h, wo_s, nffn, wr, br)
    x1_p, x1_s = outs[0].reshape(x_p.shape), outs[1].reshape(x_s.shape)
    return (x1_p, x1_s) + tuple(outs[2:])


def _moe_schedule(cnt, n_slots):
    c = cnt[0, EXPERT_LANE0:EXPERT_LANE0 + N_EXPERTS].astype(jnp.int32)
    seg_end = jnp.cumsum(c)
    seg_start = seg_end - c
    first_tile = seg_start // MOE_TILE
    tiles = jnp.where(c > 0, (seg_end - 1) // MOE_TILE - first_tile + 1, 0)
    cum = jnp.cumsum(tiles)
    n_items = cum[-1]
    item = jnp.minimum(jnp.arange(n_slots, dtype=jnp.int32), n_items - 1)
    item_exp = jnp.sum(item[:, None] >= cum[None, :], axis=1).astype(jnp.int32)
    shares = ((c > 0) & (seg_start % MOE_TILE != 0)).astype(jnp.int32)
    item_tile = item - jnp.sum((item[:, None] >= (cum - tiles)[None, :]) * shares[None, :], axis=1)
    return (seg_start.astype(jnp.int32), seg_end.astype(jnp.int32), item_tile.astype(jnp.int32), item_exp,
            n_items.reshape(1).astype(jnp.int32))


def _token_tile(ref, t):
    return ref.at[pl.ds(pl.multiple_of(t * TOKEN_TILE_ROWS, TOKEN_TILE_ROWS), TOKEN_TILE_ROWS)]


def _store_token_tiles(ref, x):
    rows = x.shape[0]
    for c in range(TOKEN_TILE_ROWS):
        ref[pl.ds(c, rows, stride=TOKEN_TILE_ROWS), :] = x[:, c * LANES:(c + 1) * LANES]


def _load_token_tiles(ref, rows):
    return [ref[pl.ds(c, rows, stride=TOKEN_TILE_ROWS), :] for c in range(TOKEN_TILE_ROWS)]


def _positions_kernel(seg_ref, p1_ref, p2_ref, o1_ref, o2_ref):
    for p_ref, o_ref in ((p1_ref, o1_ref), (p2_ref, o2_ref)):
        pair = p_ref[...]
        expert = pair >> RANK_BITS
        pos = pair & ((1 << RANK_BITS) - 1)
        for e in range(N_EXPERTS):
            pos = pos + jnp.where(expert == e, seg_ref[e], 0)
        o_ref[...] = pos


def _positions(seg, pair1, pair2):
    n = pair1.shape[0]
    shape2 = (n // LANES, LANES)
    full = pl.BlockSpec(shape2, lambda: (0, 0))
    pos1, pos2 = pl.pallas_call(
        _positions_kernel,
        in_specs=[pl.BlockSpec(memory_space=pltpu.SMEM), full, full],
        out_specs=[full, full],
        out_shape=[jax.ShapeDtypeStruct(shape2, jnp.int32)] * 2,
        compiler_params=pltpu.CompilerParams(vmem_limit_bytes=VMEM_LIMIT),
        name="moe_positions",
    )(seg, pair1.reshape(shape2), pair2.reshape(shape2))
    return pos1.reshape(n), pos2.reshape(n)


def _inverse_kernel(p1_ref, p2_ref, inv_ref):
    ts = p1_ref.shape[0]
    base = pl.program_id(0) * ts

    def body(t, carry):
        inv_ref[p1_ref[t]] = base + t
        inv_ref[p2_ref[t]] = base + t
        return carry

    lax.fori_loop(0, ts, body, 0, unroll=2 * DMA_UNROLL)


def _inverse(pos1, pos2, ts):
    n = pos1.shape[0]
    idx_spec = pl.BlockSpec((ts,), lambda i: (i,), memory_space=pltpu.SMEM)
    return pl.pallas_call(
        _inverse_kernel,
        grid=(n // ts,),
        in_specs=[idx_spec, idx_spec],
        out_specs=pl.BlockSpec(memory_space=pltpu.SMEM),
        out_shape=jax.ShapeDtypeStruct((2 * n,), jnp.int32),
        compiler_params=_cparams(("arbitrary",)),
        name="moe_inverse",
    )(pos1, pos2)


def _row_gather(src_ref, idx_of, dst_ref, dst_row0, sem, n_rows):
    def copy(t):
        return pltpu.make_async_copy(_token_tile(src_ref, idx_of(t)), _token_tile(dst_ref, dst_row0 + t), sem)

    def start():
        def body(t, carry):
            copy(2 * t).start(priority=GATHER_PRIORITY)
            copy(2 * t + 1).start(priority=GATHER_PRIORITY)
            return carry
        lax.fori_loop(0, n_rows // 2, body, 0, unroll=DMA_UNROLL // 2)

    def wait():
        def body(t, carry):
            copy(t).wait()
            return carry
        lax.fori_loop(0, n_rows, body, 0, unroll=DMA_UNROLL)

    return start, wait


def _experts_kernel(lo_ref, hi_ref, tile_ref, exp_ref, items_ref, inv_ref, inv_next_ref, h_ref,
                    wg_ref, wu_ref, wd_ref, ys_ref, xbuf_ref, sems, wg_s, wu_s, wd_s, *, n_slots):
    i = pl.program_id(0)
    prev = jnp.maximum(i - 1, 0)
    e = exp_ref[i]
    slot = i % 2

    @pl.when((i == 0) | (e != exp_ref[prev]))
    def _():
        wg_s[...] = wg_ref[0].astype(BF16)
        wu_s[...] = wu_ref[0].astype(BF16)
        wd_s[...] = wd_ref[0].astype(BF16)

    start_this, wait_this = _row_gather(h_ref, lambda t: inv_ref[t], xbuf_ref, slot * MOE_TILE,
                                        sems.at[slot], MOE_TILE)
    _, wait_next = _row_gather(h_ref, lambda t: inv_next_ref[t], xbuf_ref, (1 - slot) * MOE_TILE,
                               sems.at[1 - slot], MOE_TILE)

    def start_next(rows):
        for t in rows:
            pltpu.make_async_copy(_token_tile(h_ref, inv_next_ref[t]),
                                  _token_tile(xbuf_ref, (1 - slot) * MOE_TILE + t),
                                  sems.at[1 - slot]).start(priority=GATHER_PRIORITY)

    @pl.when(i == 0)
    def _():
        start_this()

    wait_this()
    third = -(-MOE_TILE // 3)
    base = pl.multiple_of(slot * (MOE_TILE * TOKEN_TILE_ROWS), MOE_TILE * TOKEN_TILE_ROWS)
    x = jnp.concatenate([xbuf_ref[pl.ds(base + c, MOE_TILE, stride=TOKEN_TILE_ROWS), :]
                         for c in range(TOKEN_TILE_ROWS)], axis=-1).astype(BF16)
    start_next(range(0, third))
    gate = _dot(x, wg_s[...])
    start_next(range(third, 2 * third))
    up = _dot(x, wu_s[...])
    start_next(range(2 * third, MOE_TILE))
    out = _dot((_silu(gate) * up).astype(BF16), wd_s[...])
    row = tile_ref[i] * MOE_TILE + lax.broadcasted_iota(jnp.int32, (MOE_TILE, 1), 0)
    mine = (row >= lo_ref[e]) & (row < hi_ref[e])
    first_visit = (i == 0) | (tile_ref[i] != tile_ref[prev])

    @pl.when(first_visit)
    def _():
        _store_token_tiles(ys_ref, jnp.where(mine, out, 0.0))

    @pl.when(jnp.logical_not(first_visit))
    def _():
        old = jnp.concatenate(_load_token_tiles(ys_ref, MOE_TILE), axis=-1)
        _store_token_tiles(ys_ref, jnp.where(mine, out, old))

    @pl.when(i == n_slots - 1)
    def _():
        wait_next()


def _experts(seg_lo, seg_hi, item_tile, item_exp, n_items, inv, h2, wg, wu, wd):
    n_slots = item_tile.shape[0]
    _, d, de = wg.shape

    def w_spec(shape):
        return pl.BlockSpec((1,) + shape, lambda i, lo, hi, tile, ex, items: (ex[i], 0, 0))

    return pl.pallas_call(
        functools.partial(_experts_kernel, n_slots=n_slots),
        grid_spec=pltpu.PrefetchScalarGridSpec(
            num_scalar_prefetch=5, grid=(n_slots,),
            in_specs=[pl.BlockSpec((MOE_TILE,), lambda i, lo, hi, tile, ex, items: (tile[i],),
                                   memory_space=pltpu.SMEM),
                      pl.BlockSpec((MOE_TILE,), lambda i, lo, hi, tile, ex, items: (
                          tile[jnp.minimum(i + 1, n_slots - 1)],), memory_space=pltpu.SMEM),
                      pl.BlockSpec(memory_space=pltpu.HBM),
                      w_spec((d, de)), w_spec((d, de)), w_spec((de, d))],
            out_specs=pl.BlockSpec((MOE_TILE * TOKEN_TILE_ROWS, LANES),
                                   lambda i, lo, hi, tile, ex, items: (tile[i], 0)),
            scratch_shapes=[pltpu.VMEM((2 * MOE_TILE * TOKEN_TILE_ROWS, LANES), F32),
                            pltpu.SemaphoreType.DMA((2,)),
                            pltpu.VMEM((d, de), BF16), pltpu.VMEM((d, de), BF16), pltpu.VMEM((de, d), BF16)]),
        out_shape=jax.ShapeDtypeStruct((inv.shape[0] * TOKEN_TILE_ROWS, LANES), F32),
        compiler_params=_cparams(("arbitrary",)),
        name="moe_experts",
    )(seg_lo, seg_hi, item_tile, item_exp, n_items, inv, inv, h2, wg, wu, wd)


def _combine_kernel(p1_ref, p2_ref, p1n_ref, p2n_ref, rw_ref, x1_ref, gate_ref, shift_ref, scale_ref,
                    nfin_ref, ys_ref, y_ref, r1_ref, r2_ref, sems, *, n_steps):
    _, rows, d = x1_ref.shape
    s = pl.program_id(0)
    slot = s % 2

    def gathers(pa_ref, pb_ref, slot):
        g1 = _row_gather(ys_ref, lambda t: pa_ref[t], r1_ref, slot * rows, sems.at[slot], rows)
        g2 = _row_gather(ys_ref, lambda t: pb_ref[t], r2_ref, slot * rows, sems.at[slot], rows)
        return g1, g2

    this = gathers(p1_ref, p2_ref, slot)
    nxt = gathers(p1n_ref, p2n_ref, 1 - slot)

    @pl.when(s == 0)
    def _():
        this[0][0]()
        this[1][0]()

    @pl.when(s + 1 < n_steps)
    def _():
        nxt[0][0]()
        nxt[1][0]()

    this[0][1]()
    this[1][1]()

    def per_token(ref):
        per_seq = ref[0]
        seqs = per_seq.shape[0]
        return jnp.broadcast_to(per_seq[:, None, :], (seqs, rows // seqs, d)).reshape(rows, d)

    rw = rw_ref[...]
    w1 = rw[:, 0:1]
    w2 = rw[:, 1:2]
    base = pl.multiple_of(slot * (rows * TOKEN_TILE_ROWS), rows * TOKEN_TILE_ROWS)
    moe = jnp.concatenate(
        [w1 * r1_ref[pl.ds(base + c, rows, stride=TOKEN_TILE_ROWS), :]
         + w2 * r2_ref[pl.ds(base + c, rows, stride=TOKEN_TILE_ROWS), :] for c in range(TOKEN_TILE_ROWS)], axis=-1)
    x2 = x1_ref[0] + per_token(gate_ref) * moe
    hf = _rms(x2) * nfin_ref[...]
    y_ref[0] = hf * (1.0 + per_token(scale_ref)) + per_token(shift_ref)


def _combine(pos1, pos2, rw, x1, ada3, adaf3, nfin, ys, rows, tile0):
    b, t, d = x1.shape
    n_steps = b * t // rows
    seqs = max(rows // t, 1)
    tiles_per_seq = max(t // rows, 1)
    x3 = x1.reshape(n_steps, rows, d)
    ada_v = ada3.reshape(b // seqs, seqs, -1)
    adaf_v = adaf3.reshape(b // seqs, seqs, -1)
    x_spec = pl.BlockSpec((1, rows, d), lambda i: (i, 0, 0))

    def idx_spec(step):
        return pl.BlockSpec((rows,), lambda i: (tile0 + step(i),), memory_space=pltpu.SMEM)

    def ada_spec(col):
        return pl.BlockSpec((1, seqs, d), lambda i: (i // tiles_per_seq, 0, col))

    def cur(i):
        return i

    def nxt(i):
        return jnp.minimum(i + 1, n_steps - 1)

    return pl.pallas_call(
        functools.partial(_combine_kernel, n_steps=n_steps),
        grid=(n_steps,),
        in_specs=[idx_spec(cur), idx_spec(cur), idx_spec(nxt), idx_spec(nxt),
                  pl.BlockSpec((rows, ROUTER_LANES), lambda i: (tile0 + i, 0)),
                  x_spec, ada_spec(5), ada_spec(0), ada_spec(1),
                  pl.BlockSpec((1, d), lambda i: (0, 0)),
                  pl.BlockSpec(memory_space=pltpu.HBM)],
        out_specs=x_spec,
        scratch_shapes=[pltpu.VMEM((2 * rows * TOKEN_TILE_ROWS, LANES), F32),
                        pltpu.VMEM((2 * rows * TOKEN_TILE_ROWS, LANES), F32),
                        pltpu.SemaphoreType.DMA((2,))],
        out_shape=jax.ShapeDtypeStruct(x3.shape, F32),
        compiler_params=_cparams(("arbitrary",)),
        name="moe_combine",
    )(pos1, pos2, pos1, pos2, rw, x3, ada_v, adaf_v, adaf_v, nfin, ys).reshape(x1.shape)


def _mixer(x, ada3, s_h, s_re, s_im, p, *, bb, tt, hgrn_seqs, hgrn_tokens, sequential):
    b, t, d = x.shape
    n = b * t
    q, k, g, v, gs, u = _inproj(x, ada3, p['norm_mix'], p['w_in'], p['lb'], bb, tt)
    oh, s_h_new = _hgrn(q, k, g, v, gs, s_h, p['hgrn_norm'], b, t, hgrn_seqs, hgrn_tokens)
    if sequential:
        xr, xi = s_re.reshape(b, 1, -1), s_im.reshape(b, 1, -1)
        n_tokens = t
    else:
        xr, xi = s_re.reshape(1, b, -1), s_im.reshape(1, b, -1)
        n_tokens = n
    ys, fr, fi = _s5(u, xr, xi, p['ssm_bd'], p['ssm_inc'], p['ssm_m'], p['ssm_a8'], n_tokens, sequential)
    states = (s_h_new[None], fr.reshape(1, b, s_re.shape[-2], s_re.shape[-1]),
              fi.reshape(1, b, s_re.shape[-2], s_re.shape[-1]))
    return oh, ys, states


def kernel(x_prompt, x_sample, c_prompt, c_sample, state_hgrn, state_ssm_re, state_ssm_im, hgrn_lb_logits, w_ada, b_ada, norm_mix, w_in, hgrn_norm, ssm_a_re, ssm_a_im, ssm_log_dt, ssm_b_re, ssm_b_im, ssm_c_re, ssm_c_im, ssm_d, ssm_w_glu, ssm_b_glu, ssm_norm, w_out, norm_ffn, moe_w_group, moe_b_group, moe_w_router, moe_b_router, moe_w_gate, moe_w_up, moe_w_down, w_ada_final, b_ada_final, norm_final):
    depth = w_ada.shape[0]
    assert depth == 1
    d = x_prompt.shape[-1]
    bp = x_prompt.shape[0]
    dh = hgrn_norm.shape[-1]
    dk = dh // HGRN_HEADS
    de = moe_w_gate.shape[-1]
    n_exp = MOE_GROUPS * MOE_PER_GROUP

    lb = jax.nn.softmax(hgrn_lb_logits.astype(F32), axis=0)[0].reshape(1, dh)
    bd, inc, m, a8 = _s5_prepare(ssm_a_re[0], ssm_a_im[0], ssm_log_dt[0], ssm_b_re[0], ssm_b_im[0],
                                 ssm_c_re[0], ssm_c_im[0], ssm_d[0])

    def router_lanes(group_part, expert_part):
        rows = group_part.shape[0]
        return jnp.concatenate(
            [group_part, jnp.zeros((rows, EXPERT_LANE0 - MOE_GROUPS), F32), expert_part,
             jnp.zeros((rows, ROUTER_LANES - EXPERT_LANE0 - n_exp), F32)], axis=1)

    w_rt = router_lanes(moe_w_group[0], moe_w_router[0].transpose(1, 0, 2).reshape(d, n_exp))
    b_rt = router_lanes(moe_b_group[0].reshape(1, MOE_GROUPS), moe_b_router[0].reshape(1, n_exp))
    p = dict(
        lb=lb, norm_mix=norm_mix[0].reshape(1, d), w_in=w_in[0].astype(BF16),
        hgrn_norm=hgrn_norm[0].reshape(1, dh),
        ssm_bd=bd, ssm_inc=inc, ssm_m=m, ssm_a8=a8,
        w_glu=ssm_w_glu[0].astype(BF16), b_glu=ssm_b_glu[0].reshape(1, -1), ssm_norm=ssm_norm[0].reshape(1, -1),
        wo_h=w_out[0, :dh].astype(BF16), wo_s=w_out[0, dh:].astype(BF16),
        norm_ffn=norm_ffn[0].reshape(1, d), w_rt=w_rt.astype(BF16), b_rt=b_rt,
    )
    wg = moe_w_gate[0].reshape(n_exp, d, de)
    wu = moe_w_up[0].reshape(n_exp, d, de)
    wd = moe_w_down[0].reshape(n_exp, de, d)
    nfin = norm_final.reshape(1, d)

    c_all = jnp.concatenate([c_prompt, c_sample], axis=0)
    ada = _silu_linear(c_all, w_ada[0], b_ada[0])
    adaf = _silu_linear(c_all, w_ada_final, b_ada_final)
    ada_p, ada_s = ada[:bp].reshape(bp, 1, -1), ada[bp:].reshape(x_sample.shape[0], 1, -1)
    adaf_p, adaf_s = adaf[:bp].reshape(bp, 1, -1), adaf[bp:].reshape(x_sample.shape[0], 1, -1)

    zeros_h = jnp.zeros((bp, HGRN_HEADS, dk, dk), F32)
    zeros_s = jnp.zeros((bp,) + state_ssm_re.shape[2:], F32)
    oh_p, ys_p, st_p = _mixer(x_prompt, ada_p, zeros_h, zeros_s, zeros_s, p,
                              bb=1, tt=256, hgrn_seqs=1, hgrn_tokens=512, sequential=True)
    oh_s, ys_s, st_s = _mixer(x_sample, ada_s, state_hgrn[0], state_ssm_re[0], state_ssm_im[0], p,
                              bb=32, tt=8, hgrn_seqs=16, hgrn_tokens=8, sequential=False)
    x1_p, x1_s, h2, pair1, pair2, rw, cnt = _outproj(
        x_prompt, x_sample, oh_p, oh_s, ys_p, ys_s, ada_p, ada_s, p['w_glu'], p['b_glu'], p['ssm_norm'],
        p['wo_h'], p['wo_s'], p['norm_ffn'], p['w_rt'], p['b_rt'], TOKEN_ROWS)

    n_pairs = 2 * pair1.shape[0]
    n_slots = n_pairs // MOE_TILE + N_EXPERTS
    seg, seg_end, item_tile, item_exp, n_items = _moe_schedule(cnt, n_slots)
    pos1, pos2 = _positions(seg, pair1, pair2)
    inv = _inverse(pos1, pos2, 1024)
    ys = _experts(seg, seg_end, item_tile, item_exp, n_items, inv, h2, wg, wu, wd)
    tiles_p = x_prompt.shape[0] * x_prompt.shape[1] // TOKEN_ROWS
    y_p = _combine(pos1, pos2, rw, x1_p, ada_p, adaf_p, nfin, ys, TOKEN_ROWS, 0)
    y_s = _combine(pos1, pos2, rw, x1_s, ada_s, adaf_s, nfin, ys, TOKEN_ROWS, tiles_p)
    return (y_p, y_s) + st_p + st_s
```

```python
import functools
import math

import jax
import jax.numpy as jnp
from jax import lax
from jax.experimental import pallas as pl
from jax.experimental.pallas import tpu as pltpu

F32 = jnp.float32
BF16 = jnp.bfloat16
HIGHEST = lax.Precision.HIGHEST

EPS = 1e-6
MAX_REAL = -1e-4
HGRN_HEADS = 4
HGRN_CHUNK = 128
HGRN_SAFE_EXPONENT = 80.0
HGRN_EXACT_BLOCK = 8
SSM_GROUP = 16
SSM_STATE = 64
SSM_CHUNK = 8
SSM_SETS = 4
MOE_GROUPS = 4
MOE_PER_GROUP = 8
N_EXPERTS = MOE_GROUPS * MOE_PER_GROUP
ROUTER_LANES = 128
EXPERT_LANE0 = 32
RANK_BITS = 20
MOE_TILE = 256
LANES = 128
TOKEN_TILE_ROWS = 8
DMA_UNROLL = 8
TOKEN_ROWS = 512
VMEM_LIMIT = 56 * 1024 * 1024


def _cparams(sem):
    return pltpu.CompilerParams(dimension_semantics=sem, vmem_limit_bytes=VMEM_LIMIT)


def _silu(x):
    return x * jax.nn.sigmoid(x)


def _rms(x):
    return x * lax.rsqrt(jnp.mean(x * x, axis=-1, keepdims=True) + EPS)


def _dot(a, b):
    return jnp.dot(a, b, preferred_element_type=F32)


def _dot_nt(a, b):
    return lax.dot_general(a, b, (((1,), (1,)), ((), ())), preferred_element_type=F32)


def _dot_tn(a, b, precision=None):
    return lax.dot_general(a, b, (((0,), (0,)), ((), ())), preferred_element_type=F32,
                           precision=precision)


def _silu_linear_kernel(c_ref, w_ref, b_ref, o_ref):
    a = _silu(c_ref[...]).astype(BF16)
    o_ref[...] = _dot(a, w_ref[...].astype(BF16)) + b_ref[...]


def _silu_linear(c, w, b):
    m, d = c.shape
    n = w.shape[1]
    tn = 1024
    return pl.pallas_call(
        _silu_linear_kernel,
        grid=(n // tn,),
        in_specs=[pl.BlockSpec((m, d), lambda j: (0, 0)),
                  pl.BlockSpec((d, tn), lambda j: (0, j)),
                  pl.BlockSpec((1, tn), lambda j: (0, j))],
        out_specs=pl.BlockSpec((m, tn), lambda j: (0, j)),
        out_shape=jax.ShapeDtypeStruct((m, n), F32),
        compiler_params=_cparams(("parallel",)),
        name="silu_linear",
    )(c, w, b.reshape(1, n))


def _inproj_kernel(x_ref, shift_ref, scale_ref, gain_ref, w_ref, lb_ref,
                   q_ref, k_ref, g_ref, v_ref, gs_ref, u_ref, *, dh):
    bb, tt, d = x_ref.shape
    h = _rms(x_ref[...]) * gain_ref[...]
    h = h * (1.0 + scale_ref[...]) + shift_ref[...]
    proj = _dot(h.reshape(bb * tt, d).astype(BF16), w_ref[...])
    lb = lb_ref[...]
    f = lb + (1.0 - lb) * jax.nn.sigmoid(proj[:, dh:2 * dh])
    q_ref[...] = _silu(proj[:, :dh]) * (float(dh // HGRN_HEADS) ** -0.5)
    k_ref[...] = 1.0 - f
    g_ref[...] = jnp.log(f)
    v_ref[...] = proj[:, 2 * dh:3 * dh]
    gs_ref[...] = _silu(proj[:, 3 * dh:4 * dh])
    for s in range(SSM_SETS):
        u_ref[s] = proj[:, 4 * dh + s * LANES:4 * dh + (s + 1) * LANES]


def _inproj(x, ada3, gain, w_in_bf, lb, bb, tt):
    b, t, d = x.shape
    dh = lb.shape[-1]
    nt = t // tt
    rows = bb * tt
    n = b * t
    row_spec = pl.BlockSpec((rows, dh), lambda i, j: (i * nt + j, 0))
    out = jax.ShapeDtypeStruct((n, dh), F32)
    return pl.pallas_call(
        functools.partial(_inproj_kernel, dh=dh),
        grid=(b // bb, nt),
        in_specs=[pl.BlockSpec((bb, tt, d), lambda i, j: (i, j, 0)),
                  pl.BlockSpec((bb, 1, d), lambda i, j: (i, 0, 0)),
                  pl.BlockSpec((bb, 1, d), lambda i, j: (i, 0, 1)),
                  pl.BlockSpec((1, d), lambda i, j: (0, 0)),
                  pl.BlockSpec(w_in_bf.shape, lambda i, j: (0, 0)),
                  pl.BlockSpec((1, dh), lambda i, j: (0, 0))],
        out_specs=[row_spec] * 5 + [pl.BlockSpec((SSM_SETS, rows, LANES), lambda i, j: (0, i * nt + j, 0))],
        out_shape=[out] * 5 + [jax.ShapeDtypeStruct((SSM_SETS, n, LANES), F32)],
        compiler_params=_cparams(("parallel", "parallel")),
        name="inproj",
    )(x, ada3, ada3, gain, w_in_bf, lb)


def _split3(x):
    hi = x.astype(BF16)
    r1 = x - hi.astype(F32)
    mid = r1.astype(BF16)
    lo = (r1 - mid.astype(F32)).astype(BF16)
    return hi, mid, lo


def _hgrn_kernel(q_ref, k_ref, g_ref, v_ref, gs_ref, s0_ref, hn_ref, o_ref, sf_ref,
                 st_ref, intra_ref, qh_ref, kh_ref, ea_ref, sums_ref, *, tl, nt):
    j = pl.program_id(1)
    rows_total, dh = q_ref.shape
    dk = dh // HGRN_HEADS
    c = HGRN_CHUNK
    seqs = c // tl
    n_chunks = rows_total // c

    @pl.when(j == 0)
    def _():
        st_ref[...] = s0_ref[...]

    r = lax.broadcasted_iota(jnp.int32, (c, c), 0)
    s = lax.broadcasted_iota(jnp.int32, (c, c), 1)
    same_seq = (r // tl) == (s // tl)
    causal = same_seq & (r >= s)
    upto_mid = same_seq & ((s % tl) < tl // 2)
    one_seq = tl == c
    sums = (causal if one_seq else jnp.concatenate([causal, upto_mid, same_seq], axis=0)).astype(BF16)
    ref_rows = 1 if one_seq else c
    eye3 = (lax.broadcasted_iota(jnp.int32, (dk, 3 * dk), 1) % dk
            == lax.broadcasted_iota(jnp.int32, (dk, 3 * dk), 0)).astype(BF16)

    def decay_matrix(e_row):
        parts = jnp.concatenate(_split3(e_row), axis=-1)
        return _dot_nt(eye3, jnp.broadcast_to(parts, (dk, 3 * dk)))

    worst = jnp.float32(0.0)
    for ci in range(n_chunks):
        rows = slice(ci * c, (ci + 1) * c)
        g_parts = _split3(g_ref[rows, :])
        acc = _dot(sums, g_parts[0]) + _dot(sums, g_parts[1]) + _dot(sums, g_parts[2])
        if one_seq:
            a_mid, a_end = acc[c // 2 - 1:c // 2], acc[c - 1:c]
            sums_ref[ci * 3 * c:ci * 3 * c + c, :] = acc
            sums_ref[ci * 3 * c + c:ci * 3 * c + c + 1, :] = a_mid
            sums_ref[ci * 3 * c + 2 * c:ci * 3 * c + 2 * c + 1, :] = a_end
        else:
            a_mid, a_end = acc[c:2 * c], acc[2 * c:]
            sums_ref[ci * 3 * c:(ci + 1) * 3 * c, :] = acc
        worst = jnp.maximum(worst, jnp.max(jnp.maximum(jnp.abs(a_mid), jnp.abs(a_end - a_mid))))
    factorised_is_safe = worst < HGRN_SAFE_EXPONENT

    @pl.when(factorised_is_safe)
    def _():
        for ci in range(n_chunks):
            rows = slice(ci * c, (ci + 1) * c)
            a = sums_ref[ci * 3 * c:ci * 3 * c + c, :]
            a_mid = sums_ref[ci * 3 * c + c:ci * 3 * c + c + ref_rows, :]
            a_end = sums_ref[ci * 3 * c + 2 * c:ci * 3 * c + 2 * c + ref_rows, :]
            e_mid = jnp.exp(a_mid)
            e_tail = jnp.exp(a_end - a_mid)
            qt = q_ref[rows, :] * jnp.exp(a - a_mid)
            kt = k_ref[rows, :] * jnp.exp(a_mid - a)
            qh_ref[rows, :] = qt * e_mid
            kh_ref[rows, :] = kt * e_tail
            ea_ref[rows.start:rows.start + ref_rows, :] = e_mid * e_tail
            qt = qt.astype(BF16)
            kt = kt.astype(BF16)
            v = v_ref[rows, :].astype(BF16)
            for h in range(HGRN_HEADS):
                lanes = slice(h * dk, (h + 1) * dk)
                sc = jnp.where(causal, _dot_nt(qt[:, lanes], kt[:, lanes]), 0.0).astype(BF16)
                intra_ref[rows, lanes] = _dot(sc, v[:, lanes])

            for si in range(seqs):
                seq = ci * seqs + si if tl < c else 0
                srows = slice(si * tl, (si + 1) * tl)
                orows = slice(ci * c + si * tl, ci * c + (si + 1) * tl)
                for h in range(HGRN_HEADS):
                    lanes = slice(h * dk, (h + 1) * dk)
                    state = st_ref[seq, h]
                    o = intra_ref[orows, lanes] + _dot(qh_ref[orows, lanes].astype(BF16), state.astype(BF16))
                    decay = decay_matrix(ea_ref[orows.start:orows.start + 1, lanes])
                    st_ref[seq, h] = decay * state + _dot_tn(kh_ref[orows, lanes].astype(BF16),
                                                             v[srows, lanes])
                    o_ref[orows, lanes] = _rms(o) * hn_ref[:, lanes] * gs_ref[orows, lanes]

    @pl.when(jnp.logical_not(factorised_is_safe))
    def _():
        blk = HGRN_EXACT_BLOCK
        blocks_per_seq = max(tl // blk, 1)
        tri = (lax.broadcasted_iota(jnp.int32, (blk, blk), 0)
               >= lax.broadcasted_iota(jnp.int32, (blk, blk), 1)).astype(BF16)
        sub = lax.broadcasted_iota(jnp.int32, (blk, dk), 0)

        def block(bi, carry):
            rows = pl.ds(pl.multiple_of(bi * blk, blk), blk)
            seq = bi // blocks_per_seq if tl < c else 0
            g_parts = _split3(g_ref[rows, :])
            a = _dot(tri, g_parts[0]) + _dot(tri, g_parts[1]) + _dot(tri, g_parts[2])
            a_end = a[blk - 1:blk]
            q = q_ref[rows, :]
            k = k_ref[rows, :]
            v = v_ref[rows, :]
            qh = (q * jnp.exp(a)).astype(BF16)
            kh = (k * jnp.exp(a_end - a)).astype(BF16)
            ea = jnp.exp(a_end)
            vb = v.astype(BF16)
            for h in range(HGRN_HEADS):
                lanes = slice(h * dk, (h + 1) * dk)
                state = st_ref[seq, h]
                intra = []
                for t in range(blk):
                    live = sub <= t
                    decay_t = jnp.where(live, jnp.exp(jnp.where(live, a[t:t + 1, lanes] - a[:, lanes], 0.0)), 0.0)
                    score = jnp.sum(q[t:t + 1, lanes] * k[:, lanes] * decay_t, axis=-1, keepdims=True)
                    intra.append(jnp.sum(score * v[:, lanes], axis=0, keepdims=True))
                o = jnp.concatenate(intra, axis=0) + _dot(qh[:, lanes], state.astype(BF16))
                st_ref[seq, h] = decay_matrix(ea[:, lanes]) * state + _dot_tn(kh[:, lanes], vb[:, lanes])
                o_ref[rows, lanes] = _rms(o) * hn_ref[:, lanes] * gs_ref[rows, lanes]
            return carry

        lax.fori_loop(0, rows_total // blk, block, 0)

    @pl.when(j == nt - 1)
    def _():
        sf_ref[...] = st_ref[...]


def _hgrn(q, k, g, v, gs, s0, hnorm, b, t, nseq, tt):
    n, dh = q.shape
    nt = t // tt
    dk = dh // HGRN_HEADS
    rows = nseq * tt
    tl = min(tt, HGRN_CHUNK)
    row_spec = pl.BlockSpec((rows, dh), lambda i, j: (i * nt + j, 0))
    st_spec = pl.BlockSpec((nseq, HGRN_HEADS, dk, dk), lambda i, j: (i, 0, 0, 0))
    return pl.pallas_call(
        functools.partial(_hgrn_kernel, tl=tl, nt=nt),
        grid=(b // nseq, nt),
        in_specs=[row_spec] * 5 + [st_spec, pl.BlockSpec((1, dh), lambda i, j: (0, 0))],
        out_specs=[row_spec, st_spec],
        out_shape=[jax.ShapeDtypeStruct((n, dh), F32),
                   jax.ShapeDtypeStruct((b, HGRN_HEADS, dk, dk), F32)],
        scratch_shapes=[pltpu.VMEM((nseq, HGRN_HEADS, dk, dk), F32),
                        pltpu.VMEM((rows, dh), F32), pltpu.VMEM((rows, dh), F32),
                        pltpu.VMEM((rows, dh), F32), pltpu.VMEM((rows, dh), F32),
                        pltpu.VMEM((3 * rows, dh), F32)],
        compiler_params=_cparams(("parallel", "arbitrary")),
        name="hgrn",
    )(q, k, g, v, gs, s0, hnorm)


def _s5_prepare(a_re, a_im, log_dt, b_re, b_im, c_re, c_im, d_skip):
    ng, npp = a_re.shape
    nh = b_re.shape[-1]
    L = SSM_CHUNK
    gs = ng // SSM_SETS
    lam_re = jnp.minimum(a_re, MAX_REAL)
    lam_im = a_im
    dt = jnp.exp(log_dt)
    mag = jnp.exp(lam_re * dt)
    ab_re = mag * jnp.cos(lam_im * dt)
    ab_im = mag * jnp.sin(lam_im * dt)
    den = lam_re * lam_re + lam_im * lam_im
    co_re = ((ab_re - 1.0) * lam_re + ab_im * lam_im) / den
    co_im = (ab_im * lam_re - (ab_re - 1.0) * lam_im) / den
    bb_re = co_re[..., None] * b_re - co_im[..., None] * b_im
    bb_im = co_re[..., None] * b_im + co_im[..., None] * b_re
    tau = jnp.arange(L + 1, dtype=F32)[:, None, None]
    pw_mag = jnp.exp(tau * (lam_re * dt))
    pw_re = pw_mag * jnp.cos(tau * (lam_im * dt))
    pw_im = pw_mag * jnp.sin(tau * (lam_im * dt))
    ab_b_re = pw_re[:L, :, :, None] * bb_re - pw_im[:L, :, :, None] * bb_im
    ab_b_im = pw_re[:L, :, :, None] * bb_im + pw_im[:L, :, :, None] * bb_re
    kern = (jnp.einsum('gkp,lgph->lghk', c_re, ab_b_re, precision=HIGHEST)
            - jnp.einsum('gkp,lgph->lghk', c_im, ab_b_im, precision=HIGHEST))
    skip = d_skip[None, :, :, None] * jnp.eye(nh, dtype=F32)
    kern = kern + jnp.where(jnp.arange(L)[:, None, None, None] == 0, skip, 0.0)

    def group_block_diag(c):
        rows, w = c.shape[-2:]
        repeat = (jnp.arange(gs * w)[None, :] % w == jnp.arange(w)[:, None]).astype(F32)
        tiled = jnp.dot(c, repeat, precision=HIGHEST)
        rg = jnp.arange(rows)[:, None] // (rows // gs)
        cq = jnp.arange(gs * w)[None, :] // w
        return jnp.where(rg == cq, tiled, 0.0).astype(BF16).transpose(1, 0, 2, 3).reshape(
            SSM_SETS, L * rows, gs * w)

    bd = group_block_diag(kern.reshape(L, SSM_SETS, gs * nh, nh)).reshape(SSM_SETS, L, gs * nh, gs * nh)
    n_re = group_block_diag(ab_b_re[::-1].transpose(0, 1, 3, 2).reshape(L, SSM_SETS, gs * nh, npp))
    n_im = group_block_diag(ab_b_im[::-1].transpose(0, 1, 3, 2).reshape(L, SSM_SETS, gs * nh, npp))
    inc = jnp.concatenate([n_re, n_im], axis=-1)
    ca_re = c_re[None] * pw_re[1:, :, None, :] - c_im[None] * pw_im[1:, :, None, :]
    ca_im = c_re[None] * pw_im[1:, :, None, :] + c_im[None] * pw_re[1:, :, None, :]
    m = jnp.concatenate([group_block_diag(ca_re.reshape(L, SSM_SETS, gs * nh, npp)),
                         group_block_diag(-ca_im.reshape(L, SSM_SETS, gs * nh, npp))], axis=-1)
    a8 = jnp.concatenate([pw_re[L].reshape(SSM_SETS, 1, gs * npp),
                          pw_im[L].reshape(SSM_SETS, 1, gs * npp)], axis=-1)
    return bd, inc, m, a8


def _s5_kernel(u_ref, xr_ref, xi_ref, bd_ref, inc_ref, m_ref, a8_ref, y_ref, fr_ref, fi_ref, wt_ref, *, sequential):
    n = u_ref.shape[1] // SSM_CHUNK
    ns = xr_ref.shape[-1]
    u = jnp.concatenate([u_ref[0, pl.ds(s, n, stride=SSM_CHUNK), :] for s in range(SSM_CHUNK)], axis=-1)

    ny = SSM_CHUNK * LANES

    @pl.when(pl.program_id(1) == 0)
    def _():
        wt_ref[:, :ny] = jnp.zeros((wt_ref.shape[0], ny), BF16)
        for s in range(SSM_CHUNK):
            for t in range(s, SSM_CHUNK):
                wt_ref[s * LANES:(s + 1) * LANES, t * LANES:(t + 1) * LANES] = bd_ref[0, t - s]
        wt_ref[:, ny:] = inc_ref[0]

    res = _dot(u.astype(BF16), wt_ref[...])
    y_local = res[:, :ny]
    d_re = res[:, ny:ny + ns]
    d_im = res[:, ny + ns:]
    a_re = a8_ref[0][:, :ns]
    a_im = a8_ref[0][:, ns:]
    x0_re = xr_ref[0]
    x0_im = xi_ref[0]
    if sequential:
        row = lax.broadcasted_iota(jnp.int32, (n, ns), 0)
        first = row == 0
        x_re = d_re + jnp.where(first, a_re * x0_re - a_im * x0_im, 0.0)
        x_im = d_im + jnp.where(first, a_re * x0_im + a_im * x0_re, 0.0)
        p_re, p_im = a_re, a_im
        step = 1
        while step < n:
            s_re = jnp.where(row >= step, pltpu.roll(x_re, step, 0), 0.0)
            s_im = jnp.where(row >= step, pltpu.roll(x_im, step, 0), 0.0)
            x_re, x_im = x_re + p_re * s_re - p_im * s_im, x_im + p_re * s_im + p_im * s_re
            p_re, p_im = p_re * p_re - p_im * p_im, 2.0 * p_re * p_im
            step *= 2
        fr_ref[0] = x_re[n - 1:n]
        fi_ref[0] = x_im[n - 1:n]
        xc_re = jnp.where(first, x0_re, pltpu.roll(x_re, 1, 0))
        xc_im = jnp.where(first, x0_im, pltpu.roll(x_im, 1, 0))
    else:
        xc_re, xc_im = x0_re, x0_im
        fr_ref[0] = a_re * x0_re - a_im * x0_im + d_re
        fi_ref[0] = a_re * x0_im + a_im * x0_re + d_im
    xc = jnp.concatenate([xc_re, xc_im], axis=-1).astype(BF16)
    y = y_local + _dot_nt(xc, m_ref[0])
    for t in range(SSM_CHUNK):
        y_ref[0, pl.ds(t, n, stride=SSM_CHUNK), :] = y[:, t * LANES:(t + 1) * LANES]


def _s5(u, x_re, x_im, bd, inc, m, a8, n_tokens, sequential):
    sets = u.shape[0]
    nb, rb, _ = x_re.shape
    ns = m.shape[1] // 2
    st_spec = pl.BlockSpec((1, rb, ns), lambda gi, i: (i, 0, gi))
    st_shape = jax.ShapeDtypeStruct(x_re.shape, F32)
    tok_spec = pl.BlockSpec((1, n_tokens, LANES), lambda gi, i: (gi, i, 0))
    return pl.pallas_call(
        functools.partial(_s5_kernel, sequential=sequential),
        grid=(sets, nb),
        in_specs=[tok_spec, st_spec, st_spec,
                  pl.BlockSpec((1,) + bd.shape[1:], lambda gi, i: (gi, 0, 0, 0)),
                  pl.BlockSpec((1,) + inc.shape[1:], lambda gi, i: (gi, 0, 0)),
                  pl.BlockSpec((1,) + m.shape[1:], lambda gi, i: (gi, 0, 0)),
                  pl.BlockSpec((1, 1, 2 * ns), lambda gi, i: (gi, 0, 0))],
        out_specs=[tok_spec, st_spec, st_spec],
        out_shape=[jax.ShapeDtypeStruct(u.shape, F32), st_shape, st_shape],
        scratch_shapes=[pltpu.VMEM((inc.shape[1], SSM_CHUNK * LANES + inc.shape[2]), BF16)],
        compiler_params=_cparams(("parallel", "arbitrary")),
        name="s5",
    )(u, x_re, x_im, bd, inc, m, a8)


def _gelu_tanh(x):
    return 0.5 * x * (1.0 + jnp.tanh(math.sqrt(2.0 / math.pi) * (x + 0.044715 * (x * x * x))))


def _outproj_kernel(xp_ref, xs_ref, ohp_ref, ohs_ref, ysp_ref, yss_ref,
                    gate_p_ref, shift_p_ref, scale_p_ref, gate_s_ref, shift_s_ref, scale_s_ref,
                    wglu_ref, bglu_ref, sn_ref, wo_h_ref, wo_s_ref, nf_ref, wr_ref, br_ref,
                    x1p_ref, x1s_ref, h2_ref, pair1_ref, pair2_ref, rw_ref, cnt_ref, *, prompt_tiles):
    is_prompt = pl.program_id(0) < prompt_tiles
    _, rows, d = xp_ref.shape

    def per_token(p_ref, s_ref):
        per_seq = s_ref[0]
        seqs = per_seq.shape[0]
        rep = jnp.broadcast_to(per_seq[:, None, :], (seqs, rows // seqs, d)).reshape(rows, d)
        return jnp.where(is_prompt, p_ref[0], rep)

    x = jnp.where(is_prompt, xp_ref[0], xs_ref[0])
    oh = jnp.where(is_prompt, ohp_ref[...], ohs_ref[...])
    ys = jnp.concatenate([jnp.where(is_prompt, ysp_ref[s], yss_ref[s]) for s in range(SSM_SETS)], axis=-1)
    y = _gelu_tanh(ys)
    y = y * jax.nn.sigmoid(_dot(y.astype(BF16), wglu_ref[...]) + bglu_ref[...])
    o_s = _rms(y) * sn_ref[...]
    mix = _dot(oh.astype(BF16), wo_h_ref[...]) + _dot(o_s.astype(BF16), wo_s_ref[...])
    x1 = x + per_token(gate_p_ref, gate_s_ref) * mix

    @pl.when(is_prompt)
    def _():
        x1p_ref[0] = x1

    @pl.when(jnp.logical_not(is_prompt))
    def _():
        x1s_ref[0] = x1

    h2 = _rms(x1) * nf_ref[...]
    h2 = h2 * (1.0 + per_token(scale_p_ref, scale_s_ref)) + per_token(shift_p_ref, shift_s_ref)
    _store_token_tiles(h2_ref, h2)

    logits = _dot(h2.astype(BF16), wr_ref[...]) + br_ref[...]
    lane = lax.broadcasted_iota(jnp.int32, logits.shape, 1)
    neg = -jnp.inf
    gl = jnp.where(lane < MOE_GROUPS, logits, neg)
    gmax = jnp.max(gl, axis=-1, keepdims=True)
    gidx = jnp.min(jnp.where(gl == gmax, lane, ROUTER_LANES), axis=-1, keepdims=True)
    grp_w = 1.0 / jnp.sum(jnp.exp(gl - gmax), axis=-1, keepdims=True)
    e0 = EXPERT_LANE0 + gidx * MOE_PER_GROUP
    sel = jnp.where((lane >= e0) & (lane < e0 + MOE_PER_GROUP), logits, neg)
    m1 = jnp.max(sel, axis=-1, keepdims=True)
    i1 = jnp.min(jnp.where(sel == m1, lane, ROUTER_LANES), axis=-1, keepdims=True)
    sel2 = jnp.where(lane == i1, neg, sel)
    m2 = jnp.max(sel2, axis=-1, keepdims=True)
    i2 = jnp.min(jnp.where(sel2 == m2, lane, ROUTER_LANES), axis=-1, keepdims=True)
    e2 = jnp.exp(m2 - m1)
    w1 = 1.0 / (1.0 + e2)
    w2 = e2 / (1.0 + e2)
    rw_ref[...] = grp_w * (jnp.where(lane == 0, w1, 0.0) + jnp.where(lane == 1, w2, 0.0))

    @pl.when(pl.program_id(0) == 0)
    def _():
        cnt_ref[...] = jnp.zeros_like(cnt_ref)

    picked = (lane == i1) | (lane == i2)
    earlier = (lax.broadcasted_iota(jnp.int32, (rows, rows), 0)
               > lax.broadcasted_iota(jnp.int32, (rows, rows), 1))
    base = cnt_ref[...]
    before = _dot(earlier.astype(BF16), picked.astype(BF16)) + base
    rank1 = jnp.sum(jnp.where(lane == i1, before, 0.0), axis=-1, keepdims=True).astype(jnp.int32)
    rank2 = jnp.sum(jnp.where(lane == i2, before, 0.0), axis=-1, keepdims=True).astype(jnp.int32)
    cnt_ref[...] = base + jnp.sum(picked.astype(F32), axis=0, keepdims=True)
    info = jnp.where(lane == 0, ((i1 - EXPERT_LANE0) << RANK_BITS) | rank1,
                     jnp.where(lane == 1, ((i2 - EXPERT_LANE0) << RANK_BITS) | rank2, 0)).T
    pair1_ref[...] = info[0]
    pair2_ref[...] = info[1]


def _outproj(x_p, x_s, oh_p, oh_s, ys_p, ys_s, ada_p, ada_s, wglu_bf, bglu, snorm, wo_h, wo_s, nffn, wr, br, rows):
    d = x_p.shape[-1]
    dh = oh_p.shape[-1]
    n_p = x_p.shape[0] * x_p.shape[1]
    n_s = x_s.shape[0] * x_s.shape[1]
    n = n_p + n_s
    tiles_p = n_p // rows
    seqs = rows // x_s.shape[1]

    def pt(i):
        return jnp.minimum(i, tiles_p - 1)

    def st(i):
        return jnp.maximum(i - tiles_p, 0)

    tiles_per_seq = x_p.shape[1] // rows

    def ada_p_spec(col):
        return pl.BlockSpec((1, 1, d), lambda i: (pt(i) // tiles_per_seq, 0, col))

    def ada_s_spec(col):
        return pl.BlockSpec((1, seqs, d), lambda i: (st(i), 0, col))

    def full(a):
        return pl.BlockSpec(a.shape, lambda i: (0,) * a.ndim)

    xp3 = x_p.reshape(tiles_p, rows, d)
    xs3 = x_s.reshape(n_s // rows, rows, d)
    ada_s3 = ada_s.reshape(n_s // rows, seqs, -1)
    outs = pl.pallas_call(
        functools.partial(_outproj_kernel, prompt_tiles=tiles_p),
        grid=(n // rows,),
        in_specs=[pl.BlockSpec((1, rows, d), lambda i: (pt(i), 0, 0)),
                  pl.BlockSpec((1, rows, d), lambda i: (st(i), 0, 0)),
                  pl.BlockSpec((rows, dh), lambda i: (pt(i), 0)),
                  pl.BlockSpec((rows, dh), lambda i: (st(i), 0)),
                  pl.BlockSpec((SSM_SETS, rows, LANES), lambda i: (0, pt(i), 0)),
                  pl.BlockSpec((SSM_SETS, rows, LANES), lambda i: (0, st(i), 0)),
                  ada_p_spec(2), ada_p_spec(3), ada_p_spec(4), ada_s_spec(2), ada_s_spec(3), ada_s_spec(4),
                  full(wglu_bf), full(bglu), full(snorm), full(wo_h), full(wo_s), full(nffn),
                  full(wr), full(br)],
        out_specs=[pl.BlockSpec((1, rows, d), lambda i: (pt(i), 0, 0)),
                   pl.BlockSpec((1, rows, d), lambda i: (st(i), 0, 0)),
                   pl.BlockSpec((rows * TOKEN_TILE_ROWS, LANES), lambda i: (i, 0)),
                   pl.BlockSpec((rows,), lambda i: (i,)),
                   pl.BlockSpec((rows,), lambda i: (i,)),
                   pl.BlockSpec((rows, ROUTER_LANES), lambda i: (i, 0)),
                   pl.BlockSpec((1, ROUTER_LANES), lambda i: (0, 0))],
        out_shape=[jax.ShapeDtypeStruct(xp3.shape, F32),
                   jax.ShapeDtypeStruct(xs3.shape, F32),
                   jax.ShapeDtypeStruct((n * TOKEN_TILE_ROWS, LANES), F32),
                   jax.ShapeDtypeStruct((n,), jnp.int32),
                   jax.ShapeDtypeStruct((n,), jnp.int32),
                   jax.ShapeDtypeStruct((n, ROUTER_LANES), F32),
                   jax.ShapeDtypeStruct((1, ROUTER_LANES), F32)],
        compiler_params=_cparams(("arbitrary",)),
        name="outproj",
    )(xp3, xs3, oh_p, oh_s, ys_p, ys_s, ada_p, ada_p, ada_p, ada_s3, ada_s3, ada_s3,
      wglu_bf, bglu, snorm, wo_h, wo_s, nffn, wr, br)
    x1_p, x1_s = outs[0].reshape(x_p.shape), outs[1].reshape(x_s.shape)
    return (x1_p, x1_s) + tuple(outs[2:])


def _moe_schedule(cnt, n_slots):
    c = cnt[0, EXPERT_LANE0:EXPERT_LANE0 + N_EXPERTS].astype(jnp.int32)
    seg_end = jnp.cumsum(c)
    seg_start = seg_end - c
    first_tile = seg_start // MOE_TILE
    tiles = jnp.where(c > 0, (seg_end - 1) // MOE_TILE - first_tile + 1, 0)
    cum = jnp.cumsum(tiles)
    n_items = cum[-1]
    item = jnp.minimum(jnp.arange(n_slots, dtype=jnp.int32), n_items - 1)
    item_exp = jnp.sum(item[:, None] >= cum[None, :], axis=1).astype(jnp.int32)
    shares = ((c > 0) & (seg_start % MOE_TILE != 0)).astype(jnp.int32)
    item_tile = item - jnp.sum((item[:, None] >= (cum - tiles)[None, :]) * shares[None, :], axis=1)
    return seg_start.astype(jnp.int32), seg_end.astype(jnp.int32), item_tile.astype(jnp.int32), item_exp


def _token_tile(ref, t):
    return ref.at[pl.ds(pl.multiple_of(t * TOKEN_TILE_ROWS, TOKEN_TILE_ROWS), TOKEN_TILE_ROWS)]


def _store_token_tiles(ref, x):
    rows = x.shape[0]
    for c in range(TOKEN_TILE_ROWS):
        ref[pl.ds(c, rows, stride=TOKEN_TILE_ROWS), :] = x[:, c * LANES:(c + 1) * LANES]


def _load_token_tiles(ref, rows):
    return [ref[pl.ds(c, rows, stride=TOKEN_TILE_ROWS), :] for c in range(TOKEN_TILE_ROWS)]


def _positions_kernel(seg_ref, p1_ref, p2_ref, o1_ref, o2_ref):
    for p_ref, o_ref in ((p1_ref, o1_ref), (p2_ref, o2_ref)):
        pair = p_ref[...]
        expert = pair >> RANK_BITS
        pos = pair & ((1 << RANK_BITS) - 1)
        for e in range(N_EXPERTS):
            pos = pos + jnp.where(expert == e, seg_ref[e], 0)
        o_ref[...] = pos


def _positions(seg, pair1, pair2):
    n = pair1.shape[0]
    shape2 = (n // LANES, LANES)
    full = pl.BlockSpec(shape2, lambda: (0, 0))
    pos1, pos2 = pl.pallas_call(
        _positions_kernel,
        in_specs=[pl.BlockSpec(memory_space=pltpu.SMEM), full, full],
        out_specs=[full, full],
        out_shape=[jax.ShapeDtypeStruct(shape2, jnp.int32)] * 2,
        compiler_params=pltpu.CompilerParams(vmem_limit_bytes=VMEM_LIMIT),
        name="moe_positions",
    )(seg, pair1.reshape(shape2), pair2.reshape(shape2))
    return pos1.reshape(n), pos2.reshape(n)


def _inverse_kernel(p1_ref, p2_ref, inv_ref):
    ts = p1_ref.shape[0]
    base = pl.program_id(0) * ts

    def body(t, carry):
        inv_ref[p1_ref[t]] = base + t
        inv_ref[p2_ref[t]] = base + t
        return carry

    lax.fori_loop(0, ts, body, 0, unroll=2 * DMA_UNROLL)


def _inverse(pos1, pos2, ts):
    n = pos1.shape[0]
    idx_spec = pl.BlockSpec((ts,), lambda i: (i,), memory_space=pltpu.SMEM)
    return pl.pallas_call(
        _inverse_kernel,
        grid=(n // ts,),
        in_specs=[idx_spec, idx_spec],
        out_specs=pl.BlockSpec(memory_space=pltpu.SMEM),
        out_shape=jax.ShapeDtypeStruct((2 * n,), jnp.int32),
        compiler_params=_cparams(("arbitrary",)),
        name="moe_inverse",
    )(pos1, pos2)


def _row_gather(src_ref, idx_of, dst_ref, dst_row0, sem, n_rows):
    def copy(t):
        return pltpu.make_async_copy(_token_tile(src_ref, idx_of(t)), _token_tile(dst_ref, dst_row0 + t), sem)

    def start():
        def body(t, carry):
            copy(2 * t).start(priority=0)
            copy(2 * t + 1).start(priority=1)
            return carry
        lax.fori_loop(0, n_rows // 2, body, 0, unroll=DMA_UNROLL // 2)

    def wait():
        def body(t, carry):
            copy(t).wait()
            return carry
        lax.fori_loop(0, n_rows, body, 0, unroll=DMA_UNROLL)

    return start, wait


def _experts_kernel(lo_ref, hi_ref, tile_ref, exp_ref, inv_ref, inv_next_ref, h_ref,
                    wg_ref, wu_ref, wd_ref, ys_ref, xbuf_ref, sems, wg_s, wu_s, wd_s, *, n_slots):
    i = pl.program_id(0)
    prev = jnp.maximum(i - 1, 0)
    e = exp_ref[i]
    slot = i % 2

    @pl.when((i == 0) | (e != exp_ref[prev]))
    def _():
        wg_s[...] = wg_ref[0].astype(BF16)
        wu_s[...] = wu_ref[0].astype(BF16)
        wd_s[...] = wd_ref[0].astype(BF16)

    start_this, wait_this = _row_gather(h_ref, lambda t: inv_ref[t], xbuf_ref, slot * MOE_TILE,
                                        sems.at[slot], MOE_TILE)
    _, wait_next = _row_gather(h_ref, lambda t: inv_next_ref[t], xbuf_ref, (1 - slot) * MOE_TILE,
                               sems.at[1 - slot], MOE_TILE)

    def start_next(rows):
        for t in rows:
            pltpu.make_async_copy(_token_tile(h_ref, inv_next_ref[t]),
                                  _token_tile(xbuf_ref, (1 - slot) * MOE_TILE + t),
                                  sems.at[1 - slot]).start(priority=t % 2)

    @pl.when(i == 0)
    def _():
        start_this()

    wait_this()
    third = -(-MOE_TILE // 3)
    base = pl.multiple_of(slot * (MOE_TILE * TOKEN_TILE_ROWS), MOE_TILE * TOKEN_TILE_ROWS)
    x = jnp.concatenate([xbuf_ref[pl.ds(base + c, MOE_TILE, stride=TOKEN_TILE_ROWS), :]
                         for c in range(TOKEN_TILE_ROWS)], axis=-1).astype(BF16)
    start_next(range(0, third))
    gate = _dot(x, wg_s[...])
    start_next(range(third, 2 * third))
    up = _dot(x, wu_s[...])
    start_next(range(2 * third, MOE_TILE))
    out = _dot((_silu(gate) * up).astype(BF16), wd_s[...])
    row = tile_ref[i] * MOE_TILE + lax.broadcasted_iota(jnp.int32, (MOE_TILE, 1), 0)
    mine = (row >= lo_ref[e]) & (row < hi_ref[e])
    first_visit = (i == 0) | (tile_ref[i] != tile_ref[prev])

    @pl.when(first_visit)
    def _():
        _store_token_tiles(ys_ref, jnp.where(mine, out, 0.0))

    @pl.when(jnp.logical_not(first_visit))
    def _():
        old = jnp.concatenate(_load_token_tiles(ys_ref, MOE_TILE), axis=-1)
        _store_token_tiles(ys_ref, jnp.where(mine, out, old))

    @pl.when(i == n_slots - 1)
    def _():
        wait_next()


def _experts(seg_lo, seg_hi, item_tile, item_exp, inv, h2, wg, wu, wd):
    n_slots = item_tile.shape[0]
    _, d, de = wg.shape

    def w_spec(shape):
        return pl.BlockSpec((1,) + shape, lambda i, lo, hi, tile, ex: (ex[i], 0, 0))

    return pl.pallas_call(
        functools.partial(_experts_kernel, n_slots=n_slots),
        grid_spec=pltpu.PrefetchScalarGridSpec(
            num_scalar_prefetch=4, grid=(n_slots,),
            in_specs=[pl.BlockSpec((MOE_TILE,), lambda i, lo, hi, tile, ex: (tile[i],),
                                   memory_space=pltpu.SMEM),
                      pl.BlockSpec((MOE_TILE,), lambda i, lo, hi, tile, ex: (
                          tile[jnp.minimum(i + 1, n_slots - 1)],), memory_space=pltpu.SMEM),
                      pl.BlockSpec(memory_space=pltpu.HBM),
                      w_spec((d, de)), w_spec((d, de)), w_spec((de, d))],
            out_specs=pl.BlockSpec((MOE_TILE * TOKEN_TILE_ROWS, LANES),
                                   lambda i, lo, hi, tile, ex: (tile[i], 0)),
            scratch_shapes=[pltpu.VMEM((2 * MOE_TILE * TOKEN_TILE_ROWS, LANES), F32),
                            pltpu.SemaphoreType.DMA((2,)),
                            pltpu.VMEM((d, de), BF16), pltpu.VMEM((d, de), BF16), pltpu.VMEM((de, d), BF16)]),
        out_shape=jax.ShapeDtypeStruct((inv.shape[0] * TOKEN_TILE_ROWS, LANES), F32),
        compiler_params=_cparams(("arbitrary",)),
        name="moe_experts",
    )(seg_lo, seg_hi, item_tile, item_exp, inv, inv, h2, wg, wu, wd)


def _combine_kernel(p1_ref, p2_ref, p1n_ref, p2n_ref, rw_ref, x1_ref, gate_ref, shift_ref, scale_ref,
                    nfin_ref, ys_ref, y_ref, r1_ref, r2_ref, sems, *, n_steps):
    _, rows, d = x1_ref.shape
    s = pl.program_id(0)
    slot = s % 2

    def gathers(pa_ref, pb_ref, slot):
        g1 = _row_gather(ys_ref, lambda t: pa_ref[t], r1_ref, slot * rows, sems.at[slot], rows)
        g2 = _row_gather(ys_ref, lambda t: pb_ref[t], r2_ref, slot * rows, sems.at[slot], rows)
        return g1, g2

    this = gathers(p1_ref, p2_ref, slot)
    nxt = gathers(p1n_ref, p2n_ref, 1 - slot)

    @pl.when(s == 0)
    def _():
        this[0][0]()
        this[1][0]()

    @pl.when(s + 1 < n_steps)
    def _():
        nxt[0][0]()
        nxt[1][0]()

    this[0][1]()
    this[1][1]()

    def per_token(ref):
        per_seq = ref[0]
        seqs = per_seq.shape[0]
        return jnp.broadcast_to(per_seq[:, None, :], (seqs, rows // seqs, d)).reshape(rows, d)

    rw = rw_ref[...]
    w1 = rw[:, 0:1]
    w2 = rw[:, 1:2]
    base = pl.multiple_of(slot * (rows * TOKEN_TILE_ROWS), rows * TOKEN_TILE_ROWS)
    moe = jnp.concatenate(
        [w1 * r1_ref[pl.ds(base + c, rows, stride=TOKEN_TILE_ROWS), :]
         + w2 * r2_ref[pl.ds(base + c, rows, stride=TOKEN_TILE_ROWS), :] for c in range(TOKEN_TILE_ROWS)], axis=-1)
    x2 = x1_ref[0] + per_token(gate_ref) * moe
    hf = _rms(x2) * nfin_ref[...]
    y_ref[0] = hf * (1.0 + per_token(scale_ref)) + per_token(shift_ref)


def _combine(pos1, pos2, rw, x1, ada3, adaf3, nfin, ys, rows, tile0):
    b, t, d = x1.shape
    n_steps = b * t // rows
    seqs = max(rows // t, 1)
    tiles_per_seq = max(t // rows, 1)
    x3 = x1.reshape(n_steps, rows, d)
    ada_v = ada3.reshape(b // seqs, seqs, -1)
    adaf_v = adaf3.reshape(b // seqs, seqs, -1)
    x_spec = pl.BlockSpec((1, rows, d), lambda i: (i, 0, 0))

    def idx_spec(step):
        return pl.BlockSpec((rows,), lambda i: (tile0 + step(i),), memory_space=pltpu.SMEM)

    def ada_spec(col):
        return pl.BlockSpec((1, seqs, d), lambda i: (i // tiles_per_seq, 0, col))

    def cur(i):
        return i

    def nxt(i):
        return jnp.minimum(i + 1, n_steps - 1)

    return pl.pallas_call(
        functools.partial(_combine_kernel, n_steps=n_steps),
        grid=(n_steps,),
        in_specs=[idx_spec(cur), idx_spec(cur), idx_spec(nxt), idx_spec(nxt),
                  pl.BlockSpec((rows, ROUTER_LANES), lambda i: (tile0 + i, 0)),
                  x_spec, ada_spec(5), ada_spec(0), ada_spec(1),
                  pl.BlockSpec((1, d), lambda i: (0, 0)),
                  pl.BlockSpec(memory_space=pltpu.HBM)],
        out_specs=x_spec,
        scratch_shapes=[pltpu.VMEM((2 * rows * TOKEN_TILE_ROWS, LANES), F32),
                        pltpu.VMEM((2 * rows * TOKEN_TILE_ROWS, LANES), F32),
                        pltpu.SemaphoreType.DMA((2,))],
        out_shape=jax.ShapeDtypeStruct(x3.shape, F32),
        compiler_params=_cparams(("arbitrary",)),
        name="moe_combine",
    )(pos1, pos2, pos1, pos2, rw, x3, ada_v, adaf_v, adaf_v, nfin, ys).reshape(x1.shape)


def _mixer(x, ada3, s_h, s_re, s_im, p, *, bb, tt, hgrn_seqs, hgrn_tokens, sequential):
    b, t, d = x.shape
    n = b * t
    q, k, g, v, gs, u = _inproj(x, ada3, p['norm_mix'], p['w_in'], p['lb'], bb, tt)
    oh, s_h_new = _hgrn(q, k, g, v, gs, s_h, p['hgrn_norm'], b, t, hgrn_seqs, hgrn_tokens)
    if sequential:
        xr, xi = s_re.reshape(b, 1, -1), s_im.reshape(b, 1, -1)
        n_tokens = t
    else:
        xr, xi = s_re.reshape(1, b, -1), s_im.reshape(1, b, -1)
        n_tokens = n
    ys, fr, fi = _s5(u, xr, xi, p['ssm_bd'], p['ssm_inc'], p['ssm_m'], p['ssm_a8'], n_tokens, sequential)
    states = (s_h_new[None], fr.reshape(1, b, s_re.shape[-2], s_re.shape[-1]),
              fi.reshape(1, b, s_re.shape[-2], s_re.shape[-1]))
    return oh, ys, states


def kernel(x_prompt, x_sample, c_prompt, c_sample, state_hgrn, state_ssm_re, state_ssm_im, hgrn_lb_logits, w_ada, b_ada, norm_mix, w_in, hgrn_norm, ssm_a_re, ssm_a_im, ssm_log_dt, ssm_b_re, ssm_b_im, ssm_c_re, ssm_c_im, ssm_d, ssm_w_glu, ssm_b_glu, ssm_norm, w_out, norm_ffn, moe_w_group, moe_b_group, moe_w_router, moe_b_router, moe_w_gate, moe_w_up, moe_w_down, w_ada_final, b_ada_final, norm_final):
    depth = w_ada.shape[0]
    assert depth == 1
    d = x_prompt.shape[-1]
    bp = x_prompt.shape[0]
    dh = hgrn_norm.shape[-1]
    dk = dh // HGRN_HEADS
    de = moe_w_gate.shape[-1]
    n_exp = MOE_GROUPS * MOE_PER_GROUP

    lb = jax.nn.softmax(hgrn_lb_logits.astype(F32), axis=0)[0].reshape(1, dh)
    bd, inc, m, a8 = _s5_prepare(ssm_a_re[0], ssm_a_im[0], ssm_log_dt[0], ssm_b_re[0], ssm_b_im[0],
                                 ssm_c_re[0], ssm_c_im[0], ssm_d[0])

    def router_lanes(group_part, expert_part):
        rows = group_part.shape[0]
        return jnp.concatenate(
            [group_part, jnp.zeros((rows, EXPERT_LANE0 - MOE_GROUPS), F32), expert_part,
             jnp.zeros((rows, ROUTER_LANES - EXPERT_LANE0 - n_exp), F32)], axis=1)

    w_rt = router_lanes(moe_w_group[0], moe_w_router[0].transpose(1, 0, 2).reshape(d, n_exp))
    b_rt = router_lanes(moe_b_group[0].reshape(1, MOE_GROUPS), moe_b_router[0].reshape(1, n_exp))
    p = dict(
        lb=lb, norm_mix=norm_mix[0].reshape(1, d), w_in=w_in[0].astype(BF16),
        hgrn_norm=hgrn_norm[0].reshape(1, dh),
        ssm_bd=bd, ssm_inc=inc, ssm_m=m, ssm_a8=a8,
        w_glu=ssm_w_glu[0].astype(BF16), b_glu=ssm_b_glu[0].reshape(1, -1), ssm_norm=ssm_norm[0].reshape(1, -1),
        wo_h=w_out[0, :dh].astype(BF16), wo_s=w_out[0, dh:].astype(BF16),
        norm_ffn=norm_ffn[0].reshape(1, d), w_rt=w_rt.astype(BF16), b_rt=b_rt,
    )
    wg = moe_w_gate[0].reshape(n_exp, d, de)
    wu = moe_w_up[0].reshape(n_exp, d, de)
    wd = moe_w_down[0].reshape(n_exp, de, d)
    nfin = norm_final.reshape(1, d)

    c_all = jnp.concatenate([c_prompt, c_sample], axis=0)
    ada = _silu_linear(c_all, w_ada[0], b_ada[0])
    adaf = _silu_linear(c_all, w_ada_final, b_ada_final)
    ada_p, ada_s = ada[:bp].reshape(bp, 1, -1), ada[bp:].reshape(x_sample.shape[0], 1, -1)
    adaf_p, adaf_s = adaf[:bp].reshape(bp, 1, -1), adaf[bp:].reshape(x_sample.shape[0], 1, -1)

    zeros_h = jnp.zeros((bp, HGRN_HEADS, dk, dk), F32)
    zeros_s = jnp.zeros((bp,) + state_ssm_re.shape[2:], F32)
    oh_p, ys_p, st_p = _mixer(x_prompt, ada_p, zeros_h, zeros_s, zeros_s, p,
                              bb=1, tt=256, hgrn_seqs=1, hgrn_tokens=512, sequential=True)
    oh_s, ys_s, st_s = _mixer(x_sample, ada_s, state_hgrn[0], state_ssm_re[0], state_ssm_im[0], p,
                              bb=32, tt=8, hgrn_seqs=16, hgrn_tokens=8, sequential=False)
    x1_p, x1_s, h2, pair1, pair2, rw, cnt = _outproj(
        x_prompt, x_sample, oh_p, oh_s, ys_p, ys_s, ada_p, ada_s, p['w_glu'], p['b_glu'], p['ssm_norm'],
        p['wo_h'], p['wo_s'], p['norm_ffn'], p['w_rt'], p['b_rt'], TOKEN_ROWS)

    n_pairs = 2 * pair1.shape[0]
    n_slots = n_pairs // MOE_TILE + N_EXPERTS
    seg, seg_end, item_tile, item_exp = _moe_schedule(cnt, n_slots)
    pos1, pos2 = _positions(seg, pair1, pair2)
    inv = _inverse(pos1, pos2, 1024)
    ys = _experts(seg, seg_end, item_tile, item_exp, inv, h2, wg, wu, wd)
    tiles_p = x_prompt.shape[0] * x_prompt.shape[1] // TOKEN_ROWS
    y_p = _combine(pos1, pos2, rw, x1_p, ada_p, adaf_p, nfin, ys, TOKEN_ROWS, 0)
    y_s = _combine(pos1, pos2, rw, x1_s, ada_s, adaf_s, nfin, ys, TOKEN_ROWS, tiles_p)
    return (y_p, y_s) + st_p + st_s
```

```python
import functools
import math

import jax
import jax.numpy as jnp
from jax import lax
from jax.experimental import pallas as pl
from jax.experimental.pallas import tpu as pltpu

F32 = jnp.float32
BF16 = jnp.bfloat16
HIGHEST = lax.Precision.HIGHEST

EPS = 1e-6
MAX_REAL = -1e-4
HGRN_HEADS = 4
HGRN_CHUNK = 128
HGRN_SAFE_EXPONENT = 80.0
HGRN_EXACT_BLOCK = 8
SSM_GROUP = 16
SSM_STATE = 64
SSM_CHUNK = 8
SSM_SETS = 4
MOE_GROUPS = 4
MOE_PER_GROUP = 8
N_EXPERTS = MOE_GROUPS * MOE_PER_GROUP
ROUTER_LANES = 128
EXPERT_LANE0 = 32
RANK_BITS = 20
MOE_TILE = 256
LANES = 128
TOKEN_TILE_ROWS = 8
DMA_UNROLL = 8
TOKEN_ROWS = 512
VMEM_LIMIT = 56 * 1024 * 1024


def _cparams(sem):
    return pltpu.CompilerParams(dimension_semantics=sem, vmem_limit_bytes=VMEM_LIMIT)


def _silu(x):
    return x * jax.nn.sigmoid(x)


def _rms(x):
    return x * lax.rsqrt(jnp.mean(x * x, axis=-1, keepdims=True) + EPS)


def _dot(a, b):
    return jnp.dot(a, b, preferred_element_type=F32)


def _dot_nt(a, b):
    return lax.dot_general(a, b, (((1,), (1,)), ((), ())), preferred_element_type=F32)


def _dot_tn(a, b, precision=None):
    return lax.dot_general(a, b, (((0,), (0,)), ((), ())), preferred_element_type=F32,
                           precision=precision)


def _silu_linear_kernel(c_ref, w_ref, b_ref, o_ref):
    a = _silu(c_ref[...]).astype(BF16)
    o_ref[...] = _dot(a, w_ref[...].astype(BF16)) + b_ref[...]


def _silu_linear(c, w, b):
    m, d = c.shape
    n = w.shape[1]
    tn = 1024
    return pl.pallas_call(
        _silu_linear_kernel,
        grid=(n // tn,),
        in_specs=[pl.BlockSpec((m, d), lambda j: (0, 0)),
                  pl.BlockSpec((d, tn), lambda j: (0, j)),
                  pl.BlockSpec((1, tn), lambda j: (0, j))],
        out_specs=pl.BlockSpec((m, tn), lambda j: (0, j)),
        out_shape=jax.ShapeDtypeStruct((m, n), F32),
        compiler_params=_cparams(("parallel",)),
        name="silu_linear",
    )(c, w, b.reshape(1, n))


def _inproj_kernel(x_ref, shift_ref, scale_ref, gain_ref, w_ref, lb_ref,
                   q_ref, k_ref, g_ref, v_ref, gs_ref, u_ref, *, dh):
    bb, tt, d = x_ref.shape
    h = _rms(x_ref[...]) * gain_ref[...]
    h = h * (1.0 + scale_ref[...]) + shift_ref[...]
    proj = _dot(h.reshape(bb * tt, d).astype(BF16), w_ref[...])
    lb = lb_ref[...]
    f = lb + (1.0 - lb) * jax.nn.sigmoid(proj[:, dh:2 * dh])
    q_ref[...] = _silu(proj[:, :dh]) * (float(dh // HGRN_HEADS) ** -0.5)
    k_ref[...] = 1.0 - f
    g_ref[...] = jnp.log(f)
    v_ref[...] = proj[:, 2 * dh:3 * dh]
    gs_ref[...] = _silu(proj[:, 3 * dh:4 * dh])
    for s in range(SSM_SETS):
        u_ref[s] = proj[:, 4 * dh + s * LANES:4 * dh + (s + 1) * LANES]


def _inproj(x, ada3, gain, w_in_bf, lb, bb, tt):
    b, t, d = x.shape
    dh = lb.shape[-1]
    nt = t // tt
    rows = bb * tt
    n = b * t
    row_spec = pl.BlockSpec((rows, dh), lambda i, j: (i * nt + j, 0))
    out = jax.ShapeDtypeStruct((n, dh), F32)
    return pl.pallas_call(
        functools.partial(_inproj_kernel, dh=dh),
        grid=(b // bb, nt),
        in_specs=[pl.BlockSpec((bb, tt, d), lambda i, j: (i, j, 0)),
                  pl.BlockSpec((bb, 1, d), lambda i, j: (i, 0, 0)),
                  pl.BlockSpec((bb, 1, d), lambda i, j: (i, 0, 1)),
                  pl.BlockSpec((1, d), lambda i, j: (0, 0)),
                  pl.BlockSpec(w_in_bf.shape, lambda i, j: (0, 0)),
                  pl.BlockSpec((1, dh), lambda i, j: (0, 0))],
        out_specs=[row_spec] * 5 + [pl.BlockSpec((SSM_SETS, rows, LANES), lambda i, j: (0, i * nt + j, 0))],
        out_shape=[out] * 5 + [jax.ShapeDtypeStruct((SSM_SETS, n, LANES), F32)],
        compiler_params=_cparams(("parallel", "parallel")),
        name="inproj",
    )(x, ada3, ada3, gain, w_in_bf, lb)


def _split3(x):
    hi = x.astype(BF16)
    r1 = x - hi.astype(F32)
    mid = r1.astype(BF16)
    lo = (r1 - mid.astype(F32)).astype(BF16)
    return hi, mid, lo


def _hgrn_kernel(q_ref, k_ref, g_ref, v_ref, gs_ref, s0_ref, hn_ref, o_ref, sf_ref,
                 st_ref, intra_ref, qh_ref, kh_ref, ea_ref, sums_ref, *, tl, nt):
    j = pl.program_id(1)
    rows_total, dh = q_ref.shape
    dk = dh // HGRN_HEADS
    c = HGRN_CHUNK
    seqs = c // tl
    n_chunks = rows_total // c

    @pl.when(j == 0)
    def _():
        st_ref[...] = s0_ref[...]

    r = lax.broadcasted_iota(jnp.int32, (c, c), 0)
    s = lax.broadcasted_iota(jnp.int32, (c, c), 1)
    same_seq = (r // tl) == (s // tl)
    causal = same_seq & (r >= s)
    upto_mid = same_seq & ((s % tl) < tl // 2)
    one_seq = tl == c
    sums = (causal if one_seq else jnp.concatenate([causal, upto_mid, same_seq], axis=0)).astype(BF16)
    ref_rows = 1 if one_seq else c
    eye3 = (lax.broadcasted_iota(jnp.int32, (dk, 3 * dk), 1) % dk
            == lax.broadcasted_iota(jnp.int32, (dk, 3 * dk), 0)).astype(BF16)

    def decay_matrix(e_row):
        parts = jnp.concatenate(_split3(e_row), axis=-1)
        return _dot_nt(eye3, jnp.broadcast_to(parts, (dk, 3 * dk)))

    worst = jnp.float32(0.0)
    for ci in range(n_chunks):
        rows = slice(ci * c, (ci + 1) * c)
        g_parts = _split3(g_ref[rows, :])
        acc = _dot(sums, g_parts[0]) + _dot(sums, g_parts[1]) + _dot(sums, g_parts[2])
        if one_seq:
            a_mid, a_end = acc[c // 2 - 1:c // 2], acc[c - 1:c]
            sums_ref[ci * 3 * c:ci * 3 * c + c, :] = acc
            sums_ref[ci * 3 * c + c:ci * 3 * c + c + 1, :] = a_mid
            sums_ref[ci * 3 * c + 2 * c:ci * 3 * c + 2 * c + 1, :] = a_end
        else:
            a_mid, a_end = acc[c:2 * c], acc[2 * c:]
            sums_ref[ci * 3 * c:(ci + 1) * 3 * c, :] = acc
        worst = jnp.maximum(worst, jnp.max(jnp.maximum(jnp.abs(a_mid), jnp.abs(a_end - a_mid))))
    factorised_is_safe = worst < HGRN_SAFE_EXPONENT

    @pl.when(factorised_is_safe)
    def _():
        for ci in range(n_chunks):
            rows = slice(ci * c, (ci + 1) * c)
            a = sums_ref[ci * 3 * c:ci * 3 * c + c, :]
            a_mid = sums_ref[ci * 3 * c + c:ci * 3 * c + c + ref_rows, :]
            a_end = sums_ref[ci * 3 * c + 2 * c:ci * 3 * c + 2 * c + ref_rows, :]
            e_mid = jnp.exp(a_mid)
            e_tail = jnp.exp(a_end - a_mid)
            qt = q_ref[rows, :] * jnp.exp(a - a_mid)
            kt = k_ref[rows, :] * jnp.exp(a_mid - a)
            qh_ref[rows, :] = qt * e_mid
            kh_ref[rows, :] = kt * e_tail
            ea_ref[rows.start:rows.start + ref_rows, :] = e_mid * e_tail
            qt = qt.astype(BF16)
            kt = kt.astype(BF16)
            v = v_ref[rows, :].astype(BF16)
            for h in range(HGRN_HEADS):
                lanes = slice(h * dk, (h + 1) * dk)
                sc = jnp.where(causal, _dot_nt(qt[:, lanes], kt[:, lanes]), 0.0).astype(BF16)
                intra_ref[rows, lanes] = _dot(sc, v[:, lanes])

            for si in range(seqs):
                seq = ci * seqs + si if tl < c else 0
                srows = slice(si * tl, (si + 1) * tl)
                orows = slice(ci * c + si * tl, ci * c + (si + 1) * tl)
                for h in range(HGRN_HEADS):
                    lanes = slice(h * dk, (h + 1) * dk)
                    state = st_ref[seq, h]
                    o = intra_ref[orows, lanes] + _dot(qh_ref[orows, lanes].astype(BF16), state.astype(BF16))
                    decay = decay_matrix(ea_ref[orows.start:orows.start + 1, lanes])
                    st_ref[seq, h] = decay * state + _dot_tn(kh_ref[orows, lanes].astype(BF16),
                                                             v[srows, lanes])
                    o_ref[orows, lanes] = _rms(o) * hn_ref[:, lanes] * gs_ref[orows, lanes]

    @pl.when(jnp.logical_not(factorised_is_safe))
    def _():
        blk = HGRN_EXACT_BLOCK
        blocks_per_seq = max(tl // blk, 1)
        tri = (lax.broadcasted_iota(jnp.int32, (blk, blk), 0)
               >= lax.broadcasted_iota(jnp.int32, (blk, blk), 1)).astype(BF16)
        sub = lax.broadcasted_iota(jnp.int32, (blk, dk), 0)

        def block(bi, carry):
            rows = pl.ds(pl.multiple_of(bi * blk, blk), blk)
            seq = bi // blocks_per_seq if tl < c else 0
            g_parts = _split3(g_ref[rows, :])
            a = _dot(tri, g_parts[0]) + _dot(tri, g_parts[1]) + _dot(tri, g_parts[2])
            a_end = a[blk - 1:blk]
            q = q_ref[rows, :]
            k = k_ref[rows, :]
            v = v_ref[rows, :]
            qh = (q * jnp.exp(a)).astype(BF16)
            kh = (k * jnp.exp(a_end - a)).astype(BF16)
            ea = jnp.exp(a_end)
            vb = v.astype(BF16)
            for h in range(HGRN_HEADS):
                lanes = slice(h * dk, (h + 1) * dk)
                state = st_ref[seq, h]
                intra = []
                for t in range(blk):
                    live = sub <= t
                    decay_t = jnp.where(live, jnp.exp(jnp.where(live, a[t:t + 1, lanes] - a[:, lanes], 0.0)), 0.0)
                    score = jnp.sum(q[t:t + 1, lanes] * k[:, lanes] * decay_t, axis=-1, keepdims=True)
                    intra.append(jnp.sum(score * v[:, lanes], axis=0, keepdims=True))
                o = jnp.concatenate(intra, axis=0) + _dot(qh[:, lanes], state.astype(BF16))
                st_ref[seq, h] = decay_matrix(ea[:, lanes]) * state + _dot_tn(kh[:, lanes], vb[:, lanes])
                o_ref[rows, lanes] = _rms(o) * hn_ref[:, lanes] * gs_ref[rows, lanes]
            return carry

        lax.fori_loop(0, rows_total // blk, block, 0)

    @pl.when(j == nt - 1)
    def _():
        sf_ref[...] = st_ref[...]


def _hgrn(q, k, g, v, gs, s0, hnorm, b, t, nseq, tt):
    n, dh = q.shape
    nt = t // tt
    dk = dh // HGRN_HEADS
    rows = nseq * tt
    tl = min(tt, HGRN_CHUNK)
    row_spec = pl.BlockSpec((rows, dh), lambda i, j: (i * nt + j, 0))
    st_spec = pl.BlockSpec((nseq, HGRN_HEADS, dk, dk), lambda i, j: (i, 0, 0, 0))
    return pl.pallas_call(
        functools.partial(_hgrn_kernel, tl=tl, nt=nt),
        grid=(b // nseq, nt),
        in_specs=[row_spec] * 5 + [st_spec, pl.BlockSpec((1, dh), lambda i, j: (0, 0))],
        out_specs=[row_spec, st_spec],
        out_shape=[jax.ShapeDtypeStruct((n, dh), F32),
                   jax.ShapeDtypeStruct((b, HGRN_HEADS, dk, dk), F32)],
        scratch_shapes=[pltpu.VMEM((nseq, HGRN_HEADS, dk, dk), F32),
                        pltpu.VMEM((rows, dh), F32), pltpu.VMEM((rows, dh), F32),
                        pltpu.VMEM((rows, dh), F32), pltpu.VMEM((rows, dh), F32),
                        pltpu.VMEM((3 * rows, dh), F32)],
        compiler_params=_cparams(("parallel", "arbitrary")),
        name="hgrn",
    )(q, k, g, v, gs, s0, hnorm)


def _s5_prepare(a_re, a_im, log_dt, b_re, b_im, c_re, c_im, d_skip):
    ng, npp = a_re.shape
    nh = b_re.shape[-1]
    L = SSM_CHUNK
    gs = ng // SSM_SETS
    lam_re = jnp.minimum(a_re, MAX_REAL)
    lam_im = a_im
    dt = jnp.exp(log_dt)
    mag = jnp.exp(lam_re * dt)
    ab_re = mag * jnp.cos(lam_im * dt)
    ab_im = mag * jnp.sin(lam_im * dt)
    den = lam_re * lam_re + lam_im * lam_im
    co_re = ((ab_re - 1.0) * lam_re + ab_im * lam_im) / den
    co_im = (ab_im * lam_re - (ab_re - 1.0) * lam_im) / den
    bb_re = co_re[..., None] * b_re - co_im[..., None] * b_im
    bb_im = co_re[..., None] * b_im + co_im[..., None] * b_re
    tau = jnp.arange(L + 1, dtype=F32)[:, None, None]
    pw_mag = jnp.exp(tau * (lam_re * dt))
    pw_re = pw_mag * jnp.cos(tau * (lam_im * dt))
    pw_im = pw_mag * jnp.sin(tau * (lam_im * dt))
    ab_b_re = pw_re[:L, :, :, None] * bb_re - pw_im[:L, :, :, None] * bb_im
    ab_b_im = pw_re[:L, :, :, None] * bb_im + pw_im[:L, :, :, None] * bb_re
    kern = (jnp.einsum('gkp,lgph->lghk', c_re, ab_b_re, precision=HIGHEST)
            - jnp.einsum('gkp,lgph->lghk', c_im, ab_b_im, precision=HIGHEST))
    skip = d_skip[None, :, :, None] * jnp.eye(nh, dtype=F32)
    kern = kern + jnp.where(jnp.arange(L)[:, None, None, None] == 0, skip, 0.0)

    def group_block_diag(c):
        rows, w = c.shape[-2:]
        repeat = (jnp.arange(gs * w)[None, :] % w == jnp.arange(w)[:, None]).astype(F32)
        tiled = jnp.dot(c, repeat, precision=HIGHEST)
        rg = jnp.arange(rows)[:, None] // (rows // gs)
        cq = jnp.arange(gs * w)[None, :] // w
        return jnp.where(rg == cq, tiled, 0.0).astype(BF16).transpose(1, 0, 2, 3).reshape(
            SSM_SETS, L * rows, gs * w)

    bd = group_block_diag(kern.reshape(L, SSM_SETS, gs * nh, nh)).reshape(SSM_SETS, L, gs * nh, gs * nh)
    n_re = group_block_diag(ab_b_re[::-1].transpose(0, 1, 3, 2).reshape(L, SSM_SETS, gs * nh, npp))
    n_im = group_block_diag(ab_b_im[::-1].transpose(0, 1, 3, 2).reshape(L, SSM_SETS, gs * nh, npp))
    inc = jnp.concatenate([n_re, n_im], axis=-1)
    ca_re = c_re[None] * pw_re[1:, :, None, :] - c_im[None] * pw_im[1:, :, None, :]
    ca_im = c_re[None] * pw_im[1:, :, None, :] + c_im[None] * pw_re[1:, :, None, :]
    m = jnp.concatenate([group_block_diag(ca_re.reshape(L, SSM_SETS, gs * nh, npp)),
                         group_block_diag(-ca_im.reshape(L, SSM_SETS, gs * nh, npp))], axis=-1)
    a8 = jnp.concatenate([pw_re[L].reshape(SSM_SETS, 1, gs * npp),
                          pw_im[L].reshape(SSM_SETS, 1, gs * npp)], axis=-1)
    return bd, inc, m, a8


def _s5_kernel(u_ref, xr_ref, xi_ref, bd_ref, inc_ref, m_ref, a8_ref, y_ref, fr_ref, fi_ref, wt_ref, *, sequential):
    n = u_ref.shape[1] // SSM_CHUNK
    ns = xr_ref.shape[-1]
    u = jnp.concatenate([u_ref[0, pl.ds(s, n, stride=SSM_CHUNK), :] for s in range(SSM_CHUNK)], axis=-1)

    ny = SSM_CHUNK * LANES

    @pl.when(pl.program_id(1) == 0)
    def _():
        wt_ref[:, :ny] = jnp.zeros((wt_ref.shape[0], ny), BF16)
        for s in range(SSM_CHUNK):
            for t in range(s, SSM_CHUNK):
                wt_ref[s * LANES:(s + 1) * LANES, t * LANES:(t + 1) * LANES] = bd_ref[0, t - s]
        wt_ref[:, ny:] = inc_ref[0]

    res = _dot(u.astype(BF16), wt_ref[...])
    y_local = res[:, :ny]
    d_re = res[:, ny:ny + ns]
    d_im = res[:, ny + ns:]
    a_re = a8_ref[0][:, :ns]
    a_im = a8_ref[0][:, ns:]
    x0_re = xr_ref[0]
    x0_im = xi_ref[0]
    if sequential:
        row = lax.broadcasted_iota(jnp.int32, (n, ns), 0)
        first = row == 0
        x_re = d_re + jnp.where(first, a_re * x0_re - a_im * x0_im, 0.0)
        x_im = d_im + jnp.where(first, a_re * x0_im + a_im * x0_re, 0.0)
        p_re, p_im = a_re, a_im
        step = 1
        while step < n:
            s_re = jnp.where(row >= step, pltpu.roll(x_re, step, 0), 0.0)
            s_im = jnp.where(row >= step, pltpu.roll(x_im, step, 0), 0.0)
            x_re, x_im = x_re + p_re * s_re - p_im * s_im, x_im + p_re * s_im + p_im * s_re
            p_re, p_im = p_re * p_re - p_im * p_im, 2.0 * p_re * p_im
            step *= 2
        fr_ref[0] = x_re[n - 1:n]
        fi_ref[0] = x_im[n - 1:n]
        xc_re = jnp.where(first, x0_re, pltpu.roll(x_re, 1, 0))
        xc_im = jnp.where(first, x0_im, pltpu.roll(x_im, 1, 0))
    else:
        xc_re, xc_im = x0_re, x0_im
        fr_ref[0] = a_re * x0_re - a_im * x0_im + d_re
        fi_ref[0] = a_re * x0_im + a_im * x0_re + d_im
    xc = jnp.concatenate([xc_re, xc_im], axis=-1).astype(BF16)
    y = y_local + _dot_nt(xc, m_ref[0])
    for t in range(SSM_CHUNK):
        y_ref[0, pl.ds(t, n, stride=SSM_CHUNK), :] = y[:, t * LANES:(t + 1) * LANES]


def _s5(u, x_re, x_im, bd, inc, m, a8, n_tokens, sequential):
    sets = u.shape[0]
    nb, rb, _ = x_re.shape
    ns = m.shape[1] // 2
    st_spec = pl.BlockSpec((1, rb, ns), lambda gi, i: (i, 0, gi))
    st_shape = jax.ShapeDtypeStruct(x_re.shape, F32)
    tok_spec = pl.BlockSpec((1, n_tokens, LANES), lambda gi, i: (gi, i, 0))
    return pl.pallas_call(
        functools.partial(_s5_kernel, sequential=sequential),
        grid=(sets, nb),
        in_specs=[tok_spec, st_spec, st_spec,
                  pl.BlockSpec((1,) + bd.shape[1:], lambda gi, i: (gi, 0, 0, 0)),
                  pl.BlockSpec((1,) + inc.shape[1:], lambda gi, i: (gi, 0, 0)),
                  pl.BlockSpec((1,) + m.shape[1:], lambda gi, i: (gi, 0, 0)),
                  pl.BlockSpec((1, 1, 2 * ns), lambda gi, i: (gi, 0, 0))],
        out_specs=[tok_spec, st_spec, st_spec],
        out_shape=[jax.ShapeDtypeStruct(u.shape, F32), st_shape, st_shape],
        scratch_shapes=[pltpu.VMEM((inc.shape[1], SSM_CHUNK * LANES + inc.shape[2]), BF16)],
        compiler_params=_cparams(("parallel", "arbitrary")),
        name="s5",
    )(u, x_re, x_im, bd, inc, m, a8)


def _gelu_tanh(x):
    return 0.5 * x * (1.0 + jnp.tanh(math.sqrt(2.0 / math.pi) * (x + 0.044715 * (x * x * x))))


def _outproj_kernel(xp_ref, xs_ref, ohp_ref, ohs_ref, ysp_ref, yss_ref,
                    gate_p_ref, shift_p_ref, scale_p_ref, gate_s_ref, shift_s_ref, scale_s_ref,
                    wglu_ref, bglu_ref, sn_ref, wo_h_ref, wo_s_ref, nf_ref, wr_ref, br_ref,
                    x1p_ref, x1s_ref, h2_ref, pair1_ref, pair2_ref, rw_ref, cnt_ref, *, prompt_tiles):
    is_prompt = pl.program_id(0) < prompt_tiles
    _, rows, d = xp_ref.shape

    def per_token(p_ref, s_ref):
        per_seq = s_ref[0]
        seqs = per_seq.shape[0]
        rep = jnp.broadcast_to(per_seq[:, None, :], (seqs, rows // seqs, d)).reshape(rows, d)
        return jnp.where(is_prompt, p_ref[0], rep)

    x = jnp.where(is_prompt, xp_ref[0], xs_ref[0])
    oh = jnp.where(is_prompt, ohp_ref[...], ohs_ref[...])
    ys = jnp.concatenate([jnp.where(is_prompt, ysp_ref[s], yss_ref[s]) for s in range(SSM_SETS)], axis=-1)
    y = _gelu_tanh(ys)
    y = y * jax.nn.sigmoid(_dot(y.astype(BF16), wglu_ref[...]) + bglu_ref[...])
    o_s = _rms(y) * sn_ref[...]
    mix = _dot(oh.astype(BF16), wo_h_ref[...]) + _dot(o_s.astype(BF16), wo_s_ref[...])
    x1 = x + per_token(gate_p_ref, gate_s_ref) * mix

    @pl.when(is_prompt)
    def _():
        x1p_ref[0] = x1

    @pl.when(jnp.logical_not(is_prompt))
    def _():
        x1s_ref[0] = x1

    h2 = _rms(x1) * nf_ref[...]
    h2 = h2 * (1.0 + per_token(scale_p_ref, scale_s_ref)) + per_token(shift_p_ref, shift_s_ref)
    _store_token_tiles(h2_ref, h2)

    logits = _dot(h2.astype(BF16), wr_ref[...]) + br_ref[...]
    lane = lax.broadcasted_iota(jnp.int32, logits.shape, 1)
    neg = -jnp.inf
    gl = jnp.where(lane < MOE_GROUPS, logits, neg)
    gmax = jnp.max(gl, axis=-1, keepdims=True)
    gidx = jnp.min(jnp.where(gl == gmax, lane, ROUTER_LANES), axis=-1, keepdims=True)
    grp_w = 1.0 / jnp.sum(jnp.exp(gl - gmax), axis=-1, keepdims=True)
    e0 = EXPERT_LANE0 + gidx * MOE_PER_GROUP
    sel = jnp.where((lane >= e0) & (lane < e0 + MOE_PER_GROUP), logits, neg)
    m1 = jnp.max(sel, axis=-1, keepdims=True)
    i1 = jnp.min(jnp.where(sel == m1, lane, ROUTER_LANES), axis=-1, keepdims=True)
    sel2 = jnp.where(lane == i1, neg, sel)
    m2 = jnp.max(sel2, axis=-1, keepdims=True)
    i2 = jnp.min(jnp.where(sel2 == m2, lane, ROUTER_LANES), axis=-1, keepdims=True)
    e2 = jnp.exp(m2 - m1)
    w1 = 1.0 / (1.0 + e2)
    w2 = e2 / (1.0 + e2)
    rw_ref[...] = grp_w * (jnp.where(lane == 0, w1, 0.0) + jnp.where(lane == 1, w2, 0.0))

    @pl.when(pl.program_id(0) == 0)
    def _():
        cnt_ref[...] = jnp.zeros_like(cnt_ref)

    picked = (lane == i1) | (lane == i2)
    earlier = (lax.broadcasted_iota(jnp.int32, (rows, rows), 0)
               > lax.broadcasted_iota(jnp.int32, (rows, rows), 1))
    base = cnt_ref[...]
    before = _dot(earlier.astype(BF16), picked.astype(BF16)) + base
    rank1 = jnp.sum(jnp.where(lane == i1, before, 0.0), axis=-1, keepdims=True).astype(jnp.int32)
    rank2 = jnp.sum(jnp.where(lane == i2, before, 0.0), axis=-1, keepdims=True).astype(jnp.int32)
    cnt_ref[...] = base + jnp.sum(picked.astype(F32), axis=0, keepdims=True)
    info = jnp.where(lane == 0, ((i1 - EXPERT_LANE0) << RANK_BITS) | rank1,
                     jnp.where(lane == 1, ((i2 - EXPERT_LANE0) << RANK_BITS) | rank2, 0)).T
    pair1_ref[...] = info[0]
    pair2_ref[...] = info[1]


def _outproj(x_p, x_s, oh_p, oh_s, ys_p, ys_s, ada_p, ada_s, wglu_bf, bglu, snorm, wo_h, wo_s, nffn, wr, br, rows):
    d = x_p.shape[-1]
    dh = oh_p.shape[-1]
    n_p = x_p.shape[0] * x_p.shape[1]
    n_s = x_s.shape[0] * x_s.shape[1]
    n = n_p + n_s
    tiles_p = n_p // rows
    seqs = rows // x_s.shape[1]

    def pt(i):
        return jnp.minimum(i, tiles_p - 1)

    def st(i):
        return jnp.maximum(i - tiles_p, 0)

    tiles_per_seq = x_p.shape[1] // rows

    def ada_p_spec(col):
        return pl.BlockSpec((1, 1, d), lambda i: (pt(i) // tiles_per_seq, 0, col))

    def ada_s_spec(col):
        return pl.BlockSpec((1, seqs, d), lambda i: (st(i), 0, col))

    def full(a):
        return pl.BlockSpec(a.shape, lambda i: (0,) * a.ndim)

    xp3 = x_p.reshape(tiles_p, rows, d)
    xs3 = x_s.reshape(n_s // rows, rows, d)
    ada_s3 = ada_s.reshape(n_s // rows, seqs, -1)
    outs = pl.pallas_call(
        functools.partial(_outproj_kernel, prompt_tiles=tiles_p),
        grid=(n // rows,),
        in_specs=[pl.BlockSpec((1, rows, d), lambda i: (pt(i), 0, 0)),
                  pl.BlockSpec((1, rows, d), lambda i: (st(i), 0, 0)),
                  pl.BlockSpec((rows, dh), lambda i: (pt(i), 0)),
                  pl.BlockSpec((rows, dh), lambda i: (st(i), 0)),
                  pl.BlockSpec((SSM_SETS, rows, LANES), lambda i: (0, pt(i), 0)),
                  pl.BlockSpec((SSM_SETS, rows, LANES), lambda i: (0, st(i), 0)),
                  ada_p_spec(2), ada_p_spec(3), ada_p_spec(4), ada_s_spec(2), ada_s_spec(3), ada_s_spec(4),
                  full(wglu_bf), full(bglu), full(snorm), full(wo_h), full(wo_s), full(nffn),
                  full(wr), full(br)],
        out_specs=[pl.BlockSpec((1, rows, d), lambda i: (pt(i), 0, 0)),
                   pl.BlockSpec((1, rows, d), lambda i: (st(i), 0, 0)),
                   pl.BlockSpec((rows * TOKEN_TILE_ROWS, LANES), lambda i: (i, 0)),
                   pl.BlockSpec((rows,), lambda i: (i,)),
                   pl.BlockSpec((rows,), lambda i: (i,)),
                   pl.BlockSpec((rows, ROUTER_LANES), lambda i: (i, 0)),
                   pl.BlockSpec((1, ROUTER_LANES), lambda i: (0, 0))],
        out_shape=[jax.ShapeDtypeStruct(xp3.shape, F32),
                   jax.ShapeDtypeStruct(xs3.shape, F32),
                   jax.ShapeDtypeStruct((n * TOKEN_TILE_ROWS, LANES), F32),
                   jax.ShapeDtypeStruct((n,), jnp.int32),
                   jax.ShapeDtypeStruct((n,), jnp.int32),
                   jax.ShapeDtypeStruct((n, ROUTER_LANES), F32),
                   jax.ShapeDtypeStruct((1, ROUTER_LANES), F32)],
        compiler_params=_cparams(("arbitrary",)),
        name="outproj",
    )(xp3, xs3, oh_p, oh_s, ys_p, ys_s, ada_p, ada_p, ada_p, ada_s3, ada_s3, ada_s3,
      wglu_bf, bglu, snorm, wo_h, wo_s, nffn, wr, br)
    x1_p, x1_s = outs[0].reshape(x_p.shape), outs[1].reshape(x_s.shape)
    return (x1_p, x1_s) + tuple(outs[2:])


def _moe_schedule(cnt, n_slots):
    c = cnt[0, EXPERT_LANE0:EXPERT_LANE0 + N_EXPERTS].astype(jnp.int32)
    seg_end = jnp.cumsum(c)
    seg_start = seg_end - c
    first_tile = seg_start // MOE_TILE
    tiles = jnp.where(c > 0, (seg_end - 1) // MOE_TILE - first_tile + 1, 0)
    cum = jnp.cumsum(tiles)
    n_items = cum[-1]
    item = jnp.minimum(jnp.arange(n_slots, dtype=jnp.int32), n_items - 1)
    item_exp = jnp.sum(item[:, None] >= cum[None, :], axis=1).astype(jnp.int32)
    shares = ((c > 0) & (seg_start % MOE_TILE != 0)).astype(jnp.int32)
    item_tile = item - jnp.sum((item[:, None] >= (cum - tiles)[None, :]) * shares[None, :], axis=1)
    return seg_start.astype(jnp.int32), seg_end.astype(jnp.int32), item_tile.astype(jnp.int32), item_exp


def _token_tile(ref, t):
    return ref.at[pl.ds(pl.multiple_of(t * TOKEN_TILE_ROWS, TOKEN_TILE_ROWS), TOKEN_TILE_ROWS)]


def _store_token_tiles(ref, x):
    rows = x.shape[0]
    for c in range(TOKEN_TILE_ROWS):
        ref[pl.ds(c, rows, stride=TOKEN_TILE_ROWS), :] = x[:, c * LANES:(c + 1) * LANES]


def _load_token_tiles(ref, rows):
    return [ref[pl.ds(c, rows, stride=TOKEN_TILE_ROWS), :] for c in range(TOKEN_TILE_ROWS)]


def _positions_kernel(seg_ref, p1_ref, p2_ref, o1_ref, o2_ref):
    for p_ref, o_ref in ((p1_ref, o1_ref), (p2_ref, o2_ref)):
        pair = p_ref[...]
        expert = pair >> RANK_BITS
        pos = pair & ((1 << RANK_BITS) - 1)
        for e in range(N_EXPERTS):
            pos = pos + jnp.where(expert == e, seg_ref[e], 0)
        o_ref[...] = pos


def _positions(seg, pair1, pair2):
    n = pair1.shape[0]
    shape2 = (n // LANES, LANES)
    full = pl.BlockSpec(shape2, lambda: (0, 0))
    pos1, pos2 = pl.pallas_call(
        _positions_kernel,
        in_specs=[pl.BlockSpec(memory_space=pltpu.SMEM), full, full],
        out_specs=[full, full],
        out_shape=[jax.ShapeDtypeStruct(shape2, jnp.int32)] * 2,
        compiler_params=pltpu.CompilerParams(vmem_limit_bytes=VMEM_LIMIT),
        name="moe_positions",
    )(seg, pair1.reshape(shape2), pair2.reshape(shape2))
    return pos1.reshape(n), pos2.reshape(n)


def _inverse_kernel(p1_ref, p2_ref, inv_ref):
    ts = p1_ref.shape[0]
    base = pl.program_id(0) * ts

    def body(t, carry):
        inv_ref[p1_ref[t]] = base + t
        inv_ref[p2_ref[t]] = base + t
        return carry

    lax.fori_loop(0, ts, body, 0, unroll=2 * DMA_UNROLL)


def _inverse(pos1, pos2, ts):
    n = pos1.shape[0]
    idx_spec = pl.BlockSpec((ts,), lambda i: (i,), memory_space=pltpu.SMEM)
    return pl.pallas_call(
        _inverse_kernel,
        grid=(n // ts,),
        in_specs=[idx_spec, idx_spec],
        out_specs=pl.BlockSpec(memory_space=pltpu.SMEM),
        out_shape=jax.ShapeDtypeStruct((2 * n,), jnp.int32),
        compiler_params=_cparams(("arbitrary",)),
        name="moe_inverse",
    )(pos1, pos2)


def _row_gather(src_ref, idx_of, dst_ref, dst_row0, sem, n_rows):
    def copy(t):
        return pltpu.make_async_copy(_token_tile(src_ref, idx_of(t)), _token_tile(dst_ref, dst_row0 + t), sem)

    def start():
        def body(t, carry):
            copy(2 * t).start(priority=0)
            copy(2 * t + 1).start(priority=1)
            return carry
        lax.fori_loop(0, n_rows // 2, body, 0, unroll=DMA_UNROLL // 2)

    def wait():
        def body(t, carry):
            copy(t).wait()
            return carry
        lax.fori_loop(0, n_rows, body, 0, unroll=DMA_UNROLL)

    return start, wait


def _experts_kernel(lo_ref, hi_ref, tile_ref, exp_ref, inv_ref, inv_next_ref, h_ref,
                    wg_ref, wu_ref, wd_ref, ys_ref, xbuf_ref, sems, wg_s, wu_s, wd_s, *, n_slots):
    i = pl.program_id(0)
    prev = jnp.maximum(i - 1, 0)
    e = exp_ref[i]
    slot = i % 2

    @pl.when((i == 0) | (e != exp_ref[prev]))
    def _():
        wg_s[...] = wg_ref[0].astype(BF16)
        wu_s[...] = wu_ref[0].astype(BF16)
        wd_s[...] = wd_ref[0].astype(BF16)

    start_this, wait_this = _row_gather(h_ref, lambda t: inv_ref[t], xbuf_ref, slot * MOE_TILE,
                                        sems.at[slot], MOE_TILE)
    _, wait_next = _row_gather(h_ref, lambda t: inv_next_ref[t], xbuf_ref, (1 - slot) * MOE_TILE,
                               sems.at[1 - slot], MOE_TILE)

    def start_next(rows):
        for t in rows:
            pltpu.make_async_copy(_token_tile(h_ref, inv_next_ref[t]),
                                  _token_tile(xbuf_ref, (1 - slot) * MOE_TILE + t),
                                  sems.at[1 - slot]).start(priority=t % 2)

    @pl.when(i == 0)
    def _():
        start_this()

    wait_this()
    third = -(-MOE_TILE // 3)
    base = pl.multiple_of(slot * (MOE_TILE * TOKEN_TILE_ROWS), MOE_TILE * TOKEN_TILE_ROWS)
    x = jnp.concatenate([xbuf_ref[pl.ds(base + c, MOE_TILE, stride=TOKEN_TILE_ROWS), :]
                         for c in range(TOKEN_TILE_ROWS)], axis=-1).astype(BF16)
    start_next(range(0, third))
    gate = _dot(x, wg_s[...])
    start_next(range(third, 2 * third))
    up = _dot(x, wu_s[...])
    start_next(range(2 * third, MOE_TILE))
    out = _dot((_silu(gate) * up).astype(BF16), wd_s[...])
    row = tile_ref[i] * MOE_TILE + lax.broadcasted_iota(jnp.int32, (MOE_TILE, 1), 0)
    mine = (row >= lo_ref[e]) & (row < hi_ref[e])
    first_visit = (i == 0) | (tile_ref[i] != tile_ref[prev])

    @pl.when(first_visit)
    def _():
        _store_token_tiles(ys_ref, jnp.where(mine, out, 0.0))

    @pl.when(jnp.logical_not(first_visit))
    def _():
        old = jnp.concatenate(_load_token_tiles(ys_ref, MOE_TILE), axis=-1)
        _store_token_tiles(ys_ref, jnp.where(mine, out, old))

    @pl.when(i == n_slots - 1)
    def _():
        wait_next()


def _experts(seg_lo, seg_hi, item_tile, item_exp, inv, h2, wg, wu, wd):
    n_slots = item_tile.shape[0]
    _, d, de = wg.shape

    def w_spec(shape):
        return pl.BlockSpec((1,) + shape, lambda i, lo, hi, tile, ex: (ex[i], 0, 0))

    return pl.pallas_call(
        functools.partial(_experts_kernel, n_slots=n_slots),
        grid_spec=pltpu.PrefetchScalarGridSpec(
            num_scalar_prefetch=4, grid=(n_slots,),
            in_specs=[pl.BlockSpec((MOE_TILE,), lambda i, lo, hi, tile, ex: (tile[i],),
                                   memory_space=pltpu.SMEM),
                      pl.BlockSpec((MOE_TILE,), lambda i, lo, hi, tile, ex: (
                          tile[jnp.minimum(i + 1, n_slots - 1)],), memory_space=pltpu.SMEM),
                      pl.BlockSpec(memory_space=pltpu.HBM),
                      w_spec((d, de)), w_spec((d, de)), w_spec((de, d))],
            out_specs=pl.BlockSpec((MOE_TILE * TOKEN_TILE_ROWS, LANES),
                                   lambda i, lo, hi, tile, ex: (tile[i], 0)),
            scratch_shapes=[pltpu.VMEM((2 * MOE_TILE * TOKEN_TILE_ROWS, LANES), F32),
                            pltpu.SemaphoreType.DMA((2,)),
                            pltpu.VMEM((d, de), BF16), pltpu.VMEM((d, de), BF16), pltpu.VMEM((de, d), BF16)]),
        out_shape=jax.ShapeDtypeStruct((inv.shape[0] * TOKEN_TILE_ROWS, LANES), F32),
        compiler_params=_cparams(("arbitrary",)),
        name="moe_experts",
    )(seg_lo, seg_hi, item_tile, item_exp, inv, inv, h2, wg, wu, wd)


def _combine_kernel(p1_ref, p2_ref, p1n_ref, p2n_ref, rw_ref, x1_ref, gate_ref, shift_ref, scale_ref,
                    nfin_ref, ys_ref, y_ref, r1_ref, r2_ref, sems, *, n_steps):
    _, rows, d = x1_ref.shape
    s = pl.program_id(0)
    slot = s % 2

    def gathers(pa_ref, pb_ref, slot):
        g1 = _row_gather(ys_ref, lambda t: pa_ref[t], r1_ref, slot * rows, sems.at[slot], rows)
        g2 = _row_gather(ys_ref, lambda t: pb_ref[t], r2_ref, slot * rows, sems.at[slot], rows)
        return g1, g2

    this = gathers(p1_ref, p2_ref, slot)
    nxt = gathers(p1n_ref, p2n_ref, 1 - slot)

    def start_next(tokens):
        for t in tokens:
            for k, (p_ref, r_ref) in enumerate(((p1n_ref, r1_ref), (p2n_ref, r2_ref))):
                pltpu.make_async_copy(_token_tile(ys_ref, p_ref[t]), _token_tile(r_ref, (1 - slot) * rows + t),
                                      sems.at[1 - slot]).start(priority=k)

    @pl.when(s == 0)
    def _():
        this[0][0]()
        this[1][0]()

    this[0][1]()
    this[1][1]()

    def per_token(ref):
        per_seq = ref[0]
        seqs = per_seq.shape[0]
        return jnp.broadcast_to(per_seq[:, None, :], (seqs, rows // seqs, d)).reshape(rows, d)

    rw = rw_ref[...]
    w1 = rw[:, 0:1]
    w2 = rw[:, 1:2]
    base = pl.multiple_of(slot * (rows * TOKEN_TILE_ROWS), rows * TOKEN_TILE_ROWS)
    per_piece = rows // TOKEN_TILE_ROWS
    pieces = []
    for c in range(TOKEN_TILE_ROWS):
        pieces.append(w1 * r1_ref[pl.ds(base + c, rows, stride=TOKEN_TILE_ROWS), :]
                      + w2 * r2_ref[pl.ds(base + c, rows, stride=TOKEN_TILE_ROWS), :])
        start_next(range(c * per_piece, (c + 1) * per_piece))
    moe = jnp.concatenate(pieces, axis=-1)
    x2 = x1_ref[0] + per_token(gate_ref) * moe
    hf = _rms(x2) * nfin_ref[...]
    y_ref[0] = hf * (1.0 + per_token(scale_ref)) + per_token(shift_ref)

    @pl.when(s == n_steps - 1)
    def _():
        nxt[0][1]()
        nxt[1][1]()


def _combine(pos1, pos2, rw, x1, ada3, adaf3, nfin, ys, rows, tile0):
    b, t, d = x1.shape
    n_steps = b * t // rows
    seqs = max(rows // t, 1)
    tiles_per_seq = max(t // rows, 1)
    x3 = x1.reshape(n_steps, rows, d)
    ada_v = ada3.reshape(b // seqs, seqs, -1)
    adaf_v = adaf3.reshape(b // seqs, seqs, -1)
    x_spec = pl.BlockSpec((1, rows, d), lambda i: (i, 0, 0))

    def idx_spec(step):
        return pl.BlockSpec((rows,), lambda i: (tile0 + step(i),), memory_space=pltpu.SMEM)

    def ada_spec(col):
        return pl.BlockSpec((1, seqs, d), lambda i: (i // tiles_per_seq, 0, col))

    def cur(i):
        return i

    def nxt(i):
        return jnp.minimum(i + 1, n_steps - 1)

    return pl.pallas_call(
        functools.partial(_combine_kernel, n_steps=n_steps),
        grid=(n_steps,),
        in_specs=[idx_spec(cur), idx_spec(cur), idx_spec(nxt), idx_spec(nxt),
                  pl.BlockSpec((rows, ROUTER_LANES), lambda i: (tile0 + i, 0)),
                  x_spec, ada_spec(5), ada_spec(0), ada_spec(1),
                  pl.BlockSpec((1, d), lambda i: (0, 0)),
                  pl.BlockSpec(memory_space=pltpu.HBM)],
        out_specs=x_spec,
        scratch_shapes=[pltpu.VMEM((2 * rows * TOKEN_TILE_ROWS, LANES), F32),
                        pltpu.VMEM((2 * rows * TOKEN_TILE_ROWS, LANES), F32),
                        pltpu.SemaphoreType.DMA((2,))],
        out_shape=jax.ShapeDtypeStruct(x3.shape, F32),
        compiler_params=_cparams(("arbitrary",)),
        name="moe_combine",
    )(pos1, pos2, pos1, pos2, rw, x3, ada_v, adaf_v, adaf_v, nfin, ys).reshape(x1.shape)


def _mixer(x, ada3, s_h, s_re, s_im, p, *, bb, tt, hgrn_seqs, hgrn_tokens, sequential):
    b, t, d = x.shape
    n = b * t
    q, k, g, v, gs, u = _inproj(x, ada3, p['norm_mix'], p['w_in'], p['lb'], bb, tt)
    oh, s_h_new = _hgrn(q, k, g, v, gs, s_h, p['hgrn_norm'], b, t, hgrn_seqs, hgrn_tokens)
    if sequential:
        xr, xi = s_re.reshape(b, 1, -1), s_im.reshape(b, 1, -1)
        n_tokens = t
    else:
        xr, xi = s_re.reshape(1, b, -1), s_im.reshape(1, b, -1)
        n_tokens = n
    ys, fr, fi = _s5(u, xr, xi, p['ssm_bd'], p['ssm_inc'], p['ssm_m'], p['ssm_a8'], n_tokens, sequential)
    states = (s_h_new[None], fr.reshape(1, b, s_re.shape[-2], s_re.shape[-1]),
              fi.reshape(1, b, s_re.shape[-2], s_re.shape[-1]))
    return oh, ys, states


def kernel(x_prompt, x_sample, c_prompt, c_sample, state_hgrn, state_ssm_re, state_ssm_im, hgrn_lb_logits, w_ada, b_ada, norm_mix, w_in, hgrn_norm, ssm_a_re, ssm_a_im, ssm_log_dt, ssm_b_re, ssm_b_im, ssm_c_re, ssm_c_im, ssm_d, ssm_w_glu, ssm_b_glu, ssm_norm, w_out, norm_ffn, moe_w_group, moe_b_group, moe_w_router, moe_b_router, moe_w_gate, moe_w_up, moe_w_down, w_ada_final, b_ada_final, norm_final):
    depth = w_ada.shape[0]
    assert depth == 1
    d = x_prompt.shape[-1]
    bp = x_prompt.shape[0]
    dh = hgrn_norm.shape[-1]
    dk = dh // HGRN_HEADS
    de = moe_w_gate.shape[-1]
    n_exp = MOE_GROUPS * MOE_PER_GROUP

    lb = jax.nn.softmax(hgrn_lb_logits.astype(F32), axis=0)[0].reshape(1, dh)
    bd, inc, m, a8 = _s5_prepare(ssm_a_re[0], ssm_a_im[0], ssm_log_dt[0], ssm_b_re[0], ssm_b_im[0],
                                 ssm_c_re[0], ssm_c_im[0], ssm_d[0])

    def router_lanes(group_part, expert_part):
        rows = group_part.shape[0]
        return jnp.concatenate(
            [group_part, jnp.zeros((rows, EXPERT_LANE0 - MOE_GROUPS), F32), expert_part,
             jnp.zeros((rows, ROUTER_LANES - EXPERT_LANE0 - n_exp), F32)], axis=1)

    w_rt = router_lanes(moe_w_group[0], moe_w_router[0].transpose(1, 0, 2).reshape(d, n_exp))
    b_rt = router_lanes(moe_b_group[0].reshape(1, MOE_GROUPS), moe_b_router[0].reshape(1, n_exp))
    p = dict(
        lb=lb, norm_mix=norm_mix[0].reshape(1, d), w_in=w_in[0].astype(BF16),
        hgrn_norm=hgrn_norm[0].reshape(1, dh),
        ssm_bd=bd, ssm_inc=inc, ssm_m=m, ssm_a8=a8,
        w_glu=ssm_w_glu[0].astype(BF16), b_glu=ssm_b_glu[0].reshape(1, -1), ssm_norm=ssm_norm[0].reshape(1, -1),
        wo_h=w_out[0, :dh].astype(BF16), wo_s=w_out[0, dh:].astype(BF16),
        norm_ffn=norm_ffn[0].reshape(1, d), w_rt=w_rt.astype(BF16), b_rt=b_rt,
    )
    wg = moe_w_gate[0].reshape(n_exp, d, de)
    wu = moe_w_up[0].reshape(n_exp, d, de)
    wd = moe_w_down[0].reshape(n_exp, de, d)
    nfin = norm_final.reshape(1, d)

    c_all = jnp.concatenate([c_prompt, c_sample], axis=0)
    ada = _silu_linear(c_all, w_ada[0], b_ada[0])
    adaf = _silu_linear(c_all, w_ada_final, b_ada_final)
    ada_p, ada_s = ada[:bp].reshape(bp, 1, -1), ada[bp:].reshape(x_sample.shape[0], 1, -1)
    adaf_p, adaf_s = adaf[:bp].reshape(bp, 1, -1), adaf[bp:].reshape(x_sample.shape[0], 1, -1)

    zeros_h = jnp.zeros((bp, HGRN_HEADS, dk, dk), F32)
    zeros_s = jnp.zeros((bp,) + state_ssm_re.shape[2:], F32)
    oh_p, ys_p, st_p = _mixer(x_prompt, ada_p, zeros_h, zeros_s, zeros_s, p,
                              bb=1, tt=256, hgrn_seqs=1, hgrn_tokens=512, sequential=True)
    oh_s, ys_s, st_s = _mixer(x_sample, ada_s, state_hgrn[0], state_ssm_re[0], state_ssm_im[0], p,
                              bb=32, tt=8, hgrn_seqs=16, hgrn_tokens=8, sequential=False)
    x1_p, x1_s, h2, pair1, pair2, rw, cnt = _outproj(
        x_prompt, x_sample, oh_p, oh_s, ys_p, ys_s, ada_p, ada_s, p['w_glu'], p['b_glu'], p['ssm_norm'],
        p['wo_h'], p['wo_s'], p['norm_ffn'], p['w_rt'], p['b_rt'], TOKEN_ROWS)

    n_pairs = 2 * pair1.shape[0]
    n_slots = n_pairs // MOE_TILE + N_EXPERTS
    seg, seg_end, item_tile, item_exp = _moe_schedule(cnt, n_slots)
    pos1, pos2 = _positions(seg, pair1, pair2)
    inv = _inverse(pos1, pos2, 1024)
    ys = _experts(seg, seg_end, item_tile, item_exp, inv, h2, wg, wu, wd)
    tiles_p = x_prompt.shape[0] * x_prompt.shape[1] // TOKEN_ROWS
    y_p = _combine(pos1, pos2, rw, x1_p, ada_p, adaf_p, nfin, ys, TOKEN_ROWS, 0)
    y_s = _combine(pos1, pos2, rw, x1_s, ada_s, adaf_s, nfin, ys, TOKEN_ROWS, tiles_p)
    return (y_p, y_s) + st_p + st_s
```

```python
import functools
import math

import jax
import jax.numpy as jnp
from jax import lax
from jax.experimental import pallas as pl
from jax.experimental.pallas import tpu as pltpu

F32 = jnp.float32
BF16 = jnp.bfloat16
HIGHEST = lax.Precision.HIGHEST

EPS = 1e-6
MAX_REAL = -1e-4
HGRN_HEADS = 4
HGRN_CHUNK = 128
HGRN_SAFE_EXPONENT = 80.0
HGRN_EXACT_BLOCK = 8
SSM_GROUP = 16
SSM_STATE = 64
SSM_CHUNK = 8
SSM_SETS = 4
MOE_GROUPS = 4
MOE_PER_GROUP = 8
N_EXPERTS = MOE_GROUPS * MOE_PER_GROUP
ROUTER_LANES = 128
EXPERT_LANE0 = 32
RANK_BITS = 20
MOE_TILE = 256
LANES = 128
TOKEN_TILE_ROWS = 8
DMA_UNROLL = 8
TOKEN_ROWS = 512
VMEM_LIMIT = 56 * 1024 * 1024


def _cparams(sem):
    return pltpu.CompilerParams(dimension_semantics=sem, vmem_limit_bytes=VMEM_LIMIT)


def _silu(x):
    return x * jax.nn.sigmoid(x)


def _rms(x):
    return x * lax.rsqrt(jnp.mean(x * x, axis=-1, keepdims=True) + EPS)


def _dot(a, b):
    return jnp.dot(a, b, preferred_element_type=F32)


def _dot_nt(a, b):
    return lax.dot_general(a, b, (((1,), (1,)), ((), ())), preferred_element_type=F32)


def _dot_tn(a, b, precision=None):
    return lax.dot_general(a, b, (((0,), (0,)), ((), ())), preferred_element_type=F32,
                           precision=precision)


def _silu_linear_kernel(c_ref, w_ref, b_ref, o_ref):
    a = _silu(c_ref[...]).astype(BF16)
    o_ref[...] = _dot(a, w_ref[...].astype(BF16)) + b_ref[...]


def _silu_linear(c, w, b):
    m, d = c.shape
    n = w.shape[1]
    tn = 1024
    return pl.pallas_call(
        _silu_linear_kernel,
        grid=(n // tn,),
        in_specs=[pl.BlockSpec((m, d), lambda j: (0, 0)),
                  pl.BlockSpec((d, tn), lambda j: (0, j)),
                  pl.BlockSpec((1, tn), lambda j: (0, j))],
        out_specs=pl.BlockSpec((m, tn), lambda j: (0, j)),
        out_shape=jax.ShapeDtypeStruct((m, n), F32),
        compiler_params=_cparams(("parallel",)),
        name="silu_linear",
    )(c, w, b.reshape(1, n))


def _inproj_kernel(x_ref, shift_ref, scale_ref, gain_ref, w_ref, lb_ref,
                   q_ref, k_ref, g_ref, v_ref, gs_ref, u_ref, *, dh):
    bb, tt, d = x_ref.shape
    h = _rms(x_ref[...]) * gain_ref[...]
    h = h * (1.0 + scale_ref[...]) + shift_ref[...]
    proj = _dot(h.reshape(bb * tt, d).astype(BF16), w_ref[...])
    lb = lb_ref[...]
    f = lb + (1.0 - lb) * jax.nn.sigmoid(proj[:, dh:2 * dh])
    q_ref[...] = _silu(proj[:, :dh]) * (float(dh // HGRN_HEADS) ** -0.5)
    k_ref[...] = 1.0 - f
    g_ref[...] = jnp.log(f)
    v_ref[...] = proj[:, 2 * dh:3 * dh]
    gs_ref[...] = _silu(proj[:, 3 * dh:4 * dh])
    for s in range(SSM_SETS):
        u_ref[s] = proj[:, 4 * dh + s * LANES:4 * dh + (s + 1) * LANES]


def _inproj(x, ada3, gain, w_in_bf, lb, bb, tt):
    b, t, d = x.shape
    dh = lb.shape[-1]
    nt = t // tt
    rows = bb * tt
    n = b * t
    row_spec = pl.BlockSpec((rows, dh), lambda i, j: (i * nt + j, 0))
    out = jax.ShapeDtypeStruct((n, dh), F32)
    return pl.pallas_call(
        functools.partial(_inproj_kernel, dh=dh),
        grid=(b // bb, nt),
        in_specs=[pl.BlockSpec((bb, tt, d), lambda i, j: (i, j, 0)),
                  pl.BlockSpec((bb, 1, d), lambda i, j: (i, 0, 0)),
                  pl.BlockSpec((bb, 1, d), lambda i, j: (i, 0, 1)),
                  pl.BlockSpec((1, d), lambda i, j: (0, 0)),
                  pl.BlockSpec(w_in_bf.shape, lambda i, j: (0, 0)),
                  pl.BlockSpec((1, dh), lambda i, j: (0, 0))],
        out_specs=[row_spec] * 5 + [pl.BlockSpec((SSM_SETS, rows, LANES), lambda i, j: (0, i * nt + j, 0))],
        out_shape=[out] * 5 + [jax.ShapeDtypeStruct((SSM_SETS, n, LANES), F32)],
        compiler_params=_cparams(("parallel", "parallel")),
        name="inproj",
    )(x, ada3, ada3, gain, w_in_bf, lb)


def _split3(x):
    hi = x.astype(BF16)
    r1 = x - hi.astype(F32)
    mid = r1.astype(BF16)
    lo = (r1 - mid.astype(F32)).astype(BF16)
    return hi, mid, lo


def _hgrn_kernel(q_ref, k_ref, g_ref, v_ref, gs_ref, s0_ref, hn_ref, o_ref, sf_ref,
                 st_ref, intra_ref, qh_ref, kh_ref, ea_ref, sums_ref, *, tl, nt):
    j = pl.program_id(1)
    rows_total, dh = q_ref.shape
    dk = dh // HGRN_HEADS
    c = HGRN_CHUNK
    seqs = c // tl
    n_chunks = rows_total // c

    @pl.when(j == 0)
    def _():
        st_ref[...] = s0_ref[...]

    r = lax.broadcasted_iota(jnp.int32, (c, c), 0)
    s = lax.broadcasted_iota(jnp.int32, (c, c), 1)
    same_seq = (r // tl) == (s // tl)
    causal = same_seq & (r >= s)
    upto_mid = same_seq & ((s % tl) < tl // 2)
    one_seq = tl == c
    sums = (causal if one_seq else jnp.concatenate([causal, upto_mid, same_seq], axis=0)).astype(BF16)
    ref_rows = 1 if one_seq else c
    eye3 = (lax.broadcasted_iota(jnp.int32, (dk, 3 * dk), 1) % dk
            == lax.broadcasted_iota(jnp.int32, (dk, 3 * dk), 0)).astype(BF16)

    def decay_matrix(e_row):
        parts = jnp.concatenate(_split3(e_row), axis=-1)
        return _dot_nt(eye3, jnp.broadcast_to(parts, (dk, 3 * dk)))

    worst = jnp.float32(0.0)
    for ci in range(n_chunks):
        rows = slice(ci * c, (ci + 1) * c)
        g_parts = _split3(g_ref[rows, :])
        acc = _dot(sums, g_parts[0]) + _dot(sums, g_parts[1]) + _dot(sums, g_parts[2])
        if one_seq:
            a_mid, a_end = acc[c // 2 - 1:c // 2], acc[c - 1:c]
            sums_ref[ci * 3 * c:ci * 3 * c + c, :] = acc
            sums_ref[ci * 3 * c + c:ci * 3 * c + c + 1, :] = a_mid
            sums_ref[ci * 3 * c + 2 * c:ci * 3 * c + 2 * c + 1, :] = a_end
        else:
            a_mid, a_end = acc[c:2 * c], acc[2 * c:]
            sums_ref[ci * 3 * c:(ci + 1) * 3 * c, :] = acc
        worst = jnp.maximum(worst, jnp.max(jnp.maximum(jnp.abs(a_mid), jnp.abs(a_end - a_mid))))
    factorised_is_safe = worst < HGRN_SAFE_EXPONENT

    @pl.when(factorised_is_safe)
    def _():
        for ci in range(n_chunks):
            rows = slice(ci * c, (ci + 1) * c)
            a = sums_ref[ci * 3 * c:ci * 3 * c + c, :]
            a_mid = sums_ref[ci * 3 * c + c:ci * 3 * c + c + ref_rows, :]
            a_end = sums_ref[ci * 3 * c + 2 * c:ci * 3 * c + 2 * c + ref_rows, :]
            e_mid = jnp.exp(a_mid)
            e_tail = jnp.exp(a_end - a_mid)
            qt = q_ref[rows, :] * jnp.exp(a - a_mid)
            kt = k_ref[rows, :] * jnp.exp(a_mid - a)
            qh_ref[rows, :] = qt * e_mid
            kh_ref[rows, :] = kt * e_tail
            ea_ref[rows.start:rows.start + ref_rows, :] = e_mid * e_tail
            qt = qt.astype(BF16)
            kt = kt.astype(BF16)
            v = v_ref[rows, :].astype(BF16)
            for h in range(HGRN_HEADS):
                lanes = slice(h * dk, (h + 1) * dk)
                sc = jnp.where(causal, _dot_nt(qt[:, lanes], kt[:, lanes]), 0.0).astype(BF16)
                intra_ref[rows, lanes] = _dot(sc, v[:, lanes])

            for si in range(seqs):
                seq = ci * seqs + si if tl < c else 0
                srows = slice(si * tl, (si + 1) * tl)
                orows = slice(ci * c + si * tl, ci * c + (si + 1) * tl)
                for h in range(HGRN_HEADS):
                    lanes = slice(h * dk, (h + 1) * dk)
                    state = st_ref[seq, h]
                    o = intra_ref[orows, lanes] + _dot(qh_ref[orows, lanes].astype(BF16), state.astype(BF16))
                    decay = decay_matrix(ea_ref[orows.start:orows.start + 1, lanes])
                    st_ref[seq, h] = decay * state + _dot_tn(kh_ref[orows, lanes].astype(BF16),
                                                             v[srows, lanes])
                    o_ref[orows, lanes] = _rms(o) * hn_ref[:, lanes] * gs_ref[orows, lanes]

    @pl.when(jnp.logical_not(factorised_is_safe))
    def _():
        blk = HGRN_EXACT_BLOCK
        blocks_per_seq = max(tl // blk, 1)
        tri = (lax.broadcasted_iota(jnp.int32, (blk, blk), 0)
               >= lax.broadcasted_iota(jnp.int32, (blk, blk), 1)).astype(BF16)
        sub = lax.broadcasted_iota(jnp.int32, (blk, dk), 0)

        def block(bi, carry):
            rows = pl.ds(pl.multiple_of(bi * blk, blk), blk)
            seq = bi // blocks_per_seq if tl < c else 0
            g_parts = _split3(g_ref[rows, :])
            a = _dot(tri, g_parts[0]) + _dot(tri, g_parts[1]) + _dot(tri, g_parts[2])
            a_end = a[blk - 1:blk]
            q = q_ref[rows, :]
            k = k_ref[rows, :]
            v = v_ref[rows, :]
            qh = (q * jnp.exp(a)).astype(BF16)
            kh = (k * jnp.exp(a_end - a)).astype(BF16)
            ea = jnp.exp(a_end)
            vb = v.astype(BF16)
            for h in range(HGRN_HEADS):
                lanes = slice(h * dk, (h + 1) * dk)
                state = st_ref[seq, h]
                intra = []
                for t in range(blk):
                    live = sub <= t
                    decay_t = jnp.where(live, jnp.exp(jnp.where(live, a[t:t + 1, lanes] - a[:, lanes], 0.0)), 0.0)
                    score = jnp.sum(q[t:t + 1, lanes] * k[:, lanes] * decay_t, axis=-1, keepdims=True)
                    intra.append(jnp.sum(score * v[:, lanes], axis=0, keepdims=True))
                o = jnp.concatenate(intra, axis=0) + _dot(qh[:, lanes], state.astype(BF16))
                st_ref[seq, h] = decay_matrix(ea[:, lanes]) * state + _dot_tn(kh[:, lanes], vb[:, lanes])
                o_ref[rows, lanes] = _rms(o) * hn_ref[:, lanes] * gs_ref[rows, lanes]
            return carry

        lax.fori_loop(0, rows_total // blk, block, 0)

    @pl.when(j == nt - 1)
    def _():
        sf_ref[...] = st_ref[...]


def _hgrn(q, k, g, v, gs, s0, hnorm, b, t, nseq, tt):
    n, dh = q.shape
    nt = t // tt
    dk = dh // HGRN_HEADS
    rows = nseq * tt
    tl = min(tt, HGRN_CHUNK)
    row_spec = pl.BlockSpec((rows, dh), lambda i, j: (i * nt + j, 0))
    st_spec = pl.BlockSpec((nseq, HGRN_HEADS, dk, dk), lambda i, j: (i, 0, 0, 0))
    return pl.pallas_call(
        functools.partial(_hgrn_kernel, tl=tl, nt=nt),
        grid=(b // nseq, nt),
        in_specs=[row_spec] * 5 + [st_spec, pl.BlockSpec((1, dh), lambda i, j: (0, 0))],
        out_specs=[row_spec, st_spec],
        out_shape=[jax.ShapeDtypeStruct((n, dh), F32),
                   jax.ShapeDtypeStruct((b, HGRN_HEADS, dk, dk), F32)],
        scratch_shapes=[pltpu.VMEM((nseq, HGRN_HEADS, dk, dk), F32),
                        pltpu.VMEM((rows, dh), F32), pltpu.VMEM((rows, dh), F32),
                        pltpu.VMEM((rows, dh), F32), pltpu.VMEM((rows, dh), F32),
                        pltpu.VMEM((3 * rows, dh), F32)],
        compiler_params=_cparams(("parallel", "arbitrary")),
        name="hgrn",
    )(q, k, g, v, gs, s0, hnorm)


def _s5_prepare(a_re, a_im, log_dt, b_re, b_im, c_re, c_im, d_skip):
    ng, npp = a_re.shape
    nh = b_re.shape[-1]
    L = SSM_CHUNK
    gs = ng // SSM_SETS
    lam_re = jnp.minimum(a_re, MAX_REAL)
    lam_im = a_im
    dt = jnp.exp(log_dt)
    mag = jnp.exp(lam_re * dt)
    ab_re = mag * jnp.cos(lam_im * dt)
    ab_im = mag * jnp.sin(lam_im * dt)
    den = lam_re * lam_re + lam_im * lam_im
    co_re = ((ab_re - 1.0) * lam_re + ab_im * lam_im) / den
    co_im = (ab_im * lam_re - (ab_re - 1.0) * lam_im) / den
    bb_re = co_re[..., None] * b_re - co_im[..., None] * b_im
    bb_im = co_re[..., None] * b_im + co_im[..., None] * b_re
    tau = jnp.arange(L + 1, dtype=F32)[:, None, None]
    pw_mag = jnp.exp(tau * (lam_re * dt))
    pw_re = pw_mag * jnp.cos(tau * (lam_im * dt))
    pw_im = pw_mag * jnp.sin(tau * (lam_im * dt))
    ab_b_re = pw_re[:L, :, :, None] * bb_re - pw_im[:L, :, :, None] * bb_im
    ab_b_im = pw_re[:L, :, :, None] * bb_im + pw_im[:L, :, :, None] * bb_re
    kern = (jnp.einsum('gkp,lgph->lghk', c_re, ab_b_re, precision=HIGHEST)
            - jnp.einsum('gkp,lgph->lghk', c_im, ab_b_im, precision=HIGHEST))
    skip = d_skip[None, :, :, None] * jnp.eye(nh, dtype=F32)
    kern = kern + jnp.where(jnp.arange(L)[:, None, None, None] == 0, skip, 0.0)

    def group_block_diag(c):
        rows, w = c.shape[-2:]
        repeat = (jnp.arange(gs * w)[None, :] % w == jnp.arange(w)[:, None]).astype(F32)
        tiled = jnp.dot(c, repeat, precision=HIGHEST)
        rg = jnp.arange(rows)[:, None] // (rows // gs)
        cq = jnp.arange(gs * w)[None, :] // w
        return jnp.where(rg == cq, tiled, 0.0).astype(BF16).transpose(1, 0, 2, 3).reshape(
            SSM_SETS, L * rows, gs * w)

    bd = group_block_diag(kern.reshape(L, SSM_SETS, gs * nh, nh)).reshape(SSM_SETS, L, gs * nh, gs * nh)
    n_re = group_block_diag(ab_b_re[::-1].transpose(0, 1, 3, 2).reshape(L, SSM_SETS, gs * nh, npp))
    n_im = group_block_diag(ab_b_im[::-1].transpose(0, 1, 3, 2).reshape(L, SSM_SETS, gs * nh, npp))
    inc = jnp.concatenate([n_re, n_im], axis=-1)
    ca_re = c_re[None] * pw_re[1:, :, None, :] - c_im[None] * pw_im[1:, :, None, :]
    ca_im = c_re[None] * pw_im[1:, :, None, :] + c_im[None] * pw_re[1:, :, None, :]
    m = jnp.concatenate([group_block_diag(ca_re.reshape(L, SSM_SETS, gs * nh, npp)),
                         group_block_diag(-ca_im.reshape(L, SSM_SETS, gs * nh, npp))], axis=-1)
    a8 = jnp.concatenate([pw_re[L].reshape(SSM_SETS, 1, gs * npp),
                          pw_im[L].reshape(SSM_SETS, 1, gs * npp)], axis=-1)
    return bd, inc, m, a8


def _s5_kernel(u_ref, xr_ref, xi_ref, bd_ref, inc_ref, m_ref, a8_ref, y_ref, fr_ref, fi_ref, wt_ref, *, sequential):
    n = u_ref.shape[1] // SSM_CHUNK
    ns = xr_ref.shape[-1]
    u = jnp.concatenate([u_ref[0, pl.ds(s, n, stride=SSM_CHUNK), :] for s in range(SSM_CHUNK)], axis=-1)

    ny = SSM_CHUNK * LANES

    @pl.when(pl.program_id(1) == 0)
    def _():
        wt_ref[:, :ny] = jnp.zeros((wt_ref.shape[0], ny), BF16)
        for s in range(SSM_CHUNK):
            for t in range(s, SSM_CHUNK):
                wt_ref[s * LANES:(s + 1) * LANES, t * LANES:(t + 1) * LANES] = bd_ref[0, t - s]
        wt_ref[:, ny:] = inc_ref[0]

    res = _dot(u.astype(BF16), wt_ref[...])
    y_local = res[:, :ny]
    d_re = res[:, ny:ny + ns]
    d_im = res[:, ny + ns:]
    a_re = a8_ref[0][:, :ns]
    a_im = a8_ref[0][:, ns:]
    x0_re = xr_ref[0]
    x0_im = xi_ref[0]
    if sequential:
        row = lax.broadcasted_iota(jnp.int32, (n, ns), 0)
        first = row == 0
        x_re = d_re + jnp.where(first, a_re * x0_re - a_im * x0_im, 0.0)
        x_im = d_im + jnp.where(first, a_re * x0_im + a_im * x0_re, 0.0)
        p_re, p_im = a_re, a_im
        step = 1
        while step < n:
            s_re = jnp.where(row >= step, pltpu.roll(x_re, step, 0), 0.0)
            s_im = jnp.where(row >= step, pltpu.roll(x_im, step, 0), 0.0)
            x_re, x_im = x_re + p_re * s_re - p_im * s_im, x_im + p_re * s_im + p_im * s_re
            p_re, p_im = p_re * p_re - p_im * p_im, 2.0 * p_re * p_im
            step *= 2
        fr_ref[0] = x_re[n - 1:n]
        fi_ref[0] = x_im[n - 1:n]
        xc_re = jnp.where(first, x0_re, pltpu.roll(x_re, 1, 0))
        xc_im = jnp.where(first, x0_im, pltpu.roll(x_im, 1, 0))
    else:
        xc_re, xc_im = x0_re, x0_im
        fr_ref[0] = a_re * x0_re - a_im * x0_im + d_re
        fi_ref[0] = a_re * x0_im + a_im * x0_re + d_im
    xc = jnp.concatenate([xc_re, xc_im], axis=-1).astype(BF16)
    y = y_local + _dot_nt(xc, m_ref[0])
    for t in range(SSM_CHUNK):
        y_ref[0, pl.ds(t, n, stride=SSM_CHUNK), :] = y[:, t * LANES:(t + 1) * LANES]


def _s5(u, x_re, x_im, bd, inc, m, a8, n_tokens, sequential):
    sets = u.shape[0]
    nb, rb, _ = x_re.shape
    ns = m.shape[1] // 2
    st_spec = pl.BlockSpec((1, rb, ns), lambda gi, i: (i, 0, gi))
    st_shape = jax.ShapeDtypeStruct(x_re.shape, F32)
    tok_spec = pl.BlockSpec((1, n_tokens, LANES), lambda gi, i: (gi, i, 0))
    return pl.pallas_call(
        functools.partial(_s5_kernel, sequential=sequential),
        grid=(sets, nb),
        in_specs=[tok_spec, st_spec, st_spec,
                  pl.BlockSpec((1,) + bd.shape[1:], lambda gi, i: (gi, 0, 0, 0)),
                  pl.BlockSpec((1,) + inc.shape[1:], lambda gi, i: (gi, 0, 0)),
                  pl.BlockSpec((1,) + m.shape[1:], lambda gi, i: (gi, 0, 0)),
                  pl.BlockSpec((1, 1, 2 * ns), lambda gi, i: (gi, 0, 0))],
        out_specs=[tok_spec, st_spec, st_spec],
        out_shape=[jax.ShapeDtypeStruct(u.shape, F32), st_shape, st_shape],
        scratch_shapes=[pltpu.VMEM((inc.shape[1], SSM_CHUNK * LANES + inc.shape[2]), BF16)],
        compiler_params=_cparams(("parallel", "arbitrary")),
        name="s5",
    )(u, x_re, x_im, bd, inc, m, a8)


def _gelu_tanh(x):
    return 0.5 * x * (1.0 + jnp.tanh(math.sqrt(2.0 / math.pi) * (x + 0.044715 * (x * x * x))))


def _outproj_kernel(xp_ref, xs_ref, ohp_ref, ohs_ref, ysp_ref, yss_ref,
                    gate_p_ref, shift_p_ref, scale_p_ref, gate_s_ref, shift_s_ref, scale_s_ref,
                    wglu_ref, bglu_ref, sn_ref, wo_h_ref, wo_s_ref, nf_ref, wr_ref, br_ref,
                    x1p_ref, x1s_ref, h2_ref, pair1_ref, pair2_ref, rw_ref, cnt_ref, *, prompt_tiles):
    is_prompt = pl.program_id(0) < prompt_tiles
    _, rows, d = xp_ref.shape

    def per_token(p_ref, s_ref):
        per_seq = s_ref[0]
        seqs = per_seq.shape[0]
        rep = jnp.broadcast_to(per_seq[:, None, :], (seqs, rows // seqs, d)).reshape(rows, d)
        return jnp.where(is_prompt, p_ref[0], rep)

    x = jnp.where(is_prompt, xp_ref[0], xs_ref[0])
    oh = jnp.where(is_prompt, ohp_ref[...], ohs_ref[...])
    ys = jnp.concatenate([jnp.where(is_prompt, ysp_ref[s], yss_ref[s]) for s in range(SSM_SETS)], axis=-1)
    y = _gelu_tanh(ys)
    y = y * jax.nn.sigmoid(_dot(y.astype(BF16), wglu_ref[...]) + bglu_ref[...])
    o_s = _rms(y) * sn_ref[...]
    mix = _dot(oh.astype(BF16), wo_h_ref[...]) + _dot(o_s.astype(BF16), wo_s_ref[...])
    x1 = x + per_token(gate_p_ref, gate_s_ref) * mix

    @pl.when(is_prompt)
    def _():
        x1p_ref[0] = x1

    @pl.when(jnp.logical_not(is_prompt))
    def _():
        x1s_ref[0] = x1

    h2 = _rms(x1) * nf_ref[...]
    h2 = h2 * (1.0 + per_token(scale_p_ref, scale_s_ref)) + per_token(shift_p_ref, shift_s_ref)
    _store_token_tiles(h2_ref, h2)

    logits = _dot(h2.astype(BF16), wr_ref[...]) + br_ref[...]
    lane = lax.broadcasted_iota(jnp.int32, logits.shape, 1)
    neg = -jnp.inf
    gl = jnp.where(lane < MOE_GROUPS, logits, neg)
    gmax = jnp.max(gl, axis=-1, keepdims=True)
    gidx = jnp.min(jnp.where(gl == gmax, lane, ROUTER_LANES), axis=-1, keepdims=True)
    grp_w = 1.0 / jnp.sum(jnp.exp(gl - gmax), axis=-1, keepdims=True)
    e0 = EXPERT_LANE0 + gidx * MOE_PER_GROUP
    sel = jnp.where((lane >= e0) & (lane < e0 + MOE_PER_GROUP), logits, neg)
    m1 = jnp.max(sel, axis=-1, keepdims=True)
    i1 = jnp.min(jnp.where(sel == m1, lane, ROUTER_LANES), axis=-1, keepdims=True)
    sel2 = jnp.where(lane == i1, neg, sel)
    m2 = jnp.max(sel2, axis=-1, keepdims=True)
    i2 = jnp.min(jnp.where(sel2 == m2, lane, ROUTER_LANES), axis=-1, keepdims=True)
    e2 = jnp.exp(m2 - m1)
    w1 = 1.0 / (1.0 + e2)
    w2 = e2 / (1.0 + e2)
    rw_ref[...] = grp_w * (jnp.where(lane == 0, w1, 0.0) + jnp.where(lane == 1, w2, 0.0))

    @pl.when(pl.program_id(0) == 0)
    def _():
        cnt_ref[...] = jnp.zeros_like(cnt_ref)

    picked = (lane == i1) | (lane == i2)
    earlier = (lax.broadcasted_iota(jnp.int32, (rows, rows), 0)
               > lax.broadcasted_iota(jnp.int32, (rows, rows), 1))
    base = cnt_ref[...]
    before = _dot(earlier.astype(BF16), picked.astype(BF16)) + base
    rank1 = jnp.sum(jnp.where(lane == i1, before, 0.0), axis=-1, keepdims=True).astype(jnp.int32)
    rank2 = jnp.sum(jnp.where(lane == i2, before, 0.0), axis=-1, keepdims=True).astype(jnp.int32)
    cnt_ref[...] = base + jnp.sum(picked.astype(F32), axis=0, keepdims=True)
    info = jnp.where(lane == 0, ((i1 - EXPERT_LANE0) << RANK_BITS) | rank1,
                     jnp.where(lane == 1, ((i2 - EXPERT_LANE0) << RANK_BITS) | rank2, 0)).T
    pair1_ref[...] = info[0]
    pair2_ref[...] = info[1]


def _outproj(x_p, x_s, oh_p, oh_s, ys_p, ys_s, ada_p, ada_s, wglu_bf, bglu, snorm, wo_h, wo_s, nffn, wr, br, rows):
    d = x_p.shape[-1]
    dh = oh_p.shape[-1]
    n_p = x_p.shape[0] * x_p.shape[1]
    n_s = x_s.shape[0] * x_s.shape[1]
    n = n_p + n_s
    tiles_p = n_p // rows
    seqs = rows // x_s.shape[1]

    def pt(i):
        return jnp.minimum(i, tiles_p - 1)

    def st(i):
        return jnp.maximum(i - tiles_p, 0)

    tiles_per_seq = x_p.shape[1] // rows

    def ada_p_spec(col):
        return pl.BlockSpec((1, 1, d), lambda i: (pt(i) // tiles_per_seq, 0, col))

    def ada_s_spec(col):
        return pl.BlockSpec((1, seqs, d), lambda i: (st(i), 0, col))

    def full(a):
        return pl.BlockSpec(a.shape, lambda i: (0,) * a.ndim)

    xp3 = x_p.reshape(tiles_p, rows, d)
    xs3 = x_s.reshape(n_s // rows, rows, d)
    ada_s3 = ada_s.reshape(n_s // rows, seqs, -1)
    outs = pl.pallas_call(
        functools.partial(_outproj_kernel, prompt_tiles=tiles_p),
        grid=(n // rows,),
        in_specs=[pl.BlockSpec((1, rows, d), lambda i: (pt(i), 0, 0)),
                  pl.BlockSpec((1, rows, d), lambda i: (st(i), 0, 0)),
                  pl.BlockSpec((rows, dh), lambda i: (pt(i), 0)),
                  pl.BlockSpec((rows, dh), lambda i: (st(i), 0)),
                  pl.BlockSpec((SSM_SETS, rows, LANES), lambda i: (0, pt(i), 0)),
                  pl.BlockSpec((SSM_SETS, rows, LANES), lambda i: (0, st(i), 0)),
                  ada_p_spec(2), ada_p_spec(3), ada_p_spec(4), ada_s_spec(2), ada_s_spec(3), ada_s_spec(4),
                  full(wglu_bf), full(bglu), full(snorm), full(wo_h), full(wo_s), full(nffn),
                  full(wr), full(br)],
        out_specs=[pl.BlockSpec((1, rows, d), lambda i: (pt(i), 0, 0)),
                   pl.BlockSpec((1, rows, d), lambda i: (st(i), 0, 0)),
                   pl.BlockSpec((rows * TOKEN_TILE_ROWS, LANES), lambda i: (i, 0)),
                   pl.BlockSpec((rows,), lambda i: (i,)),
                   pl.BlockSpec((rows,), lambda i: (i,)),
                   pl.BlockSpec((rows, ROUTER_LANES), lambda i: (i, 0)),
                   pl.BlockSpec((1, ROUTER_LANES), lambda i: (0, 0))],
        out_shape=[jax.ShapeDtypeStruct(xp3.shape, F32),
                   jax.ShapeDtypeStruct(xs3.shape, F32),
                   jax.ShapeDtypeStruct((n * TOKEN_TILE_ROWS, LANES), F32),
                   jax.ShapeDtypeStruct((n,), jnp.int32),
                   jax.ShapeDtypeStruct((n,), jnp.int32),
                   jax.ShapeDtypeStruct((n, ROUTER_LANES), F32),
                   jax.ShapeDtypeStruct((1, ROUTER_LANES), F32)],
        compiler_params=_cparams(("arbitrary",)),
        name="outproj",
    )(xp3, xs3, oh_p, oh_s, ys_p, ys_s, ada_p, ada_p, ada_p, ada_s3, ada_s3, ada_s3,
      wglu_bf, bglu, snorm, wo_h, wo_s, nffn, wr, br)
    x1_p, x1_s = outs[0].reshape(x_p.shape), outs[1].reshape(x_s.shape)
    return (x1_p, x1_s) + tuple(outs[2:])


def _moe_schedule(cnt, n_slots):
    c = cnt[0, EXPERT_LANE0:EXPERT_LANE0 + N_EXPERTS].astype(jnp.int32)
    seg_end = jnp.cumsum(c)
    seg_start = seg_end - c
    first_tile = seg_start // MOE_TILE
    tiles = jnp.where(c > 0, (seg_end - 1) // MOE_TILE - first_tile + 1, 0)
    cum = jnp.cumsum(tiles)
    n_items = cum[-1]
    item = jnp.minimum(jnp.arange(n_slots, dtype=jnp.int32), n_items - 1)
    item_exp = jnp.sum(item[:, None] >= cum[None, :], axis=1).astype(jnp.int32)
    shares = ((c > 0) & (seg_start % MOE_TILE != 0)).astype(jnp.int32)
    item_tile = item - jnp.sum((item[:, None] >= (cum - tiles)[None, :]) * shares[None, :], axis=1)
    return seg_start.astype(jnp.int32), seg_end.astype(jnp.int32), item_tile.astype(jnp.int32), item_exp


def _token_tile(ref, t):
    return ref.at[pl.ds(pl.multiple_of(t * TOKEN_TILE_ROWS, TOKEN_TILE_ROWS), TOKEN_TILE_ROWS)]


def _store_token_tiles(ref, x):
    rows = x.shape[0]
    for c in range(TOKEN_TILE_ROWS):
        ref[pl.ds(c, rows, stride=TOKEN_TILE_ROWS), :] = x[:, c * LANES:(c + 1) * LANES]


def _load_token_tiles(ref, rows):
    return [ref[pl.ds(c, rows, stride=TOKEN_TILE_ROWS), :] for c in range(TOKEN_TILE_ROWS)]


def _positions_kernel(seg_ref, p1_ref, p2_ref, o1_ref, o2_ref):
    for p_ref, o_ref in ((p1_ref, o1_ref), (p2_ref, o2_ref)):
        pair = p_ref[...]
        expert = pair >> RANK_BITS
        pos = pair & ((1 << RANK_BITS) - 1)
        for e in range(N_EXPERTS):
            pos = pos + jnp.where(expert == e, seg_ref[e], 0)
        o_ref[...] = pos


def _positions(seg, pair1, pair2):
    n = pair1.shape[0]
    shape2 = (n // LANES, LANES)
    full = pl.BlockSpec(shape2, lambda: (0, 0))
    pos1, pos2 = pl.pallas_call(
        _positions_kernel,
        in_specs=[pl.BlockSpec(memory_space=pltpu.SMEM), full, full],
        out_specs=[full, full],
        out_shape=[jax.ShapeDtypeStruct(shape2, jnp.int32)] * 2,
        compiler_params=pltpu.CompilerParams(vmem_limit_bytes=VMEM_LIMIT),
        name="moe_positions",
    )(seg, pair1.reshape(shape2), pair2.reshape(shape2))
    return pos1.reshape(n), pos2.reshape(n)


def _inverse_kernel(p1_ref, p2_ref, inv_ref):
    ts = p1_ref.shape[0]
    base = pl.program_id(0) * ts

    def body(t, carry):
        inv_ref[p1_ref[t]] = base + t
        inv_ref[p2_ref[t]] = base + t
        return carry

    lax.fori_loop(0, ts, body, 0, unroll=2 * DMA_UNROLL)


def _inverse(pos1, pos2, ts):
    n = pos1.shape[0]
    idx_spec = pl.BlockSpec((ts,), lambda i: (i,), memory_space=pltpu.SMEM)
    return pl.pallas_call(
        _inverse_kernel,
        grid=(n // ts,),
        in_specs=[idx_spec, idx_spec],
        out_specs=pl.BlockSpec(memory_space=pltpu.SMEM),
        out_shape=jax.ShapeDtypeStruct((2 * n,), jnp.int32),
        compiler_params=_cparams(("arbitrary",)),
        name="moe_inverse",
    )(pos1, pos2)


def _row_gather(src_ref, idx_of, dst_ref, dst_row0, sem, n_rows):
    def copy(t):
        return pltpu.make_async_copy(_token_tile(src_ref, idx_of(t)), _token_tile(dst_ref, dst_row0 + t), sem)

    def start():
        def body(t, carry):
            copy(2 * t).start(priority=0)
            copy(2 * t + 1).start(priority=1)
            return carry
        lax.fori_loop(0, n_rows // 2, body, 0, unroll=DMA_UNROLL // 2)

    def wait():
        def body(t, carry):
            copy(t).wait()
            return carry
        lax.fori_loop(0, n_rows, body, 0, unroll=DMA_UNROLL)

    return start, wait


def _experts_kernel(lo_ref, hi_ref, tile_ref, exp_ref, inv_ref, inv_next_ref, h_ref,
                    wg_ref, wu_ref, wd_ref, ys_ref, xbuf_ref, sems, wg_s, wu_s, wd_s, *, n_slots):
    i = pl.program_id(0)
    prev = jnp.maximum(i - 1, 0)
    e = exp_ref[i]
    slot = i % 2

    @pl.when((i == 0) | (e != exp_ref[prev]))
    def _():
        wg_s[...] = wg_ref[0].astype(BF16)
        wu_s[...] = wu_ref[0].astype(BF16)
        wd_s[...] = wd_ref[0].astype(BF16)

    start_this, wait_this = _row_gather(h_ref, lambda t: inv_ref[t], xbuf_ref, slot * MOE_TILE,
                                        sems.at[slot], MOE_TILE)
    _, wait_next = _row_gather(h_ref, lambda t: inv_next_ref[t], xbuf_ref, (1 - slot) * MOE_TILE,
                               sems.at[1 - slot], MOE_TILE)

    def start_next(rows):
        for t in rows:
            pltpu.make_async_copy(_token_tile(h_ref, inv_next_ref[t]),
                                  _token_tile(xbuf_ref, (1 - slot) * MOE_TILE + t),
                                  sems.at[1 - slot]).start(priority=t % 2)

    @pl.when(i == 0)
    def _():
        start_this()

    wait_this()
    third = -(-MOE_TILE // 3)
    base = pl.multiple_of(slot * (MOE_TILE * TOKEN_TILE_ROWS), MOE_TILE * TOKEN_TILE_ROWS)
    x = jnp.concatenate([xbuf_ref[pl.ds(base + c, MOE_TILE, stride=TOKEN_TILE_ROWS), :]
                         for c in range(TOKEN_TILE_ROWS)], axis=-1).astype(BF16)
    start_next(range(0, third))
    gate = _dot(x, wg_s[...])
    start_next(range(third, 2 * third))
    up = _dot(x, wu_s[...])
    start_next(range(2 * third, MOE_TILE))
    out = _dot((_silu(gate) * up).astype(BF16), wd_s[...])
    row = tile_ref[i] * MOE_TILE + lax.broadcasted_iota(jnp.int32, (MOE_TILE, 1), 0)
    mine = (row >= lo_ref[e]) & (row < hi_ref[e])
    first_visit = (i == 0) | (tile_ref[i] != tile_ref[prev])

    @pl.when(first_visit)
    def _():
        _store_token_tiles(ys_ref, jnp.where(mine, out, 0.0))

    @pl.when(jnp.logical_not(first_visit))
    def _():
        old = jnp.concatenate(_load_token_tiles(ys_ref, MOE_TILE), axis=-1)
        _store_token_tiles(ys_ref, jnp.where(mine, out, old))

    @pl.when(i == n_slots - 1)
    def _():
        wait_next()


def _experts(seg_lo, seg_hi, item_tile, item_exp, inv, h2, wg, wu, wd):
    n_slots = item_tile.shape[0]
    _, d, de = wg.shape

    def w_spec(shape):
        return pl.BlockSpec((1,) + shape, lambda i, lo, hi, tile, ex: (ex[i], 0, 0))

    return pl.pallas_call(
        functools.partial(_experts_kernel, n_slots=n_slots),
        grid_spec=pltpu.PrefetchScalarGridSpec(
            num_scalar_prefetch=4, grid=(n_slots,),
            in_specs=[pl.BlockSpec((MOE_TILE,), lambda i, lo, hi, tile, ex: (tile[i],),
                                   memory_space=pltpu.SMEM),
                      pl.BlockSpec((MOE_TILE,), lambda i, lo, hi, tile, ex: (
                          tile[jnp.minimum(i + 1, n_slots - 1)],), memory_space=pltpu.SMEM),
                      pl.BlockSpec(memory_space=pltpu.HBM),
                      w_spec((d, de)), w_spec((d, de)), w_spec((de, d))],
            out_specs=pl.BlockSpec((MOE_TILE * TOKEN_TILE_ROWS, LANES),
                                   lambda i, lo, hi, tile, ex: (tile[i], 0)),
            scratch_shapes=[pltpu.VMEM((2 * MOE_TILE * TOKEN_TILE_ROWS, LANES), F32),
                            pltpu.SemaphoreType.DMA((2,)),
                            pltpu.VMEM((d, de), BF16), pltpu.VMEM((d, de), BF16), pltpu.VMEM((de, d), BF16)]),
        out_shape=jax.ShapeDtypeStruct((inv.shape[0] * TOKEN_TILE_ROWS, LANES), F32),
        compiler_params=_cparams(("arbitrary",)),
        name="moe_experts",
    )(seg_lo, seg_hi, item_tile, item_exp, inv, inv, h2, wg, wu, wd)


def _combine_kernel(p1_ref, p2_ref, p1n_ref, p2n_ref, rw_ref, x1_ref, gate_ref, shift_ref, scale_ref,
                    nfin_ref, ys_ref, y_ref, r1_ref, r2_ref, sems, *, n_steps):
    _, rows, d = x1_ref.shape
    s = pl.program_id(0)
    slot = s % 2

    def gathers(pa_ref, pb_ref, slot):
        g1 = _row_gather(ys_ref, lambda t: pa_ref[t], r1_ref, slot * rows, sems.at[slot], rows)
        g2 = _row_gather(ys_ref, lambda t: pb_ref[t], r2_ref, slot * rows, sems.at[slot], rows)
        return g1, g2

    this = gathers(p1_ref, p2_ref, slot)
    nxt = gathers(p1n_ref, p2n_ref, 1 - slot)

    def start_next(tokens):
        for t in tokens:
            for k, (p_ref, r_ref) in enumerate(((p1n_ref, r1_ref), (p2n_ref, r2_ref))):
                pltpu.make_async_copy(_token_tile(ys_ref, p_ref[t]), _token_tile(r_ref, (1 - slot) * rows + t),
                                      sems.at[1 - slot]).start(priority=k)

    @pl.when(s == 0)
    def _():
        this[0][0]()
        this[1][0]()

    this[0][1]()
    this[1][1]()

    def per_token(ref):
        per_seq = ref[0]
        seqs = per_seq.shape[0]
        return jnp.broadcast_to(per_seq[:, None, :], (seqs, rows // seqs, d)).reshape(rows, d)

    rw = rw_ref[...]
    w1 = rw[:, 0:1]
    w2 = rw[:, 1:2]
    base = pl.multiple_of(slot * (rows * TOKEN_TILE_ROWS), rows * TOKEN_TILE_ROWS)
    per_piece = rows // TOKEN_TILE_ROWS
    pieces = []
    for c in range(TOKEN_TILE_ROWS):
        pieces.append(w1 * r1_ref[pl.ds(base + c, rows, stride=TOKEN_TILE_ROWS), :]
                      + w2 * r2_ref[pl.ds(base + c, rows, stride=TOKEN_TILE_ROWS), :])
        start_next(range(c * per_piece, (c + 1) * per_piece))
    moe = jnp.concatenate(pieces, axis=-1)
    x2 = x1_ref[0] + per_token(gate_ref) * moe
    hf = _rms(x2) * nfin_ref[...]
    y_ref[0] = hf * (1.0 + per_token(scale_ref)) + per_token(shift_ref)

    @pl.when(s == n_steps - 1)
    def _():
        nxt[0][1]()
        nxt[1][1]()


def _combine(pos1, pos2, rw, x1, ada3, adaf3, nfin, ys, rows, tile0):
    b, t, d = x1.shape
    n_steps = b * t // rows
    seqs = max(rows // t, 1)
    tiles_per_seq = max(t // rows, 1)
    x3 = x1.reshape(n_steps, rows, d)
    ada_v = ada3.reshape(b // seqs, seqs, -1)
    adaf_v = adaf3.reshape(b // seqs, seqs, -1)
    x_spec = pl.BlockSpec((1, rows, d), lambda i: (i, 0, 0))

    def idx_spec(step):
        return pl.BlockSpec((rows,), lambda i: (tile0 + step(i),), memory_space=pltpu.SMEM)

    def ada_spec(col):
        return pl.BlockSpec((1, seqs, d), lambda i: (i // tiles_per_seq, 0, col))

    def cur(i):
        return i

    def nxt(i):
        return jnp.minimum(i + 1, n_steps - 1)

    return pl.pallas_call(
        functools.partial(_combine_kernel, n_steps=n_steps),
        grid=(n_steps,),
        in_specs=[idx_spec(cur), idx_spec(cur), idx_spec(nxt), idx_spec(nxt),
                  pl.BlockSpec((rows, ROUTER_LANES), lambda i: (tile0 + i, 0)),
                  x_spec, ada_spec(5), ada_spec(0), ada_spec(1),
                  pl.BlockSpec((1, d), lambda i: (0, 0)),
                  pl.BlockSpec(memory_space=pltpu.HBM)],
        out_specs=x_spec,
        scratch_shapes=[pltpu.VMEM((2 * rows * TOKEN_TILE_ROWS, LANES), F32),
                        pltpu.VMEM((2 * rows * TOKEN_TILE_ROWS, LANES), F32),
                        pltpu.SemaphoreType.DMA((2,))],
        out_shape=jax.ShapeDtypeStruct(x3.shape, F32),
        compiler_params=_cparams(("arbitrary",)),
        name="moe_combine",
    )(pos1, pos2, pos1, pos2, rw, x3, ada_v, adaf_v, adaf_v, nfin, ys).reshape(x1.shape)


def _mixer(x, ada3, s_h, s_re, s_im, p, *, bb, tt, hgrn_seqs, hgrn_tokens, sequential):
    b, t, d = x.shape
    n = b * t
    q, k, g, v, gs, u = _inproj(x, ada3, p['norm_mix'], p['w_in'], p['lb'], bb, tt)
    oh, s_h_new = _hgrn(q, k, g, v, gs, s_h, p['hgrn_norm'], b, t, hgrn_seqs, hgrn_tokens)
    if sequential:
        xr, xi = s_re.reshape(b, 1, -1), s_im.reshape(b, 1, -1)
        n_tokens = t
    else:
        xr, xi = s_re.reshape(1, b, -1), s_im.reshape(1, b, -1)
        n_tokens = n
    ys, fr, fi = _s5(u, xr, xi, p['ssm_bd'], p['ssm_inc'], p['ssm_m'], p['ssm_a8'], n_tokens, sequential)
    states = (s_h_new[None], fr.reshape(1, b, s_re.shape[-2], s_re.shape[-1]),
              fi.reshape(1, b, s_re.shape[-2], s_re.shape[-1]))
    return oh, ys, states


def kernel(x_prompt, x_sample, c_prompt, c_sample, state_hgrn, state_ssm_re, state_ssm_im, hgrn_lb_logits, w_ada, b_ada, norm_mix, w_in, hgrn_norm, ssm_a_re, ssm_a_im, ssm_log_dt, ssm_b_re, ssm_b_im, ssm_c_re, ssm_c_im, ssm_d, ssm_w_glu, ssm_b_glu, ssm_norm, w_out, norm_ffn, moe_w_group, moe_b_group, moe_w_router, moe_b_router, moe_w_gate, moe_w_up, moe_w_down, w_ada_final, b_ada_final, norm_final):
    depth = w_ada.shape[0]
    assert depth == 1
    d = x_prompt.shape[-1]
    bp = x_prompt.shape[0]
    dh = hgrn_norm.shape[-1]
    dk = dh // HGRN_HEADS
    de = moe_w_gate.shape[-1]
    n_exp = MOE_GROUPS * MOE_PER_GROUP

    lb = jax.nn.softmax(hgrn_lb_logits.astype(F32), axis=0)[0].reshape(1, dh)
    bd, inc, m, a8 = _s5_prepare(ssm_a_re[0], ssm_a_im[0], ssm_log_dt[0], ssm_b_re[0], ssm_b_im[0],
                                 ssm_c_re[0], ssm_c_im[0], ssm_d[0])

    def router_lanes(group_part, expert_part):
        rows = group_part.shape[0]
        return jnp.concatenate(
            [group_part, jnp.zeros((rows, EXPERT_LANE0 - MOE_GROUPS), F32), expert_part,
             jnp.zeros((rows, ROUTER_LANES - EXPERT_LANE0 - n_exp), F32)], axis=1)

    w_rt = router_lanes(moe_w_group[0], moe_w_router[0].transpose(1, 0, 2).reshape(d, n_exp))
    b_rt = router_lanes(moe_b_group[0].reshape(1, MOE_GROUPS), moe_b_router[0].reshape(1, n_exp))
    p = dict(
        lb=lb, norm_mix=norm_mix[0].reshape(1, d), w_in=w_in[0].astype(BF16),
        hgrn_norm=hgrn_norm[0].reshape(1, dh),
        ssm_bd=bd, ssm_inc=inc, ssm_m=m, ssm_a8=a8,
        w_glu=ssm_w_glu[0].astype(BF16), b_glu=ssm_b_glu[0].reshape(1, -1), ssm_norm=ssm_norm[0].reshape(1, -1),
        wo_h=w_out[0, :dh].astype(BF16), wo_s=w_out[0, dh:].astype(BF16),
        norm_ffn=norm_ffn[0].reshape(1, d), w_rt=w_rt.astype(BF16), b_rt=b_rt,
    )
    wg = moe_w_gate[0].reshape(n_exp, d, de)
    wu = moe_w_up[0].reshape(n_exp, d, de)
    wd = moe_w_down[0].reshape(n_exp, de, d)
    nfin = norm_final.reshape(1, d)

    c_all = jnp.concatenate([c_prompt, c_sample], axis=0)
    ada = _silu_linear(c_all, w_ada[0], b_ada[0])
    adaf = _silu_linear(c_all, w_ada_final, b_ada_final)
    ada_p, ada_s = ada[:bp].reshape(bp, 1, -1), ada[bp:].reshape(x_sample.shape[0], 1, -1)
    adaf_p, adaf_s = adaf[:bp].reshape(bp, 1, -1), adaf[bp:].reshape(x_sample.shape[0], 1, -1)

    zeros_h = jnp.zeros((bp, HGRN_HEADS, dk, dk), F32)
    zeros_s = jnp.zeros((bp,) + state_ssm_re.shape[2:], F32)
    oh_p, ys_p, st_p = _mixer(x_prompt, ada_p, zeros_h, zeros_s, zeros_s, p,
                              bb=1, tt=512, hgrn_seqs=1, hgrn_tokens=512, sequential=True)
    oh_s, ys_s, st_s = _mixer(x_sample, ada_s, state_hgrn[0], state_ssm_re[0], state_ssm_im[0], p,
                              bb=32, tt=8, hgrn_seqs=16, hgrn_tokens=8, sequential=False)
    x1_p, x1_s, h2, pair1, pair2, rw, cnt = _outproj(
        x_prompt, x_sample, oh_p, oh_s, ys_p, ys_s, ada_p, ada_s, p['w_glu'], p['b_glu'], p['ssm_norm'],
        p['wo_h'], p['wo_s'], p['norm_ffn'], p['w_rt'], p['b_rt'], TOKEN_ROWS)

    n_pairs = 2 * pair1.shape[0]
    n_slots = n_pairs // MOE_TILE + N_EXPERTS
    seg, seg_end, item_tile, item_exp = _moe_schedule(cnt, n_slots)
    pos1, pos2 = _positions(seg, pair1, pair2)
    inv = _inverse(pos1, pos2, 1024)
    ys = _experts(seg, seg_end, item_tile, item_exp, inv, h2, wg, wu, wd)
    tiles_p = x_prompt.shape[0] * x_prompt.shape[1] // TOKEN_ROWS
    y_p = _combine(pos1, pos2, rw, x1_p, ada_p, adaf_p, nfin, ys, TOKEN_ROWS, 0)
    y_s = _combine(pos1, pos2, rw, x1_s, ada_s, adaf_s, nfin, ys, TOKEN_ROWS, tiles_p)
    return (y_p, y_s) + st_p + st_s
```

```python
import functools
import math

import jax
import jax.numpy as jnp
from jax import lax
from jax.experimental import pallas as pl
from jax.experimental.pallas import tpu as pltpu

F32 = jnp.float32
BF16 = jnp.bfloat16
HIGHEST = lax.Precision.HIGHEST

EPS = 1e-6
MAX_REAL = -1e-4
HGRN_HEADS = 4
HGRN_CHUNK = 128
HGRN_SAFE_EXPONENT = 80.0
HGRN_EXACT_BLOCK = 8
SSM_GROUP = 16
SSM_STATE = 64
SSM_CHUNK = 8
SSM_SETS = 4
MOE_GROUPS = 4
MOE_PER_GROUP = 8
N_EXPERTS = MOE_GROUPS * MOE_PER_GROUP
ROUTER_LANES = 128
EXPERT_LANE0 = 32
RANK_BITS = 20
MOE_TILE = 256
LANES = 128
TOKEN_TILE_ROWS = 8
DMA_UNROLL = 8
TOKEN_ROWS = 512
VMEM_LIMIT = 56 * 1024 * 1024


def _cparams(sem):
    return pltpu.CompilerParams(dimension_semantics=sem, vmem_limit_bytes=VMEM_LIMIT)


def _silu(x):
    return x * jax.nn.sigmoid(x)


def _rms(x):
    return x * lax.rsqrt(jnp.mean(x * x, axis=-1, keepdims=True) + EPS)


def _dot(a, b):
    return jnp.dot(a, b, preferred_element_type=F32)


def _dot_nt(a, b):
    return lax.dot_general(a, b, (((1,), (1,)), ((), ())), preferred_element_type=F32)


def _dot_tn(a, b, precision=None):
    return lax.dot_general(a, b, (((0,), (0,)), ((), ())), preferred_element_type=F32,
                           precision=precision)


def _silu_linear_kernel(c_ref, w_ref, b_ref, o_ref):
    a = _silu(c_ref[...]).astype(BF16)
    o_ref[...] = _dot(a, w_ref[...].astype(BF16)) + b_ref[...]


def _silu_linear(c, w, b):
    m, d = c.shape
    n = w.shape[1]
    tn = 1024
    return pl.pallas_call(
        _silu_linear_kernel,
        grid=(n // tn,),
        in_specs=[pl.BlockSpec((m, d), lambda j: (0, 0)),
                  pl.BlockSpec((d, tn), lambda j: (0, j)),
                  pl.BlockSpec((1, tn), lambda j: (0, j))],
        out_specs=pl.BlockSpec((m, tn), lambda j: (0, j)),
        out_shape=jax.ShapeDtypeStruct((m, n), F32),
        compiler_params=_cparams(("parallel",)),
        name="silu_linear",
    )(c, w, b.reshape(1, n))


def _inproj_kernel(x_ref, shift_ref, scale_ref, gain_ref, w_ref, lb_ref,
                   q_ref, k_ref, g_ref, v_ref, gs_ref, u_ref, *, dh):
    bb, tt, d = x_ref.shape
    h = _rms(x_ref[...]) * gain_ref[...]
    h = h * (1.0 + scale_ref[...]) + shift_ref[...]
    proj = _dot(h.reshape(bb * tt, d).astype(BF16), w_ref[...])
    lb = lb_ref[...]
    f = lb + (1.0 - lb) * jax.nn.sigmoid(proj[:, dh:2 * dh])
    q_ref[...] = _silu(proj[:, :dh]) * (float(dh // HGRN_HEADS) ** -0.5)
    k_ref[...] = 1.0 - f
    g_ref[...] = jnp.log(f)
    v_ref[...] = proj[:, 2 * dh:3 * dh]
    gs_ref[...] = _silu(proj[:, 3 * dh:4 * dh])
    for s in range(SSM_SETS):
        u_ref[s] = proj[:, 4 * dh + s * LANES:4 * dh + (s + 1) * LANES]


def _inproj(x, ada3, gain, w_in_bf, lb, bb, tt):
    b, t, d = x.shape
    dh = lb.shape[-1]
    nt = t // tt
    rows = bb * tt
    n = b * t
    row_spec = pl.BlockSpec((rows, dh), lambda i, j: (i * nt + j, 0))
    out = jax.ShapeDtypeStruct((n, dh), F32)
    return pl.pallas_call(
        functools.partial(_inproj_kernel, dh=dh),
        grid=(b // bb, nt),
        in_specs=[pl.BlockSpec((bb, tt, d), lambda i, j: (i, j, 0)),
                  pl.BlockSpec((bb, 1, d), lambda i, j: (i, 0, 0)),
                  pl.BlockSpec((bb, 1, d), lambda i, j: (i, 0, 1)),
                  pl.BlockSpec((1, d), lambda i, j: (0, 0)),
                  pl.BlockSpec(w_in_bf.shape, lambda i, j: (0, 0)),
                  pl.BlockSpec((1, dh), lambda i, j: (0, 0))],
        out_specs=[row_spec] * 5 + [pl.BlockSpec((SSM_SETS, rows, LANES), lambda i, j: (0, i * nt + j, 0))],
        out_shape=[out] * 5 + [jax.ShapeDtypeStruct((SSM_SETS, n, LANES), F32)],
        compiler_params=_cparams(("parallel", "parallel")),
        name="inproj",
    )(x, ada3, ada3, gain, w_in_bf, lb)


def _split3(x):
    hi = x.astype(BF16)
    r1 = x - hi.astype(F32)
    mid = r1.astype(BF16)
    lo = (r1 - mid.astype(F32)).astype(BF16)
    return hi, mid, lo


def _hgrn_kernel(q_ref, k_ref, g_ref, v_ref, gs_ref, s0_ref, hn_ref, o_ref, sf_ref,
                 st_ref, intra_ref, qh_ref, kh_ref, ea_ref, sums_ref, *, tl, nt):
    j = pl.program_id(1)
    rows_total, dh = q_ref.shape
    dk = dh // HGRN_HEADS
    c = HGRN_CHUNK
    seqs = c // tl
    n_chunks = rows_total // c

    @pl.when(j == 0)
    def _():
        st_ref[...] = s0_ref[...]

    r = lax.broadcasted_iota(jnp.int32, (c, c), 0)
    s = lax.broadcasted_iota(jnp.int32, (c, c), 1)
    same_seq = (r // tl) == (s // tl)
    causal = same_seq & (r >= s)
    upto_mid = same_seq & ((s % tl) < tl // 2)
    one_seq = tl == c
    sums = (causal if one_seq else jnp.concatenate([causal, upto_mid, same_seq], axis=0)).astype(BF16)
    ref_rows = 1 if one_seq else c
    eye3 = (lax.broadcasted_iota(jnp.int32, (dk, 3 * dk), 1) % dk
            == lax.broadcasted_iota(jnp.int32, (dk, 3 * dk), 0)).astype(BF16)

    def decay_matrix(e_row):
        parts = jnp.concatenate(_split3(e_row), axis=-1)
        return _dot_nt(eye3, jnp.broadcast_to(parts, (dk, 3 * dk)))

    worst = jnp.float32(0.0)
    for ci in range(n_chunks):
        rows = slice(ci * c, (ci + 1) * c)
        g_parts = _split3(g_ref[rows, :])
        acc = _dot(sums, g_parts[0]) + _dot(sums, g_parts[1]) + _dot(sums, g_parts[2])
        if one_seq:
            a_mid, a_end = acc[c // 2 - 1:c // 2], acc[c - 1:c]
            sums_ref[ci * 3 * c:ci * 3 * c + c, :] = acc
            sums_ref[ci * 3 * c + c:ci * 3 * c + c + 1, :] = a_mid
            sums_ref[ci * 3 * c + 2 * c:ci * 3 * c + 2 * c + 1, :] = a_end
        else:
            a_mid, a_end = acc[c:2 * c], acc[2 * c:]
            sums_ref[ci * 3 * c:(ci + 1) * 3 * c, :] = acc
        worst = jnp.maximum(worst, jnp.max(jnp.maximum(jnp.abs(a_mid), jnp.abs(a_end - a_mid))))
    factorised_is_safe = worst < HGRN_SAFE_EXPONENT

    @pl.when(factorised_is_safe)
    def _():
        for ci in range(n_chunks):
            rows = slice(ci * c, (ci + 1) * c)
            a = sums_ref[ci * 3 * c:ci * 3 * c + c, :]
            a_mid = sums_ref[ci * 3 * c + c:ci * 3 * c + c + ref_rows, :]
            a_end = sums_ref[ci * 3 * c + 2 * c:ci * 3 * c + 2 * c + ref_rows, :]
            e_mid = jnp.exp(a_mid)
            e_tail = jnp.exp(a_end - a_mid)
            qt = q_ref[rows, :] * jnp.exp(a - a_mid)
            kt = k_ref[rows, :] * jnp.exp(a_mid - a)
            qh_ref[rows, :] = qt * e_mid
            kh_ref[rows, :] = kt * e_tail
            ea_ref[rows.start:rows.start + ref_rows, :] = e_mid * e_tail
            qt = qt.astype(BF16)
            kt = kt.astype(BF16)
            v = v_ref[rows, :].astype(BF16)
            for h in range(HGRN_HEADS):
                lanes = slice(h * dk, (h + 1) * dk)
                sc = jnp.where(causal, _dot_nt(qt[:, lanes], kt[:, lanes]), 0.0).astype(BF16)
                intra_ref[rows, lanes] = _dot(sc, v[:, lanes])

            for si in range(seqs):
                seq = ci * seqs + si if tl < c else 0
                srows = slice(si * tl, (si + 1) * tl)
                orows = slice(ci * c + si * tl, ci * c + (si + 1) * tl)
                for h in range(HGRN_HEADS):
                    lanes = slice(h * dk, (h + 1) * dk)
                    state = st_ref[seq, h]
                    o = intra_ref[orows, lanes] + _dot(qh_ref[orows, lanes].astype(BF16), state.astype(BF16))
                    decay = decay_matrix(ea_ref[orows.start:orows.start + 1, lanes])
                    st_ref[seq, h] = decay * state + _dot_tn(kh_ref[orows, lanes].astype(BF16),
                                                             v[srows, lanes])
                    o_ref[orows, lanes] = _rms(o) * hn_ref[:, lanes] * gs_ref[orows, lanes]

    @pl.when(jnp.logical_not(factorised_is_safe))
    def _():
        blk = HGRN_EXACT_BLOCK
        blocks_per_seq = max(tl // blk, 1)
        tri = (lax.broadcasted_iota(jnp.int32, (blk, blk), 0)
               >= lax.broadcasted_iota(jnp.int32, (blk, blk), 1)).astype(BF16)
        sub = lax.broadcasted_iota(jnp.int32, (blk, dk), 0)

        def block(bi, carry):
            rows = pl.ds(pl.multiple_of(bi * blk, blk), blk)
            seq = bi // blocks_per_seq if tl < c else 0
            g_parts = _split3(g_ref[rows, :])
            a = _dot(tri, g_parts[0]) + _dot(tri, g_parts[1]) + _dot(tri, g_parts[2])
            a_end = a[blk - 1:blk]
            q = q_ref[rows, :]
            k = k_ref[rows, :]
            v = v_ref[rows, :]
            qh = (q * jnp.exp(a)).astype(BF16)
            kh = (k * jnp.exp(a_end - a)).astype(BF16)
            ea = jnp.exp(a_end)
            vb = v.astype(BF16)
            for h in range(HGRN_HEADS):
                lanes = slice(h * dk, (h + 1) * dk)
                state = st_ref[seq, h]
                intra = []
                for t in range(blk):
                    live = sub <= t
                    decay_t = jnp.where(live, jnp.exp(jnp.where(live, a[t:t + 1, lanes] - a[:, lanes], 0.0)), 0.0)
                    score = jnp.sum(q[t:t + 1, lanes] * k[:, lanes] * decay_t, axis=-1, keepdims=True)
                    intra.append(jnp.sum(score * v[:, lanes], axis=0, keepdims=True))
                o = jnp.concatenate(intra, axis=0) + _dot(qh[:, lanes], state.astype(BF16))
                st_ref[seq, h] = decay_matrix(ea[:, lanes]) * state + _dot_tn(kh[:, lanes], vb[:, lanes])
                o_ref[rows, lanes] = _rms(o) * hn_ref[:, lanes] * gs_ref[rows, lanes]
            return carry

        lax.fori_loop(0, rows_total // blk, block, 0)

    @pl.when(j == nt - 1)
    def _():
        sf_ref[...] = st_ref[...]


def _hgrn(q, k, g, v, gs, s0, hnorm, b, t, nseq, tt):
    n, dh = q.shape
    nt = t // tt
    dk = dh // HGRN_HEADS
    rows = nseq * tt
    tl = min(tt, HGRN_CHUNK)
    row_spec = pl.BlockSpec((rows, dh), lambda i, j: (i * nt + j, 0))
    st_spec = pl.BlockSpec((nseq, HGRN_HEADS, dk, dk), lambda i, j: (i, 0, 0, 0))
    return pl.pallas_call(
        functools.partial(_hgrn_kernel, tl=tl, nt=nt),
        grid=(b // nseq, nt),
        in_specs=[row_spec] * 5 + [st_spec, pl.BlockSpec((1, dh), lambda i, j: (0, 0))],
        out_specs=[row_spec, st_spec],
        out_shape=[jax.ShapeDtypeStruct((n, dh), F32),
                   jax.ShapeDtypeStruct((b, HGRN_HEADS, dk, dk), F32)],
        scratch_shapes=[pltpu.VMEM((nseq, HGRN_HEADS, dk, dk), F32),
                        pltpu.VMEM((rows, dh), F32), pltpu.VMEM((rows, dh), F32),
                        pltpu.VMEM((rows, dh), F32), pltpu.VMEM((rows, dh), F32),
                        pltpu.VMEM((3 * rows, dh), F32)],
        compiler_params=_cparams(("parallel", "arbitrary")),
        name="hgrn",
    )(q, k, g, v, gs, s0, hnorm)


def _s5_prepare(a_re, a_im, log_dt, b_re, b_im, c_re, c_im, d_skip):
    ng, npp = a_re.shape
    nh = b_re.shape[-1]
    L = SSM_CHUNK
    gs = ng // SSM_SETS
    lam_re = jnp.minimum(a_re, MAX_REAL)
    lam_im = a_im
    dt = jnp.exp(log_dt)
    mag = jnp.exp(lam_re * dt)
    ab_re = mag * jnp.cos(lam_im * dt)
    ab_im = mag * jnp.sin(lam_im * dt)
    den = lam_re * lam_re + lam_im * lam_im
    co_re = ((ab_re - 1.0) * lam_re + ab_im * lam_im) / den
    co_im = (ab_im * lam_re - (ab_re - 1.0) * lam_im) / den
    bb_re = co_re[..., None] * b_re - co_im[..., None] * b_im
    bb_im = co_re[..., None] * b_im + co_im[..., None] * b_re
    tau = jnp.arange(L + 1, dtype=F32)[:, None, None]
    pw_mag = jnp.exp(tau * (lam_re * dt))
    pw_re = pw_mag * jnp.cos(tau * (lam_im * dt))
    pw_im = pw_mag * jnp.sin(tau * (lam_im * dt))
    ab_b_re = pw_re[:L, :, :, None] * bb_re - pw_im[:L, :, :, None] * bb_im
    ab_b_im = pw_re[:L, :, :, None] * bb_im + pw_im[:L, :, :, None] * bb_re
    kern = (jnp.einsum('gkp,lgph->lghk', c_re, ab_b_re, precision=HIGHEST)
            - jnp.einsum('gkp,lgph->lghk', c_im, ab_b_im, precision=HIGHEST))
    skip = d_skip[None, :, :, None] * jnp.eye(nh, dtype=F32)
    kern = kern + jnp.where(jnp.arange(L)[:, None, None, None] == 0, skip, 0.0)

    def group_block_diag(c):
        rows, w = c.shape[-2:]
        repeat = (jnp.arange(gs * w)[None, :] % w == jnp.arange(w)[:, None]).astype(F32)
        tiled = jnp.dot(c, repeat, precision=HIGHEST)
        rg = jnp.arange(rows)[:, None] // (rows // gs)
        cq = jnp.arange(gs * w)[None, :] // w
        return jnp.where(rg == cq, tiled, 0.0).astype(BF16).transpose(1, 0, 2, 3).reshape(
            SSM_SETS, L * rows, gs * w)

    bd = group_block_diag(kern.reshape(L, SSM_SETS, gs * nh, nh)).reshape(SSM_SETS, L, gs * nh, gs * nh)
    n_re = group_block_diag(ab_b_re[::-1].transpose(0, 1, 3, 2).reshape(L, SSM_SETS, gs * nh, npp))
    n_im = group_block_diag(ab_b_im[::-1].transpose(0, 1, 3, 2).reshape(L, SSM_SETS, gs * nh, npp))
    inc = jnp.concatenate([n_re, n_im], axis=-1)
    ca_re = c_re[None] * pw_re[1:, :, None, :] - c_im[None] * pw_im[1:, :, None, :]
    ca_im = c_re[None] * pw_im[1:, :, None, :] + c_im[None] * pw_re[1:, :, None, :]
    m = jnp.concatenate([group_block_diag(ca_re.reshape(L, SSM_SETS, gs * nh, npp)),
                         group_block_diag(-ca_im.reshape(L, SSM_SETS, gs * nh, npp))], axis=-1)
    a8 = jnp.concatenate([pw_re[L].reshape(SSM_SETS, 1, gs * npp),
                          pw_im[L].reshape(SSM_SETS, 1, gs * npp)], axis=-1)
    return bd, inc, m, a8


def _s5_kernel(u_ref, xr_ref, xi_ref, bd_ref, inc_ref, m_ref, a8_ref, y_ref, fr_ref, fi_ref, wt_ref, *, sequential):
    n = u_ref.shape[1] // SSM_CHUNK
    ns = xr_ref.shape[-1]
    u = jnp.concatenate([u_ref[0, pl.ds(s, n, stride=SSM_CHUNK), :] for s in range(SSM_CHUNK)], axis=-1)

    ny = SSM_CHUNK * LANES

    @pl.when(pl.program_id(1) == 0)
    def _():
        wt_ref[:, :ny] = jnp.zeros((wt_ref.shape[0], ny), BF16)
        for s in range(SSM_CHUNK):
            for t in range(s, SSM_CHUNK):
                wt_ref[s * LANES:(s + 1) * LANES, t * LANES:(t + 1) * LANES] = bd_ref[0, t - s]
        wt_ref[:, ny:] = inc_ref[0]

    res = _dot(u.astype(BF16), wt_ref[...])
    y_local = res[:, :ny]
    d_re = res[:, ny:ny + ns]
    d_im = res[:, ny + ns:]
    a_re = a8_ref[0][:, :ns]
    a_im = a8_ref[0][:, ns:]
    x0_re = xr_ref[0]
    x0_im = xi_ref[0]
    if sequential:
        row = lax.broadcasted_iota(jnp.int32, (n, ns), 0)
        first = row == 0
        x_re = d_re + jnp.where(first, a_re * x0_re - a_im * x0_im, 0.0)
        x_im = d_im + jnp.where(first, a_re * x0_im + a_im * x0_re, 0.0)
        p_re, p_im = a_re, a_im
        step = 1
        while step < n:
            s_re = jnp.where(row >= step, pltpu.roll(x_re, step, 0), 0.0)
            s_im = jnp.where(row >= step, pltpu.roll(x_im, step, 0), 0.0)
            x_re, x_im = x_re + p_re * s_re - p_im * s_im, x_im + p_re * s_im + p_im * s_re
            p_re, p_im = p_re * p_re - p_im * p_im, 2.0 * p_re * p_im
            step *= 2
        fr_ref[0] = x_re[n - 1:n]
        fi_ref[0] = x_im[n - 1:n]
        xc_re = jnp.where(first, x0_re, pltpu.roll(x_re, 1, 0))
        xc_im = jnp.where(first, x0_im, pltpu.roll(x_im, 1, 0))
    else:
        xc_re, xc_im = x0_re, x0_im
        fr_ref[0] = a_re * x0_re - a_im * x0_im + d_re
        fi_ref[0] = a_re * x0_im + a_im * x0_re + d_im
    xc = jnp.concatenate([xc_re, xc_im], axis=-1).astype(BF16)
    y = y_local + _dot_nt(xc, m_ref[0])
    for t in range(SSM_CHUNK):
        y_ref[0, pl.ds(t, n, stride=SSM_CHUNK), :] = y[:, t * LANES:(t + 1) * LANES]


def _s5(u, x_re, x_im, bd, inc, m, a8, n_tokens, sequential):
    sets = u.shape[0]
    nb, rb, _ = x_re.shape
    ns = m.shape[1] // 2
    st_spec = pl.BlockSpec((1, rb, ns), lambda gi, i: (i, 0, gi))
    st_shape = jax.ShapeDtypeStruct(x_re.shape, F32)
    tok_spec = pl.BlockSpec((1, n_tokens, LANES), lambda gi, i: (gi, i, 0))
    return pl.pallas_call(
        functools.partial(_s5_kernel, sequential=sequential),
        grid=(sets, nb),
        in_specs=[tok_spec, st_spec, st_spec,
                  pl.BlockSpec((1,) + bd.shape[1:], lambda gi, i: (gi, 0, 0, 0)),
                  pl.BlockSpec((1,) + inc.shape[1:], lambda gi, i: (gi, 0, 0)),
                  pl.BlockSpec((1,) + m.shape[1:], lambda gi, i: (gi, 0, 0)),
                  pl.BlockSpec((1, 1, 2 * ns), lambda gi, i: (gi, 0, 0))],
        out_specs=[tok_spec, st_spec, st_spec],
        out_shape=[jax.ShapeDtypeStruct(u.shape, F32), st_shape, st_shape],
        scratch_shapes=[pltpu.VMEM((inc.shape[1], SSM_CHUNK * LANES + inc.shape[2]), BF16)],
        compiler_params=_cparams(("parallel", "arbitrary")),
        name="s5",
    )(u, x_re, x_im, bd, inc, m, a8)


def _gelu_tanh(x):
    return 0.5 * x * (1.0 + jnp.tanh(math.sqrt(2.0 / math.pi) * (x + 0.044715 * (x * x * x))))


def _outproj_kernel(xp_ref, xs_ref, ohp_ref, ohs_ref, ysp_ref, yss_ref,
                    gate_p_ref, shift_p_ref, scale_p_ref, gate_s_ref, shift_s_ref, scale_s_ref,
                    wglu_ref, bglu_ref, sn_ref, wo_h_ref, wo_s_ref, nf_ref, wr_ref, br_ref,
                    x1p_ref, x1s_ref, h2_ref, pair1_ref, pair2_ref, rw_ref, cnt_ref, *, prompt_tiles):
    is_prompt = pl.program_id(0) < prompt_tiles
    _, rows, d = xp_ref.shape

    def per_token(p_ref, s_ref):
        per_seq = jnp.where(is_prompt, p_ref[0], s_ref[0])
        seqs = per_seq.shape[0]
        return jnp.broadcast_to(per_seq[:, None, :], (seqs, rows // seqs, d)).reshape(rows, d)

    x = jnp.where(is_prompt, xp_ref[0], xs_ref[0])
    oh = jnp.where(is_prompt, ohp_ref[...], ohs_ref[...])
    ys = jnp.concatenate([jnp.where(is_prompt, ysp_ref[s], yss_ref[s]) for s in range(SSM_SETS)], axis=-1)
    y = _gelu_tanh(ys)
    y = y * jax.nn.sigmoid(_dot(y.astype(BF16), wglu_ref[...]) + bglu_ref[...])
    o_s = _rms(y) * sn_ref[...]
    mix = _dot(oh.astype(BF16), wo_h_ref[...]) + _dot(o_s.astype(BF16), wo_s_ref[...])
    x1 = x + per_token(gate_p_ref, gate_s_ref) * mix

    @pl.when(is_prompt)
    def _():
        x1p_ref[0] = x1

    @pl.when(jnp.logical_not(is_prompt))
    def _():
        x1s_ref[0] = x1

    h2 = _rms(x1) * nf_ref[...]
    h2 = h2 * (1.0 + per_token(scale_p_ref, scale_s_ref)) + per_token(shift_p_ref, shift_s_ref)
    _store_token_tiles(h2_ref, h2)

    logits = _dot(h2.astype(BF16), wr_ref[...]) + br_ref[...]
    lane = lax.broadcasted_iota(jnp.int32, logits.shape, 1)
    neg = -jnp.inf
    gl = jnp.where(lane < MOE_GROUPS, logits, neg)
    gmax = jnp.max(gl, axis=-1, keepdims=True)
    gidx = jnp.min(jnp.where(gl == gmax, lane, ROUTER_LANES), axis=-1, keepdims=True)
    grp_w = 1.0 / jnp.sum(jnp.exp(gl - gmax), axis=-1, keepdims=True)
    e0 = EXPERT_LANE0 + gidx * MOE_PER_GROUP
    sel = jnp.where((lane >= e0) & (lane < e0 + MOE_PER_GROUP), logits, neg)
    m1 = jnp.max(sel, axis=-1, keepdims=True)
    i1 = jnp.min(jnp.where(sel == m1, lane, ROUTER_LANES), axis=-1, keepdims=True)
    sel2 = jnp.where(lane == i1, neg, sel)
    m2 = jnp.max(sel2, axis=-1, keepdims=True)
    i2 = jnp.min(jnp.where(sel2 == m2, lane, ROUTER_LANES), axis=-1, keepdims=True)
    e2 = jnp.exp(m2 - m1)
    w1 = 1.0 / (1.0 + e2)
    w2 = e2 / (1.0 + e2)
    rw_ref[...] = grp_w * (jnp.where(lane == 0, w1, 0.0) + jnp.where(lane == 1, w2, 0.0))

    @pl.when(pl.program_id(0) == 0)
    def _():
        cnt_ref[...] = jnp.zeros_like(cnt_ref)

    picked = (lane == i1) | (lane == i2)
    earlier = (lax.broadcasted_iota(jnp.int32, (rows, rows), 0)
               > lax.broadcasted_iota(jnp.int32, (rows, rows), 1))
    base = cnt_ref[...]
    before = _dot(earlier.astype(BF16), picked.astype(BF16)) + base
    rank1 = jnp.sum(jnp.where(lane == i1, before, 0.0), axis=-1, keepdims=True).astype(jnp.int32)
    rank2 = jnp.sum(jnp.where(lane == i2, before, 0.0), axis=-1, keepdims=True).astype(jnp.int32)
    cnt_ref[...] = base + jnp.sum(picked.astype(F32), axis=0, keepdims=True)
    info = jnp.where(lane == 0, ((i1 - EXPERT_LANE0) << RANK_BITS) | rank1,
                     jnp.where(lane == 1, ((i2 - EXPERT_LANE0) << RANK_BITS) | rank2, 0)).T
    pair1_ref[...] = info[0]
    pair2_ref[...] = info[1]


def _outproj(x_p, x_s, oh_p, oh_s, ys_p, ys_s, ada_p, ada_s, wglu_bf, bglu, snorm, wo_h, wo_s, nffn, wr, br, rows):
    d = x_p.shape[-1]
    dh = oh_p.shape[-1]
    n_p = x_p.shape[0] * x_p.shape[1]
    n_s = x_s.shape[0] * x_s.shape[1]
    n = n_p + n_s
    tiles_p = n_p // rows
    seqs = rows // x_s.shape[1]

    def pt(i):
        return jnp.minimum(i, tiles_p - 1)

    def st(i):
        return jnp.maximum(i - tiles_p, 0)

    tiles_per_seq = x_p.shape[1] // rows

    def ada_p_spec(col):
        return pl.BlockSpec((1, 1, d), lambda i: (pt(i) // tiles_per_seq, 0, col))

    def ada_s_spec(col):
        return pl.BlockSpec((1, seqs, d), lambda i: (st(i), 0, col))

    def full(a):
        return pl.BlockSpec(a.shape, lambda i: (0,) * a.ndim)

    xp3 = x_p.reshape(tiles_p, rows, d)
    xs3 = x_s.reshape(n_s // rows, rows, d)
    ada_s3 = ada_s.reshape(n_s // rows, seqs, -1)
    outs = pl.pallas_call(
        functools.partial(_outproj_kernel, prompt_tiles=tiles_p),
        grid=(n // rows,),
        in_specs=[pl.BlockSpec((1, rows, d), lambda i: (pt(i), 0, 0)),
                  pl.BlockSpec((1, rows, d), lambda i: (st(i), 0, 0)),
                  pl.BlockSpec((rows, dh), lambda i: (pt(i), 0)),
                  pl.BlockSpec((rows, dh), lambda i: (st(i), 0)),
                  pl.BlockSpec((SSM_SETS, rows, LANES), lambda i: (0, pt(i), 0)),
                  pl.BlockSpec((SSM_SETS, rows, LANES), lambda i: (0, st(i), 0)),
                  ada_p_spec(2), ada_p_spec(3), ada_p_spec(4), ada_s_spec(2), ada_s_spec(3), ada_s_spec(4),
                  full(wglu_bf), full(bglu), full(snorm), full(wo_h), full(wo_s), full(nffn),
                  full(wr), full(br)],
        out_specs=[pl.BlockSpec((1, rows, d), lambda i: (pt(i), 0, 0)),
                   pl.BlockSpec((1, rows, d), lambda i: (st(i), 0, 0)),
                   pl.BlockSpec((rows * TOKEN_TILE_ROWS, LANES), lambda i: (i, 0)),
                   pl.BlockSpec((rows,), lambda i: (i,)),
                   pl.BlockSpec((rows,), lambda i: (i,)),
                   pl.BlockSpec((rows, ROUTER_LANES), lambda i: (i, 0)),
                   pl.BlockSpec((1, ROUTER_LANES), lambda i: (0, 0))],
        out_shape=[jax.ShapeDtypeStruct(xp3.shape, F32),
                   jax.ShapeDtypeStruct(xs3.shape, F32),
                   jax.ShapeDtypeStruct((n * TOKEN_TILE_ROWS, LANES), F32),
                   jax.ShapeDtypeStruct((n,), jnp.int32),
                   jax.ShapeDtypeStruct((n,), jnp.int32),
                   jax.ShapeDtypeStruct((n, ROUTER_LANES), F32),
                   jax.ShapeDtypeStruct((1, ROUTER_LANES), F32)],
        compiler_params=_cparams(("arbitrary",)),
        name="outproj",
    )(xp3, xs3, oh_p, oh_s, ys_p, ys_s, ada_p, ada_p, ada_p, ada_s3, ada_s3, ada_s3,
      wglu_bf, bglu, snorm, wo_h, wo_s, nffn, wr, br)
    x1_p, x1_s = outs[0].reshape(x_p.shape), outs[1].reshape(x_s.shape)
    return (x1_p, x1_s) + tuple(outs[2:])


def _moe_schedule(cnt, n_slots):
    c = cnt[0, EXPERT_LANE0:EXPERT_LANE0 + N_EXPERTS].astype(jnp.int32)
    seg_end = jnp.cumsum(c)
    seg_start = seg_end - c
    first_tile = seg_start // MOE_TILE
    tiles = jnp.where(c > 0, (seg_end - 1) // MOE_TILE - first_tile + 1, 0)
    cum = jnp.cumsum(tiles)
    n_items = cum[-1]
    item = jnp.minimum(jnp.arange(n_slots, dtype=jnp.int32), n_items - 1)
    item_exp = jnp.sum(item[:, None] >= cum[None, :], axis=1).astype(jnp.int32)
    shares = ((c > 0) & (seg_start % MOE_TILE != 0)).astype(jnp.int32)
    item_tile = item - jnp.sum((item[:, None] >= (cum - tiles)[None, :]) * shares[None, :], axis=1)
    return seg_start.astype(jnp.int32), seg_end.astype(jnp.int32), item_tile.astype(jnp.int32), item_exp


def _token_tile(ref, t):
    return ref.at[pl.ds(pl.multiple_of(t * TOKEN_TILE_ROWS, TOKEN_TILE_ROWS), TOKEN_TILE_ROWS)]


def _store_token_tiles(ref, x):
    rows = x.shape[0]
    for c in range(TOKEN_TILE_ROWS):
        ref[pl.ds(c, rows, stride=TOKEN_TILE_ROWS), :] = x[:, c * LANES:(c + 1) * LANES]


def _load_token_tiles(ref, rows):
    return [ref[pl.ds(c, rows, stride=TOKEN_TILE_ROWS), :] for c in range(TOKEN_TILE_ROWS)]


def _positions_kernel(seg_ref, p1_ref, p2_ref, o1_ref, o2_ref):
    for p_ref, o_ref in ((p1_ref, o1_ref), (p2_ref, o2_ref)):
        pair = p_ref[...]
        expert = pair >> RANK_BITS
        pos = pair & ((1 << RANK_BITS) - 1)
        for e in range(N_EXPERTS):
            pos = pos + jnp.where(expert == e, seg_ref[e], 0)
        o_ref[...] = pos


def _positions(seg, pair1, pair2):
    n = pair1.shape[0]
    shape2 = (n // LANES, LANES)
    full = pl.BlockSpec(shape2, lambda: (0, 0))
    pos1, pos2 = pl.pallas_call(
        _positions_kernel,
        in_specs=[pl.BlockSpec(memory_space=pltpu.SMEM), full, full],
        out_specs=[full, full],
        out_shape=[jax.ShapeDtypeStruct(shape2, jnp.int32)] * 2,
        compiler_params=pltpu.CompilerParams(vmem_limit_bytes=VMEM_LIMIT),
        name="moe_positions",
    )(seg, pair1.reshape(shape2), pair2.reshape(shape2))
    return pos1.reshape(n), pos2.reshape(n)


def _inverse_kernel(p1_ref, p2_ref, inv_ref):
    ts = p1_ref.shape[0]
    base = pl.program_id(0) * ts

    def body(t, carry):
        inv_ref[p1_ref[t]] = base + t
        inv_ref[p2_ref[t]] = base + t
        return carry

    lax.fori_loop(0, ts, body, 0, unroll=2 * DMA_UNROLL)


def _inverse(pos1, pos2, ts):
    n = pos1.shape[0]
    idx_spec = pl.BlockSpec((ts,), lambda i: (i,), memory_space=pltpu.SMEM)
    return pl.pallas_call(
        _inverse_kernel,
        grid=(n // ts,),
        in_specs=[idx_spec, idx_spec],
        out_specs=pl.BlockSpec(memory_space=pltpu.SMEM),
        out_shape=jax.ShapeDtypeStruct((2 * n,), jnp.int32),
        compiler_params=_cparams(("arbitrary",)),
        name="moe_inverse",
    )(pos1, pos2)


def _row_gather(src_ref, idx_of, dst_ref, dst_row0, sem, n_rows):
    def copy(t):
        return pltpu.make_async_copy(_token_tile(src_ref, idx_of(t)), _token_tile(dst_ref, dst_row0 + t), sem)

    def start():
        def body(t, carry):
            copy(2 * t).start(priority=0)
            copy(2 * t + 1).start(priority=1)
            return carry
        lax.fori_loop(0, n_rows // 2, body, 0, unroll=DMA_UNROLL // 2)

    def wait():
        def body(t, carry):
            copy(t).wait()
            return carry
        lax.fori_loop(0, n_rows, body, 0, unroll=DMA_UNROLL)

    return start, wait


def _experts_kernel(lo_ref, hi_ref, tile_ref, exp_ref, inv_ref, inv_next_ref, h_ref,
                    wg_ref, wu_ref, wd_ref, ys_ref, xbuf_ref, sems, wg_s, wu_s, wd_s, *, n_slots):
    i = pl.program_id(0)
    prev = jnp.maximum(i - 1, 0)
    e = exp_ref[i]
    slot = i % 2

    @pl.when((i == 0) | (e != exp_ref[prev]))
    def _():
        wg_s[...] = wg_ref[0].astype(BF16)
        wu_s[...] = wu_ref[0].astype(BF16)
        wd_s[...] = wd_ref[0].astype(BF16)

    start_this, wait_this = _row_gather(h_ref, lambda t: inv_ref[t], xbuf_ref, slot * MOE_TILE,
                                        sems.at[slot], MOE_TILE)
    _, wait_next = _row_gather(h_ref, lambda t: inv_next_ref[t], xbuf_ref, (1 - slot) * MOE_TILE,
                               sems.at[1 - slot], MOE_TILE)

    def start_next(rows):
        for t in rows:
            pltpu.make_async_copy(_token_tile(h_ref, inv_next_ref[t]),
                                  _token_tile(xbuf_ref, (1 - slot) * MOE_TILE + t),
                                  sems.at[1 - slot]).start(priority=t % 2)

    @pl.when(i == 0)
    def _():
        start_this()

    wait_this()
    third = -(-MOE_TILE // 3)
    base = pl.multiple_of(slot * (MOE_TILE * TOKEN_TILE_ROWS), MOE_TILE * TOKEN_TILE_ROWS)
    x = jnp.concatenate([xbuf_ref[pl.ds(base + c, MOE_TILE, stride=TOKEN_TILE_ROWS), :]
                         for c in range(TOKEN_TILE_ROWS)], axis=-1).astype(BF16)
    start_next(range(0, third))
    gate = _dot(x, wg_s[...])
    start_next(range(third, 2 * third))
    up = _dot(x, wu_s[...])
    start_next(range(2 * third, MOE_TILE))
    out = _dot((_silu(gate) * up).astype(BF16), wd_s[...])
    row = tile_ref[i] * MOE_TILE + lax.broadcasted_iota(jnp.int32, (MOE_TILE, 1), 0)
    mine = (row >= lo_ref[e]) & (row < hi_ref[e])
    first_visit = (i == 0) | (tile_ref[i] != tile_ref[prev])

    @pl.when(first_visit)
    def _():
        _store_token_tiles(ys_ref, jnp.where(mine, out, 0.0))

    @pl.when(jnp.logical_not(first_visit))
    def _():
        old = jnp.concatenate(_load_token_tiles(ys_ref, MOE_TILE), axis=-1)
        _store_token_tiles(ys_ref, jnp.where(mine, out, old))

    @pl.when(i == n_slots - 1)
    def _():
        wait_next()


def _experts(seg_lo, seg_hi, item_tile, item_exp, inv, h2, wg, wu, wd):
    n_slots = item_tile.shape[0]
    _, d, de = wg.shape

    def w_spec(shape):
        return pl.BlockSpec((1,) + shape, lambda i, lo, hi, tile, ex: (ex[i], 0, 0))

    return pl.pallas_call(
        functools.partial(_experts_kernel, n_slots=n_slots),
        grid_spec=pltpu.PrefetchScalarGridSpec(
            num_scalar_prefetch=4, grid=(n_slots,),
            in_specs=[pl.BlockSpec((MOE_TILE,), lambda i, lo, hi, tile, ex: (tile[i],),
                                   memory_space=pltpu.SMEM),
                      pl.BlockSpec((MOE_TILE,), lambda i, lo, hi, tile, ex: (
                          tile[jnp.minimum(i + 1, n_slots - 1)],), memory_space=pltpu.SMEM),
                      pl.BlockSpec(memory_space=pltpu.HBM),
                      w_spec((d, de)), w_spec((d, de)), w_spec((de, d))],
            out_specs=pl.BlockSpec((MOE_TILE * TOKEN_TILE_ROWS, LANES),
                                   lambda i, lo, hi, tile, ex: (tile[i], 0)),
            scratch_shapes=[pltpu.VMEM((2 * MOE_TILE * TOKEN_TILE_ROWS, LANES), F32),
                            pltpu.SemaphoreType.DMA((2,)),
                            pltpu.VMEM((d, de), BF16), pltpu.VMEM((d, de), BF16), pltpu.VMEM((de, d), BF16)]),
        out_shape=jax.ShapeDtypeStruct((inv.shape[0] * TOKEN_TILE_ROWS, LANES), F32),
        compiler_params=_cparams(("arbitrary",)),
        name="moe_experts",
    )(seg_lo, seg_hi, item_tile, item_exp, inv, inv, h2, wg, wu, wd)


def _combine_kernel(p1_ref, p2_ref, p1n_ref, p2n_ref, rw_ref, x1_ref, gate_ref, shift_ref, scale_ref,
                    nfin_ref, ys_ref, y_ref, r1_ref, r2_ref, sems, *, n_steps):
    _, rows, d = x1_ref.shape
    s = pl.program_id(0)
    slot = s % 2

    def gathers(pa_ref, pb_ref, slot):
        g1 = _row_gather(ys_ref, lambda t: pa_ref[t], r1_ref, slot * rows, sems.at[slot], rows)
        g2 = _row_gather(ys_ref, lambda t: pb_ref[t], r2_ref, slot * rows, sems.at[slot], rows)
        return g1, g2

    this = gathers(p1_ref, p2_ref, slot)
    nxt = gathers(p1n_ref, p2n_ref, 1 - slot)

    def start_next(tokens):
        for t in tokens:
            for k, (p_ref, r_ref) in enumerate(((p1n_ref, r1_ref), (p2n_ref, r2_ref))):
                pltpu.make_async_copy(_token_tile(ys_ref, p_ref[t]), _token_tile(r_ref, (1 - slot) * rows + t),
                                      sems.at[1 - slot]).start(priority=k)

    @pl.when(s == 0)
    def _():
        this[0][0]()
        this[1][0]()

    this[0][1]()
    this[1][1]()

    def per_token(ref):
        per_seq = ref[0]
        seqs = per_seq.shape[0]
        return jnp.broadcast_to(per_seq[:, None, :], (seqs, rows // seqs, d)).reshape(rows, d)

    rw = rw_ref[...]
    w1 = rw[:, 0:1]
    w2 = rw[:, 1:2]
    base = pl.multiple_of(slot * (rows * TOKEN_TILE_ROWS), rows * TOKEN_TILE_ROWS)
    per_piece = rows // TOKEN_TILE_ROWS
    pieces = []
    for c in range(TOKEN_TILE_ROWS):
        pieces.append(w1 * r1_ref[pl.ds(base + c, rows, stride=TOKEN_TILE_ROWS), :]
                      + w2 * r2_ref[pl.ds(base + c, rows, stride=TOKEN_TILE_ROWS), :])
        start_next(range(c * per_piece, (c + 1) * per_piece))
    moe = jnp.concatenate(pieces, axis=-1)
    x2 = x1_ref[0] + per_token(gate_ref) * moe
    hf = _rms(x2) * nfin_ref[...]
    y_ref[0] = hf * (1.0 + per_token(scale_ref)) + per_token(shift_ref)

    @pl.when(s == n_steps - 1)
    def _():
        nxt[0][1]()
        nxt[1][1]()


def _combine(pos1, pos2, rw, x1, ada3, adaf3, nfin, ys, rows, tile0):
    b, t, d = x1.shape
    n_steps = b * t // rows
    seqs = max(rows // t, 1)
    tiles_per_seq = max(t // rows, 1)
    x3 = x1.reshape(n_steps, rows, d)
    ada_v = ada3.reshape(b // seqs, seqs, -1)
    adaf_v = adaf3.reshape(b // seqs, seqs, -1)
    x_spec = pl.BlockSpec((1, rows, d), lambda i: (i, 0, 0))

    def idx_spec(step):
        return pl.BlockSpec((rows,), lambda i: (tile0 + step(i),), memory_space=pltpu.SMEM)

    def ada_spec(col):
        return pl.BlockSpec((1, seqs, d), lambda i: (i // tiles_per_seq, 0, col))

    def cur(i):
        return i

    def nxt(i):
        return jnp.minimum(i + 1, n_steps - 1)

    return pl.pallas_call(
        functools.partial(_combine_kernel, n_steps=n_steps),
        grid=(n_steps,),
        in_specs=[idx_spec(cur), idx_spec(cur), idx_spec(nxt), idx_spec(nxt),
                  pl.BlockSpec((rows, ROUTER_LANES), lambda i: (tile0 + i, 0)),
                  x_spec, ada_spec(5), ada_spec(0), ada_spec(1),
                  pl.BlockSpec((1, d), lambda i: (0, 0)),
                  pl.BlockSpec(memory_space=pltpu.HBM)],
        out_specs=x_spec,
        scratch_shapes=[pltpu.VMEM((2 * rows * TOKEN_TILE_ROWS, LANES), F32),
                        pltpu.VMEM((2 * rows * TOKEN_TILE_ROWS, LANES), F32),
                        pltpu.SemaphoreType.DMA((2,))],
        out_shape=jax.ShapeDtypeStruct(x3.shape, F32),
        compiler_params=_cparams(("arbitrary",)),
        name="moe_combine",
    )(pos1, pos2, pos1, pos2, rw, x3, ada_v, adaf_v, adaf_v, nfin, ys).reshape(x1.shape)


def _mixer(x, ada3, s_h, s_re, s_im, p, *, bb, tt, hgrn_seqs, hgrn_tokens, sequential):
    b, t, d = x.shape
    n = b * t
    q, k, g, v, gs, u = _inproj(x, ada3, p['norm_mix'], p['w_in'], p['lb'], bb, tt)
    oh, s_h_new = _hgrn(q, k, g, v, gs, s_h, p['hgrn_norm'], b, t, hgrn_seqs, hgrn_tokens)
    if sequential:
        xr, xi = s_re.reshape(b, 1, -1), s_im.reshape(b, 1, -1)
        n_tokens = t
    else:
        xr, xi = s_re.reshape(1, b, -1), s_im.reshape(1, b, -1)
        n_tokens = n
    ys, fr, fi = _s5(u, xr, xi, p['ssm_bd'], p['ssm_inc'], p['ssm_m'], p['ssm_a8'], n_tokens, sequential)
    states = (s_h_new[None], fr.reshape(1, b, s_re.shape[-2], s_re.shape[-1]),
              fi.reshape(1, b, s_re.shape[-2], s_re.shape[-1]))
    return oh, ys, states


def kernel(x_prompt, x_sample, c_prompt, c_sample, state_hgrn, state_ssm_re, state_ssm_im, hgrn_lb_logits, w_ada, b_ada, norm_mix, w_in, hgrn_norm, ssm_a_re, ssm_a_im, ssm_log_dt, ssm_b_re, ssm_b_im, ssm_c_re, ssm_c_im, ssm_d, ssm_w_glu, ssm_b_glu, ssm_norm, w_out, norm_ffn, moe_w_group, moe_b_group, moe_w_router, moe_b_router, moe_w_gate, moe_w_up, moe_w_down, w_ada_final, b_ada_final, norm_final):
    depth = w_ada.shape[0]
    assert depth == 1
    d = x_prompt.shape[-1]
    bp = x_prompt.shape[0]
    dh = hgrn_norm.shape[-1]
    dk = dh // HGRN_HEADS
    de = moe_w_gate.shape[-1]
    n_exp = MOE_GROUPS * MOE_PER_GROUP

    lb = jax.nn.softmax(hgrn_lb_logits.astype(F32), axis=0)[0].reshape(1, dh)
    bd, inc, m, a8 = _s5_prepare(ssm_a_re[0], ssm_a_im[0], ssm_log_dt[0], ssm_b_re[0], ssm_b_im[0],
                                 ssm_c_re[0], ssm_c_im[0], ssm_d[0])

    def router_lanes(group_part, expert_part):
        rows = group_part.shape[0]
        return jnp.concatenate(
            [group_part, jnp.zeros((rows, EXPERT_LANE0 - MOE_GROUPS), F32), expert_part,
             jnp.zeros((rows, ROUTER_LANES - EXPERT_LANE0 - n_exp), F32)], axis=1)

    w_rt = router_lanes(moe_w_group[0], moe_w_router[0].transpose(1, 0, 2).reshape(d, n_exp))
    b_rt = router_lanes(moe_b_group[0].reshape(1, MOE_GROUPS), moe_b_router[0].reshape(1, n_exp))
    p = dict(
        lb=lb, norm_mix=norm_mix[0].reshape(1, d), w_in=w_in[0].astype(BF16),
        hgrn_norm=hgrn_norm[0].reshape(1, dh),
        ssm_bd=bd, ssm_inc=inc, ssm_m=m, ssm_a8=a8,
        w_glu=ssm_w_glu[0].astype(BF16), b_glu=ssm_b_glu[0].reshape(1, -1), ssm_norm=ssm_norm[0].reshape(1, -1),
        wo_h=w_out[0, :dh].astype(BF16), wo_s=w_out[0, dh:].astype(BF16),
        norm_ffn=norm_ffn[0].reshape(1, d), w_rt=w_rt.astype(BF16), b_rt=b_rt,
    )
    wg = moe_w_gate[0].reshape(n_exp, d, de)
    wu = moe_w_up[0].reshape(n_exp, d, de)
    wd = moe_w_down[0].reshape(n_exp, de, d)
    nfin = norm_final.reshape(1, d)

    c_all = jnp.concatenate([c_prompt, c_sample], axis=0)
    ada = _silu_linear(c_all, w_ada[0], b_ada[0])
    adaf = _silu_linear(c_all, w_ada_final, b_ada_final)
    ada_p, ada_s = ada[:bp].reshape(bp, 1, -1), ada[bp:].reshape(x_sample.shape[0], 1, -1)
    adaf_p, adaf_s = adaf[:bp].reshape(bp, 1, -1), adaf[bp:].reshape(x_sample.shape[0], 1, -1)

    zeros_h = jnp.zeros((bp, HGRN_HEADS, dk, dk), F32)
    zeros_s = jnp.zeros((bp,) + state_ssm_re.shape[2:], F32)
    oh_p, ys_p, st_p = _mixer(x_prompt, ada_p, zeros_h, zeros_s, zeros_s, p,
                              bb=1, tt=512, hgrn_seqs=1, hgrn_tokens=512, sequential=True)
    oh_s, ys_s, st_s = _mixer(x_sample, ada_s, state_hgrn[0], state_ssm_re[0], state_ssm_im[0], p,
                              bb=32, tt=8, hgrn_seqs=16, hgrn_tokens=8, sequential=False)
    x1_p, x1_s, h2, pair1, pair2, rw, cnt = _outproj(
        x_prompt, x_sample, oh_p, oh_s, ys_p, ys_s, ada_p, ada_s, p['w_glu'], p['b_glu'], p['ssm_norm'],
        p['wo_h'], p['wo_s'], p['norm_ffn'], p['w_rt'], p['b_rt'], TOKEN_ROWS)

    n_pairs = 2 * pair1.shape[0]
    n_slots = n_pairs // MOE_TILE + N_EXPERTS
    seg, seg_end, item_tile, item_exp = _moe_schedule(cnt, n_slots)
    pos1, pos2 = _positions(seg, pair1, pair2)
    inv = _inverse(pos1, pos2, 1024)
    ys = _experts(seg, seg_end, item_tile, item_exp, inv, h2, wg, wu, wd)
    tiles_p = x_prompt.shape[0] * x_prompt.shape[1] // TOKEN_ROWS
    y_p = _combine(pos1, pos2, rw, x1_p, ada_p, adaf_p, nfin, ys, TOKEN_ROWS, 0)
    y_s = _combine(pos1, pos2, rw, x1_s, ada_s, adaf_s, nfin, ys, TOKEN_ROWS, tiles_p)
    return (y_p, y_s) + st_p + st_s
```
